```python
import math
import jax, jax.numpy as jnp
from jax import lax
import numpy as np

D_MODEL = 2048
BATCH = 2
SEQ = 4096
DEPTH = 4

N_META = 16
BLOCK = 128
WINDOW = 128
HEAD_DIM = 64
A_HEADS = 16
A_KV_HEADS = 2
A_WIDTH = 1024
A_KV_WIDTH = 128
B_HEADS = 16
B_WIDTH = 1024
B_Q_RANK = 512
B_KV_RANK = 256
IDX_HEADS = 8
IDX_DIM = 64
TOPK_MAX = 256
N_BUCKETS = 32
MAX_DISTANCE = 128
EPS = 1e-6
NEG = -1e30
INVALID_POS = 1 << 30

kernel_name = "hybrid_swa_sink_dsa_gated_trunk"

IN_SPLITS = (A_WIDTH, A_KV_WIDTH, A_KV_WIDTH, A_WIDTH,
             B_Q_RANK, B_KV_RANK, B_WIDTH, IDX_DIM, IDX_HEADS,
             D_MODEL, D_MODEL)
IN_COLS = 8264


def _split_points():
    return [int(v) for v in np.cumsum(np.array(IN_SPLITS))[:-1]]


def _rmsnorm(x, g):
    xf = x.astype(jnp.float32)
    y = xf * lax.rsqrt(jnp.mean(xf * xf, axis=-1, keepdims=True) + EPS)
    return (y * g.astype(jnp.float32)).astype(x.dtype)


def _t5_bucket(dist):
    n = jnp.maximum(dist, 0)
    max_exact = N_BUCKETS // 2
    nf = jnp.maximum(n, 1).astype(jnp.float32)
    large = max_exact + (jnp.log(nf / max_exact) / math.log(MAX_DISTANCE / max_exact)
                         * (N_BUCKETS - max_exact)).astype(jnp.int32)
    large = jnp.minimum(large, N_BUCKETS - 1)
    return jnp.where(n < max_exact, n, large)


def _sink_attention(q, k, v, qpos, kpos, sinks, bias_a):
    G, R = A_KV_HEADS, A_HEADS // A_KV_HEADS
    s = jnp.einsum('...qgrd,...kgd->...grqk', q.astype(jnp.float32),
                   k.astype(jnp.float32)) * (HEAD_DIM ** -0.5)
    d = qpos[..., :, None] - kpos[..., None, :]
    mask = (d >= 0) & ((d < WINDOW) | (kpos[..., None, :] < N_META))
    b = jnp.moveaxis(bias_a[_t5_bucket(d)], -1, -3)
    b = b.reshape(d.shape[:-2] + (G, R) + d.shape[-2:]).astype(jnp.float32)
    logits = jnp.where(mask[..., None, None, :, :], s + b, NEG)
    sk = sinks.reshape(G, R)[:, :, None, None].astype(jnp.float32)
    m = jnp.maximum(jnp.max(logits, axis=-1, keepdims=True), sk)
    p = jnp.exp(logits - m)
    p = p / (jnp.sum(p, axis=-1, keepdims=True) + jnp.exp(sk - m))
    o = jnp.einsum('...grqk,...kgd->...qgrd', p, v.astype(jnp.float32))
    return o.astype(q.dtype)


def _mixer_a(q, k, v, sinks, bias_a):
    Bn, L = q.shape[0], q.shape[1]
    S = L - N_META
    nb = S // BLOCK
    G, R = A_KV_HEADS, A_HEADS // A_KV_HEADS
    q = q.reshape(Bn, L, G, R, HEAD_DIM)
    qm, qr = q[:, :N_META], q[:, N_META:]
    km, kr = k[:, :N_META], k[:, N_META:]
    vm, vr = v[:, :N_META], v[:, N_META:]
    meta_pos = jnp.arange(N_META, dtype=jnp.int32)
    o_meta = _sink_attention(qm, km, vm, meta_pos, meta_pos, sinks, bias_a)
    qb = qr.reshape(Bn, nb, BLOCK, G, R, HEAD_DIM)
    kb = kr.reshape(Bn, nb, BLOCK, G, HEAD_DIM)
    vb = vr.reshape(Bn, nb, BLOCK, G, HEAD_DIM)
    pad = ((0, 0), (1, 0), (0, 0), (0, 0), (0, 0))
    kprev = jnp.pad(kb, pad)[:, :-1]
    vprev = jnp.pad(vb, pad)[:, :-1]
    kmb = jnp.broadcast_to(km[:, None], (Bn, nb, N_META, G, HEAD_DIM))
    vmb = jnp.broadcast_to(vm[:, None], (Bn, nb, N_META, G, HEAD_DIM))
    kcat = jnp.concatenate([kmb, kprev, kb], axis=2)
    vcat = jnp.concatenate([vmb, vprev, vb], axis=2)
    blk_pos = N_META + jnp.arange(S, dtype=jnp.int32).reshape(nb, BLOCK)
    prev_pos = jnp.concatenate([jnp.full((1, BLOCK), INVALID_POS, jnp.int32), blk_pos[:-1]], axis=0)
    kpos = jnp.concatenate([jnp.broadcast_to(meta_pos, (nb, N_META)), prev_pos, blk_pos], axis=1)
    o_real = _sink_attention(qb, kcat, vcat, blk_pos, kpos, sinks, bias_a)
    o_real = o_real.reshape(Bn, S, A_WIDTH)
    return jnp.concatenate([o_meta.reshape(Bn, N_META, A_WIDTH), o_real], axis=1)


def _mixer_b(q_lat, ckv, q_idx, k_idx, w_idx, bias_b, w_uv, topk):
    Bn, L = q_lat.shape[0], q_lat.shape[1]
    S = L - N_META
    nb = S // BLOCK
    kpos = jnp.arange(L, dtype=jnp.int32)
    k_idx32 = k_idx.astype(jnp.float32)

    def attend(ql, qi, wi, qpos):
        sc = jnp.einsum('bqhd,bkd->bqhk', qi.astype(jnp.float32), k_idx32) * (IDX_DIM ** -0.5)
        isc = jnp.einsum('bqh,bqhk->bqk', wi.astype(jnp.float32), jax.nn.relu(sc))
        isc = jnp.where(kpos[None, :] <= qpos[:, None], isc, NEG)
        _, idx = lax.top_k(isc, topk)
        valid = idx <= qpos[None, :, None]
        sel = jax.vmap(lambda c, i: c[i])(ckv, idx).astype(jnp.float32)
        s = jnp.einsum('bqhr,bqkr->bqhk', ql.astype(jnp.float32), sel) * (HEAD_DIM ** -0.5)
        b = bias_b[_t5_bucket(qpos[None, :, None] - idx)].astype(jnp.float32)
        s = jnp.where(valid[:, :, None, :], s + jnp.moveaxis(b, -1, -2), NEG)
        p = jax.nn.softmax(s, axis=-1)
        return jnp.einsum('bqhk,bqkr->bqhr', p, sel).astype(ql.dtype)

    meta_pos = jnp.arange(N_META, dtype=jnp.int32)
    o_meta = attend(q_lat[:, :N_META], q_idx[:, :N_META], w_idx[:, :N_META], meta_pos)

    def to_blocks(a):
        a = a[:, N_META:]
        a = a.reshape((Bn, nb, BLOCK) + a.shape[2:])
        return jnp.moveaxis(a, 1, 0)

    blk_pos = N_META + jnp.arange(S, dtype=jnp.int32).reshape(nb, BLOCK)
    xs = (to_blocks(q_lat), to_blocks(q_idx), to_blocks(w_idx), blk_pos)
    o_real = lax.map(lambda a: attend(a[0], a[1], a[2], a[3]), xs)
    o_real = jnp.moveaxis(o_real, 0, 1).reshape(Bn, S, B_HEADS, B_KV_RANK)
    o_lat = jnp.concatenate([o_meta, o_real], axis=1)
    o = jnp.einsum('blhr,hrd->blhd', o_lat, w_uv)
    return o.reshape(Bn, L, B_WIDTH)


def setup_inputs(seed: int = 0) -> dict:
    key = jax.random.key(seed)
    ks = jax.random.split(key, 20)
    f = jnp.float32
    nrm = lambda k, shape, scale: jax.random.normal(k, shape, f) * scale
    return {
        "x": nrm(ks[0], (BATCH, SEQ, D_MODEL), 1.0),
        "meta_tokens": nrm(ks[1], (N_META, D_MODEL), 1.0),
        "bias_table": nrm(ks[2], (N_BUCKETS, A_HEADS + B_HEADS), 0.5),
        "norm_g": 1.0 + nrm(ks[3], (DEPTH, D_MODEL), 0.02),
        "w_in": nrm(ks[4], (DEPTH, D_MODEL, IN_COLS), D_MODEL ** -0.5),
        "q_norm_g": 1.0 + nrm(ks[5], (DEPTH, B_Q_RANK), 0.02),
        "kv_norm_g": 1.0 + nrm(ks[6], (DEPTH, B_KV_RANK), 0.02),
        "w_qb": nrm(ks[7], (DEPTH, B_Q_RANK, B_WIDTH), B_Q_RANK ** -0.5),
        "w_iq": nrm(ks[8], (DEPTH, B_Q_RANK, IDX_HEADS * IDX_DIM), B_Q_RANK ** -0.5),
        "w_uk": nrm(ks[9], (DEPTH, B_HEADS, HEAD_DIM, B_KV_RANK), B_KV_RANK ** -0.5),
        "w_uv": nrm(ks[10], (DEPTH, B_HEADS, B_KV_RANK, HEAD_DIM), B_KV_RANK ** -0.5),
        "sinks": nrm(ks[11], (DEPTH, A_HEADS), 1.0),
        "w_proj_a": nrm(ks[12], (DEPTH, A_WIDTH, D_MODEL), A_WIDTH ** -0.5),
        "w_proj_b": nrm(ks[13], (DEPTH, B_WIDTH, D_MODEL), B_WIDTH ** -0.5),
        "w_out": nrm(ks[14], (DEPTH, D_MODEL, D_MODEL), D_MODEL ** -0.5),
        "final_g": 1.0 + nrm(ks[15], (D_MODEL,), 0.02),
    }


def reference(x, meta_tokens, bias_table, norm_g, w_in, q_norm_g, kv_norm_g, w_qb, w_iq,
              w_uk, w_uv, sinks, w_proj_a, w_proj_b, w_out, final_g):
    Bn, S, D = x.shape
    L = S + N_META
    topk = min(TOPK_MAX, S // 4)
    h = jnp.concatenate([jnp.broadcast_to(meta_tokens.astype(x.dtype)[None], (Bn, N_META, D)), x], axis=1)
    bias_a = bias_table[:, :A_HEADS]
    bias_b = bias_table[:, A_HEADS:]
    splits = _split_points()
    for l in range(DEPTH):
        u = _rmsnorm(h, norm_g[l])
        proj = jnp.einsum('bld,dc->blc', u, w_in[l])
        qa, ka, va, za, cq, ckv, zb, kidx, widx, ga, gb = jnp.split(proj, splits, axis=-1)
        ya = _mixer_a(qa.reshape(Bn, L, A_HEADS, HEAD_DIM),
                      ka.reshape(Bn, L, A_KV_HEADS, HEAD_DIM),
                      va.reshape(Bn, L, A_KV_HEADS, HEAD_DIM), sinks[l], bias_a)
        ya = ya * jax.nn.silu(za)
        cq = _rmsnorm(cq, q_norm_g[l])
        ckv = _rmsnorm(ckv, kv_norm_g[l])
        qb = jnp.einsum('blr,rc->blc', cq, w_qb[l]).reshape(Bn, L, B_HEADS, HEAD_DIM)
        q_lat = jnp.einsum('blhd,hdr->blhr', qb, w_uk[l])
        q_idx = jnp.einsum('blr,rc->blc', cq, w_iq[l]).reshape(Bn, L, IDX_HEADS, IDX_DIM)
        yb = _mixer_b(q_lat, ckv, q_idx, kidx, widx * (IDX_HEADS ** -0.5), bias_b, w_uv[l], topk)
        yb = yb * jax.nn.silu(zb)
        merged = (jax.nn.sigmoid(ga) * jnp.einsum('blc,cd->bld', ya, w_proj_a[l])
                  + jax.nn.sigmoid(gb) * jnp.einsum('blc,cd->bld', yb, w_proj_b[l]))
        h = h + jnp.einsum('bld,de->ble', merged, w_out[l])
    return _rmsnorm(h, final_g)[:, N_META:]
```

```python
import functools
import math

import numpy as np
import jax
import jax.numpy as jnp
from jax import lax
from jax.experimental import pallas as pl
from jax.experimental.pallas import tpu as pltpu

D_MODEL = 2048
N_META = 16
WINDOW = 128
HEAD_DIM = 64
A_HEADS = 16
A_WIDTH = 1024
B_HEADS = 16
B_WIDTH = 1024
B_Q_RANK = 512
B_KV_RANK = 256
IDX_HEADS = 8
IDX_DIM = 64
TOPK_MAX = 256
N_BUCKETS = 32
MAX_DISTANCE = 128
EPS = 1e-6
NEG = -1e30

LANES = 128
QB_A = 128
QB_B = 256
VMEM_LIMIT = 56 * 1024 * 1024
INT_MIN = -(2 ** 31)
INT_MAX = 2 ** 31 - 1

C_QA, C_ZA, C_ZB, C_CQ, C_CKV, C_KA, C_VA, C_GA, C_GB = (
    0, 1024, 2048, 3072, 3584, 3840, 3968, 4096, 6144)
N_MAIN = 8192

F32 = jnp.float32
BF16 = jnp.bfloat16
_NT = (((1,), (1,)), ((), ()))


def _cparams(n_grid):
    return pltpu.CompilerParams(
        dimension_semantics=("arbitrary",) * n_grid,
        vmem_limit_bytes=VMEM_LIMIT)


def _t5_bucket_np(d):
    max_exact = N_BUCKETS // 2
    nf = np.maximum(d, 1).astype(np.float32)
    large = max_exact + (np.log(nf / np.float32(max_exact)) / np.float32(math.log(MAX_DISTANCE / max_exact))
                         * np.float32(N_BUCKETS - max_exact)).astype(np.int32)
    large = np.minimum(large, N_BUCKETS - 1)
    return np.where(d < max_exact, d, large).astype(np.int32)


def _bias_tiles_kernel(tab_ref, idx_ref, out_a_ref, out_b_ref):
    h = pl.program_id(0)
    idx = idx_ref[...]
    acc_a = jnp.zeros(idx.shape, F32)
    acc_b = jnp.zeros(idx.shape, F32)
    far_b = tab_ref[N_BUCKETS - 1, A_HEADS + h]
    for b in range(N_BUCKETS):
        hit = idx == b
        acc_a = jnp.where(hit, tab_ref[b, h], acc_a)
        acc_b = jnp.where(hit, tab_ref[b, A_HEADS + h] - far_b, acc_b)
    out_a_ref[0] = acc_a
    out_b_ref[0] = acc_b[:, :2 * LANES]


def _bias_tiles(bias_table):
    r = np.arange(QB_A)[:, None]
    k = np.arange(QB_A)[None, :]
    prev = _t5_bucket_np(np.maximum(QB_A + r - k, 0))
    cur = _t5_bucket_np(np.maximum(r - k, 0))
    far = np.full((QB_A, QB_A), N_BUCKETS - 1, np.int32)
    idx = jnp.asarray(np.concatenate([prev, cur, far], axis=1))
    return pl.pallas_call(
        _bias_tiles_kernel,
        grid=(A_HEADS,),
        in_specs=[pl.BlockSpec(memory_space=pltpu.SMEM),
                  pl.BlockSpec((QB_A, 3 * LANES), lambda h: (0, 0))],
        out_specs=[pl.BlockSpec((1, QB_A, 3 * LANES), lambda h: (h, 0, 0)),
                   pl.BlockSpec((1, QB_A, 2 * LANES), lambda h: (h, 0, 0))],
        out_shape=[jax.ShapeDtypeStruct((A_HEADS, QB_A, 3 * LANES), F32),
                   jax.ShapeDtypeStruct((B_HEADS, QB_A, 2 * LANES), F32)],
        compiler_params=_cparams(1),
        name="bias_tiles",
    )(bias_table, idx)


def _rms(x, g):
    return x * lax.rsqrt(jnp.mean(x * x, axis=-1, keepdims=True) + EPS) * g


def _rmsnorm_kernel(h_ref, g_ref, u_ref):
    u_ref[...] = _rms(h_ref[...], g_ref[...]).astype(u_ref.dtype)


def _rmsnorm(h, g, tm=512):
    rows, d = h.shape
    return pl.pallas_call(
        _rmsnorm_kernel,
        grid=(rows // tm,),
        in_specs=[pl.BlockSpec((tm, d), lambda i: (i, 0)),
                  pl.BlockSpec((1, d), lambda i: (0, 0))],
        out_specs=pl.BlockSpec((tm, d), lambda i: (i, 0)),
        out_shape=jax.ShapeDtypeStruct((rows, d), BF16),
        compiler_params=_cparams(1),
        name="rmsnorm",
    )(h, g.reshape(1, d))


def _matmul_kernel(x_ref, w_ref, o_ref):
    o_ref[...] = jnp.dot(x_ref[...], w_ref[...], preferred_element_type=F32).astype(o_ref.dtype)


def _matmul(x, w, tm=512, tn=1024):
    rows, kdim = x.shape
    n = w.shape[1]
    return pl.pallas_call(
        _matmul_kernel,
        grid=(n // tn, rows // tm),
        in_specs=[pl.BlockSpec((tm, kdim), lambda j, i: (i, 0)),
                  pl.BlockSpec((kdim, tn), lambda j, i: (0, j))],
        out_specs=pl.BlockSpec((tm, tn), lambda j, i: (i, j)),
        out_shape=jax.ShapeDtypeStruct((rows, n), F32),
        compiler_params=_cparams(2),
        name="in_proj",
    )(x, w)


def _silu(z):
    return z / (1.0 + jnp.exp(-z))


def _sigmoid(z):
    return 1.0 / (1.0 + jnp.exp(-z))


def _mixer_a_kernel(sink_ref, q_ref, kc_ref, vc_ref, kp_ref, vp_ref, km_ref, vm_ref, za_ref,
                    bias_ref, o_ref):
    j = pl.program_id(1)
    nk = 3 * QB_A
    q = (q_ref[...] * (HEAD_DIM ** -0.5)).astype(BF16)
    kcat = jnp.concatenate([kp_ref[...], kc_ref[...], km_ref[...]], axis=0)
    vcat = jnp.concatenate([vp_ref[...], vc_ref[...], vm_ref[...]], axis=0)
    lane = lax.broadcasted_iota(jnp.int32, (nk, LANES), 1)
    lo = lane < HEAD_DIM
    kswap = pltpu.roll(kcat, HEAD_DIM, 1)
    vswap = pltpu.roll(vcat, HEAD_DIM, 1)
    zero = jnp.zeros_like(kcat)
    k_half = ((jnp.where(lo, kcat, zero).astype(BF16), jnp.where(lo, zero, kswap).astype(BF16)),
              (jnp.where(lo, kswap, zero).astype(BF16), jnp.where(lo, zero, kcat).astype(BF16)))
    v_half = ((jnp.where(lo, vcat, zero).astype(BF16), jnp.where(lo, zero, vswap).astype(BF16)),
              (jnp.where(lo, vswap, zero).astype(BF16), jnp.where(lo, zero, vcat).astype(BF16)))

    r = lax.broadcasted_iota(jnp.int32, (QB_A, nk), 0)
    c = lax.broadcasted_iota(jnp.int32, (QB_A, nk), 1)
    jv = jnp.zeros((QB_A, nk), jnp.int32) + j
    in_prev = (c < QB_A) & (jv >= 1) & ((c > r) | ((jv == 1) & (c < N_META)))
    in_cur = (c >= QB_A) & (c < 2 * QB_A) & (c - QB_A <= r)
    in_meta = (c >= 2 * QB_A) & (jv >= 2) & (c - 2 * QB_A < N_META)
    mask = in_prev | in_cur | in_meta

    for pair in range(A_HEADS // 2):
        g = pair // (A_HEADS // 4)
        q2 = q[:, pair * LANES:(pair + 1) * LANES]
        o2 = jnp.zeros((QB_A, LANES), F32)
        for par in range(2):
            hd = 2 * pair + par
            s = lax.dot_general(q2, k_half[g][par], _NT, preferred_element_type=F32)
            logits = jnp.where(mask, s + bias_ref[hd], NEG)
            sink = sink_ref[hd]
            m = jnp.maximum(jnp.max(logits, axis=-1, keepdims=True), sink)
            p = jnp.exp(logits - m)
            denom = jnp.sum(p, axis=-1, keepdims=True) + jnp.exp(sink - m)
            pv = jnp.dot(p.astype(BF16), v_half[g][par], preferred_element_type=F32)
            o2 = o2 + pv / denom
        z = za_ref[:, pair * LANES:(pair + 1) * LANES]
        o_ref[:, pair * LANES:(pair + 1) * LANES] = (o2 * _silu(z)).astype(o_ref.dtype)


def _mixer_a(proj, sinks, bias_a, n_batch, lp):
    nb = lp // QB_A
    rows = n_batch * lp
    ck, cv = C_KA // LANES, C_VA // LANES
    cur = lambda b, j: b * nb + j
    prev = lambda b, j: b * nb + jnp.maximum(j - 1, 0)
    first = lambda b, j: b * nb
    return pl.pallas_call(
        _mixer_a_kernel,
        grid=(n_batch, nb),
        in_specs=[
            pl.BlockSpec(memory_space=pltpu.SMEM),
            pl.BlockSpec((QB_A, A_WIDTH), lambda b, j: (cur(b, j), C_QA // A_WIDTH)),
            pl.BlockSpec((QB_A, LANES), lambda b, j: (cur(b, j), ck)),
            pl.BlockSpec((QB_A, LANES), lambda b, j: (cur(b, j), cv)),
            pl.BlockSpec((QB_A, LANES), lambda b, j: (prev(b, j), ck)),
            pl.BlockSpec((QB_A, LANES), lambda b, j: (prev(b, j), cv)),
            pl.BlockSpec((QB_A, LANES), lambda b, j: (first(b, j), ck)),
            pl.BlockSpec((QB_A, LANES), lambda b, j: (first(b, j), cv)),
            pl.BlockSpec((QB_A, A_WIDTH), lambda b, j: (cur(b, j), C_ZA // A_WIDTH)),
            pl.BlockSpec((A_HEADS, QB_A, 3 * LANES), lambda b, j: (0, 0, 0)),
        ],
        out_specs=pl.BlockSpec((QB_A, A_WIDTH), lambda b, j: (cur(b, j), 0)),
        out_shape=jax.ShapeDtypeStruct((rows, A_WIDTH), BF16),
        compiler_params=_cparams(2),
        name="mixer_a",
    )(sinks, proj, proj, proj, proj, proj, proj, proj, proj, bias_a)


def _bprep_kernel(cq_ref, ckv_ref, u_ref, qg_ref, kvg_ref, wqb_ref, wiq_ref, wuk_ref, wsm_ref,
                  ql_ref, qi_ref, wi_ref, ckvn_ref, klo_ref, khi_ref):
    cqn = _rms(cq_ref[...], qg_ref[...]).astype(BF16)
    ckvn_ref[0] = _rms(ckv_ref[...], kvg_ref[...]).astype(ckvn_ref.dtype)
    qb = jnp.dot(cqn, wqb_ref[...], preferred_element_type=F32).astype(BF16)
    for pair in range(B_HEADS // 2):
        ql2 = jnp.dot(qb[:, pair * LANES:(pair + 1) * LANES], wuk_ref[pair],
                      preferred_element_type=F32)
        ql_ref[0, 2 * pair] = ql2[:, :B_KV_RANK].astype(ql_ref.dtype)
        ql_ref[0, 2 * pair + 1] = ql2[:, B_KV_RANK:].astype(ql_ref.dtype)
    qi = jnp.dot(cqn, wiq_ref[...], preferred_element_type=F32)
    for pair in range(IDX_HEADS // 2):
        qi_ref[0, pair] = qi[:, pair * LANES:(pair + 1) * LANES].astype(qi_ref.dtype)
    small = jnp.dot(u_ref[...], wsm_ref[...], preferred_element_type=F32)
    klo_ref[0] = small[:, :LANES].astype(klo_ref.dtype)
    khi_ref[0] = small[:, LANES:2 * LANES].astype(khi_ref.dtype)
    wi_ref[0] = small[:, 2 * LANES:] * (IDX_HEADS ** -0.5)


def _bprep(proj, u, qg, kvg, wqb, wiq, wuk_bd, wsmall, n_batch, lp, tm=256):
    nt = lp // tm
    row = lambda b, i: b * nt + i
    const2 = lambda b, i: (0, 0)
    return pl.pallas_call(
        _bprep_kernel,
        grid=(n_batch, nt),
        in_specs=[
            pl.BlockSpec((tm, B_Q_RANK), lambda b, i: (row(b, i), C_CQ // B_Q_RANK)),
            pl.BlockSpec((tm, B_KV_RANK), lambda b, i: (row(b, i), C_CKV // B_KV_RANK)),
            pl.BlockSpec((tm, D_MODEL), lambda b, i: (row(b, i), 0)),
            pl.BlockSpec((1, B_Q_RANK), const2),
            pl.BlockSpec((1, B_KV_RANK), const2),
            pl.BlockSpec((B_Q_RANK, B_WIDTH), const2),
            pl.BlockSpec((B_Q_RANK, IDX_HEADS * IDX_DIM), const2),
            pl.BlockSpec((B_HEADS // 2, LANES, 2 * B_KV_RANK), lambda b, i: (0, 0, 0)),
            pl.BlockSpec((D_MODEL, 3 * LANES), const2),
        ],
        out_specs=[
            pl.BlockSpec((1, B_HEADS, tm, B_KV_RANK), lambda b, i: (b, 0, i, 0)),
            pl.BlockSpec((1, IDX_HEADS // 2, tm, LANES), lambda b, i: (b, 0, i, 0)),
            pl.BlockSpec((1, tm, LANES), lambda b, i: (b, i, 0)),
            pl.BlockSpec((1, tm, B_KV_RANK), lambda b, i: (b, i, 0)),
            pl.BlockSpec((1, tm, LANES), lambda b, i: (b, i, 0)),
            pl.BlockSpec((1, tm, LANES), lambda b, i: (b, i, 0)),
        ],
        out_shape=[
            jax.ShapeDtypeStruct((n_batch, B_HEADS, lp, B_KV_RANK), BF16),
            jax.ShapeDtypeStruct((n_batch, IDX_HEADS // 2, lp, LANES), BF16),
            jax.ShapeDtypeStruct((n_batch, lp, LANES), F32),
            jax.ShapeDtypeStruct((n_batch, lp, B_KV_RANK), BF16),
            jax.ShapeDtypeStruct((n_batch, lp, LANES), BF16),
            jax.ShapeDtypeStruct((n_batch, lp, LANES), BF16),
        ],
        compiler_params=_cparams(2),
        name="bprep",
    )(proj, proj, u, qg, kvg, wqb, wiq, wuk_bd, wsmall)


def _mixer_b_kernel(ql_ref, qi_ref, wi_ref, ckv_ref, klo_ref, khi_ref, zb_ref, bias_ref, wuv_ref,
                    o_ref, key_scr, acc_scr, m_scr, l_scr, p_scr, thr_scr, jmax_scr):
    jq = pl.program_id(1)
    nch = jq + 1
    qb, kc = QB_B, QB_B
    rows = B_HEADS * qb
    row = lax.broadcasted_iota(jnp.int32, (qb, kc), 0)
    col = lax.broadcasted_iota(jnp.int32, (qb, kc), 1)
    qpos = jq * qb + row

    qi = qi_ref[0].reshape(IDX_HEADS // 2 * qb, LANES)
    wi = wi_ref[0]

    def idx_body(c, carry):
        start = pl.multiple_of(c * kc, kc)
        sc = (lax.dot_general(qi, klo_ref[0, pl.ds(start, kc), :], _NT, preferred_element_type=F32),
              lax.dot_general(qi, khi_ref[0, pl.ds(start, kc), :], _NT, preferred_element_type=F32))
        isc = jnp.zeros((qb, kc), F32)
        for pair in range(IDX_HEADS // 2):
            for par in range(2):
                hd = 2 * pair + par
                isc = isc + wi[:, hd:hd + 1] * jnp.maximum(sc[par][pair * qb:(pair + 1) * qb], 0.0)
        bits = lax.bitcast_convert_type(isc, jnp.int32)
        key = bits ^ ((bits >> 31) & INT_MAX)
        key_scr[c] = jnp.where(c * kc + col <= qpos, key, INT_MIN)
        return carry

    lax.fori_loop(0, nch, idx_body, 0)

    thr_scr[...] = jnp.full((qb, 1), INT_MIN, jnp.int32)
    jmax_scr[...] = jnp.full((qb, 1), -1, jnp.int32)

    def count(pred_fn):
        def body(c, acc):
            return acc + jnp.where(pred_fn(c, key_scr[c]), 1.0, 0.0)
        acc = lax.fori_loop(0, nch, body, jnp.zeros((qb, kc), F32))
        return jnp.sum(acc, axis=-1, keepdims=True)

    @pl.when(jq >= 1)
    def _():
        kf = float(TOPK_MAX)

        def bis_body(_, st):
            lo, hi, c_hi = st
            mid = (lo >> 1) + (hi >> 1) + (lo & hi & 1)
            cnt = count(lambda c, k: k >= mid)
            ge = cnt >= kf
            return jnp.where(ge, mid, lo), jnp.where(ge, hi, mid), jnp.where(ge, c_hi, cnt)

        lo0 = jnp.full((qb, 1), INT_MIN + 1, jnp.int32)
        hi0 = jnp.full((qb, 1), INT_MAX, jnp.int32)
        lo, hi, c_hi = lax.fori_loop(0, 32, bis_body, (lo0, hi0, jnp.zeros((qb, 1), F32)))
        c_lo = count(lambda c, k: k >= lo)
        thr_scr[...] = lo
        jmax_scr[...] = jnp.full((qb, 1), INT_MAX, jnp.int32)
        tie = c_lo > kf

        @pl.when(jnp.max(jnp.where(tie, 1.0, 0.0)) > 0.5)
        def _():
            need = kf - c_hi

            def tie_body(_, st):
                jlo, jhi = st
                mid = (jlo + jhi) >> 1
                cnt = count(lambda c, k: (k == lo) & (c * kc + col <= mid))
                ge = cnt >= need
                return jnp.where(ge, jlo, mid), jnp.where(ge, mid, jhi)

            jlo0 = jnp.full((qb, 1), -1, jnp.int32)
            jhi0 = jnp.full((qb, 1), 8191, jnp.int32)
            _, jhi = lax.fori_loop(0, 13, tie_body, (jlo0, jhi0))
            jmax_scr[...] = jnp.where(tie, jhi, INT_MAX)

    thr = thr_scr[...]
    jmax = jmax_scr[...]

    def sel_body(c, carry):
        k = key_scr[c]
        sel = (k > thr) | ((k == thr) & (c * kc + col <= jmax))
        key_scr[c] = jnp.where(sel, 1, 0)
        return carry

    lax.fori_loop(0, nch, sel_body, 0)

    ql = ql_ref[0].reshape(rows, B_KV_RANK)
    m_scr[...] = jnp.full(m_scr.shape, -jnp.inf, F32)
    l_scr[...] = jnp.zeros(l_scr.shape, F32)
    acc_scr[...] = jnp.zeros(acc_scr.shape, F32)

    def attend(c, near):
        start = pl.multiple_of(c * kc, kc)
        kv = ckv_ref[0, pl.ds(start, kc), :]
        s = lax.dot_general(ql, kv, _NT, preferred_element_type=F32)
        dead = key_scr[c] == 0
        for hd in range(B_HEADS):
            sl = slice(hd * qb, (hd + 1) * qb)
            sh = s[sl]
            if near == "cur":
                t1, t0 = bias_ref[hd, :, :LANES], bias_ref[hd, :, LANES:]
                sh = sh + jnp.concatenate([jnp.concatenate([t0, t0], axis=1),
                                           jnp.concatenate([t1, t0], axis=1)], axis=0)
            elif near == "prev":
                t1 = bias_ref[hd, :, :LANES]
                top = sh[:QB_A] + jnp.concatenate([jnp.zeros_like(t1), t1], axis=1)
                sh = jnp.concatenate([top, sh[QB_A:]], axis=0)
            sh = jnp.where(dead, NEG, sh)
            m_old = m_scr[sl]
            m_new = jnp.maximum(m_old, jnp.max(sh, axis=-1, keepdims=True))
            alpha = jnp.exp(m_old - m_new)
            p = jnp.exp(sh - jnp.concatenate([m_new, m_new], axis=1))
            l_scr[sl] = alpha * l_scr[sl] + jnp.sum(p, axis=-1, keepdims=True)
            m_scr[sl] = m_new
            p_scr[sl] = p.astype(BF16)
            acc_scr[sl] = acc_scr[sl] * jnp.concatenate([alpha, alpha], axis=1)
        acc_scr[...] += jnp.dot(p_scr[...], kv, preferred_element_type=F32)

    def far_body(c, carry):
        attend(c, None)
        return carry

    lax.fori_loop(0, jq - 1, far_body, 0)

    @pl.when(jq >= 1)
    def _():
        attend(jq - 1, "prev")

    attend(jq, "cur")

    for pair in range(B_HEADS // 2):
        y2 = jnp.zeros((qb, LANES), F32)
        for par in range(2):
            hd = 2 * pair + par
            sl = slice(hd * qb, (hd + 1) * qb)
            l = l_scr[sl]
            o_lat = acc_scr[sl] / jnp.concatenate([l, l], axis=1)
            y2 = y2 + jnp.dot(o_lat.astype(BF16), wuv_ref[pair, par], preferred_element_type=F32)
        z = zb_ref[:, pair * LANES:(pair + 1) * LANES]
        o_ref[:, pair * LANES:(pair + 1) * LANES] = (y2 * _silu(z)).astype(o_ref.dtype)


def _mixer_b(ql, qi, wi, ckvn, klo, khi, proj, bias_b, wuv_pairs, n_batch, lp):
    nb = lp // QB_B
    rows = n_batch * lp
    whole = lambda b, j: (b, 0, 0)
    return pl.pallas_call(
        _mixer_b_kernel,
        grid=(n_batch, nb),
        in_specs=[
            pl.BlockSpec((1, B_HEADS, QB_B, B_KV_RANK), lambda b, j: (b, 0, j, 0)),
            pl.BlockSpec((1, IDX_HEADS // 2, QB_B, LANES), lambda b, j: (b, 0, j, 0)),
            pl.BlockSpec((1, QB_B, LANES), lambda b, j: (b, j, 0)),
            pl.BlockSpec((1, lp, B_KV_RANK), whole),
            pl.BlockSpec((1, lp, LANES), whole),
            pl.BlockSpec((1, lp, LANES), whole),
            pl.BlockSpec((QB_B, B_WIDTH), lambda b, j: (b * nb + j, C_ZB // B_WIDTH)),
            pl.BlockSpec((B_HEADS, QB_A, 2 * LANES), lambda b, j: (0, 0, 0)),
            pl.BlockSpec((B_HEADS // 2, 2, B_KV_RANK, LANES), lambda b, j: (0, 0, 0, 0)),
        ],
        out_specs=pl.BlockSpec((QB_B, B_WIDTH), lambda b, j: (b * nb + j, 0)),
        out_shape=jax.ShapeDtypeStruct((rows, B_WIDTH), BF16),
        scratch_shapes=[
            pltpu.VMEM((nb, QB_B, QB_B), jnp.int32),
            pltpu.VMEM((B_HEADS * QB_B, B_KV_RANK), F32),
            pltpu.VMEM((B_HEADS * QB_B, LANES), F32),
            pltpu.VMEM((B_HEADS * QB_B, LANES), F32),
            pltpu.VMEM((B_HEADS * QB_B, QB_B), BF16),
            pltpu.VMEM((QB_B, 1), jnp.int32),
            pltpu.VMEM((QB_B, 1), jnp.int32),
        ],
        compiler_params=_cparams(2),
        name="mixer_b",
    )(ql, qi, wi, ckvn, klo, khi, proj, bias_b, wuv_pairs)


def _merge_kernel(ya_ref, yb_ref, ga_ref, gb_ref, h_ref, wpa_ref, wpb_ref, wo_ref, g_ref, *out_refs,
                  last):
    pa = jnp.dot(ya_ref[...], wpa_ref[...], preferred_element_type=F32)
    pb = jnp.dot(yb_ref[...], wpb_ref[...], preferred_element_type=F32)
    merged = _sigmoid(ga_ref[...]) * pa + _sigmoid(gb_ref[...]) * pb
    h_new = h_ref[...] + jnp.dot(merged.astype(BF16), wo_ref[...], preferred_element_type=F32)
    normed = _rms(h_new, g_ref[...])
    if last:
        out_refs[0][...] = normed
    else:
        out_refs[0][...] = h_new
        out_refs[1][...] = normed.astype(BF16)


def _merge(ya, yb, proj, h, wpa, wpb, wo, g_next, last, tm=256):
    rows, d = h.shape
    const = lambda i: (0, 0)
    once = pl.Buffered(1)
    if last:
        out_specs = [pl.BlockSpec((tm, d), lambda i: (i, 0))]
        out_shape = [jax.ShapeDtypeStruct((rows, d), F32)]
    else:
        out_specs = [pl.BlockSpec((tm, d), lambda i: (i, 0)), pl.BlockSpec((tm, d), lambda i: (i, 0))]
        out_shape = [jax.ShapeDtypeStruct((rows, d), F32), jax.ShapeDtypeStruct((rows, d), BF16)]
    return pl.pallas_call(
        functools.partial(_merge_kernel, last=last),
        grid=(rows // tm,),
        in_specs=[
            pl.BlockSpec((tm, A_WIDTH), lambda i: (i, 0)),
            pl.BlockSpec((tm, B_WIDTH), lambda i: (i, 0)),
            pl.BlockSpec((tm, d), lambda i: (i, C_GA // D_MODEL)),
            pl.BlockSpec((tm, d), lambda i: (i, C_GB // D_MODEL)),
            pl.BlockSpec((tm, d), lambda i: (i, 0)),
            pl.BlockSpec((A_WIDTH, d), const, pipeline_mode=once),
            pl.BlockSpec((B_WIDTH, d), const, pipeline_mode=once),
            pl.BlockSpec((d, d), const, pipeline_mode=once),
            pl.BlockSpec((1, d), const),
        ],
        out_specs=out_specs,
        out_shape=out_shape,
        compiler_params=_cparams(1),
        name="merge_out",
    )(ya, yb, proj, proj, h, wpa, wpb, wo, g_next.reshape(1, d))


def _pack_weights(w_in, w_uk, w_uv, w_qb, w_iq, w_proj_a, w_proj_b, w_out):
    depth = w_in.shape[0]
    sec = lambda a, b: w_in[:, :, a:b]
    qa, ka, va, za = sec(0, 1024), sec(1024, 1152), sec(1152, 1280), sec(1280, 2304)
    cq, ckv, zb = sec(2304, 2816), sec(2816, 3072), sec(3072, 4096)
    kidx, widx = sec(4096, 4160), sec(4160, 4168)
    ga, gb = sec(4168, 6216), sec(6216, 8264)
    w_main = jnp.concatenate([qa, za, zb, cq, ckv, ka, va, ga, gb], axis=-1).astype(BF16)
    z64 = jnp.zeros_like(kidx)
    zpad = jnp.zeros(widx.shape[:2] + (LANES - IDX_HEADS,), w_in.dtype)
    w_small = jnp.concatenate([kidx, z64, z64, kidx, widx, zpad], axis=-1).astype(BF16)
    uk = (w_uk * (HEAD_DIM ** -0.5)).reshape(depth, B_HEADS // 2, 2, HEAD_DIM, B_KV_RANK)
    zuk = jnp.zeros_like(uk[:, :, 0])
    wuk_bd = jnp.concatenate([jnp.concatenate([uk[:, :, 0], zuk], axis=-1),
                              jnp.concatenate([zuk, uk[:, :, 1]], axis=-1)], axis=-2).astype(BF16)
    uv = w_uv.reshape(depth, B_HEADS // 2, 2, B_KV_RANK, HEAD_DIM)
    zuv = jnp.zeros_like(uv[:, :, 0])
    wuv_pairs = jnp.stack([jnp.concatenate([uv[:, :, 0], zuv], axis=-1),
                           jnp.concatenate([zuv, uv[:, :, 1]], axis=-1)], axis=2).astype(BF16)
    w_iq8 = (w_iq * (IDX_DIM ** -0.5)).astype(BF16)
    return (w_main, w_small, wuk_bd, wuv_pairs, w_qb.astype(BF16), w_iq8,
            w_proj_a.astype(BF16), w_proj_b.astype(BF16), w_out.astype(BF16))


def kernel(x, meta_tokens, bias_table, norm_g, w_in, q_norm_g, kv_norm_g, w_qb, w_iq, w_uk, w_uv,
           sinks, w_proj_a, w_proj_b, w_out, final_g):
    n_batch, seq, d = x.shape
    depth = w_in.shape[0]
    length = seq + N_META
    lp = -(-length // QB_B) * QB_B
    assert min(TOPK_MAX, seq // 4) == TOPK_MAX and d == D_MODEL

    (w_main, w_small, wuk_bd, wuv_pairs, wqb, wiq, wpa, wpb, wo) = _pack_weights(
        w_in, w_uk, w_uv, w_qb, w_iq, w_proj_a, w_proj_b, w_out)
    bias_a, bias_b = _bias_tiles(bias_table)

    meta = jnp.broadcast_to(meta_tokens.astype(x.dtype)[None], (n_batch, N_META, d))
    pad = jnp.zeros((n_batch, lp - length, d), x.dtype)
    h = jnp.concatenate([meta, x, pad], axis=1).reshape(n_batch * lp, d)

    u = _rmsnorm(h, norm_g[0])
    out = None
    for l in range(depth):
        proj = _matmul(u, w_main[l])
        ya = _mixer_a(proj, sinks[l], bias_a, n_batch, lp)
        ql, qi, wi, ckvn, klo, khi = _bprep(
            proj, u, q_norm_g[l].reshape(1, -1), kv_norm_g[l].reshape(1, -1),
            wqb[l], wiq[l], wuk_bd[l], w_small[l], n_batch, lp)
        yb = _mixer_b(ql, qi, wi, ckvn, klo, khi, proj, bias_b, wuv_pairs[l], n_batch, lp)
        last = l == depth - 1
        g_next = final_g if last else norm_g[l + 1]
        res = _merge(ya, yb, proj, h, wpa[l], wpb[l], wo[l], g_next, last)
        if last:
            out = res[0]
        else:
            h, u = res
    return out.reshape(n_batch, lp, d)[:, N_META:length]
```

```python
import functools
import math

import numpy as np
import jax
import jax.numpy as jnp
from jax import lax
from jax.experimental import pallas as pl
from jax.experimental.pallas import tpu as pltpu

D_MODEL = 2048
N_META = 16
WINDOW = 128
HEAD_DIM = 64
A_HEADS = 16
A_WIDTH = 1024
B_HEADS = 16
B_WIDTH = 1024
B_Q_RANK = 512
B_KV_RANK = 256
IDX_HEADS = 8
IDX_DIM = 64
TOPK_MAX = 256
N_BUCKETS = 32
MAX_DISTANCE = 128
EPS = 1e-6
NEG = -1e30

LANES = 128
QB_A = 128
QB_B = 256
SUB_B = 32
BIS_UNROLL = 3
LOG2E = math.log2(math.e)
VMEM_LIMIT = 56 * 1024 * 1024
INT_MIN = -(2 ** 31)
INT_MAX = 2 ** 31 - 1

C_QA, C_ZA, C_ZB, C_CQ, C_CKV, C_KA, C_VA, C_GA, C_GB = (
    0, 1024, 2048, 3072, 3584, 3840, 3968, 4096, 6144)
N_MAIN = 8192

F32 = jnp.float32
BF16 = jnp.bfloat16
_NT = (((1,), (1,)), ((), ()))


def _cparams(n_grid):
    return pltpu.CompilerParams(
        dimension_semantics=("arbitrary",) * n_grid,
        vmem_limit_bytes=VMEM_LIMIT)


def _t5_bucket_np(d):
    max_exact = N_BUCKETS // 2
    nf = np.maximum(d, 1).astype(np.float32)
    large = max_exact + (np.log(nf / np.float32(max_exact)) / np.float32(math.log(MAX_DISTANCE / max_exact))
                         * np.float32(N_BUCKETS - max_exact)).astype(np.int32)
    large = np.minimum(large, N_BUCKETS - 1)
    return np.where(d < max_exact, d, large).astype(np.int32)


def _bias_tiles_kernel(tab_ref, idx_ref, out_a_ref, out_b_ref):
    h = pl.program_id(0)
    idx = idx_ref[...]
    acc_a = jnp.zeros(idx.shape, F32)
    acc_b = jnp.zeros(idx.shape, F32)
    far_b = tab_ref[N_BUCKETS - 1, A_HEADS + h]
    for b in range(N_BUCKETS):
        hit = idx == b
        acc_a = jnp.where(hit, tab_ref[b, h], acc_a)
        acc_b = jnp.where(hit, (tab_ref[b, A_HEADS + h] - far_b) * LOG2E, acc_b)
    out_a_ref[0] = acc_a
    out_b_ref[0] = acc_b[:, :2 * LANES]


def _bias_tiles(bias_table):
    r = np.arange(QB_A)[:, None]
    k = np.arange(QB_A)[None, :]
    prev = _t5_bucket_np(np.maximum(QB_A + r - k, 0))
    cur = _t5_bucket_np(np.maximum(r - k, 0))
    far = np.full((QB_A, QB_A), N_BUCKETS - 1, np.int32)
    idx = jnp.asarray(np.concatenate([prev, cur, far], axis=1))
    return pl.pallas_call(
        _bias_tiles_kernel,
        grid=(A_HEADS,),
        in_specs=[pl.BlockSpec(memory_space=pltpu.SMEM),
                  pl.BlockSpec((QB_A, 3 * LANES), lambda h: (0, 0))],
        out_specs=[pl.BlockSpec((1, QB_A, 3 * LANES), lambda h: (h, 0, 0)),
                   pl.BlockSpec((1, QB_A, 2 * LANES), lambda h: (h, 0, 0))],
        out_shape=[jax.ShapeDtypeStruct((A_HEADS, QB_A, 3 * LANES), F32),
                   jax.ShapeDtypeStruct((B_HEADS, QB_A, 2 * LANES), F32)],
        compiler_params=_cparams(1),
        name="bias_tiles",
    )(bias_table, idx)


def _rms(x, g):
    return x * lax.rsqrt(jnp.mean(x * x, axis=-1, keepdims=True) + EPS) * g


def _rmsnorm_kernel(h_ref, g_ref, u_ref):
    u_ref[...] = _rms(h_ref[...], g_ref[...]).astype(u_ref.dtype)


def _rmsnorm(h, g, tm=512):
    rows, d = h.shape
    return pl.pallas_call(
        _rmsnorm_kernel,
        grid=(rows // tm,),
        in_specs=[pl.BlockSpec((tm, d), lambda i: (i, 0)),
                  pl.BlockSpec((1, d), lambda i: (0, 0))],
        out_specs=pl.BlockSpec((tm, d), lambda i: (i, 0)),
        out_shape=jax.ShapeDtypeStruct((rows, d), BF16),
        compiler_params=_cparams(1),
        name="rmsnorm",
    )(h, g.reshape(1, d))


def _matmul_kernel(x_ref, w_ref, o_ref):
    o_ref[...] = jnp.dot(x_ref[...], w_ref[...], preferred_element_type=F32).astype(o_ref.dtype)


def _matmul(x, w, tm=512, tn=1024):
    rows, kdim = x.shape
    n = w.shape[1]
    return pl.pallas_call(
        _matmul_kernel,
        grid=(n // tn, rows // tm),
        in_specs=[pl.BlockSpec((tm, kdim), lambda j, i: (i, 0)),
                  pl.BlockSpec((kdim, tn), lambda j, i: (0, j))],
        out_specs=pl.BlockSpec((tm, tn), lambda j, i: (i, j)),
        out_shape=jax.ShapeDtypeStruct((rows, n), F32),
        compiler_params=_cparams(2),
        name="in_proj",
    )(x, w)


def _silu(z):
    return z / (1.0 + jnp.exp(-z))


def _sigmoid(z):
    return 1.0 / (1.0 + jnp.exp(-z))


def _mixer_a_kernel(sink_ref, q_ref, kc_ref, vc_ref, kp_ref, vp_ref, km_ref, vm_ref, za_ref,
                    bias_ref, o_ref):
    j = pl.program_id(1)
    nk = 3 * QB_A
    q = (q_ref[...] * (HEAD_DIM ** -0.5)).astype(BF16)
    kcat = jnp.concatenate([kp_ref[...], kc_ref[...], km_ref[...]], axis=0)
    vcat = jnp.concatenate([vp_ref[...], vc_ref[...], vm_ref[...]], axis=0)
    lane = lax.broadcasted_iota(jnp.int32, (nk, LANES), 1)
    lo = lane < HEAD_DIM
    kswap = pltpu.roll(kcat, HEAD_DIM, 1)
    vswap = pltpu.roll(vcat, HEAD_DIM, 1)
    zero = jnp.zeros_like(kcat)
    k_half = ((jnp.where(lo, kcat, zero).astype(BF16), jnp.where(lo, zero, kswap).astype(BF16)),
              (jnp.where(lo, kswap, zero).astype(BF16), jnp.where(lo, zero, kcat).astype(BF16)))
    v_half = ((jnp.where(lo, vcat, zero).astype(BF16), jnp.where(lo, zero, vswap).astype(BF16)),
              (jnp.where(lo, vswap, zero).astype(BF16), jnp.where(lo, zero, vcat).astype(BF16)))

    r = lax.broadcasted_iota(jnp.int32, (QB_A, nk), 0)
    c = lax.broadcasted_iota(jnp.int32, (QB_A, nk), 1)
    jv = jnp.zeros((QB_A, nk), jnp.int32) + j
    in_prev = (c < QB_A) & (jv >= 1) & ((c > r) | ((jv == 1) & (c < N_META)))
    in_cur = (c >= QB_A) & (c < 2 * QB_A) & (c - QB_A <= r)
    in_meta = (c >= 2 * QB_A) & (jv >= 2) & (c - 2 * QB_A < N_META)
    mask = in_prev | in_cur | in_meta

    for pair in range(A_HEADS // 2):
        g = pair // (A_HEADS // 4)
        q2 = q[:, pair * LANES:(pair + 1) * LANES]
        o2 = jnp.zeros((QB_A, LANES), F32)
        for par in range(2):
            hd = 2 * pair + par
            s = lax.dot_general(q2, k_half[g][par], _NT, preferred_element_type=F32)
            logits = jnp.where(mask, s + bias_ref[hd], NEG)
            sink = sink_ref[hd]
            m = jnp.maximum(jnp.max(logits, axis=-1, keepdims=True), sink)
            p = jnp.exp(logits - m)
            denom = jnp.sum(p, axis=-1, keepdims=True) + jnp.exp(sink - m)
            pv = jnp.dot(p.astype(BF16), v_half[g][par], preferred_element_type=F32)
            o2 = o2 + pv / denom
        z = za_ref[:, pair * LANES:(pair + 1) * LANES]
        o_ref[:, pair * LANES:(pair + 1) * LANES] = (o2 * _silu(z)).astype(o_ref.dtype)


def _mixer_a(proj, sinks, bias_a, n_batch, lp):
    nb = lp // QB_A
    rows = n_batch * lp
    ck, cv = C_KA // LANES, C_VA // LANES
    cur = lambda b, j: b * nb + j
    prev = lambda b, j: b * nb + jnp.maximum(j - 1, 0)
    first = lambda b, j: b * nb
    return pl.pallas_call(
        _mixer_a_kernel,
        grid=(n_batch, nb),
        in_specs=[
            pl.BlockSpec(memory_space=pltpu.SMEM),
            pl.BlockSpec((QB_A, A_WIDTH), lambda b, j: (cur(b, j), C_QA // A_WIDTH)),
            pl.BlockSpec((QB_A, LANES), lambda b, j: (cur(b, j), ck)),
            pl.BlockSpec((QB_A, LANES), lambda b, j: (cur(b, j), cv)),
            pl.BlockSpec((QB_A, LANES), lambda b, j: (prev(b, j), ck)),
            pl.BlockSpec((QB_A, LANES), lambda b, j: (prev(b, j), cv)),
            pl.BlockSpec((QB_A, LANES), lambda b, j: (first(b, j), ck)),
            pl.BlockSpec((QB_A, LANES), lambda b, j: (first(b, j), cv)),
            pl.BlockSpec((QB_A, A_WIDTH), lambda b, j: (cur(b, j), C_ZA // A_WIDTH)),
            pl.BlockSpec((A_HEADS, QB_A, 3 * LANES), lambda b, j: (0, 0, 0)),
        ],
        out_specs=pl.BlockSpec((QB_A, A_WIDTH), lambda b, j: (cur(b, j), 0)),
        out_shape=jax.ShapeDtypeStruct((rows, A_WIDTH), BF16),
        compiler_params=_cparams(2),
        name="mixer_a",
    )(sinks, proj, proj, proj, proj, proj, proj, proj, proj, bias_a)


def _bprep_kernel(cq_ref, ckv_ref, u_ref, qg_ref, kvg_ref, wqb_ref, wiq_ref, wuk_ref, wsm_ref,
                  ql_ref, qi_ref, wi_ref, ckvn_ref, klo_ref, khi_ref):
    cqn = _rms(cq_ref[...], qg_ref[...]).astype(BF16)
    ckvn_ref[0] = _rms(ckv_ref[...], kvg_ref[...]).astype(ckvn_ref.dtype)
    qb = jnp.dot(cqn, wqb_ref[...], preferred_element_type=F32).astype(BF16)
    for pair in range(B_HEADS // 2):
        ql2 = jnp.dot(qb[:, pair * LANES:(pair + 1) * LANES], wuk_ref[pair],
                      preferred_element_type=F32)
        ql_ref[0, 2 * pair] = ql2[:, :B_KV_RANK].astype(ql_ref.dtype)
        ql_ref[0, 2 * pair + 1] = ql2[:, B_KV_RANK:].astype(ql_ref.dtype)
    qi = jnp.dot(cqn, wiq_ref[...], preferred_element_type=F32)
    for pair in range(IDX_HEADS // 2):
        qi_ref[0, pair] = qi[:, pair * LANES:(pair + 1) * LANES].astype(qi_ref.dtype)
    small = jnp.dot(u_ref[...], wsm_ref[...], preferred_element_type=F32)
    klo_ref[0] = small[:, :LANES].astype(klo_ref.dtype)
    khi_ref[0] = small[:, LANES:2 * LANES].astype(khi_ref.dtype)
    wi_ref[0] = small[:, 2 * LANES:] * (IDX_HEADS ** -0.5)


def _bprep(proj, u, qg, kvg, wqb, wiq, wuk_bd, wsmall, n_batch, lp, tm=256):
    nt = lp // tm
    row = lambda b, i: b * nt + i
    const2 = lambda b, i: (0, 0)
    return pl.pallas_call(
        _bprep_kernel,
        grid=(n_batch, nt),
        in_specs=[
            pl.BlockSpec((tm, B_Q_RANK), lambda b, i: (row(b, i), C_CQ // B_Q_RANK)),
            pl.BlockSpec((tm, B_KV_RANK), lambda b, i: (row(b, i), C_CKV // B_KV_RANK)),
            pl.BlockSpec((tm, D_MODEL), lambda b, i: (row(b, i), 0)),
            pl.BlockSpec((1, B_Q_RANK), const2),
            pl.BlockSpec((1, B_KV_RANK), const2),
            pl.BlockSpec((B_Q_RANK, B_WIDTH), const2),
            pl.BlockSpec((B_Q_RANK, IDX_HEADS * IDX_DIM), const2),
            pl.BlockSpec((B_HEADS // 2, LANES, 2 * B_KV_RANK), lambda b, i: (0, 0, 0)),
            pl.BlockSpec((D_MODEL, 3 * LANES), const2),
        ],
        out_specs=[
            pl.BlockSpec((1, B_HEADS, tm, B_KV_RANK), lambda b, i: (b, 0, i, 0)),
            pl.BlockSpec((1, IDX_HEADS // 2, tm, LANES), lambda b, i: (b, 0, i, 0)),
            pl.BlockSpec((1, tm, LANES), lambda b, i: (b, i, 0)),
            pl.BlockSpec((1, tm, B_KV_RANK), lambda b, i: (b, i, 0)),
            pl.BlockSpec((1, tm, LANES), lambda b, i: (b, i, 0)),
            pl.BlockSpec((1, tm, LANES), lambda b, i: (b, i, 0)),
        ],
        out_shape=[
            jax.ShapeDtypeStruct((n_batch, B_HEADS, lp, B_KV_RANK), BF16),
            jax.ShapeDtypeStruct((n_batch, IDX_HEADS // 2, lp, LANES), BF16),
            jax.ShapeDtypeStruct((n_batch, lp, LANES), F32),
            jax.ShapeDtypeStruct((n_batch, lp, B_KV_RANK), BF16),
            jax.ShapeDtypeStruct((n_batch, lp, LANES), BF16),
            jax.ShapeDtypeStruct((n_batch, lp, LANES), BF16),
        ],
        compiler_params=_cparams(2),
        name="bprep",
    )(proj, proj, u, qg, kvg, wqb, wiq, wuk_bd, wsmall)


def _mixer_b_kernel(ql_ref, qi_ref, wi_ref, ckv_ref, klo_ref, khi_ref, zb_ref, bias_ref, wuv_ref,
                    o_ref, key_scr, s_scr, acc_scr, m_scr, l_scr, alpha_scr, p_scr):
    jq = pl.program_id(1)
    nch = jq + 1
    qb, kc = QB_B, QB_B
    rows = B_HEADS * qb
    kf = float(TOPK_MAX)

    qi = qi_ref[0].reshape(IDX_HEADS // 2 * qb, LANES)
    wit = wi_ref[0].T
    krow = lax.broadcasted_iota(jnp.int32, (qb, kc), 0)
    krow_sub = lax.broadcasted_iota(jnp.int32, (SUB_B, kc), 0)
    qpos = jq * qb + lax.broadcasted_iota(jnp.int32, (qb, kc), 1)

    def idx_body(c, carry):
        start = pl.multiple_of(c * kc, kc)
        sc = (lax.dot_general(klo_ref[0, pl.ds(start, kc), :], qi, _NT, preferred_element_type=F32),
              lax.dot_general(khi_ref[0, pl.ds(start, kc), :], qi, _NT, preferred_element_type=F32))
        isc = jnp.zeros((kc, qb), F32)
        for pair in range(IDX_HEADS // 2):
            for par in range(2):
                hd = 2 * pair + par
                isc = isc + wit[hd:hd + 1, :] * jnp.maximum(sc[par][:, pair * qb:(pair + 1) * qb], 0.0)
        bits = lax.bitcast_convert_type(isc, jnp.int32)
        key = bits ^ ((bits >> 31) & INT_MAX)
        key_scr[c] = jnp.where(c * kc + krow <= qpos, key, INT_MIN)
        return carry

    lax.fori_loop(0, nch, idx_body, 0)

    def reduce_chunks(tile_fn, init, combine):
        def body(c, acc):
            for r0 in range(0, kc, SUB_B):
                t = tile_fn(c, r0, key_scr[c, r0:r0 + SUB_B, :])
                t = t.reshape(SUB_B // 8, 8, qb)
                for i in range(SUB_B // 8):
                    acc = combine(acc, t[i])
            return acc
        return lax.fori_loop(0, nch, body, init)

    def count(pred_fn):
        acc = reduce_chunks(lambda c, r0, k: jnp.where(pred_fn(c, r0, k), 1.0, 0.0),
                            jnp.zeros((8, qb), F32), lambda a, b: a + b)
        return jnp.sum(acc, axis=0, keepdims=True)

    def unkey(k):
        return lax.bitcast_convert_type(k ^ ((k >> 31) & INT_MAX), F32)

    def tokey(v):
        b = lax.bitcast_convert_type(v, jnp.int32)
        return b ^ ((b >> 31) & INT_MAX)

    def select_all():
        return jnp.full((1, qb), INT_MIN, jnp.int32), jnp.full((1, qb), -1, jnp.int32)

    def select_topk():
        kmax = reduce_chunks(lambda c, r0, k: k, jnp.full((8, qb), INT_MIN, jnp.int32), jnp.maximum)
        kmax = jnp.max(kmax, axis=0, keepdims=True)

        def bis_cond(st):
            return (st[0] < 64) & (st[5] > 0)

        def bis_step(it, lo, hi, c_lo, c_hi):
            mid_i = (lo >> 1) + (hi >> 1) + (lo & hi & 1)
            mid_v = tokey(0.5 * unkey(lo) + 0.5 * unkey(hi))
            use_v = (it < 12) & (mid_v > lo) & (mid_v < hi)
            mid = jnp.where(use_v, mid_v, mid_i)
            cnt = count(lambda c, r0, k: k >= mid)
            ge = cnt >= kf
            return (jnp.where(ge, mid, lo), jnp.where(ge, hi, mid),
                    jnp.where(ge, cnt, c_lo), jnp.where(ge, c_hi, cnt))

        def bis_body(st):
            it, lo, hi, c_lo, c_hi, _ = st
            for _ in range(BIS_UNROLL):
                lo, hi, c_lo, c_hi = bis_step(it, lo, hi, c_lo, c_hi)
                it = it + 1
            active = (lo + 1 < hi) & (c_lo != kf)
            return it, lo, hi, c_lo, c_hi, jnp.max(jnp.where(active, 1, 0))

        st0 = (jnp.int32(0), jnp.full((1, qb), INT_MIN + 1, jnp.int32), kmax + 1,
               jnp.full((1, qb), 2.0 * kf, F32), jnp.zeros((1, qb), F32), jnp.int32(1))
        _, lo, hi, c_lo, c_hi, _ = lax.while_loop(bis_cond, bis_body, st0)
        tie = c_lo != kf

        def tie_break():
            need = kf - c_hi

            def tie_body(_, st):
                jlo, jhi = st
                mid = (jlo + jhi) >> 1
                cnt = count(lambda c, r0, k: (k == lo) & (c * kc + r0 + krow_sub <= mid))
                ge = cnt >= need
                return jnp.where(ge, jlo, mid), jnp.where(ge, mid, jhi)

            _, jhi = lax.fori_loop(0, 13, tie_body, (jnp.full((1, qb), -1, jnp.int32),
                                                     jnp.full((1, qb), 8191, jnp.int32)))
            return jnp.where(tie, jhi, INT_MAX)

        jmax = lax.cond(jnp.max(jnp.where(tie, 1, 0)) > 0, tie_break,
                        lambda: jnp.full((1, qb), INT_MAX, jnp.int32))
        return lo, jmax

    thr, jmax = lax.cond(jq >= 1, select_topk, select_all)

    def sel_body(c, carry):
        k = key_scr[c]
        sel = (k > thr) | ((k == thr) & (c * kc + krow <= jmax))
        key_scr[c] = lax.bitcast_convert_type(jnp.where(sel, 0.0, NEG).T, jnp.int32)
        return carry

    lax.fori_loop(0, nch, sel_body, 0)

    m_scr[...] = jnp.full(m_scr.shape, -jnp.inf, F32)
    l_scr[...] = jnp.zeros(l_scr.shape, F32)
    acc_scr[...] = jnp.zeros(acc_scr.shape, F32)

    def attend(c, near):
        start = pl.multiple_of(c * kc, kc)
        kv = ckv_ref[0, pl.ds(start, kc), :]
        s_scr[...] = lax.dot_general(ql_ref[0].reshape(rows, B_KV_RANK), kv, _NT,
                                     preferred_element_type=F32)

        def logits(hd, r0):
            sh = s_scr[hd * qb + r0:hd * qb + r0 + SUB_B]
            if near == "cur":
                t0 = bias_ref[hd, r0 % QB_A:r0 % QB_A + SUB_B, LANES:]
                t1 = bias_ref[hd, r0 % QB_A:r0 % QB_A + SUB_B, :LANES]
                sh = sh + jnp.concatenate([t0 if r0 < QB_A else t1, t0], axis=1)
            elif near == "prev" and r0 < QB_A:
                t1 = bias_ref[hd, r0:r0 + SUB_B, :LANES]
                sh = sh + jnp.concatenate([jnp.zeros_like(t1), t1], axis=1)
            return sh + lax.bitcast_convert_type(key_scr[c, r0:r0 + SUB_B, :], F32)

        for hd in range(B_HEADS):
            for r0 in range(0, qb, SUB_B):
                sl = slice(hd * qb + r0, hd * qb + r0 + SUB_B)
                m_old = m_scr[sl]
                m_new = jnp.maximum(m_old, jnp.max(logits(hd, r0), axis=-1, keepdims=True))
                m_scr[sl] = m_new
                alpha_scr[sl] = jnp.exp2(m_old - m_new)
        for hd in range(B_HEADS):
            for r0 in range(0, qb, SUB_B):
                sl = slice(hd * qb + r0, hd * qb + r0 + SUB_B)
                m_new = m_scr[sl]
                p = jnp.exp2(logits(hd, r0) - jnp.concatenate([m_new, m_new], axis=1))
                l_scr[sl] = alpha_scr[sl] * l_scr[sl] + (p[:, :LANES] + p[:, LANES:])
                p_scr[sl] = p.astype(BF16)
        pv = jnp.dot(p_scr[...], kv, preferred_element_type=F32)
        for r0 in range(0, rows, SUB_B):
            sl = slice(r0, r0 + SUB_B)
            alpha = alpha_scr[sl]
            acc_scr[sl] = acc_scr[sl] * jnp.concatenate([alpha, alpha], axis=1) + pv[sl]

    def far_body(c, carry):
        attend(c, None)
        return carry

    lax.fori_loop(0, jq - 1, far_body, 0)

    @pl.when(jq >= 1)
    def _():
        attend(jq - 1, "prev")

    attend(jq, "cur")

    for pair in range(B_HEADS // 2):
        y2 = jnp.zeros((qb, LANES), F32)
        for par in range(2):
            hd = 2 * pair + par
            sl = slice(hd * qb, (hd + 1) * qb)
            l = jnp.sum(l_scr[sl], axis=-1, keepdims=True)
            o_lat = acc_scr[sl] / l
            y2 = y2 + jnp.dot(o_lat.astype(BF16), wuv_ref[pair, par], preferred_element_type=F32)
        z = zb_ref[:, pair * LANES:(pair + 1) * LANES]
        o_ref[:, pair * LANES:(pair + 1) * LANES] = (y2 * _silu(z)).astype(o_ref.dtype)


def _mixer_b(ql, qi, wi, ckvn, klo, khi, proj, bias_b, wuv_pairs, n_batch, lp):
    nb = lp // QB_B
    rows = n_batch * lp
    whole = lambda b, j: (b, 0, 0)
    return pl.pallas_call(
        _mixer_b_kernel,
        grid=(n_batch, nb),
        in_specs=[
            pl.BlockSpec((1, B_HEADS, QB_B, B_KV_RANK), lambda b, j: (b, 0, j, 0)),
            pl.BlockSpec((1, IDX_HEADS // 2, QB_B, LANES), lambda b, j: (b, 0, j, 0)),
            pl.BlockSpec((1, QB_B, LANES), lambda b, j: (b, j, 0)),
            pl.BlockSpec((1, lp, B_KV_RANK), whole),
            pl.BlockSpec((1, lp, LANES), whole),
            pl.BlockSpec((1, lp, LANES), whole),
            pl.BlockSpec((QB_B, B_WIDTH), lambda b, j: (b * nb + j, C_ZB // B_WIDTH)),
            pl.BlockSpec((B_HEADS, QB_A, 2 * LANES), lambda b, j: (0, 0, 0)),
            pl.BlockSpec((B_HEADS // 2, 2, B_KV_RANK, LANES), lambda b, j: (0, 0, 0, 0)),
        ],
        out_specs=pl.BlockSpec((QB_B, B_WIDTH), lambda b, j: (b * nb + j, 0)),
        out_shape=jax.ShapeDtypeStruct((rows, B_WIDTH), BF16),
        scratch_shapes=[
            pltpu.VMEM((nb, QB_B, QB_B), jnp.int32),
            pltpu.VMEM((B_HEADS * QB_B, QB_B), F32),
            pltpu.VMEM((B_HEADS * QB_B, B_KV_RANK), F32),
            pltpu.VMEM((B_HEADS * QB_B, LANES), F32),
            pltpu.VMEM((B_HEADS * QB_B, LANES), F32),
            pltpu.VMEM((B_HEADS * QB_B, LANES), F32),
            pltpu.VMEM((B_HEADS * QB_B, QB_B), BF16),
        ],
        compiler_params=_cparams(2),
        name="mixer_b",
    )(ql, qi, wi, ckvn, klo, khi, proj, bias_b, wuv_pairs)


def _merge_kernel(ya_ref, yb_ref, ga_ref, gb_ref, h_ref, wpa_ref, wpb_ref, wo_ref, g_ref, *out_refs,
                  last):
    pa = jnp.dot(ya_ref[...], wpa_ref[...], preferred_element_type=F32)
    pb = jnp.dot(yb_ref[...], wpb_ref[...], preferred_element_type=F32)
    merged = _sigmoid(ga_ref[...]) * pa + _sigmoid(gb_ref[...]) * pb
    h_new = h_ref[...] + jnp.dot(merged.astype(BF16), wo_ref[...], preferred_element_type=F32)
    normed = _rms(h_new, g_ref[...])
    if last:
        out_refs[0][...] = normed
    else:
        out_refs[0][...] = h_new
        out_refs[1][...] = normed.astype(BF16)


def _merge(ya, yb, proj, h, wpa, wpb, wo, g_next, last, tm=256):
    rows, d = h.shape
    const = lambda i: (0, 0)
    once = pl.Buffered(1)
    if last:
        out_specs = [pl.BlockSpec((tm, d), lambda i: (i, 0))]
        out_shape = [jax.ShapeDtypeStruct((rows, d), F32)]
    else:
        out_specs = [pl.BlockSpec((tm, d), lambda i: (i, 0)), pl.BlockSpec((tm, d), lambda i: (i, 0))]
        out_shape = [jax.ShapeDtypeStruct((rows, d), F32), jax.ShapeDtypeStruct((rows, d), BF16)]
    return pl.pallas_call(
        functools.partial(_merge_kernel, last=last),
        grid=(rows // tm,),
        in_specs=[
            pl.BlockSpec((tm, A_WIDTH), lambda i: (i, 0)),
            pl.BlockSpec((tm, B_WIDTH), lambda i: (i, 0)),
            pl.BlockSpec((tm, d), lambda i: (i, C_GA // D_MODEL)),
            pl.BlockSpec((tm, d), lambda i: (i, C_GB // D_MODEL)),
            pl.BlockSpec((tm, d), lambda i: (i, 0)),
            pl.BlockSpec((A_WIDTH, d), const, pipeline_mode=once),
            pl.BlockSpec((B_WIDTH, d), const, pipeline_mode=once),
            pl.BlockSpec((d, d), const, pipeline_mode=once),
            pl.BlockSpec((1, d), const),
        ],
        out_specs=out_specs,
        out_shape=out_shape,
        compiler_params=_cparams(1),
        name="merge_out",
    )(ya, yb, proj, proj, h, wpa, wpb, wo, g_next.reshape(1, d))


def _pack_weights(w_in, w_uk, w_uv, w_qb, w_iq, w_proj_a, w_proj_b, w_out):
    depth = w_in.shape[0]
    sec = lambda a, b: w_in[:, :, a:b]
    qa, ka, va, za = sec(0, 1024), sec(1024, 1152), sec(1152, 1280), sec(1280, 2304)
    cq, ckv, zb = sec(2304, 2816), sec(2816, 3072), sec(3072, 4096)
    kidx, widx = sec(4096, 4160), sec(4160, 4168)
    ga, gb = sec(4168, 6216), sec(6216, 8264)
    w_main = jnp.concatenate([qa, za, zb, cq, ckv, ka, va, ga, gb], axis=-1).astype(BF16)
    z64 = jnp.zeros_like(kidx)
    zpad = jnp.zeros(widx.shape[:2] + (LANES - IDX_HEADS,), w_in.dtype)
    w_small = jnp.concatenate([kidx, z64, z64, kidx, widx, zpad], axis=-1).astype(BF16)
    uk = (w_uk * (HEAD_DIM ** -0.5 * LOG2E)).reshape(depth, B_HEADS // 2, 2, HEAD_DIM, B_KV_RANK)
    zuk = jnp.zeros_like(uk[:, :, 0])
    wuk_bd = jnp.concatenate([jnp.concatenate([uk[:, :, 0], zuk], axis=-1),
                              jnp.concatenate([zuk, uk[:, :, 1]], axis=-1)], axis=-2).astype(BF16)
    uv = w_uv.reshape(depth, B_HEADS // 2, 2, B_KV_RANK, HEAD_DIM)
    zuv = jnp.zeros_like(uv[:, :, 0])
    wuv_pairs = jnp.stack([jnp.concatenate([uv[:, :, 0], zuv], axis=-1),
                           jnp.concatenate([zuv, uv[:, :, 1]], axis=-1)], axis=2).astype(BF16)
    w_iq8 = (w_iq * (IDX_DIM ** -0.5)).astype(BF16)
    return (w_main, w_small, wuk_bd, wuv_pairs, w_qb.astype(BF16), w_iq8,
            w_proj_a.astype(BF16), w_proj_b.astype(BF16), w_out.astype(BF16))


def kernel(x, meta_tokens, bias_table, norm_g, w_in, q_norm_g, kv_norm_g, w_qb, w_iq, w_uk, w_uv,
           sinks, w_proj_a, w_proj_b, w_out, final_g):
    n_batch, seq, d = x.shape
    depth = w_in.shape[0]
    length = seq + N_META
    lp = -(-length // QB_B) * QB_B
    assert min(TOPK_MAX, seq // 4) == TOPK_MAX and d == D_MODEL

    (w_main, w_small, wuk_bd, wuv_pairs, wqb, wiq, wpa, wpb, wo) = _pack_weights(
        w_in, w_uk, w_uv, w_qb, w_iq, w_proj_a, w_proj_b, w_out)
    bias_a, bias_b = _bias_tiles(bias_table)

    meta = jnp.broadcast_to(meta_tokens.astype(x.dtype)[None], (n_batch, N_META, d))
    pad = jnp.zeros((n_batch, lp - length, d), x.dtype)
    h = jnp.concatenate([meta, x, pad], axis=1).reshape(n_batch * lp, d)

    u = _rmsnorm(h, norm_g[0])
    out = None
    for l in range(depth):
        proj = _matmul(u, w_main[l])
        ya = _mixer_a(proj, sinks[l], bias_a, n_batch, lp)
        ql, qi, wi, ckvn, klo, khi = _bprep(
            proj, u, q_norm_g[l].reshape(1, -1), kv_norm_g[l].reshape(1, -1),
            wqb[l], wiq[l], wuk_bd[l], w_small[l], n_batch, lp)
        yb = _mixer_b(ql, qi, wi, ckvn, klo, khi, proj, bias_b, wuv_pairs[l], n_batch, lp)
        last = l == depth - 1
        g_next = final_g if last else norm_g[l + 1]
        res = _merge(ya, yb, proj, h, wpa[l], wpb[l], wo[l], g_next, last)
        if last:
            out = res[0]
        else:
            h, u = res
    return out.reshape(n_batch, lp, d)[:, N_META:length]
```

```python
import functools
import math

import numpy as np
import jax
import jax.numpy as jnp
from jax import lax
from jax.experimental import pallas as pl
from jax.experimental.pallas import tpu as pltpu

D_MODEL = 2048
N_META = 16
WINDOW = 128
HEAD_DIM = 64
A_HEADS = 16
A_WIDTH = 1024
B_HEADS = 16
B_WIDTH = 1024
B_Q_RANK = 512
B_KV_RANK = 256
IDX_HEADS = 8
IDX_DIM = 64
TOPK_MAX = 256
N_BUCKETS = 32
MAX_DISTANCE = 128
EPS = 1e-6
NEG = -1e30

LANES = 128
QB_A = 128
QB_B = 256
SUB_B = 32
FAR_NC = 1
BIS_UNROLL = 3
LOG2E = math.log2(math.e)
VMEM_LIMIT = 56 * 1024 * 1024
INT_MIN = -(2 ** 31)
INT_MAX = 2 ** 31 - 1

C_QA, C_ZA, C_ZB, C_CQ, C_CKV, C_KA, C_VA, C_GA, C_GB = (
    0, 1024, 2048, 3072, 3584, 3840, 3968, 4096, 6144)
N_MAIN = 8192

F32 = jnp.float32
BF16 = jnp.bfloat16
_NT = (((1,), (1,)), ((), ()))


def _cparams(n_grid):
    return pltpu.CompilerParams(
        dimension_semantics=("arbitrary",) * n_grid,
        vmem_limit_bytes=VMEM_LIMIT)


def _t5_bucket_np(d):
    max_exact = N_BUCKETS // 2
    nf = np.maximum(d, 1).astype(np.float32)
    large = max_exact + (np.log(nf / np.float32(max_exact)) / np.float32(math.log(MAX_DISTANCE / max_exact))
                         * np.float32(N_BUCKETS - max_exact)).astype(np.int32)
    large = np.minimum(large, N_BUCKETS - 1)
    return np.where(d < max_exact, d, large).astype(np.int32)


def _bias_tiles_kernel(tab_ref, idx_ref, out_a_ref, out_b_ref):
    h = pl.program_id(0)
    idx = idx_ref[...]
    acc_a = jnp.zeros(idx.shape, F32)
    acc_b = jnp.zeros(idx.shape, F32)
    far_b = tab_ref[N_BUCKETS - 1, A_HEADS + h]
    for b in range(N_BUCKETS):
        hit = idx == b
        acc_a = jnp.where(hit, tab_ref[b, h], acc_a)
        acc_b = jnp.where(hit, (tab_ref[b, A_HEADS + h] - far_b) * LOG2E, acc_b)
    out_a_ref[0] = acc_a
    out_b_ref[0] = acc_b[:, :2 * LANES]


def _bias_tiles(bias_table):
    r = np.arange(QB_A)[:, None]
    k = np.arange(QB_A)[None, :]
    prev = _t5_bucket_np(np.maximum(QB_A + r - k, 0))
    cur = _t5_bucket_np(np.maximum(r - k, 0))
    far = np.full((QB_A, QB_A), N_BUCKETS - 1, np.int32)
    idx = jnp.asarray(np.concatenate([prev, cur, far], axis=1))
    return pl.pallas_call(
        _bias_tiles_kernel,
        grid=(A_HEADS,),
        in_specs=[pl.BlockSpec(memory_space=pltpu.SMEM),
                  pl.BlockSpec((QB_A, 3 * LANES), lambda h: (0, 0))],
        out_specs=[pl.BlockSpec((1, QB_A, 3 * LANES), lambda h: (h, 0, 0)),
                   pl.BlockSpec((1, QB_A, 2 * LANES), lambda h: (h, 0, 0))],
        out_shape=[jax.ShapeDtypeStruct((A_HEADS, QB_A, 3 * LANES), F32),
                   jax.ShapeDtypeStruct((B_HEADS, QB_A, 2 * LANES), F32)],
        compiler_params=_cparams(1),
        name="bias_tiles",
    )(bias_table, idx)


def _rms(x, g):
    return x * lax.rsqrt(jnp.mean(x * x, axis=-1, keepdims=True) + EPS) * g


def _rmsnorm_kernel(h_ref, g_ref, u_ref):
    u_ref[...] = _rms(h_ref[...], g_ref[...]).astype(u_ref.dtype)


def _rmsnorm(h, g, tm=512):
    rows, d = h.shape
    return pl.pallas_call(
        _rmsnorm_kernel,
        grid=(rows // tm,),
        in_specs=[pl.BlockSpec((tm, d), lambda i: (i, 0)),
                  pl.BlockSpec((1, d), lambda i: (0, 0))],
        out_specs=pl.BlockSpec((tm, d), lambda i: (i, 0)),
        out_shape=jax.ShapeDtypeStruct((rows, d), BF16),
        compiler_params=_cparams(1),
        name="rmsnorm",
    )(h, g.reshape(1, d))


def _matmul_kernel(x_ref, w_ref, o_ref):
    o_ref[...] = jnp.dot(x_ref[...], w_ref[...], preferred_element_type=F32).astype(o_ref.dtype)


def _matmul(x, w, tm=512, tn=1024):
    rows, kdim = x.shape
    n = w.shape[1]
    return pl.pallas_call(
        _matmul_kernel,
        grid=(n // tn, rows // tm),
        in_specs=[pl.BlockSpec((tm, kdim), lambda j, i: (i, 0)),
                  pl.BlockSpec((kdim, tn), lambda j, i: (0, j))],
        out_specs=pl.BlockSpec((tm, tn), lambda j, i: (i, j)),
        out_shape=jax.ShapeDtypeStruct((rows, n), F32),
        compiler_params=_cparams(2),
        name="in_proj",
    )(x, w)


def _silu(z):
    return z / (1.0 + jnp.exp(-z))


def _sigmoid(z):
    return 1.0 / (1.0 + jnp.exp(-z))


def _mixer_a_kernel(sink_ref, q_ref, kc_ref, vc_ref, kp_ref, vp_ref, km_ref, vm_ref, za_ref,
                    bias_ref, o_ref):
    j = pl.program_id(1)
    nk = 3 * QB_A
    q = (q_ref[...] * (HEAD_DIM ** -0.5)).astype(BF16)
    kcat = jnp.concatenate([kp_ref[...], kc_ref[...], km_ref[...]], axis=0)
    vcat = jnp.concatenate([vp_ref[...], vc_ref[...], vm_ref[...]], axis=0)
    lane = lax.broadcasted_iota(jnp.int32, (nk, LANES), 1)
    lo = lane < HEAD_DIM
    kswap = pltpu.roll(kcat, HEAD_DIM, 1)
    vswap = pltpu.roll(vcat, HEAD_DIM, 1)
    zero = jnp.zeros_like(kcat)
    k_half = ((jnp.where(lo, kcat, zero).astype(BF16), jnp.where(lo, zero, kswap).astype(BF16)),
              (jnp.where(lo, kswap, zero).astype(BF16), jnp.where(lo, zero, kcat).astype(BF16)))
    v_half = ((jnp.where(lo, vcat, zero).astype(BF16), jnp.where(lo, zero, vswap).astype(BF16)),
              (jnp.where(lo, vswap, zero).astype(BF16), jnp.where(lo, zero, vcat).astype(BF16)))

    r = lax.broadcasted_iota(jnp.int32, (QB_A, nk), 0)
    c = lax.broadcasted_iota(jnp.int32, (QB_A, nk), 1)
    jv = jnp.zeros((QB_A, nk), jnp.int32) + j
    in_prev = (c < QB_A) & (jv >= 1) & ((c > r) | ((jv == 1) & (c < N_META)))
    in_cur = (c >= QB_A) & (c < 2 * QB_A) & (c - QB_A <= r)
    in_meta = (c >= 2 * QB_A) & (jv >= 2) & (c - 2 * QB_A < N_META)
    mask = in_prev | in_cur | in_meta

    for pair in range(A_HEADS // 2):
        g = pair // (A_HEADS // 4)
        q2 = q[:, pair * LANES:(pair + 1) * LANES]
        o2 = jnp.zeros((QB_A, LANES), F32)
        for par in range(2):
            hd = 2 * pair + par
            s = lax.dot_general(q2, k_half[g][par], _NT, preferred_element_type=F32)
            logits = jnp.where(mask, s + bias_ref[hd], NEG)
            sink = sink_ref[hd]
            m = jnp.maximum(jnp.max(logits, axis=-1, keepdims=True), sink)
            p = jnp.exp(logits - m)
            denom = jnp.sum(p, axis=-1, keepdims=True) + jnp.exp(sink - m)
            pv = jnp.dot(p.astype(BF16), v_half[g][par], preferred_element_type=F32)
            o2 = o2 + pv / denom
        z = za_ref[:, pair * LANES:(pair + 1) * LANES]
        o_ref[:, pair * LANES:(pair + 1) * LANES] = (o2 * _silu(z)).astype(o_ref.dtype)


def _mixer_a(proj, sinks, bias_a, n_batch, lp):
    nb = lp // QB_A
    rows = n_batch * lp
    ck, cv = C_KA // LANES, C_VA // LANES
    cur = lambda b, j: b * nb + j
    prev = lambda b, j: b * nb + jnp.maximum(j - 1, 0)
    first = lambda b, j: b * nb
    return pl.pallas_call(
        _mixer_a_kernel,
        grid=(n_batch, nb),
        in_specs=[
            pl.BlockSpec(memory_space=pltpu.SMEM),
            pl.BlockSpec((QB_A, A_WIDTH), lambda b, j: (cur(b, j), C_QA // A_WIDTH)),
            pl.BlockSpec((QB_A, LANES), lambda b, j: (cur(b, j), ck)),
            pl.BlockSpec((QB_A, LANES), lambda b, j: (cur(b, j), cv)),
            pl.BlockSpec((QB_A, LANES), lambda b, j: (prev(b, j), ck)),
            pl.BlockSpec((QB_A, LANES), lambda b, j: (prev(b, j), cv)),
            pl.BlockSpec((QB_A, LANES), lambda b, j: (first(b, j), ck)),
            pl.BlockSpec((QB_A, LANES), lambda b, j: (first(b, j), cv)),
            pl.BlockSpec((QB_A, A_WIDTH), lambda b, j: (cur(b, j), C_ZA // A_WIDTH)),
            pl.BlockSpec((A_HEADS, QB_A, 3 * LANES), lambda b, j: (0, 0, 0)),
        ],
        out_specs=pl.BlockSpec((QB_A, A_WIDTH), lambda b, j: (cur(b, j), 0)),
        out_shape=jax.ShapeDtypeStruct((rows, A_WIDTH), BF16),
        compiler_params=_cparams(2),
        name="mixer_a",
    )(sinks, proj, proj, proj, proj, proj, proj, proj, proj, bias_a)


def _bprep_kernel(cq_ref, ckv_ref, u_ref, qg_ref, kvg_ref, wqb_ref, wiq_ref, wuk_ref, wsm_ref,
                  ql_ref, qi_ref, wi_ref, ckvn_ref, klo_ref, khi_ref):
    cqn = _rms(cq_ref[...], qg_ref[...]).astype(BF16)
    ckvn_ref[0] = _rms(ckv_ref[...], kvg_ref[...]).astype(ckvn_ref.dtype)
    qb = jnp.dot(cqn, wqb_ref[...], preferred_element_type=F32).astype(BF16)
    for pair in range(B_HEADS // 2):
        ql2 = jnp.dot(qb[:, pair * LANES:(pair + 1) * LANES], wuk_ref[pair],
                      preferred_element_type=F32)
        ql_ref[0, 2 * pair] = ql2[:, :B_KV_RANK].astype(ql_ref.dtype)
        ql_ref[0, 2 * pair + 1] = ql2[:, B_KV_RANK:].astype(ql_ref.dtype)
    qi = jnp.dot(cqn, wiq_ref[...], preferred_element_type=F32)
    for pair in range(IDX_HEADS // 2):
        qi_ref[0, pair] = qi[:, pair * LANES:(pair + 1) * LANES].astype(qi_ref.dtype)
    small = jnp.dot(u_ref[...], wsm_ref[...], preferred_element_type=F32)
    klo_ref[0] = small[:, :LANES].astype(klo_ref.dtype)
    khi_ref[0] = small[:, LANES:2 * LANES].astype(khi_ref.dtype)
    wi_ref[0] = small[:, 2 * LANES:] * (IDX_HEADS ** -0.5)


def _bprep(proj, u, qg, kvg, wqb, wiq, wuk_bd, wsmall, n_batch, lp, tm=256):
    nt = lp // tm
    row = lambda b, i: b * nt + i
    const2 = lambda b, i: (0, 0)
    return pl.pallas_call(
        _bprep_kernel,
        grid=(n_batch, nt),
        in_specs=[
            pl.BlockSpec((tm, B_Q_RANK), lambda b, i: (row(b, i), C_CQ // B_Q_RANK)),
            pl.BlockSpec((tm, B_KV_RANK), lambda b, i: (row(b, i), C_CKV // B_KV_RANK)),
            pl.BlockSpec((tm, D_MODEL), lambda b, i: (row(b, i), 0)),
            pl.BlockSpec((1, B_Q_RANK), const2),
            pl.BlockSpec((1, B_KV_RANK), const2),
            pl.BlockSpec((B_Q_RANK, B_WIDTH), const2),
            pl.BlockSpec((B_Q_RANK, IDX_HEADS * IDX_DIM), const2),
            pl.BlockSpec((B_HEADS // 2, LANES, 2 * B_KV_RANK), lambda b, i: (0, 0, 0)),
            pl.BlockSpec((D_MODEL, 3 * LANES), const2),
        ],
        out_specs=[
            pl.BlockSpec((1, B_HEADS, tm, B_KV_RANK), lambda b, i: (b, 0, i, 0)),
            pl.BlockSpec((1, IDX_HEADS // 2, tm, LANES), lambda b, i: (b, 0, i, 0)),
            pl.BlockSpec((1, tm, LANES), lambda b, i: (b, i, 0)),
            pl.BlockSpec((1, tm, B_KV_RANK), lambda b, i: (b, i, 0)),
            pl.BlockSpec((1, tm, LANES), lambda b, i: (b, i, 0)),
            pl.BlockSpec((1, tm, LANES), lambda b, i: (b, i, 0)),
        ],
        out_shape=[
            jax.ShapeDtypeStruct((n_batch, B_HEADS, lp, B_KV_RANK), BF16),
            jax.ShapeDtypeStruct((n_batch, IDX_HEADS // 2, lp, LANES), BF16),
            jax.ShapeDtypeStruct((n_batch, lp, LANES), F32),
            jax.ShapeDtypeStruct((n_batch, lp, B_KV_RANK), BF16),
            jax.ShapeDtypeStruct((n_batch, lp, LANES), BF16),
            jax.ShapeDtypeStruct((n_batch, lp, LANES), BF16),
        ],
        compiler_params=_cparams(2),
        name="bprep",
    )(proj, proj, u, qg, kvg, wqb, wiq, wuk_bd, wsmall)


def _mixer_b_kernel(ql_ref, qi_ref, wi_ref, ckv_ref, klo_ref, khi_ref, zb_ref, bias_ref, wuv_ref,
                    o_ref, key_scr, s_scr, acc_scr, m_scr, l_scr, alpha_scr, p_scr):
    jq = pl.program_id(1)
    nch = jq + 1
    qb, kc = QB_B, QB_B
    rows = B_HEADS * qb
    kf = float(TOPK_MAX)

    qi = qi_ref[0].reshape(IDX_HEADS // 2 * qb, LANES)
    wit = wi_ref[0].T
    krow = lax.broadcasted_iota(jnp.int32, (qb, kc), 0)
    krow_sub = lax.broadcasted_iota(jnp.int32, (SUB_B, kc), 0)
    qpos = jq * qb + lax.broadcasted_iota(jnp.int32, (qb, kc), 1)

    def idx_body(c, carry):
        start = pl.multiple_of(c * kc, kc)
        sc = (lax.dot_general(klo_ref[0, pl.ds(start, kc), :], qi, _NT, preferred_element_type=F32),
              lax.dot_general(khi_ref[0, pl.ds(start, kc), :], qi, _NT, preferred_element_type=F32))
        isc = jnp.zeros((kc, qb), F32)
        for pair in range(IDX_HEADS // 2):
            for par in range(2):
                hd = 2 * pair + par
                isc = isc + wit[hd:hd + 1, :] * jnp.maximum(sc[par][:, pair * qb:(pair + 1) * qb], 0.0)
        bits = lax.bitcast_convert_type(isc, jnp.int32)
        key = bits ^ ((bits >> 31) & INT_MAX)
        key_scr[c] = jnp.where(c * kc + krow <= qpos, key, INT_MIN)
        return carry

    lax.fori_loop(0, nch, idx_body, 0)

    def reduce_chunks(tile_fn, init, combine):
        def body(c, acc):
            for r0 in range(0, kc, SUB_B):
                t = tile_fn(c, r0, key_scr[c, r0:r0 + SUB_B, :])
                t = t.reshape(SUB_B // 8, 8, qb)
                for i in range(SUB_B // 8):
                    acc = combine(acc, t[i])
            return acc
        return lax.fori_loop(0, nch, body, init)

    def count(pred_fn):
        acc = reduce_chunks(lambda c, r0, k: jnp.where(pred_fn(c, r0, k), 1.0, 0.0),
                            jnp.zeros((8, qb), F32), lambda a, b: a + b)
        return jnp.sum(acc, axis=0, keepdims=True)

    def unkey(k):
        return lax.bitcast_convert_type(k ^ ((k >> 31) & INT_MAX), F32)

    def tokey(v):
        b = lax.bitcast_convert_type(v, jnp.int32)
        return b ^ ((b >> 31) & INT_MAX)

    def select_all():
        return jnp.full((1, qb), INT_MIN, jnp.int32), jnp.full((1, qb), -1, jnp.int32)

    def select_topk():
        kmax = reduce_chunks(lambda c, r0, k: k, jnp.full((8, qb), INT_MIN, jnp.int32), jnp.maximum)
        kmax = jnp.max(kmax, axis=0, keepdims=True)
        kmin = reduce_chunks(lambda c, r0, k: jnp.where(k == INT_MIN, INT_MAX, k),
                             jnp.full((8, qb), INT_MAX, jnp.int32), jnp.minimum)
        kmin = jnp.min(kmin, axis=0, keepdims=True)
        c0 = count(lambda c, r0, k: k >= 0)
        c0p = count(lambda c, r0, k: k >= 1)
        n_valid = (jq * qb + 1 + lax.broadcasted_iota(jnp.int32, (1, qb), 1)).astype(F32)
        pos = c0p >= kf
        zero = (c0 >= kf) & (c0p < kf)
        lo0 = jnp.where(pos, 1, jnp.where(zero, 0, kmin))
        hi0 = jnp.where(pos, kmax + 1, jnp.where(zero, 1, 0))
        c_lo0 = jnp.where(pos, c0p, jnp.where(zero, c0, n_valid))
        c_hi0 = jnp.where(pos, 0.0, jnp.where(zero, c0p, c0))

        def bis_cond(st):
            return (st[0] < 64) & (st[5] > 0)

        def bis_step(it, lo, hi, c_lo, c_hi):
            mid_i = (lo >> 1) + (hi >> 1) + (lo & hi & 1)
            mid_v = tokey(0.5 * unkey(lo) + 0.5 * unkey(hi))
            use_v = (it < 24) & (mid_v > lo) & (mid_v < hi)
            mid = jnp.where(use_v, mid_v, mid_i)
            cnt = count(lambda c, r0, k: k >= mid)
            ge = cnt >= kf
            return (jnp.where(ge, mid, lo), jnp.where(ge, hi, mid),
                    jnp.where(ge, cnt, c_lo), jnp.where(ge, c_hi, cnt))

        def bis_body(st):
            it, lo, hi, c_lo, c_hi, _ = st
            for _ in range(BIS_UNROLL):
                lo, hi, c_lo, c_hi = bis_step(it, lo, hi, c_lo, c_hi)
                it = it + 1
            active = (lo + 1 < hi) & (c_lo != kf)
            return it, lo, hi, c_lo, c_hi, jnp.max(jnp.where(active, 1, 0))

        active0 = (lo0 + 1 < hi0) & (c_lo0 != kf)
        st0 = (jnp.int32(0), lo0, hi0, c_lo0, c_hi0, jnp.max(jnp.where(active0, 1, 0)))
        _, lo, hi, c_lo, c_hi, _ = lax.while_loop(bis_cond, bis_body, st0)
        tie = c_lo != kf

        def tie_break():
            need = kf - c_hi

            def tie_body(_, st):
                jlo, jhi = st
                mid = (jlo + jhi) >> 1
                cnt = count(lambda c, r0, k: (k == lo) & (c * kc + r0 + krow_sub <= mid))
                ge = cnt >= need
                return jnp.where(ge, jlo, mid), jnp.where(ge, mid, jhi)

            _, jhi = lax.fori_loop(0, 13, tie_body, (jnp.full((1, qb), -1, jnp.int32),
                                                     jnp.full((1, qb), 8191, jnp.int32)))
            return jnp.where(tie, jhi, INT_MAX)

        jmax = lax.cond(jnp.max(jnp.where(tie, 1, 0)) > 0, tie_break,
                        lambda: jnp.full((1, qb), INT_MAX, jnp.int32))
        return lo, jmax

    thr, jmax = lax.cond(jq >= 1, select_topk, select_all)

    def sel_body(c, carry):
        k = key_scr[c]
        sel = (k > thr) | ((k == thr) & (c * kc + krow <= jmax))
        key_scr[c] = lax.bitcast_convert_type(jnp.where(sel, 0.0, NEG).T, jnp.int32)
        return carry

    lax.fori_loop(0, nch, sel_body, 0)

    m_scr[...] = jnp.full(m_scr.shape, -jnp.inf, F32)
    l_scr[...] = jnp.zeros(l_scr.shape, F32)
    acc_scr[...] = jnp.zeros(acc_scr.shape, F32)

    def attend(c, nc, near):
        kw = nc * kc
        tiles = kw // LANES
        start = pl.multiple_of(c * kc, kc)
        kv = ckv_ref[0, pl.ds(start, kw), :]
        s_scr[:, :kw] = lax.dot_general(ql_ref[0].reshape(rows, B_KV_RANK), kv, _NT,
                                        preferred_element_type=F32)

        def logits(hd, r0):
            sh = s_scr[hd * qb + r0:hd * qb + r0 + SUB_B, :kw]
            if near == "cur":
                t0 = bias_ref[hd, r0 % QB_A:r0 % QB_A + SUB_B, LANES:]
                t1 = bias_ref[hd, r0 % QB_A:r0 % QB_A + SUB_B, :LANES]
                sh = sh + jnp.concatenate([t0 if r0 < QB_A else t1, t0], axis=1)
            elif near == "prev" and r0 < QB_A:
                t1 = bias_ref[hd, r0:r0 + SUB_B, :LANES]
                sh = sh + jnp.concatenate([jnp.zeros_like(t1), t1], axis=1)
            mask = [lax.bitcast_convert_type(key_scr[c + i, r0:r0 + SUB_B, :], F32) for i in range(nc)]
            return sh + (mask[0] if nc == 1 else jnp.concatenate(mask, axis=1))

        for hd in range(B_HEADS):
            for r0 in range(0, qb, SUB_B):
                sl = slice(hd * qb + r0, hd * qb + r0 + SUB_B)
                m_old = m_scr[sl]
                m_new = jnp.maximum(m_old, jnp.max(logits(hd, r0), axis=-1, keepdims=True))
                m_scr[sl] = m_new
                alpha_scr[sl] = jnp.exp2(m_old - m_new)
        for hd in range(B_HEADS):
            for r0 in range(0, qb, SUB_B):
                sl = slice(hd * qb + r0, hd * qb + r0 + SUB_B)
                m_new = m_scr[sl]
                p = jnp.exp2(logits(hd, r0) - jnp.concatenate([m_new] * tiles, axis=1))
                psum = p[:, :LANES]
                for t in range(1, tiles):
                    psum = psum + p[:, t * LANES:(t + 1) * LANES]
                l_scr[sl] = alpha_scr[sl] * l_scr[sl] + psum
                p_scr[sl, :kw] = p.astype(BF16)
        pv = jnp.dot(p_scr[:, :kw], kv, preferred_element_type=F32)
        for r0 in range(0, rows, SUB_B):
            sl = slice(r0, r0 + SUB_B)
            alpha = alpha_scr[sl]
            acc_scr[sl] = acc_scr[sl] * jnp.concatenate([alpha, alpha], axis=1) + pv[sl]

    n_far = jq - 1

    def far_body(i, carry):
        attend(i * FAR_NC, FAR_NC, None)
        return carry

    lax.fori_loop(0, n_far // FAR_NC, far_body, 0)

    if FAR_NC == 2:
        @pl.when((n_far >= 1) & (n_far % FAR_NC == 1))
        def _():
            attend(jq - 2, 1, None)

    @pl.when(jq >= 1)
    def _():
        attend(jq - 1, 1, "prev")

    attend(jq, 1, "cur")

    for pair in range(B_HEADS // 2):
        y2 = jnp.zeros((qb, LANES), F32)
        for par in range(2):
            hd = 2 * pair + par
            sl = slice(hd * qb, (hd + 1) * qb)
            l = jnp.sum(l_scr[sl], axis=-1, keepdims=True)
            o_lat = acc_scr[sl] / l
            y2 = y2 + jnp.dot(o_lat.astype(BF16), wuv_ref[pair, par], preferred_element_type=F32)
        z = zb_ref[:, pair * LANES:(pair + 1) * LANES]
        o_ref[:, pair * LANES:(pair + 1) * LANES] = (y2 * _silu(z)).astype(o_ref.dtype)


def _mixer_b(ql, qi, wi, ckvn, klo, khi, proj, bias_b, wuv_pairs, n_batch, lp):
    nb = lp // QB_B
    rows = n_batch * lp
    whole = lambda b, j: (b, 0, 0)
    once = pl.Buffered(1)
    return pl.pallas_call(
        _mixer_b_kernel,
        grid=(n_batch, nb),
        in_specs=[
            pl.BlockSpec((1, B_HEADS, QB_B, B_KV_RANK), lambda b, j: (b, 0, j, 0)),
            pl.BlockSpec((1, IDX_HEADS // 2, QB_B, LANES), lambda b, j: (b, 0, j, 0)),
            pl.BlockSpec((1, QB_B, LANES), lambda b, j: (b, j, 0)),
            pl.BlockSpec((1, lp, B_KV_RANK), whole, pipeline_mode=once),
            pl.BlockSpec((1, lp, LANES), whole, pipeline_mode=once),
            pl.BlockSpec((1, lp, LANES), whole, pipeline_mode=once),
            pl.BlockSpec((QB_B, B_WIDTH), lambda b, j: (b * nb + j, C_ZB // B_WIDTH)),
            pl.BlockSpec((B_HEADS, QB_A, 2 * LANES), lambda b, j: (0, 0, 0), pipeline_mode=once),
            pl.BlockSpec((B_HEADS // 2, 2, B_KV_RANK, LANES), lambda b, j: (0, 0, 0, 0),
                         pipeline_mode=once),
        ],
        out_specs=pl.BlockSpec((QB_B, B_WIDTH), lambda b, j: (b * nb + j, 0)),
        out_shape=jax.ShapeDtypeStruct((rows, B_WIDTH), BF16),
        scratch_shapes=[
            pltpu.VMEM((nb, QB_B, QB_B), jnp.int32),
            pltpu.VMEM((B_HEADS * QB_B, FAR_NC * QB_B), F32),
            pltpu.VMEM((B_HEADS * QB_B, B_KV_RANK), F32),
            pltpu.VMEM((B_HEADS * QB_B, LANES), F32),
            pltpu.VMEM((B_HEADS * QB_B, LANES), F32),
            pltpu.VMEM((B_HEADS * QB_B, LANES), F32),
            pltpu.VMEM((B_HEADS * QB_B, FAR_NC * QB_B), BF16),
        ],
        compiler_params=_cparams(2),
        name="mixer_b",
    )(ql, qi, wi, ckvn, klo, khi, proj, bias_b, wuv_pairs)


def _merge_kernel(ya_ref, yb_ref, ga_ref, gb_ref, h_ref, wpa_ref, wpb_ref, wo_ref, g_ref, *out_refs,
                  last):
    pa = jnp.dot(ya_ref[...], wpa_ref[...], preferred_element_type=F32)
    pb = jnp.dot(yb_ref[...], wpb_ref[...], preferred_element_type=F32)
    merged = _sigmoid(ga_ref[...]) * pa + _sigmoid(gb_ref[...]) * pb
    h_new = h_ref[...] + jnp.dot(merged.astype(BF16), wo_ref[...], preferred_element_type=F32)
    normed = _rms(h_new, g_ref[...])
    if last:
        out_refs[0][...] = normed
    else:
        out_refs[0][...] = h_new
        out_refs[1][...] = normed.astype(BF16)


def _merge(ya, yb, proj, h, wpa, wpb, wo, g_next, last, tm=256):
    rows, d = h.shape
    const = lambda i: (0, 0)
    once = pl.Buffered(1)
    if last:
        out_specs = [pl.BlockSpec((tm, d), lambda i: (i, 0))]
        out_shape = [jax.ShapeDtypeStruct((rows, d), F32)]
    else:
        out_specs = [pl.BlockSpec((tm, d), lambda i: (i, 0)), pl.BlockSpec((tm, d), lambda i: (i, 0))]
        out_shape = [jax.ShapeDtypeStruct((rows, d), F32), jax.ShapeDtypeStruct((rows, d), BF16)]
    return pl.pallas_call(
        functools.partial(_merge_kernel, last=last),
        grid=(rows // tm,),
        in_specs=[
            pl.BlockSpec((tm, A_WIDTH), lambda i: (i, 0)),
            pl.BlockSpec((tm, B_WIDTH), lambda i: (i, 0)),
            pl.BlockSpec((tm, d), lambda i: (i, C_GA // D_MODEL)),
            pl.BlockSpec((tm, d), lambda i: (i, C_GB // D_MODEL)),
            pl.BlockSpec((tm, d), lambda i: (i, 0)),
            pl.BlockSpec((A_WIDTH, d), const, pipeline_mode=once),
            pl.BlockSpec((B_WIDTH, d), const, pipeline_mode=once),
            pl.BlockSpec((d, d), const, pipeline_mode=once),
            pl.BlockSpec((1, d), const),
        ],
        out_specs=out_specs,
        out_shape=out_shape,
        compiler_params=_cparams(1),
        name="merge_out",
    )(ya, yb, proj, proj, h, wpa, wpb, wo, g_next.reshape(1, d))


def _pack_weights(w_in, w_uk, w_uv, w_qb, w_iq, w_proj_a, w_proj_b, w_out):
    depth = w_in.shape[0]
    sec = lambda a, b: w_in[:, :, a:b]
    qa, ka, va, za = sec(0, 1024), sec(1024, 1152), sec(1152, 1280), sec(1280, 2304)
    cq, ckv, zb = sec(2304, 2816), sec(2816, 3072), sec(3072, 4096)
    kidx, widx = sec(4096, 4160), sec(4160, 4168)
    ga, gb = sec(4168, 6216), sec(6216, 8264)
    w_main = jnp.concatenate([qa, za, zb, cq, ckv, ka, va, ga, gb], axis=-1).astype(BF16)
    z64 = jnp.zeros_like(kidx)
    zpad = jnp.zeros(widx.shape[:2] + (LANES - IDX_HEADS,), w_in.dtype)
    w_small = jnp.concatenate([kidx, z64, z64, kidx, widx, zpad], axis=-1).astype(BF16)
    uk = (w_uk * (HEAD_DIM ** -0.5 * LOG2E)).reshape(depth, B_HEADS // 2, 2, HEAD_DIM, B_KV_RANK)
    zuk = jnp.zeros_like(uk[:, :, 0])
    wuk_bd = jnp.concatenate([jnp.concatenate([uk[:, :, 0], zuk], axis=-1),
                              jnp.concatenate([zuk, uk[:, :, 1]], axis=-1)], axis=-2).astype(BF16)
    uv = w_uv.reshape(depth, B_HEADS // 2, 2, B_KV_RANK, HEAD_DIM)
    zuv = jnp.zeros_like(uv[:, :, 0])
    wuv_pairs = jnp.stack([jnp.concatenate([uv[:, :, 0], zuv], axis=-1),
                           jnp.concatenate([zuv, uv[:, :, 1]], axis=-1)], axis=2).astype(BF16)
    w_iq8 = (w_iq * (IDX_DIM ** -0.5)).astype(BF16)
    return (w_main, w_small, wuk_bd, wuv_pairs, w_qb.astype(BF16), w_iq8,
            w_proj_a.astype(BF16), w_proj_b.astype(BF16), w_out.astype(BF16))


def kernel(x, meta_tokens, bias_table, norm_g, w_in, q_norm_g, kv_norm_g, w_qb, w_iq, w_uk, w_uv,
           sinks, w_proj_a, w_proj_b, w_out, final_g):
    n_batch, seq, d = x.shape
    depth = w_in.shape[0]
    length = seq + N_META
    lp = -(-length // QB_B) * QB_B
    assert min(TOPK_MAX, seq // 4) == TOPK_MAX and d == D_MODEL

    (w_main, w_small, wuk_bd, wuv_pairs, wqb, wiq, wpa, wpb, wo) = _pack_weights(
        w_in, w_uk, w_uv, w_qb, w_iq, w_proj_a, w_proj_b, w_out)
    bias_a, bias_b = _bias_tiles(bias_table)

    meta = jnp.broadcast_to(meta_tokens.astype(x.dtype)[None], (n_batch, N_META, d))
    pad = jnp.zeros((n_batch, lp - length, d), x.dtype)
    h = jnp.concatenate([meta, x, pad], axis=1).reshape(n_batch * lp, d)

    u = _rmsnorm(h, norm_g[0])
    out = None
    for l in range(depth):
        proj = _matmul(u, w_main[l])
        ya = _mixer_a(proj, sinks[l], bias_a, n_batch, lp)
        ql, qi, wi, ckvn, klo, khi = _bprep(
            proj, u, q_norm_g[l].reshape(1, -1), kv_norm_g[l].reshape(1, -1),
            wqb[l], wiq[l], wuk_bd[l], w_small[l], n_batch, lp)
        yb = _mixer_b(ql, qi, wi, ckvn, klo, khi, proj, bias_b, wuv_pairs[l], n_batch, lp)
        last = l == depth - 1
        g_next = final_g if last else norm_g[l + 1]
        res = _merge(ya, yb, proj, h, wpa[l], wpb[l], wo[l], g_next, last)
        if last:
            out = res[0]
        else:
            h, u = res
    return out.reshape(n_batch, lp, d)[:, N_META:length]
```

```python
import functools
import math

import numpy as np
import jax
import jax.numpy as jnp
from jax import lax
from jax.experimental import pallas as pl
from jax.experimental.pallas import tpu as pltpu

D_MODEL = 2048
N_META = 16
WINDOW = 128
HEAD_DIM = 64
A_HEADS = 16
A_WIDTH = 1024
B_HEADS = 16
B_WIDTH = 1024
B_Q_RANK = 512
B_KV_RANK = 256
IDX_HEADS = 8
IDX_DIM = 64
TOPK_MAX = 256
N_BUCKETS = 32
MAX_DISTANCE = 128
EPS = 1e-6
NEG = -1e30

LANES = 128
QB_A = 128
QB_B = 256
SUB_A = 32
SUB_B = 32
FAR_NC = 1
BIS_UNROLL = 3
LOG2E = math.log2(math.e)
VMEM_LIMIT = 56 * 1024 * 1024
INT_MIN = -(2 ** 31)
INT_MAX = 2 ** 31 - 1

C_QA, C_ZA, C_ZB, C_CQ, C_CKV, C_KA, C_VA = (0, 1024, 2048, 3072, 3584, 3840, 3968)

F32 = jnp.float32
BF16 = jnp.bfloat16
_NT = (((1,), (1,)), ((), ()))


def _cparams(n_grid):
    return pltpu.CompilerParams(
        dimension_semantics=("arbitrary",) * n_grid,
        vmem_limit_bytes=VMEM_LIMIT)


def _t5_bucket_np(d):
    max_exact = N_BUCKETS // 2
    nf = np.maximum(d, 1).astype(np.float32)
    large = max_exact + (np.log(nf / np.float32(max_exact)) / np.float32(math.log(MAX_DISTANCE / max_exact))
                         * np.float32(N_BUCKETS - max_exact)).astype(np.int32)
    large = np.minimum(large, N_BUCKETS - 1)
    return np.where(d < max_exact, d, large).astype(np.int32)


def _bias_tiles_kernel(tab_ref, idx_ref, out_a_ref, out_b_ref):
    h = pl.program_id(0)
    idx = idx_ref[...]
    acc_a = jnp.zeros(idx.shape, F32)
    acc_b = jnp.zeros(idx.shape, F32)
    far_b = tab_ref[N_BUCKETS - 1, A_HEADS + h]
    for b in range(N_BUCKETS):
        hit = idx == b
        acc_a = jnp.where(hit, tab_ref[b, h], acc_a)
        acc_b = jnp.where(hit, (tab_ref[b, A_HEADS + h] - far_b) * LOG2E, acc_b)
    out_a_ref[0] = acc_a
    out_b_ref[0] = acc_b[:, :2 * LANES]


def _bias_tiles(bias_table):
    r = np.arange(QB_A)[:, None]
    k = np.arange(QB_A)[None, :]
    prev = _t5_bucket_np(np.maximum(QB_A + r - k, 0))
    cur = _t5_bucket_np(np.maximum(r - k, 0))
    far = np.full((QB_A, QB_A), N_BUCKETS - 1, np.int32)
    idx = jnp.asarray(np.concatenate([prev, cur, far], axis=1))
    return pl.pallas_call(
        _bias_tiles_kernel,
        grid=(A_HEADS,),
        in_specs=[pl.BlockSpec(memory_space=pltpu.SMEM),
                  pl.BlockSpec((QB_A, 3 * LANES), lambda h: (0, 0))],
        out_specs=[pl.BlockSpec((1, QB_A, 3 * LANES), lambda h: (h, 0, 0)),
                   pl.BlockSpec((1, QB_A, 2 * LANES), lambda h: (h, 0, 0))],
        out_shape=[jax.ShapeDtypeStruct((A_HEADS, QB_A, 3 * LANES), F32),
                   jax.ShapeDtypeStruct((B_HEADS, QB_A, 2 * LANES), F32)],
        compiler_params=_cparams(1),
        name="bias_tiles",
    )(bias_table, idx)


def _rms(x, g):
    return x * lax.rsqrt(jnp.mean(x * x, axis=-1, keepdims=True) + EPS) * g


def _rmsnorm_kernel(h_ref, g_ref, u_ref):
    u_ref[...] = _rms(h_ref[...], g_ref[...]).astype(u_ref.dtype)


def _rmsnorm(h, g, tm=512):
    rows, d = h.shape
    return pl.pallas_call(
        _rmsnorm_kernel,
        grid=(rows // tm,),
        in_specs=[pl.BlockSpec((tm, d), lambda i: (i, 0)),
                  pl.BlockSpec((1, d), lambda i: (0, 0))],
        out_specs=pl.BlockSpec((tm, d), lambda i: (i, 0)),
        out_shape=jax.ShapeDtypeStruct((rows, d), BF16),
        compiler_params=_cparams(1),
        name="rmsnorm",
    )(h, g.reshape(1, d))


def _matmul_kernel(x_ref, w_ref, o_ref):
    o_ref[...] = jnp.dot(x_ref[...], w_ref[...], preferred_element_type=F32).astype(o_ref.dtype)


def _matmul(x, w, layer, name, tm=512, tn=1024):
    rows, kdim = x.shape
    n = w.shape[2]
    return pl.pallas_call(
        _matmul_kernel,
        grid=(n // tn, rows // tm),
        in_specs=[pl.BlockSpec((tm, kdim), lambda j, i: (i, 0)),
                  pl.BlockSpec((None, kdim, tn), lambda j, i: (layer, 0, j))],
        out_specs=pl.BlockSpec((tm, tn), lambda j, i: (i, j)),
        out_shape=jax.ShapeDtypeStruct((rows, n), F32),
        compiler_params=_cparams(2),
        name=name,
    )(x, w)


_IN_BLOCKS = ((0, 1, 2, 3), (5, 6, 7, 8), (12, 13, 14, 15), (9, 10, 11, 4))
IN_BLK = 256


def _in_proj_kernel(x_ref, w0_ref, w1_ref, w2_ref, w3_ref, o_ref, wbf_scr):
    @pl.when(pl.program_id(1) == 0)
    def _():
        for s, w_ref in enumerate((w0_ref, w1_ref, w2_ref, w3_ref)):
            wbf_scr[:, s * IN_BLK:(s + 1) * IN_BLK] = w_ref[...].astype(BF16)

    o_ref[...] = jnp.dot(x_ref[...], wbf_scr[...], preferred_element_type=F32)


def _in_proj(x, w_in, layer, tm=512):
    rows, kdim = x.shape
    tn = 4 * IN_BLK

    def w_spec(s):
        def index(j, i):
            blk = jnp.int32(_IN_BLOCKS[0][s])
            for t in range(1, len(_IN_BLOCKS)):
                blk = jnp.where(j == t, _IN_BLOCKS[t][s], blk)
            return layer, 0, blk
        return pl.BlockSpec((None, kdim, IN_BLK), index)

    return pl.pallas_call(
        _in_proj_kernel,
        grid=(len(_IN_BLOCKS), rows // tm),
        in_specs=[pl.BlockSpec((tm, kdim), lambda j, i: (i, 0))] + [w_spec(s) for s in range(4)],
        out_specs=pl.BlockSpec((tm, tn), lambda j, i: (i, j)),
        out_shape=jax.ShapeDtypeStruct((rows, len(_IN_BLOCKS) * tn), F32),
        scratch_shapes=[pltpu.VMEM((kdim, tn), BF16)],
        compiler_params=_cparams(2),
        name="in_proj",
    )(x, w_in, w_in, w_in, w_in)


def _silu(z):
    return z / (1.0 + jnp.exp(-z))


def _sigmoid(z):
    return 1.0 / (1.0 + jnp.exp(-z))


def _mixer_a_kernel(sink_ref, q_ref, kc_ref, vc_ref, kp_ref, vp_ref, km_ref, vm_ref, za_ref,
                    bias_ref, o_ref, s_scr, p_scr, m_scr, d_scr, msk_scr):
    j = pl.program_id(1)
    nk = 3 * QB_A
    q = (q_ref[...] * (HEAD_DIM ** -0.5)).astype(BF16)
    kcat = jnp.concatenate([kp_ref[...], kc_ref[...], km_ref[...]], axis=0)
    vcat = jnp.concatenate([vp_ref[...], vc_ref[...], vm_ref[...]], axis=0)
    lane = lax.broadcasted_iota(jnp.int32, (nk, LANES), 1)
    lo = lane < HEAD_DIM
    kswap = pltpu.roll(kcat, HEAD_DIM, 1)
    vswap = pltpu.roll(vcat, HEAD_DIM, 1)
    zero = jnp.zeros_like(kcat)
    k_half = ((jnp.where(lo, kcat, zero).astype(BF16), jnp.where(lo, zero, kswap).astype(BF16)),
              (jnp.where(lo, kswap, zero).astype(BF16), jnp.where(lo, zero, kcat).astype(BF16)))
    v_half = ((jnp.where(lo, vcat, zero).astype(BF16), jnp.where(lo, zero, vswap).astype(BF16)),
              (jnp.where(lo, vswap, zero).astype(BF16), jnp.where(lo, zero, vcat).astype(BF16)))

    r = lax.broadcasted_iota(jnp.int32, (QB_A, nk), 0)
    c = lax.broadcasted_iota(jnp.int32, (QB_A, nk), 1)
    jv = jnp.zeros((QB_A, nk), jnp.int32) + j
    in_prev = (c < QB_A) & (jv >= 1) & ((c > r) | ((jv == 1) & (c < N_META)))
    in_cur = (c >= QB_A) & (c < 2 * QB_A) & (c - QB_A <= r)
    in_meta = (c >= 2 * QB_A) & (jv >= 2) & (c - 2 * QB_A < N_META)
    msk_scr[...] = jnp.where(in_prev | in_cur | in_meta, 0.0, NEG)
    half = lax.broadcasted_iota(jnp.int32, (QB_A, LANES), 1) < HEAD_DIM
    pairs = A_HEADS // 4

    def logits(g, i, par, r0):
        hd = 2 * (g * pairs + i) + par
        s = s_scr[i * QB_A + r0:i * QB_A + r0 + SUB_A, par * nk:(par + 1) * nk]
        return s + bias_ref[hd, r0:r0 + SUB_A, :] + msk_scr[r0:r0 + SUB_A, :], sink_ref[hd]

    for g in range(2):
        q4 = jnp.concatenate([q[:, (g * pairs + i) * LANES:(g * pairs + i + 1) * LANES]
                              for i in range(pairs)], axis=0)
        s_scr[...] = lax.dot_general(q4, jnp.concatenate(k_half[g], axis=0), _NT,
                                     preferred_element_type=F32)
        for i in range(pairs):
            for par in range(2):
                for r0 in range(0, QB_A, SUB_A):
                    lg, sink = logits(g, i, par, r0)
                    m = jnp.maximum(jnp.max(lg, axis=-1, keepdims=True), sink)
                    m_scr[par, i * QB_A + r0:i * QB_A + r0 + SUB_A] = jnp.broadcast_to(m, (SUB_A, LANES))
        for i in range(pairs):
            for par in range(2):
                for r0 in range(0, QB_A, SUB_A):
                    sl = slice(i * QB_A + r0, i * QB_A + r0 + SUB_A)
                    lg, sink = logits(g, i, par, r0)
                    m = m_scr[par, sl]
                    p = jnp.exp(lg - jnp.concatenate([m] * 3, axis=1))
                    d_scr[par, sl] = jnp.broadcast_to(
                        jnp.sum(p, axis=-1, keepdims=True), (SUB_A, LANES)) + jnp.exp(sink - m)
                    p_scr[sl, par * nk:(par + 1) * nk] = p.astype(BF16)
        pv = jnp.dot(p_scr[...], jnp.concatenate(v_half[g], axis=0), preferred_element_type=F32)
        for i in range(pairs):
            sl = slice(i * QB_A, (i + 1) * QB_A)
            col = slice((g * pairs + i) * LANES, (g * pairs + i + 1) * LANES)
            o2 = pv[sl] / jnp.where(half, d_scr[0, sl], d_scr[1, sl])
            o_ref[:, col] = (o2 * _silu(za_ref[:, col])).astype(o_ref.dtype)


def _mixer_a(proj, sinks, bias_a, n_batch, lp):
    nb = lp // QB_A
    rows = n_batch * lp
    ck, cv = C_KA // LANES, C_VA // LANES
    cur = lambda b, j: b * nb + j
    prev = lambda b, j: b * nb + jnp.maximum(j - 1, 0)
    first = lambda b, j: b * nb
    return pl.pallas_call(
        _mixer_a_kernel,
        grid=(n_batch, nb),
        in_specs=[
            pl.BlockSpec(memory_space=pltpu.SMEM),
            pl.BlockSpec((QB_A, A_WIDTH), lambda b, j: (cur(b, j), C_QA // A_WIDTH)),
            pl.BlockSpec((QB_A, LANES), lambda b, j: (cur(b, j), ck)),
            pl.BlockSpec((QB_A, LANES), lambda b, j: (cur(b, j), cv)),
            pl.BlockSpec((QB_A, LANES), lambda b, j: (prev(b, j), ck)),
            pl.BlockSpec((QB_A, LANES), lambda b, j: (prev(b, j), cv)),
            pl.BlockSpec((QB_A, LANES), lambda b, j: (first(b, j), ck)),
            pl.BlockSpec((QB_A, LANES), lambda b, j: (first(b, j), cv)),
            pl.BlockSpec((QB_A, A_WIDTH), lambda b, j: (cur(b, j), C_ZA // A_WIDTH)),
            pl.BlockSpec((A_HEADS, QB_A, 3 * LANES), lambda b, j: (0, 0, 0)),
        ],
        out_specs=pl.BlockSpec((QB_A, A_WIDTH), lambda b, j: (cur(b, j), 0)),
        out_shape=jax.ShapeDtypeStruct((rows, A_WIDTH), BF16),
        scratch_shapes=[
            pltpu.VMEM((A_HEADS // 4 * QB_A, 6 * QB_A), F32),
            pltpu.VMEM((A_HEADS // 4 * QB_A, 6 * QB_A), BF16),
            pltpu.VMEM((2, A_HEADS // 4 * QB_A, LANES), F32),
            pltpu.VMEM((2, A_HEADS // 4 * QB_A, LANES), F32),
            pltpu.VMEM((QB_A, 3 * QB_A), F32),
        ],
        compiler_params=_cparams(2),
        name="mixer_a",
    )(sinks, proj, proj, proj, proj, proj, proj, proj, proj, bias_a)


def _bprep_kernel(cq_ref, ckv_ref, u_ref, qg_ref, kvg_ref, wqb_ref, wiq_ref, wuk_ref, wsm_ref,
                  ql_ref, qi_ref, wi_ref, ckvn_ref, klo_ref, khi_ref):
    cqn = _rms(cq_ref[...], qg_ref[...]).astype(BF16)
    ckvn_ref[0] = _rms(ckv_ref[...], kvg_ref[...]).astype(ckvn_ref.dtype)
    qb = jnp.dot(cqn, wqb_ref[...], preferred_element_type=F32).astype(BF16)
    for pair in range(B_HEADS // 2):
        ql2 = jnp.dot(qb[:, pair * LANES:(pair + 1) * LANES], wuk_ref[pair],
                      preferred_element_type=F32)
        ql_ref[0, 2 * pair] = ql2[:, :B_KV_RANK].astype(ql_ref.dtype)
        ql_ref[0, 2 * pair + 1] = ql2[:, B_KV_RANK:].astype(ql_ref.dtype)
    qi = jnp.dot(cqn, wiq_ref[...], preferred_element_type=F32)
    for pair in range(IDX_HEADS // 2):
        qi_ref[0, pair] = qi[:, pair * LANES:(pair + 1) * LANES].astype(qi_ref.dtype)
    small = jnp.dot(u_ref[...], wsm_ref[...], preferred_element_type=F32)
    klo_ref[0] = small[:, :LANES].astype(klo_ref.dtype)
    khi_ref[0] = small[:, LANES:2 * LANES].astype(khi_ref.dtype)
    wi_ref[0] = small[:, 2 * LANES:] * (IDX_HEADS ** -0.5)


def _bprep(proj, u, qg, kvg, wqb, wiq, wuk_bd, wsmall, layer, n_batch, lp, tm=256):
    nt = lp // tm
    row = lambda b, i: b * nt + i
    const2 = lambda b, i: (0, 0)
    lay2 = lambda b, i: (layer, 0, 0)
    return pl.pallas_call(
        _bprep_kernel,
        grid=(n_batch, nt),
        in_specs=[
            pl.BlockSpec((tm, B_Q_RANK), lambda b, i: (row(b, i), C_CQ // B_Q_RANK)),
            pl.BlockSpec((tm, B_KV_RANK), lambda b, i: (row(b, i), C_CKV // B_KV_RANK)),
            pl.BlockSpec((tm, D_MODEL), lambda b, i: (row(b, i), 0)),
            pl.BlockSpec((1, B_Q_RANK), const2),
            pl.BlockSpec((1, B_KV_RANK), const2),
            pl.BlockSpec((None, B_Q_RANK, B_WIDTH), lay2),
            pl.BlockSpec((None, B_Q_RANK, IDX_HEADS * IDX_DIM), lay2),
            pl.BlockSpec((None, B_HEADS // 2, LANES, 2 * B_KV_RANK), lambda b, i: (layer, 0, 0, 0)),
            pl.BlockSpec((None, D_MODEL, 3 * LANES), lay2),
        ],
        out_specs=[
            pl.BlockSpec((1, B_HEADS, tm, B_KV_RANK), lambda b, i: (b, 0, i, 0)),
            pl.BlockSpec((1, IDX_HEADS // 2, tm, LANES), lambda b, i: (b, 0, i, 0)),
            pl.BlockSpec((1, tm, LANES), lambda b, i: (b, i, 0)),
            pl.BlockSpec((1, tm, B_KV_RANK), lambda b, i: (b, i, 0)),
            pl.BlockSpec((1, tm, LANES), lambda b, i: (b, i, 0)),
            pl.BlockSpec((1, tm, LANES), lambda b, i: (b, i, 0)),
        ],
        out_shape=[
            jax.ShapeDtypeStruct((n_batch, B_HEADS, lp, B_KV_RANK), BF16),
            jax.ShapeDtypeStruct((n_batch, IDX_HEADS // 2, lp, LANES), BF16),
            jax.ShapeDtypeStruct((n_batch, lp, LANES), F32),
            jax.ShapeDtypeStruct((n_batch, lp, B_KV_RANK), BF16),
            jax.ShapeDtypeStruct((n_batch, lp, LANES), BF16),
            jax.ShapeDtypeStruct((n_batch, lp, LANES), BF16),
        ],
        compiler_params=_cparams(2),
        name="bprep",
    )(proj, proj, u, qg, kvg, wqb, wiq, wuk_bd, wsmall)


def _mixer_b_kernel(ql_ref, qi_ref, wi_ref, ckv_ref, klo_ref, khi_ref, zb_ref, bias_ref, wuv_ref,
                    o_ref, key_scr, s_scr, acc_scr, m_scr, l_scr, alpha_scr, p_scr):
    jq = pl.program_id(1)
    nch = jq + 1
    qb, kc = QB_B, QB_B
    rows = B_HEADS * qb
    kf = float(TOPK_MAX)

    qi = qi_ref[0].reshape(IDX_HEADS // 2 * qb, LANES)
    wit = wi_ref[0].T
    krow = lax.broadcasted_iota(jnp.int32, (qb, kc), 0)
    krow_sub = lax.broadcasted_iota(jnp.int32, (SUB_B, kc), 0)
    qpos = jq * qb + lax.broadcasted_iota(jnp.int32, (qb, kc), 1)

    def idx_body(c, carry):
        start = pl.multiple_of(c * kc, kc)
        sc = (lax.dot_general(klo_ref[0, pl.ds(start, kc), :], qi, _NT, preferred_element_type=F32),
              lax.dot_general(khi_ref[0, pl.ds(start, kc), :], qi, _NT, preferred_element_type=F32))
        isc = jnp.zeros((kc, qb), F32)
        for pair in range(IDX_HEADS // 2):
            for par in range(2):
                hd = 2 * pair + par
                isc = isc + wit[hd:hd + 1, :] * jnp.maximum(sc[par][:, pair * qb:(pair + 1) * qb], 0.0)
        bits = lax.bitcast_convert_type(isc, jnp.int32)
        key = bits ^ ((bits >> 31) & INT_MAX)
        key_scr[c] = jnp.where(c * kc + krow <= qpos, key, INT_MIN)
        return carry

    lax.fori_loop(0, nch, idx_body, 0)

    def reduce_chunks(tile_fn, init, combine):
        def body(c, acc):
            for r0 in range(0, kc, SUB_B):
                t = tile_fn(c, r0, key_scr[c, r0:r0 + SUB_B, :])
                t = t.reshape(SUB_B // 8, 8, qb)
                for i in range(SUB_B // 8):
                    acc = combine(acc, t[i])
            return acc
        return lax.fori_loop(0, nch, body, init)

    def count(pred_fn):
        acc = reduce_chunks(lambda c, r0, k: jnp.where(pred_fn(c, r0, k), 1.0, 0.0),
                            jnp.zeros((8, qb), F32), lambda a, b: a + b)
        return jnp.sum(acc, axis=0, keepdims=True)

    def unkey(k):
        return lax.bitcast_convert_type(k ^ ((k >> 31) & INT_MAX), F32)

    def tokey(v):
        b = lax.bitcast_convert_type(v, jnp.int32)
        return b ^ ((b >> 31) & INT_MAX)

    def select_all():
        return jnp.full((1, qb), INT_MIN, jnp.int32), jnp.full((1, qb), -1, jnp.int32)

    def select_topk():
        kmax = reduce_chunks(lambda c, r0, k: k, jnp.full((8, qb), INT_MIN, jnp.int32), jnp.maximum)
        kmax = jnp.max(kmax, axis=0, keepdims=True)
        kmin = reduce_chunks(lambda c, r0, k: jnp.where(k == INT_MIN, INT_MAX, k),
                             jnp.full((8, qb), INT_MAX, jnp.int32), jnp.minimum)
        kmin = jnp.min(kmin, axis=0, keepdims=True)
        c0 = count(lambda c, r0, k: k >= 0)
        c0p = count(lambda c, r0, k: k >= 1)
        n_valid = (jq * qb + 1 + lax.broadcasted_iota(jnp.int32, (1, qb), 1)).astype(F32)
        pos = c0p >= kf
        zero = (c0 >= kf) & (c0p < kf)
        lo0 = jnp.where(pos, 1, jnp.where(zero, 0, kmin))
        hi0 = jnp.where(pos, kmax + 1, jnp.where(zero, 1, 0))
        c_lo0 = jnp.where(pos, c0p, jnp.where(zero, c0, n_valid))
        c_hi0 = jnp.where(pos, 0.0, jnp.where(zero, c0p, c0))

        def bis_cond(st):
            return (st[0] < 64) & (st[5] > 0)

        def bis_step(it, lo, hi, c_lo, c_hi):
            mid_i = (lo >> 1) + (hi >> 1) + (lo & hi & 1)
            mid_v = tokey(0.5 * unkey(lo) + 0.5 * unkey(hi))
            use_v = (it < 24) & (mid_v > lo) & (mid_v < hi)
            mid = jnp.where(use_v, mid_v, mid_i)
            cnt = count(lambda c, r0, k: k >= mid)
            ge = cnt >= kf
            return (jnp.where(ge, mid, lo), jnp.where(ge, hi, mid),
                    jnp.where(ge, cnt, c_lo), jnp.where(ge, c_hi, cnt))

        def bis_body(st):
            it, lo, hi, c_lo, c_hi, _ = st
            for _ in range(BIS_UNROLL):
                lo, hi, c_lo, c_hi = bis_step(it, lo, hi, c_lo, c_hi)
                it = it + 1
            active = (lo + 1 < hi) & (c_lo != kf)
            return it, lo, hi, c_lo, c_hi, jnp.max(jnp.where(active, 1, 0))

        active0 = (lo0 + 1 < hi0) & (c_lo0 != kf)
        st0 = (jnp.int32(0), lo0, hi0, c_lo0, c_hi0, jnp.max(jnp.where(active0, 1, 0)))
        _, lo, hi, c_lo, c_hi, _ = lax.while_loop(bis_cond, bis_body, st0)
        tie = c_lo != kf

        def tie_break():
            need = kf - c_hi

            def tie_body(_, st):
                jlo, jhi = st
                mid = (jlo + jhi) >> 1
                cnt = count(lambda c, r0, k: (k == lo) & (c * kc + r0 + krow_sub <= mid))
                ge = cnt >= need
                return jnp.where(ge, jlo, mid), jnp.where(ge, mid, jhi)

            _, jhi = lax.fori_loop(0, 13, tie_body, (jnp.full((1, qb), -1, jnp.int32),
                                                     jnp.full((1, qb), 8191, jnp.int32)))
            return jnp.where(tie, jhi, INT_MAX)

        jmax = lax.cond(jnp.max(jnp.where(tie, 1, 0)) > 0, tie_break,
                        lambda: jnp.full((1, qb), INT_MAX, jnp.int32))
        return lo, jmax

    thr, jmax = lax.cond(jq >= 1, select_topk, select_all)

    def sel_body(c, carry):
        k = key_scr[c]
        sel = (k > thr) | ((k == thr) & (c * kc + krow <= jmax))
        key_scr[c] = lax.bitcast_convert_type(jnp.where(sel, 0.0, NEG).T, jnp.int32)
        return carry

    lax.fori_loop(0, nch, sel_body, 0)

    m_scr[...] = jnp.full(m_scr.shape, -jnp.inf, F32)
    l_scr[...] = jnp.zeros(l_scr.shape, F32)
    acc_scr[...] = jnp.zeros(acc_scr.shape, F32)

    def attend(c, nc, near):
        kw = nc * kc
        tiles = kw // LANES
        start = pl.multiple_of(c * kc, kc)
        kv = ckv_ref[0, pl.ds(start, kw), :]
        s_scr[:, :kw] = lax.dot_general(ql_ref[0].reshape(rows, B_KV_RANK), kv, _NT,
                                        preferred_element_type=F32)

        def logits(hd, r0):
            sh = s_scr[hd * qb + r0:hd * qb + r0 + SUB_B, :kw]
            if near == "cur":
                t0 = bias_ref[hd, r0 % QB_A:r0 % QB_A + SUB_B, LANES:]
                t1 = bias_ref[hd, r0 % QB_A:r0 % QB_A + SUB_B, :LANES]
                sh = sh + jnp.concatenate([t0 if r0 < QB_A else t1, t0], axis=1)
            elif near == "prev" and r0 < QB_A:
                t1 = bias_ref[hd, r0:r0 + SUB_B, :LANES]
                sh = sh + jnp.concatenate([jnp.zeros_like(t1), t1], axis=1)
            mask = [lax.bitcast_convert_type(key_scr[c + i, r0:r0 + SUB_B, :], F32) for i in range(nc)]
            return sh + (mask[0] if nc == 1 else jnp.concatenate(mask, axis=1))

        for hd in range(B_HEADS):
            for r0 in range(0, qb, SUB_B):
                sl = slice(hd * qb + r0, hd * qb + r0 + SUB_B)
                m_old = m_scr[sl]
                m_new = jnp.maximum(m_old, jnp.max(logits(hd, r0), axis=-1, keepdims=True))
                m_scr[sl] = m_new
                alpha_scr[sl] = jnp.exp2(m_old - m_new)
        for hd in range(B_HEADS):
            for r0 in range(0, qb, SUB_B):
                sl = slice(hd * qb + r0, hd * qb + r0 + SUB_B)
                m_new = m_scr[sl]
                p = jnp.exp2(logits(hd, r0) - jnp.concatenate([m_new] * tiles, axis=1))
                psum = p[:, :LANES]
                for t in range(1, tiles):
                    psum = psum + p[:, t * LANES:(t + 1) * LANES]
                l_scr[sl] = alpha_scr[sl] * l_scr[sl] + psum
                p_scr[sl, :kw] = p.astype(BF16)
        pv = jnp.dot(p_scr[:, :kw], kv, preferred_element_type=F32)
        for r0 in range(0, rows, SUB_B):
            sl = slice(r0, r0 + SUB_B)
            alpha = alpha_scr[sl]
            acc_scr[sl] = acc_scr[sl] * jnp.concatenate([alpha, alpha], axis=1) + pv[sl]

    n_far = jq - 1

    def far_body(i, carry):
        attend(i * FAR_NC, FAR_NC, None)
        return carry

    lax.fori_loop(0, n_far // FAR_NC, far_body, 0)

    if FAR_NC == 2:
        @pl.when((n_far >= 1) & (n_far % FAR_NC == 1))
        def _():
            attend(jq - 2, 1, None)

    @pl.when(jq >= 1)
    def _():
        attend(jq - 1, 1, "prev")

    attend(jq, 1, "cur")

    for pair in range(B_HEADS // 2):
        y2 = jnp.zeros((qb, LANES), F32)
        for par in range(2):
            hd = 2 * pair + par
            sl = slice(hd * qb, (hd + 1) * qb)
            l = jnp.sum(l_scr[sl], axis=-1, keepdims=True)
            o_lat = acc_scr[sl] / l
            y2 = y2 + jnp.dot(o_lat.astype(BF16), wuv_ref[pair, par], preferred_element_type=F32)
        z = zb_ref[:, pair * LANES:(pair + 1) * LANES]
        o_ref[:, pair * LANES:(pair + 1) * LANES] = (y2 * _silu(z)).astype(o_ref.dtype)


def _mixer_b(ql, qi, wi, ckvn, klo, khi, proj, bias_b, wuv_pairs, layer, n_batch, lp):
    nb = lp // QB_B
    rows = n_batch * lp
    whole = lambda b, j: (b, 0, 0)
    once = pl.Buffered(1)
    return pl.pallas_call(
        _mixer_b_kernel,
        grid=(n_batch, nb),
        in_specs=[
            pl.BlockSpec((1, B_HEADS, QB_B, B_KV_RANK), lambda b, j: (b, 0, j, 0)),
            pl.BlockSpec((1, IDX_HEADS // 2, QB_B, LANES), lambda b, j: (b, 0, j, 0)),
            pl.BlockSpec((1, QB_B, LANES), lambda b, j: (b, j, 0)),
            pl.BlockSpec((1, lp, B_KV_RANK), whole, pipeline_mode=once),
            pl.BlockSpec((1, lp, LANES), whole, pipeline_mode=once),
            pl.BlockSpec((1, lp, LANES), whole, pipeline_mode=once),
            pl.BlockSpec((QB_B, B_WIDTH), lambda b, j: (b * nb + j, C_ZB // B_WIDTH)),
            pl.BlockSpec((B_HEADS, QB_A, 2 * LANES), lambda b, j: (0, 0, 0), pipeline_mode=once),
            pl.BlockSpec((None, B_HEADS // 2, 2, B_KV_RANK, LANES), lambda b, j: (layer, 0, 0, 0, 0),
                         pipeline_mode=once),
        ],
        out_specs=pl.BlockSpec((QB_B, B_WIDTH), lambda b, j: (b * nb + j, 0)),
        out_shape=jax.ShapeDtypeStruct((rows, B_WIDTH), BF16),
        scratch_shapes=[
            pltpu.VMEM((nb, QB_B, QB_B), jnp.int32),
            pltpu.VMEM((B_HEADS * QB_B, FAR_NC * QB_B), F32),
            pltpu.VMEM((B_HEADS * QB_B, B_KV_RANK), F32),
            pltpu.VMEM((B_HEADS * QB_B, LANES), F32),
            pltpu.VMEM((B_HEADS * QB_B, LANES), F32),
            pltpu.VMEM((B_HEADS * QB_B, LANES), F32),
            pltpu.VMEM((B_HEADS * QB_B, FAR_NC * QB_B), BF16),
        ],
        compiler_params=_cparams(2),
        name="mixer_b",
    )(ql, qi, wi, ckvn, klo, khi, proj, bias_b, wuv_pairs)


def _merge_kernel(ya_ref, yb_ref, ga_ref, gb_ref, h_ref, wpa_ref, wpb_ref, wo_ref, g_ref, *out_refs,
                  last):
    pa = jnp.dot(ya_ref[...], wpa_ref[...], preferred_element_type=F32)
    pb = jnp.dot(yb_ref[...], wpb_ref[...], preferred_element_type=F32)
    merged = _sigmoid(ga_ref[...]) * pa + _sigmoid(gb_ref[...]) * pb
    h_new = h_ref[...] + jnp.dot(merged.astype(BF16), wo_ref[...], preferred_element_type=F32)
    normed = _rms(h_new, g_ref[...])
    if last:
        out_refs[0][...] = normed
    else:
        out_refs[0][...] = h_new
        out_refs[1][...] = normed.astype(BF16)


def _merge(ya, yb, gates, h, wpa, wpb, wo, g_next, layer, last, tm=256):
    rows, d = h.shape
    const = lambda i: (0, 0)
    lay = lambda i: (layer, 0, 0)
    once = pl.Buffered(1)
    if last:
        out_specs = [pl.BlockSpec((tm, d), lambda i: (i, 0))]
        out_shape = [jax.ShapeDtypeStruct((rows, d), F32)]
    else:
        out_specs = [pl.BlockSpec((tm, d), lambda i: (i, 0)), pl.BlockSpec((tm, d), lambda i: (i, 0))]
        out_shape = [jax.ShapeDtypeStruct((rows, d), F32), jax.ShapeDtypeStruct((rows, d), BF16)]
    return pl.pallas_call(
        functools.partial(_merge_kernel, last=last),
        grid=(rows // tm,),
        in_specs=[
            pl.BlockSpec((tm, A_WIDTH), lambda i: (i, 0)),
            pl.BlockSpec((tm, B_WIDTH), lambda i: (i, 0)),
            pl.BlockSpec((tm, d), lambda i: (i, 0)),
            pl.BlockSpec((tm, d), lambda i: (i, 1)),
            pl.BlockSpec((tm, d), lambda i: (i, 0)),
            pl.BlockSpec((None, A_WIDTH, d), lay, pipeline_mode=once),
            pl.BlockSpec((None, B_WIDTH, d), lay, pipeline_mode=once),
            pl.BlockSpec((None, d, d), lay, pipeline_mode=once),
            pl.BlockSpec((1, d), const),
        ],
        out_specs=out_specs,
        out_shape=out_shape,
        compiler_params=_cparams(1),
        name="merge_out",
    )(ya, yb, gates, gates, h, wpa, wpb, wo, g_next.reshape(1, d))


def _pack_weights(w_in, w_uk, w_uv, w_qb, w_iq, w_proj_a, w_proj_b, w_out):
    depth = w_in.shape[0]
    sec = lambda a, b: w_in[:, :, a:b]
    kidx, widx = sec(4096, 4160), sec(4160, 4168)
    w_gates = sec(4168, 8264).astype(BF16)
    z64 = jnp.zeros_like(kidx)
    zpad = jnp.zeros(widx.shape[:2] + (LANES - IDX_HEADS,), w_in.dtype)
    w_small = jnp.concatenate([kidx, z64, z64, kidx, widx, zpad], axis=-1).astype(BF16)
    uk = (w_uk * (HEAD_DIM ** -0.5 * LOG2E)).reshape(depth, B_HEADS // 2, 2, HEAD_DIM, B_KV_RANK)
    zuk = jnp.zeros_like(uk[:, :, 0])
    wuk_bd = jnp.concatenate([jnp.concatenate([uk[:, :, 0], zuk], axis=-1),
                              jnp.concatenate([zuk, uk[:, :, 1]], axis=-1)], axis=-2).astype(BF16)
    uv = w_uv.reshape(depth, B_HEADS // 2, 2, B_KV_RANK, HEAD_DIM)
    zuv = jnp.zeros_like(uv[:, :, 0])
    wuv_pairs = jnp.stack([jnp.concatenate([uv[:, :, 0], zuv], axis=-1),
                           jnp.concatenate([zuv, uv[:, :, 1]], axis=-1)], axis=2).astype(BF16)
    w_iq8 = (w_iq * (IDX_DIM ** -0.5)).astype(BF16)
    return (w_gates, w_small, wuk_bd, wuv_pairs, w_qb.astype(BF16), w_iq8,
            w_proj_a.astype(BF16), w_proj_b.astype(BF16), w_out.astype(BF16))


def kernel(x, meta_tokens, bias_table, norm_g, w_in, q_norm_g, kv_norm_g, w_qb, w_iq, w_uk, w_uv,
           sinks, w_proj_a, w_proj_b, w_out, final_g):
    n_batch, seq, d = x.shape
    depth = w_in.shape[0]
    length = seq + N_META
    lp = -(-length // QB_B) * QB_B
    assert min(TOPK_MAX, seq // 4) == TOPK_MAX and d == D_MODEL

    (w_gates, w_small, wuk_bd, wuv_pairs, wqb, wiq, wpa, wpb, wo) = _pack_weights(
        w_in, w_uk, w_uv, w_qb, w_iq, w_proj_a, w_proj_b, w_out)
    bias_a, bias_b = _bias_tiles(bias_table)

    meta = jnp.broadcast_to(meta_tokens.astype(x.dtype)[None], (n_batch, N_META, d))
    pad = jnp.zeros((n_batch, lp - length, d), x.dtype)
    h = jnp.concatenate([meta, x, pad], axis=1).reshape(n_batch * lp, d)

    u = _rmsnorm(h, norm_g[0])
    out = None
    for l in range(depth):
        proj = _in_proj(u, w_in, l)
        gates = _matmul(u, w_gates, l, "gate_proj")
        ya = _mixer_a(proj, sinks[l], bias_a, n_batch, lp)
        ql, qi, wi, ckvn, klo, khi = _bprep(
            proj, u, q_norm_g[l].reshape(1, -1), kv_norm_g[l].reshape(1, -1),
            wqb, wiq, wuk_bd, w_small, l, n_batch, lp)
        yb = _mixer_b(ql, qi, wi, ckvn, klo, khi, proj, bias_b, wuv_pairs, l, n_batch, lp)
        last = l == depth - 1
        g_next = final_g if last else norm_g[l + 1]
        res = _merge(ya, yb, gates, h, wpa, wpb, wo, g_next, l, last)
        if last:
            out = res[0]
        else:
            h, u = res
    return out.reshape(n_batch, lp, d)[:, N_META:length]
```

```python
import functools
import math

import numpy as np
import jax
import jax.numpy as jnp
from jax import lax
from jax.experimental import pallas as pl
from jax.experimental.pallas import tpu as pltpu

D_MODEL = 2048
N_META = 16
WINDOW = 128
HEAD_DIM = 64
A_HEADS = 16
A_WIDTH = 1024
B_HEADS = 16
B_WIDTH = 1024
B_Q_RANK = 512
B_KV_RANK = 256
IDX_HEADS = 8
IDX_DIM = 64
TOPK_MAX = 256
N_BUCKETS = 32
MAX_DISTANCE = 128
EPS = 1e-6
NEG = -1e30

LANES = 128
QB_A = 128
QB_B = 256
SUB_A = 32
SUB_B = 32
FAR_NC = 1
BIS_UNROLL = 3
LOG2E = math.log2(math.e)
VMEM_LIMIT = 56 * 1024 * 1024
INT_MIN = -(2 ** 31)
INT_MAX = 2 ** 31 - 1

C_QA, C_ZA, C_ZB, C_CQ, C_CKV, C_KA, C_VA = (0, 1024, 2048, 3072, 3584, 3840, 3968)
GATE_COL0 = 4168

F32 = jnp.float32
BF16 = jnp.bfloat16
_NT = (((1,), (1,)), ((), ()))


def _cparams(n_grid):
    return pltpu.CompilerParams(
        dimension_semantics=("arbitrary",) * n_grid,
        vmem_limit_bytes=VMEM_LIMIT)


def _t5_bucket_np(d):
    max_exact = N_BUCKETS // 2
    nf = np.maximum(d, 1).astype(np.float32)
    large = max_exact + (np.log(nf / np.float32(max_exact)) / np.float32(math.log(MAX_DISTANCE / max_exact))
                         * np.float32(N_BUCKETS - max_exact)).astype(np.int32)
    large = np.minimum(large, N_BUCKETS - 1)
    return np.where(d < max_exact, d, large).astype(np.int32)


def _bias_tiles_kernel(tab_ref, idx_ref, out_a_ref, out_b_ref):
    h = pl.program_id(0)
    idx = idx_ref[...]
    acc_a = jnp.zeros(idx.shape, F32)
    acc_b = jnp.zeros(idx.shape, F32)
    far_b = tab_ref[N_BUCKETS - 1, A_HEADS + h]
    for b in range(N_BUCKETS):
        hit = idx == b
        acc_a = jnp.where(hit, tab_ref[b, h], acc_a)
        acc_b = jnp.where(hit, (tab_ref[b, A_HEADS + h] - far_b) * LOG2E, acc_b)
    out_a_ref[0] = acc_a
    out_b_ref[0] = acc_b[:, :2 * LANES]


def _bias_tiles(bias_table):
    r = np.arange(QB_A)[:, None]
    k = np.arange(QB_A)[None, :]
    prev = _t5_bucket_np(np.maximum(QB_A + r - k, 0))
    cur = _t5_bucket_np(np.maximum(r - k, 0))
    far = np.full((QB_A, QB_A), N_BUCKETS - 1, np.int32)
    idx = jnp.asarray(np.concatenate([prev, cur, far], axis=1))
    return pl.pallas_call(
        _bias_tiles_kernel,
        grid=(A_HEADS,),
        in_specs=[pl.BlockSpec(memory_space=pltpu.SMEM),
                  pl.BlockSpec((QB_A, 3 * LANES), lambda h: (0, 0))],
        out_specs=[pl.BlockSpec((1, QB_A, 3 * LANES), lambda h: (h, 0, 0)),
                   pl.BlockSpec((1, QB_A, 2 * LANES), lambda h: (h, 0, 0))],
        out_shape=[jax.ShapeDtypeStruct((A_HEADS, QB_A, 3 * LANES), F32),
                   jax.ShapeDtypeStruct((B_HEADS, QB_A, 2 * LANES), F32)],
        compiler_params=_cparams(1),
        name="bias_tiles",
    )(bias_table, idx)


def _rms(x, g):
    return x * lax.rsqrt(jnp.mean(x * x, axis=-1, keepdims=True) + EPS) * g


def _rmsnorm_kernel(h_ref, g_ref, u_ref):
    u_ref[...] = _rms(h_ref[...], g_ref[...]).astype(u_ref.dtype)


def _rmsnorm(h, g, tm=512):
    rows, d = h.shape
    return pl.pallas_call(
        _rmsnorm_kernel,
        grid=(rows // tm,),
        in_specs=[pl.BlockSpec((tm, d), lambda i: (i, 0)),
                  pl.BlockSpec((1, d), lambda i: (0, 0))],
        out_specs=pl.BlockSpec((tm, d), lambda i: (i, 0)),
        out_shape=jax.ShapeDtypeStruct((rows, d), BF16),
        compiler_params=_cparams(1),
        name="rmsnorm",
    )(h, g.reshape(1, d))


_IN_BLOCKS = ((0, 1, 2, 3), (5, 6, 7, 8), (12, 13, 14, 15), (9, 10, 11, 4))
IN_BLK = 256


def _in_proj_kernel(x_ref, *refs):
    w_refs, o_ref, wbf_scr = refs[:-2], refs[-2], refs[-1]

    @pl.when(pl.program_id(1) == 0)
    def _():
        r0 = 0
        for w_ref in w_refs:
            wbf_scr[r0:r0 + w_ref.shape[0], :] = w_ref[...].astype(BF16)
            r0 += w_ref.shape[0]

    o_ref[...] = lax.dot_general(x_ref[...], wbf_scr[...], _NT, preferred_element_type=F32)


def _in_proj(x, w_t, layer, tm=512):
    rows, kdim = x.shape
    tn = 4 * IN_BLK

    def w_spec(s):
        def index(j, i):
            blk = jnp.int32(_IN_BLOCKS[0][s])
            for t in range(1, len(_IN_BLOCKS)):
                blk = jnp.where(j == t, _IN_BLOCKS[t][s], blk)
            return layer, blk, 0
        return pl.BlockSpec((None, IN_BLK, kdim), index)

    return pl.pallas_call(
        _in_proj_kernel,
        grid=(len(_IN_BLOCKS), rows // tm),
        in_specs=[pl.BlockSpec((tm, kdim), lambda j, i: (i, 0))] + [w_spec(s) for s in range(4)],
        out_specs=pl.BlockSpec((tm, tn), lambda j, i: (i, j)),
        out_shape=jax.ShapeDtypeStruct((rows, len(_IN_BLOCKS) * tn), F32),
        scratch_shapes=[pltpu.VMEM((tn, kdim), BF16)],
        compiler_params=_cparams(2),
        name="in_proj",
    )(x, w_t, w_t, w_t, w_t)


def _gate_proj(x, w_t, layer, col0, n, tm=512, tn=1024):
    rows, kdim = x.shape
    return pl.pallas_call(
        _in_proj_kernel,
        grid=(n // tn, rows // tm),
        in_specs=[pl.BlockSpec((tm, kdim), lambda j, i: (i, 0)),
                  pl.BlockSpec((None, pl.Element(tn), pl.Element(kdim)),
                               lambda j, i: (layer, pl.multiple_of(col0 + j * tn, 8), 0))],
        out_specs=pl.BlockSpec((tm, tn), lambda j, i: (i, j)),
        out_shape=jax.ShapeDtypeStruct((rows, n), F32),
        scratch_shapes=[pltpu.VMEM((tn, kdim), BF16)],
        compiler_params=_cparams(2),
        name="gate_proj",
    )(x, w_t)


def _silu(z):
    return z / (1.0 + jnp.exp(-z))


def _sigmoid(z):
    return 1.0 / (1.0 + jnp.exp(-z))


def _mixer_a_kernel(sink_ref, q_ref, kc_ref, vc_ref, kp_ref, vp_ref, km_ref, vm_ref, za_ref,
                    bias_ref, o_ref, s_scr, p_scr, m_scr, d_scr, msk_scr):
    j = pl.program_id(1)
    nk = 3 * QB_A
    q = (q_ref[...] * (HEAD_DIM ** -0.5)).astype(BF16)
    kcat = jnp.concatenate([kp_ref[...], kc_ref[...], km_ref[...]], axis=0)
    vcat = jnp.concatenate([vp_ref[...], vc_ref[...], vm_ref[...]], axis=0)
    lane = lax.broadcasted_iota(jnp.int32, (nk, LANES), 1)
    lo = lane < HEAD_DIM
    kswap = pltpu.roll(kcat, HEAD_DIM, 1)
    vswap = pltpu.roll(vcat, HEAD_DIM, 1)
    zero = jnp.zeros_like(kcat)
    k_half = ((jnp.where(lo, kcat, zero).astype(BF16), jnp.where(lo, zero, kswap).astype(BF16)),
              (jnp.where(lo, kswap, zero).astype(BF16), jnp.where(lo, zero, kcat).astype(BF16)))
    v_half = ((jnp.where(lo, vcat, zero).astype(BF16), jnp.where(lo, zero, vswap).astype(BF16)),
              (jnp.where(lo, vswap, zero).astype(BF16), jnp.where(lo, zero, vcat).astype(BF16)))

    r = lax.broadcasted_iota(jnp.int32, (QB_A, nk), 0)
    c = lax.broadcasted_iota(jnp.int32, (QB_A, nk), 1)
    jv = jnp.zeros((QB_A, nk), jnp.int32) + j
    in_prev = (c < QB_A) & (jv >= 1) & ((c > r) | ((jv == 1) & (c < N_META)))
    in_cur = (c >= QB_A) & (c < 2 * QB_A) & (c - QB_A <= r)
    in_meta = (c >= 2 * QB_A) & (jv >= 2) & (c - 2 * QB_A < N_META)
    msk_scr[...] = jnp.where(in_prev | in_cur | in_meta, 0.0, NEG)
    half = lax.broadcasted_iota(jnp.int32, (QB_A, LANES), 1) < HEAD_DIM
    pairs = A_HEADS // 4

    def logits(g, i, par, r0):
        hd = 2 * (g * pairs + i) + par
        s = s_scr[i * QB_A + r0:i * QB_A + r0 + SUB_A, par * nk:(par + 1) * nk]
        return s + bias_ref[hd, r0:r0 + SUB_A, :] + msk_scr[r0:r0 + SUB_A, :], sink_ref[hd]

    for g in range(2):
        q4 = jnp.concatenate([q[:, (g * pairs + i) * LANES:(g * pairs + i + 1) * LANES]
                              for i in range(pairs)], axis=0)
        s_scr[...] = lax.dot_general(q4, jnp.concatenate(k_half[g], axis=0), _NT,
                                     preferred_element_type=F32)
        for i in range(pairs):
            for par in range(2):
                for r0 in range(0, QB_A, SUB_A):
                    lg, sink = logits(g, i, par, r0)
                    m = jnp.maximum(jnp.max(lg, axis=-1, keepdims=True), sink)
                    m_scr[par, i * QB_A + r0:i * QB_A + r0 + SUB_A] = jnp.broadcast_to(m, (SUB_A, LANES))
        for i in range(pairs):
            for par in range(2):
                for r0 in range(0, QB_A, SUB_A):
                    sl = slice(i * QB_A + r0, i * QB_A + r0 + SUB_A)
                    lg, sink = logits(g, i, par, r0)
                    m = m_scr[par, sl]
                    p = jnp.exp(lg - jnp.concatenate([m] * 3, axis=1))
                    d_scr[par, sl] = jnp.broadcast_to(
                        jnp.sum(p, axis=-1, keepdims=True), (SUB_A, LANES)) + jnp.exp(sink - m)
                    p_scr[sl, par * nk:(par + 1) * nk] = p.astype(BF16)
        pv = jnp.dot(p_scr[...], jnp.concatenate(v_half[g], axis=0), preferred_element_type=F32)
        for i in range(pairs):
            sl = slice(i * QB_A, (i + 1) * QB_A)
            col = slice((g * pairs + i) * LANES, (g * pairs + i + 1) * LANES)
            o2 = pv[sl] / jnp.where(half, d_scr[0, sl], d_scr[1, sl])
            o_ref[:, col] = (o2 * _silu(za_ref[:, col])).astype(o_ref.dtype)


def _mixer_a(proj, sinks, bias_a, n_batch, lp):
    nb = lp // QB_A
    rows = n_batch * lp
    ck, cv = C_KA // LANES, C_VA // LANES
    cur = lambda b, j: b * nb + j
    prev = lambda b, j: b * nb + jnp.maximum(j - 1, 0)
    first = lambda b, j: b * nb
    return pl.pallas_call(
        _mixer_a_kernel,
        grid=(n_batch, nb),
        in_specs=[
            pl.BlockSpec(memory_space=pltpu.SMEM),
            pl.BlockSpec((QB_A, A_WIDTH), lambda b, j: (cur(b, j), C_QA // A_WIDTH)),
            pl.BlockSpec((QB_A, LANES), lambda b, j: (cur(b, j), ck)),
            pl.BlockSpec((QB_A, LANES), lambda b, j: (cur(b, j), cv)),
            pl.BlockSpec((QB_A, LANES), lambda b, j: (prev(b, j), ck)),
            pl.BlockSpec((QB_A, LANES), lambda b, j: (prev(b, j), cv)),
            pl.BlockSpec((QB_A, LANES), lambda b, j: (first(b, j), ck)),
            pl.BlockSpec((QB_A, LANES), lambda b, j: (first(b, j), cv)),
            pl.BlockSpec((QB_A, A_WIDTH), lambda b, j: (cur(b, j), C_ZA // A_WIDTH)),
            pl.BlockSpec((A_HEADS, QB_A, 3 * LANES), lambda b, j: (0, 0, 0)),
        ],
        out_specs=pl.BlockSpec((QB_A, A_WIDTH), lambda b, j: (cur(b, j), 0)),
        out_shape=jax.ShapeDtypeStruct((rows, A_WIDTH), BF16),
        scratch_shapes=[
            pltpu.VMEM((A_HEADS // 4 * QB_A, 6 * QB_A), F32),
            pltpu.VMEM((A_HEADS // 4 * QB_A, 6 * QB_A), BF16),
            pltpu.VMEM((2, A_HEADS // 4 * QB_A, LANES), F32),
            pltpu.VMEM((2, A_HEADS // 4 * QB_A, LANES), F32),
            pltpu.VMEM((QB_A, 3 * QB_A), F32),
        ],
        compiler_params=_cparams(2),
        name="mixer_a",
    )(sinks, proj, proj, proj, proj, proj, proj, proj, proj, bias_a)


def _bprep_kernel(cq_ref, ckv_ref, u_ref, qg_ref, kvg_ref, wqb_ref, wiq_ref, wuk_ref, wsm_ref,
                  ql_ref, qi_ref, wi_ref, ckvn_ref, klo_ref, khi_ref):
    cqn = _rms(cq_ref[...], qg_ref[...]).astype(BF16)
    ckvn_ref[0] = _rms(ckv_ref[...], kvg_ref[...]).astype(ckvn_ref.dtype)
    qb = jnp.dot(cqn, wqb_ref[...], preferred_element_type=F32).astype(BF16)
    for pair in range(B_HEADS // 2):
        ql2 = jnp.dot(qb[:, pair * LANES:(pair + 1) * LANES], wuk_ref[pair],
                      preferred_element_type=F32)
        ql_ref[0, 2 * pair] = ql2[:, :B_KV_RANK].astype(ql_ref.dtype)
        ql_ref[0, 2 * pair + 1] = ql2[:, B_KV_RANK:].astype(ql_ref.dtype)
    qi = jnp.dot(cqn, wiq_ref[...], preferred_element_type=F32)
    for pair in range(IDX_HEADS // 2):
        qi_ref[0, pair] = qi[:, pair * LANES:(pair + 1) * LANES].astype(qi_ref.dtype)
    small = jnp.dot(u_ref[...], wsm_ref[...], preferred_element_type=F32)
    klo_ref[0] = small[:, :LANES].astype(klo_ref.dtype)
    khi_ref[0] = small[:, LANES:2 * LANES].astype(khi_ref.dtype)
    wi_ref[0] = small[:, 2 * LANES:] * (IDX_HEADS ** -0.5)


def _bprep(proj, u, qg, kvg, wqb, wiq, wuk_bd, wsmall, layer, n_batch, lp, tm=256):
    nt = lp // tm
    row = lambda b, i: b * nt + i
    const2 = lambda b, i: (0, 0)
    lay2 = lambda b, i: (layer, 0, 0)
    return pl.pallas_call(
        _bprep_kernel,
        grid=(n_batch, nt),
        in_specs=[
            pl.BlockSpec((tm, B_Q_RANK), lambda b, i: (row(b, i), C_CQ // B_Q_RANK)),
            pl.BlockSpec((tm, B_KV_RANK), lambda b, i: (row(b, i), C_CKV // B_KV_RANK)),
            pl.BlockSpec((tm, D_MODEL), lambda b, i: (row(b, i), 0)),
            pl.BlockSpec((1, B_Q_RANK), const2),
            pl.BlockSpec((1, B_KV_RANK), const2),
            pl.BlockSpec((None, B_Q_RANK, B_WIDTH), lay2),
            pl.BlockSpec((None, B_Q_RANK, IDX_HEADS * IDX_DIM), lay2),
            pl.BlockSpec((None, B_HEADS // 2, LANES, 2 * B_KV_RANK), lambda b, i: (layer, 0, 0, 0)),
            pl.BlockSpec((None, D_MODEL, 3 * LANES), lay2),
        ],
        out_specs=[
            pl.BlockSpec((1, B_HEADS, tm, B_KV_RANK), lambda b, i: (b, 0, i, 0)),
            pl.BlockSpec((1, IDX_HEADS // 2, tm, LANES), lambda b, i: (b, 0, i, 0)),
            pl.BlockSpec((1, tm, LANES), lambda b, i: (b, i, 0)),
            pl.BlockSpec((1, tm, B_KV_RANK), lambda b, i: (b, i, 0)),
            pl.BlockSpec((1, tm, LANES), lambda b, i: (b, i, 0)),
            pl.BlockSpec((1, tm, LANES), lambda b, i: (b, i, 0)),
        ],
        out_shape=[
            jax.ShapeDtypeStruct((n_batch, B_HEADS, lp, B_KV_RANK), BF16),
            jax.ShapeDtypeStruct((n_batch, IDX_HEADS // 2, lp, LANES), BF16),
            jax.ShapeDtypeStruct((n_batch, lp, LANES), F32),
            jax.ShapeDtypeStruct((n_batch, lp, B_KV_RANK), BF16),
            jax.ShapeDtypeStruct((n_batch, lp, LANES), BF16),
            jax.ShapeDtypeStruct((n_batch, lp, LANES), BF16),
        ],
        compiler_params=_cparams(2),
        name="bprep",
    )(proj, proj, u, qg, kvg, wqb, wiq, wuk_bd, wsmall)


def _mixer_b_kernel(ql_ref, qi_ref, wi_ref, ckv_ref, klo_ref, khi_ref, zb_ref, bias_ref, wuv_ref,
                    o_ref, key_scr, s_scr, acc_scr, m_scr, l_scr, alpha_scr, p_scr):
    jq = pl.program_id(1)
    nch = jq + 1
    qb, kc = QB_B, QB_B
    rows = B_HEADS * qb
    kf = float(TOPK_MAX)

    qi = qi_ref[0].reshape(IDX_HEADS // 2 * qb, LANES)
    wit = wi_ref[0].T
    krow = lax.broadcasted_iota(jnp.int32, (qb, kc), 0)
    krow_sub = lax.broadcasted_iota(jnp.int32, (SUB_B, kc), 0)
    qpos = jq * qb + lax.broadcasted_iota(jnp.int32, (qb, kc), 1)

    def idx_body(c, carry):
        start = pl.multiple_of(c * kc, kc)
        sc = (lax.dot_general(klo_ref[0, pl.ds(start, kc), :], qi, _NT, preferred_element_type=F32),
              lax.dot_general(khi_ref[0, pl.ds(start, kc), :], qi, _NT, preferred_element_type=F32))
        isc = jnp.zeros((kc, qb), F32)
        for pair in range(IDX_HEADS // 2):
            for par in range(2):
                hd = 2 * pair + par
                isc = isc + wit[hd:hd + 1, :] * jnp.maximum(sc[par][:, pair * qb:(pair + 1) * qb], 0.0)
        bits = lax.bitcast_convert_type(isc, jnp.int32)
        key = bits ^ ((bits >> 31) & INT_MAX)
        key_scr[c] = jnp.where(c * kc + krow <= qpos, key, INT_MIN)
        return carry

    lax.fori_loop(0, nch, idx_body, 0)

    def reduce_chunks(tile_fn, init, combine):
        def body(c, acc):
            for r0 in range(0, kc, SUB_B):
                t = tile_fn(c, r0, key_scr[c, r0:r0 + SUB_B, :])
                t = t.reshape(SUB_B // 8, 8, qb)
                for i in range(SUB_B // 8):
                    acc = combine(acc, t[i])
            return acc
        return lax.fori_loop(0, nch, body, init)

    def count(pred_fn):
        acc = reduce_chunks(lambda c, r0, k: jnp.where(pred_fn(c, r0, k), 1.0, 0.0),
                            jnp.zeros((8, qb), F32), lambda a, b: a + b)
        return jnp.sum(acc, axis=0, keepdims=True)

    def unkey(k):
        return lax.bitcast_convert_type(k ^ ((k >> 31) & INT_MAX), F32)

    def tokey(v):
        b = lax.bitcast_convert_type(v, jnp.int32)
        return b ^ ((b >> 31) & INT_MAX)

    def select_all():
        return jnp.full((1, qb), INT_MIN, jnp.int32), jnp.full((1, qb), -1, jnp.int32)

    def select_topk():
        kmax = reduce_chunks(lambda c, r0, k: k, jnp.full((8, qb), INT_MIN, jnp.int32), jnp.maximum)
        kmax = jnp.max(kmax, axis=0, keepdims=True)
        kmin = reduce_chunks(lambda c, r0, k: jnp.where(k == INT_MIN, INT_MAX, k),
                             jnp.full((8, qb), INT_MAX, jnp.int32), jnp.minimum)
        kmin = jnp.min(kmin, axis=0, keepdims=True)
        c0 = count(lambda c, r0, k: k >= 0)
        c0p = count(lambda c, r0, k: k >= 1)
        n_valid = (jq * qb + 1 + lax.broadcasted_iota(jnp.int32, (1, qb), 1)).astype(F32)
        pos = c0p >= kf
        zero = (c0 >= kf) & (c0p < kf)
        lo0 = jnp.where(pos, 1, jnp.where(zero, 0, kmin))
        hi0 = jnp.where(pos, kmax + 1, jnp.where(zero, 1, 0))
        c_lo0 = jnp.where(pos, c0p, jnp.where(zero, c0, n_valid))
        c_hi0 = jnp.where(pos, 0.0, jnp.where(zero, c0p, c0))

        def bis_cond(st):
            return (st[0] < 64) & (st[5] > 0)

        def bis_step(it, lo, hi, c_lo, c_hi):
            mid_i = (lo >> 1) + (hi >> 1) + (lo & hi & 1)
            mid_v = tokey(0.5 * unkey(lo) + 0.5 * unkey(hi))
            use_v = (it < 24) & (mid_v > lo) & (mid_v < hi)
            mid = jnp.where(use_v, mid_v, mid_i)
            cnt = count(lambda c, r0, k: k >= mid)
            ge = cnt >= kf
            return (jnp.where(ge, mid, lo), jnp.where(ge, hi, mid),
                    jnp.where(ge, cnt, c_lo), jnp.where(ge, c_hi, cnt))

        def bis_body(st):
            it, lo, hi, c_lo, c_hi, _ = st
            for _ in range(BIS_UNROLL):
                lo, hi, c_lo, c_hi = bis_step(it, lo, hi, c_lo, c_hi)
                it = it + 1
            active = (lo + 1 < hi) & (c_lo != kf)
            return it, lo, hi, c_lo, c_hi, jnp.max(jnp.where(active, 1, 0))

        active0 = (lo0 + 1 < hi0) & (c_lo0 != kf)
        st0 = (jnp.int32(0), lo0, hi0, c_lo0, c_hi0, jnp.max(jnp.where(active0, 1, 0)))
        _, lo, hi, c_lo, c_hi, _ = lax.while_loop(bis_cond, bis_body, st0)
        tie = c_lo != kf

        def tie_break():
            need = kf - c_hi

            def tie_body(_, st):
                jlo, jhi = st
                mid = (jlo + jhi) >> 1
                cnt = count(lambda c, r0, k: (k == lo) & (c * kc + r0 + krow_sub <= mid))
                ge = cnt >= need
                return jnp.where(ge, jlo, mid), jnp.where(ge, mid, jhi)

            _, jhi = lax.fori_loop(0, 13, tie_body, (jnp.full((1, qb), -1, jnp.int32),
                                                     jnp.full((1, qb), 8191, jnp.int32)))
            return jnp.where(tie, jhi, INT_MAX)

        jmax = lax.cond(jnp.max(jnp.where(tie, 1, 0)) > 0, tie_break,
                        lambda: jnp.full((1, qb), INT_MAX, jnp.int32))
        return lo, jmax

    thr, jmax = lax.cond(jq >= 1, select_topk, select_all)

    def sel_body(c, carry):
        k = key_scr[c]
        sel = (k > thr) | ((k == thr) & (c * kc + krow <= jmax))
        key_scr[c] = lax.bitcast_convert_type(jnp.where(sel, 0.0, NEG).T, jnp.int32)
        return carry

    lax.fori_loop(0, nch, sel_body, 0)

    m_scr[...] = jnp.full(m_scr.shape, -jnp.inf, F32)
    l_scr[...] = jnp.zeros(l_scr.shape, F32)
    acc_scr[...] = jnp.zeros(acc_scr.shape, F32)

    def attend(c, nc, near):
        kw = nc * kc
        tiles = kw // LANES
        start = pl.multiple_of(c * kc, kc)
        kv = ckv_ref[0, pl.ds(start, kw), :]
        s_scr[:, :kw] = lax.dot_general(ql_ref[0].reshape(rows, B_KV_RANK), kv, _NT,
                                        preferred_element_type=F32)

        def logits(hd, r0):
            sh = s_scr[hd * qb + r0:hd * qb + r0 + SUB_B, :kw]
            if near == "cur":
                t0 = bias_ref[hd, r0 % QB_A:r0 % QB_A + SUB_B, LANES:]
                t1 = bias_ref[hd, r0 % QB_A:r0 % QB_A + SUB_B, :LANES]
                sh = sh + jnp.concatenate([t0 if r0 < QB_A else t1, t0], axis=1)
            elif near == "prev" and r0 < QB_A:
                t1 = bias_ref[hd, r0:r0 + SUB_B, :LANES]
                sh = sh + jnp.concatenate([jnp.zeros_like(t1), t1], axis=1)
            mask = [lax.bitcast_convert_type(key_scr[c + i, r0:r0 + SUB_B, :], F32) for i in range(nc)]
            return sh + (mask[0] if nc == 1 else jnp.concatenate(mask, axis=1))

        for hd in range(B_HEADS):
            for r0 in range(0, qb, SUB_B):
                sl = slice(hd * qb + r0, hd * qb + r0 + SUB_B)
                m_old = m_scr[sl]
                m_new = jnp.maximum(m_old, jnp.max(logits(hd, r0), axis=-1, keepdims=True))
                m_scr[sl] = m_new
                alpha_scr[sl] = jnp.exp2(m_old - m_new)
        for hd in range(B_HEADS):
            for r0 in range(0, qb, SUB_B):
                sl = slice(hd * qb + r0, hd * qb + r0 + SUB_B)
                m_new = m_scr[sl]
                p = jnp.exp2(logits(hd, r0) - jnp.concatenate([m_new] * tiles, axis=1))
                psum = p[:, :LANES]
                for t in range(1, tiles):
                    psum = psum + p[:, t * LANES:(t + 1) * LANES]
                l_scr[sl] = alpha_scr[sl] * l_scr[sl] + psum
                p_scr[sl, :kw] = p.astype(BF16)
        pv = jnp.dot(p_scr[:, :kw], kv, preferred_element_type=F32)
        for r0 in range(0, rows, SUB_B):
            sl = slice(r0, r0 + SUB_B)
            alpha = alpha_scr[sl]
            acc_scr[sl] = acc_scr[sl] * jnp.concatenate([alpha, alpha], axis=1) + pv[sl]

    n_far = jq - 1

    def far_body(i, carry):
        attend(i * FAR_NC, FAR_NC, None)
        return carry

    lax.fori_loop(0, n_far // FAR_NC, far_body, 0)

    if FAR_NC == 2:
        @pl.when((n_far >= 1) & (n_far % FAR_NC == 1))
        def _():
            attend(jq - 2, 1, None)

    @pl.when(jq >= 1)
    def _():
        attend(jq - 1, 1, "prev")

    attend(jq, 1, "cur")

    for pair in range(B_HEADS // 2):
        y2 = jnp.zeros((qb, LANES), F32)
        for par in range(2):
            hd = 2 * pair + par
            sl = slice(hd * qb, (hd + 1) * qb)
            l = jnp.sum(l_scr[sl], axis=-1, keepdims=True)
            o_lat = acc_scr[sl] / l
            y2 = y2 + jnp.dot(o_lat.astype(BF16), wuv_ref[pair, par], preferred_element_type=F32)
        z = zb_ref[:, pair * LANES:(pair + 1) * LANES]
        o_ref[:, pair * LANES:(pair + 1) * LANES] = (y2 * _silu(z)).astype(o_ref.dtype)


def _mixer_b(ql, qi, wi, ckvn, klo, khi, proj, bias_b, wuv_pairs, layer, n_batch, lp):
    nb = lp // QB_B
    rows = n_batch * lp
    whole = lambda b, j: (b, 0, 0)
    once = pl.Buffered(1)
    return pl.pallas_call(
        _mixer_b_kernel,
        grid=(n_batch, nb),
        in_specs=[
            pl.BlockSpec((1, B_HEADS, QB_B, B_KV_RANK), lambda b, j: (b, 0, j, 0)),
            pl.BlockSpec((1, IDX_HEADS // 2, QB_B, LANES), lambda b, j: (b, 0, j, 0)),
            pl.BlockSpec((1, QB_B, LANES), lambda b, j: (b, j, 0)),
            pl.BlockSpec((1, lp, B_KV_RANK), whole, pipeline_mode=once),
            pl.BlockSpec((1, lp, LANES), whole, pipeline_mode=once),
            pl.BlockSpec((1, lp, LANES), whole, pipeline_mode=once),
            pl.BlockSpec((QB_B, B_WIDTH), lambda b, j: (b * nb + j, C_ZB // B_WIDTH)),
            pl.BlockSpec((B_HEADS, QB_A, 2 * LANES), lambda b, j: (0, 0, 0), pipeline_mode=once),
            pl.BlockSpec((None, B_HEADS // 2, 2, B_KV_RANK, LANES), lambda b, j: (layer, 0, 0, 0, 0),
                         pipeline_mode=once),
        ],
        out_specs=pl.BlockSpec((QB_B, B_WIDTH), lambda b, j: (b * nb + j, 0)),
        out_shape=jax.ShapeDtypeStruct((rows, B_WIDTH), BF16),
        scratch_shapes=[
            pltpu.VMEM((nb, QB_B, QB_B), jnp.int32),
            pltpu.VMEM((B_HEADS * QB_B, FAR_NC * QB_B), F32),
            pltpu.VMEM((B_HEADS * QB_B, B_KV_RANK), F32),
            pltpu.VMEM((B_HEADS * QB_B, LANES), F32),
            pltpu.VMEM((B_HEADS * QB_B, LANES), F32),
            pltpu.VMEM((B_HEADS * QB_B, LANES), F32),
            pltpu.VMEM((B_HEADS * QB_B, FAR_NC * QB_B), BF16),
        ],
        compiler_params=_cparams(2),
        name="mixer_b",
    )(ql, qi, wi, ckvn, klo, khi, proj, bias_b, wuv_pairs)


def _merge_kernel(ya_ref, yb_ref, ga_ref, gb_ref, h_ref, wpa_ref, wpb_ref, wo_ref, g_ref, *out_refs,
                  last):
    pa = jnp.dot(ya_ref[...], wpa_ref[...], preferred_element_type=F32)
    pb = jnp.dot(yb_ref[...], wpb_ref[...], preferred_element_type=F32)
    merged = _sigmoid(ga_ref[...]) * pa + _sigmoid(gb_ref[...]) * pb
    h_new = h_ref[...] + jnp.dot(merged.astype(BF16), wo_ref[...], preferred_element_type=F32)
    normed = _rms(h_new, g_ref[...])
    if last:
        out_refs[0][...] = normed
    else:
        out_refs[0][...] = h_new
        out_refs[1][...] = normed.astype(BF16)


def _merge(ya, yb, gates, h, wpa, wpb, wo, g_next, layer, last, tm=256):
    rows, d = h.shape
    const = lambda i: (0, 0)
    lay = lambda i: (layer, 0, 0)
    once = pl.Buffered(1)
    if last:
        out_specs = [pl.BlockSpec((tm, d), lambda i: (i, 0))]
        out_shape = [jax.ShapeDtypeStruct((rows, d), F32)]
    else:
        out_specs = [pl.BlockSpec((tm, d), lambda i: (i, 0)), pl.BlockSpec((tm, d), lambda i: (i, 0))]
        out_shape = [jax.ShapeDtypeStruct((rows, d), F32), jax.ShapeDtypeStruct((rows, d), BF16)]
    return pl.pallas_call(
        functools.partial(_merge_kernel, last=last),
        grid=(rows // tm,),
        in_specs=[
            pl.BlockSpec((tm, A_WIDTH), lambda i: (i, 0)),
            pl.BlockSpec((tm, B_WIDTH), lambda i: (i, 0)),
            pl.BlockSpec((tm, d), lambda i: (i, 0)),
            pl.BlockSpec((tm, d), lambda i: (i, 1)),
            pl.BlockSpec((tm, d), lambda i: (i, 0)),
            pl.BlockSpec((None, A_WIDTH, d), lay, pipeline_mode=once),
            pl.BlockSpec((None, B_WIDTH, d), lay, pipeline_mode=once),
            pl.BlockSpec((None, d, d), lay, pipeline_mode=once),
            pl.BlockSpec((1, d), const),
        ],
        out_specs=out_specs,
        out_shape=out_shape,
        compiler_params=_cparams(1),
        name="merge_out",
    )(ya, yb, gates, gates, h, wpa, wpb, wo, g_next.reshape(1, d))


def _pack_weights(w_in, w_uk, w_uv, w_qb, w_iq, w_proj_a, w_proj_b, w_out):
    depth = w_in.shape[0]
    sec = lambda a, b: w_in[:, :, a:b]
    kidx, widx = sec(4096, 4160), sec(4160, 4168)
    z64 = jnp.zeros_like(kidx)
    zpad = jnp.zeros(widx.shape[:2] + (LANES - IDX_HEADS,), w_in.dtype)
    w_small = jnp.concatenate([kidx, z64, z64, kidx, widx, zpad], axis=-1).astype(BF16)
    uk = (w_uk * (HEAD_DIM ** -0.5 * LOG2E)).reshape(depth, B_HEADS // 2, 2, HEAD_DIM, B_KV_RANK)
    zuk = jnp.zeros_like(uk[:, :, 0])
    wuk_bd = jnp.concatenate([jnp.concatenate([uk[:, :, 0], zuk], axis=-1),
                              jnp.concatenate([zuk, uk[:, :, 1]], axis=-1)], axis=-2).astype(BF16)
    uv = w_uv.reshape(depth, B_HEADS // 2, 2, B_KV_RANK, HEAD_DIM)
    zuv = jnp.zeros_like(uv[:, :, 0])
    wuv_pairs = jnp.stack([jnp.concatenate([uv[:, :, 0], zuv], axis=-1),
                           jnp.concatenate([zuv, uv[:, :, 1]], axis=-1)], axis=2).astype(BF16)
    w_iq8 = (w_iq * (IDX_DIM ** -0.5)).astype(BF16)
    return (w_small, wuk_bd, wuv_pairs, w_qb.astype(BF16), w_iq8,
            w_proj_a.astype(BF16), w_proj_b.astype(BF16), w_out.astype(BF16))


def kernel(x, meta_tokens, bias_table, norm_g, w_in, q_norm_g, kv_norm_g, w_qb, w_iq, w_uk, w_uv,
           sinks, w_proj_a, w_proj_b, w_out, final_g):
    n_batch, seq, d = x.shape
    depth = w_in.shape[0]
    length = seq + N_META
    lp = -(-length // QB_B) * QB_B
    assert min(TOPK_MAX, seq // 4) == TOPK_MAX and d == D_MODEL

    (w_small, wuk_bd, wuv_pairs, wqb, wiq, wpa, wpb, wo) = _pack_weights(
        w_in, w_uk, w_uv, w_qb, w_iq, w_proj_a, w_proj_b, w_out)
    w_t = jnp.swapaxes(w_in, 1, 2)
    bias_a, bias_b = _bias_tiles(bias_table)

    meta = jnp.broadcast_to(meta_tokens.astype(x.dtype)[None], (n_batch, N_META, d))
    pad = jnp.zeros((n_batch, lp - length, d), x.dtype)
    h = jnp.concatenate([meta, x, pad], axis=1).reshape(n_batch * lp, d)

    u = _rmsnorm(h, norm_g[0])
    out = None
    for l in range(depth):
        proj = _in_proj(u, w_t, l)
        gates = _gate_proj(u, w_t, l, GATE_COL0, 2 * D_MODEL)
        ya = _mixer_a(proj, sinks[l], bias_a, n_batch, lp)
        ql, qi, wi, ckvn, klo, khi = _bprep(
            proj, u, q_norm_g[l].reshape(1, -1), kv_norm_g[l].reshape(1, -1),
            wqb, wiq, wuk_bd, w_small, l, n_batch, lp)
        yb = _mixer_b(ql, qi, wi, ckvn, klo, khi, proj, bias_b, wuv_pairs, l, n_batch, lp)
        last = l == depth - 1
        g_next = final_g if last else norm_g[l + 1]
        res = _merge(ya, yb, gates, h, wpa, wpb, wo, g_next, l, last)
        if last:
            out = res[0]
        else:
            h, u = res
    return out.reshape(n_batch, lp, d)[:, N_META:length]
```

```python
import functools
import math

import numpy as np
import jax
import jax.numpy as jnp
from jax import lax
from jax.experimental import pallas as pl
from jax.experimental.pallas import tpu as pltpu

D_MODEL = 2048
N_META = 16
WINDOW = 128
HEAD_DIM = 64
A_HEADS = 16
A_WIDTH = 1024
B_HEADS = 16
B_WIDTH = 1024
B_Q_RANK = 512
B_KV_RANK = 256
IDX_HEADS = 8
IDX_DIM = 64
TOPK_MAX = 256
N_BUCKETS = 32
MAX_DISTANCE = 128
EPS = 1e-6
NEG = -1e30

LANES = 128
QB_A = 128
QB_B = 256
SUB_A = 32
SUB_B = 32
L_SAFE = 2.0 ** -100
BIS_UNROLL = 3
LOG2E = math.log2(math.e)
VMEM_LIMIT = 56 * 1024 * 1024
INT_MIN = -(2 ** 31)
INT_MAX = 2 ** 31 - 1

C_QA, C_ZA, C_ZB, C_CQ, C_CKV, C_KA, C_VA = (0, 1024, 2048, 3072, 3584, 3840, 3968)
GATE_COL0 = 4168

F32 = jnp.float32
BF16 = jnp.bfloat16
_NT = (((1,), (1,)), ((), ()))


def _cparams(n_grid):
    return pltpu.CompilerParams(
        dimension_semantics=("arbitrary",) * n_grid,
        vmem_limit_bytes=VMEM_LIMIT)


def _t5_bucket_np(d):
    max_exact = N_BUCKETS // 2
    nf = np.maximum(d, 1).astype(np.float32)
    large = max_exact + (np.log(nf / np.float32(max_exact)) / np.float32(math.log(MAX_DISTANCE / max_exact))
                         * np.float32(N_BUCKETS - max_exact)).astype(np.int32)
    large = np.minimum(large, N_BUCKETS - 1)
    return np.where(d < max_exact, d, large).astype(np.int32)


def _bias_tiles_kernel(tab_ref, idx_ref, out_a_ref, out_b_ref):
    h = pl.program_id(0)
    idx = idx_ref[...]
    acc_a = jnp.zeros(idx.shape, F32)
    acc_b = jnp.zeros(idx.shape, F32)
    far_b = tab_ref[N_BUCKETS - 1, A_HEADS + h]
    for b in range(N_BUCKETS):
        hit = idx == b
        acc_a = jnp.where(hit, tab_ref[b, h], acc_a)
        acc_b = jnp.where(hit, (tab_ref[b, A_HEADS + h] - far_b) * LOG2E, acc_b)
    out_a_ref[0] = acc_a
    out_b_ref[0] = acc_b[:, :2 * LANES]


def _bias_tiles(bias_table):
    r = np.arange(QB_A)[:, None]
    k = np.arange(QB_A)[None, :]
    prev = _t5_bucket_np(np.maximum(QB_A + r - k, 0))
    cur = _t5_bucket_np(np.maximum(r - k, 0))
    far = np.full((QB_A, QB_A), N_BUCKETS - 1, np.int32)
    idx = jnp.asarray(np.concatenate([prev, cur, far], axis=1))
    return pl.pallas_call(
        _bias_tiles_kernel,
        grid=(A_HEADS,),
        in_specs=[pl.BlockSpec(memory_space=pltpu.SMEM),
                  pl.BlockSpec((QB_A, 3 * LANES), lambda h: (0, 0))],
        out_specs=[pl.BlockSpec((1, QB_A, 3 * LANES), lambda h: (h, 0, 0)),
                   pl.BlockSpec((1, QB_A, 2 * LANES), lambda h: (h, 0, 0))],
        out_shape=[jax.ShapeDtypeStruct((A_HEADS, QB_A, 3 * LANES), F32),
                   jax.ShapeDtypeStruct((B_HEADS, QB_A, 2 * LANES), F32)],
        compiler_params=_cparams(1),
        name="bias_tiles",
    )(bias_table, idx)


def _rms(x, g):
    return x * lax.rsqrt(jnp.mean(x * x, axis=-1, keepdims=True) + EPS) * g


def _rmsnorm_kernel(h_ref, g_ref, u_ref):
    u_ref[...] = _rms(h_ref[...], g_ref[...]).astype(u_ref.dtype)


def _rmsnorm(h, g, tm=512):
    rows, d = h.shape
    return pl.pallas_call(
        _rmsnorm_kernel,
        grid=(rows // tm,),
        in_specs=[pl.BlockSpec((tm, d), lambda i: (i, 0)),
                  pl.BlockSpec((1, d), lambda i: (0, 0))],
        out_specs=pl.BlockSpec((tm, d), lambda i: (i, 0)),
        out_shape=jax.ShapeDtypeStruct((rows, d), BF16),
        compiler_params=_cparams(1),
        name="rmsnorm",
    )(h, g.reshape(1, d))


_IN_BLOCKS = ((0, 1, 2, 3), (5, 6, 7, 8), (12, 13, 14, 15), (9, 10, 11, 4))
IN_BLK = 256


def _in_proj_kernel(x_ref, *refs):
    w_refs, o_ref, wbf_scr = refs[:-2], refs[-2], refs[-1]

    @pl.when(pl.program_id(1) == 0)
    def _():
        r0 = 0
        for w_ref in w_refs:
            wbf_scr[r0:r0 + w_ref.shape[0], :] = w_ref[...].astype(BF16)
            r0 += w_ref.shape[0]

    o_ref[...] = lax.dot_general(x_ref[...], wbf_scr[...], _NT, preferred_element_type=F32)


def _in_proj(x, w_t, layer, tm=512):
    rows, kdim = x.shape
    tn = 4 * IN_BLK

    def w_spec(s):
        def index(j, i):
            blk = jnp.int32(_IN_BLOCKS[0][s])
            for t in range(1, len(_IN_BLOCKS)):
                blk = jnp.where(j == t, _IN_BLOCKS[t][s], blk)
            return layer, blk, 0
        return pl.BlockSpec((None, IN_BLK, kdim), index)

    return pl.pallas_call(
        _in_proj_kernel,
        grid=(len(_IN_BLOCKS), rows // tm),
        in_specs=[pl.BlockSpec((tm, kdim), lambda j, i: (i, 0))] + [w_spec(s) for s in range(4)],
        out_specs=pl.BlockSpec((tm, tn), lambda j, i: (i, j)),
        out_shape=jax.ShapeDtypeStruct((rows, len(_IN_BLOCKS) * tn), F32),
        scratch_shapes=[pltpu.VMEM((tn, kdim), BF16)],
        compiler_params=_cparams(2),
        name="in_proj",
    )(x, w_t, w_t, w_t, w_t)


def _gate_proj(x, w_t, layer, col0, n, tm=512, tn=1024):
    rows, kdim = x.shape
    return pl.pallas_call(
        _in_proj_kernel,
        grid=(n // tn, rows // tm),
        in_specs=[pl.BlockSpec((tm, kdim), lambda j, i: (i, 0)),
                  pl.BlockSpec((None, pl.Element(tn), pl.Element(kdim)),
                               lambda j, i: (layer, pl.multiple_of(col0 + j * tn, 8), 0))],
        out_specs=pl.BlockSpec((tm, tn), lambda j, i: (i, j)),
        out_shape=jax.ShapeDtypeStruct((rows, n), F32),
        scratch_shapes=[pltpu.VMEM((tn, kdim), BF16)],
        compiler_params=_cparams(2),
        name="gate_proj",
    )(x, w_t)


def _silu(z):
    return z / (1.0 + jnp.exp(-z))


def _sigmoid(z):
    return 1.0 / (1.0 + jnp.exp(-z))


def _mixer_a_kernel(sink_ref, q_ref, kc_ref, vc_ref, kp_ref, vp_ref, km_ref, vm_ref, za_ref,
                    bias_ref, o_ref, s_scr, p_scr, m_scr, d_scr, msk_scr):
    j = pl.program_id(1)
    nk = 3 * QB_A
    q = (q_ref[...] * (HEAD_DIM ** -0.5)).astype(BF16)
    kcat = jnp.concatenate([kp_ref[...], kc_ref[...], km_ref[...]], axis=0)
    vcat = jnp.concatenate([vp_ref[...], vc_ref[...], vm_ref[...]], axis=0)
    lane = lax.broadcasted_iota(jnp.int32, (nk, LANES), 1)
    lo = lane < HEAD_DIM
    kswap = pltpu.roll(kcat, HEAD_DIM, 1)
    vswap = pltpu.roll(vcat, HEAD_DIM, 1)
    zero = jnp.zeros_like(kcat)
    k_half = ((jnp.where(lo, kcat, zero).astype(BF16), jnp.where(lo, zero, kswap).astype(BF16)),
              (jnp.where(lo, kswap, zero).astype(BF16), jnp.where(lo, zero, kcat).astype(BF16)))
    v_half = ((jnp.where(lo, vcat, zero).astype(BF16), jnp.where(lo, zero, vswap).astype(BF16)),
              (jnp.where(lo, vswap, zero).astype(BF16), jnp.where(lo, zero, vcat).astype(BF16)))

    r = lax.broadcasted_iota(jnp.int32, (QB_A, nk), 0)
    c = lax.broadcasted_iota(jnp.int32, (QB_A, nk), 1)
    jv = jnp.zeros((QB_A, nk), jnp.int32) + j
    in_prev = (c < QB_A) & (jv >= 1) & ((c > r) | ((jv == 1) & (c < N_META)))
    in_cur = (c >= QB_A) & (c < 2 * QB_A) & (c - QB_A <= r)
    in_meta = (c >= 2 * QB_A) & (jv >= 2) & (c - 2 * QB_A < N_META)
    msk_scr[...] = jnp.where(in_prev | in_cur | in_meta, 0.0, NEG)
    half = lax.broadcasted_iota(jnp.int32, (QB_A, LANES), 1) < HEAD_DIM
    pairs = A_HEADS // 4

    def logits(g, i, par, r0):
        hd = 2 * (g * pairs + i) + par
        s = s_scr[i * QB_A + r0:i * QB_A + r0 + SUB_A, par * nk:(par + 1) * nk]
        return s + bias_ref[hd, r0:r0 + SUB_A, :] + msk_scr[r0:r0 + SUB_A, :], sink_ref[hd]

    for g in range(2):
        q4 = jnp.concatenate([q[:, (g * pairs + i) * LANES:(g * pairs + i + 1) * LANES]
                              for i in range(pairs)], axis=0)
        s_scr[...] = lax.dot_general(q4, jnp.concatenate(k_half[g], axis=0), _NT,
                                     preferred_element_type=F32)
        for i in range(pairs):
            for par in range(2):
                for r0 in range(0, QB_A, SUB_A):
                    lg, sink = logits(g, i, par, r0)
                    m = jnp.maximum(jnp.max(lg, axis=-1, keepdims=True), sink)
                    m_scr[par, i * QB_A + r0:i * QB_A + r0 + SUB_A] = jnp.broadcast_to(m, (SUB_A, LANES))
        for i in range(pairs):
            for par in range(2):
                for r0 in range(0, QB_A, SUB_A):
                    sl = slice(i * QB_A + r0, i * QB_A + r0 + SUB_A)
                    lg, sink = logits(g, i, par, r0)
                    m = m_scr[par, sl]
                    p = jnp.exp(lg - jnp.concatenate([m] * 3, axis=1))
                    d_scr[par, sl] = jnp.broadcast_to(
                        jnp.sum(p, axis=-1, keepdims=True), (SUB_A, LANES)) + jnp.exp(sink - m)
                    p_scr[sl, par * nk:(par + 1) * nk] = p.astype(BF16)
        pv = jnp.dot(p_scr[...], jnp.concatenate(v_half[g], axis=0), preferred_element_type=F32)
        for i in range(pairs):
            sl = slice(i * QB_A, (i + 1) * QB_A)
            col = slice((g * pairs + i) * LANES, (g * pairs + i + 1) * LANES)
            o2 = pv[sl] / jnp.where(half, d_scr[0, sl], d_scr[1, sl])
            o_ref[:, col] = (o2 * _silu(za_ref[:, col])).astype(o_ref.dtype)


def _mixer_a(proj, sinks, bias_a, n_batch, lp):
    nb = lp // QB_A
    rows = n_batch * lp
    ck, cv = C_KA // LANES, C_VA // LANES
    cur = lambda b, j: b * nb + j
    prev = lambda b, j: b * nb + jnp.maximum(j - 1, 0)
    first = lambda b, j: b * nb
    return pl.pallas_call(
        _mixer_a_kernel,
        grid=(n_batch, nb),
        in_specs=[
            pl.BlockSpec(memory_space=pltpu.SMEM),
            pl.BlockSpec((QB_A, A_WIDTH), lambda b, j: (cur(b, j), C_QA // A_WIDTH)),
            pl.BlockSpec((QB_A, LANES), lambda b, j: (cur(b, j), ck)),
            pl.BlockSpec((QB_A, LANES), lambda b, j: (cur(b, j), cv)),
            pl.BlockSpec((QB_A, LANES), lambda b, j: (prev(b, j), ck)),
            pl.BlockSpec((QB_A, LANES), lambda b, j: (prev(b, j), cv)),
            pl.BlockSpec((QB_A, LANES), lambda b, j: (first(b, j), ck)),
            pl.BlockSpec((QB_A, LANES), lambda b, j: (first(b, j), cv)),
            pl.BlockSpec((QB_A, A_WIDTH), lambda b, j: (cur(b, j), C_ZA // A_WIDTH)),
            pl.BlockSpec((A_HEADS, QB_A, 3 * LANES), lambda b, j: (0, 0, 0)),
        ],
        out_specs=pl.BlockSpec((QB_A, A_WIDTH), lambda b, j: (cur(b, j), 0)),
        out_shape=jax.ShapeDtypeStruct((rows, A_WIDTH), BF16),
        scratch_shapes=[
            pltpu.VMEM((A_HEADS // 4 * QB_A, 6 * QB_A), F32),
            pltpu.VMEM((A_HEADS // 4 * QB_A, 6 * QB_A), BF16),
            pltpu.VMEM((2, A_HEADS // 4 * QB_A, LANES), F32),
            pltpu.VMEM((2, A_HEADS // 4 * QB_A, LANES), F32),
            pltpu.VMEM((QB_A, 3 * QB_A), F32),
        ],
        compiler_params=_cparams(2),
        name="mixer_a",
    )(sinks, proj, proj, proj, proj, proj, proj, proj, proj, bias_a)


def _bprep_kernel(cq_ref, ckv_ref, u_ref, qg_ref, kvg_ref, wqb_ref, wiq_ref, wuk_ref, wsm_ref,
                  ql_ref, qi_ref, wi_ref, ckvn_ref, klo_ref, khi_ref, kn2_ref):
    cqn = _rms(cq_ref[...], qg_ref[...]).astype(BF16)
    ckvn = _rms(ckv_ref[...], kvg_ref[...]).astype(ckvn_ref.dtype)
    ckvn_ref[0] = ckvn
    kn2 = jnp.sum(jnp.square(ckvn.astype(F32)), axis=-1, keepdims=True)
    kn2_ref[0, 0] = jnp.broadcast_to(jnp.max(kn2, axis=0, keepdims=True), kn2_ref.shape[2:])
    qb = jnp.dot(cqn, wqb_ref[...], preferred_element_type=F32).astype(BF16)
    for pair in range(B_HEADS // 2):
        ql2 = jnp.dot(qb[:, pair * LANES:(pair + 1) * LANES], wuk_ref[pair],
                      preferred_element_type=F32)
        ql_ref[0, 2 * pair] = ql2[:, :B_KV_RANK].astype(ql_ref.dtype)
        ql_ref[0, 2 * pair + 1] = ql2[:, B_KV_RANK:].astype(ql_ref.dtype)
    qi = jnp.dot(cqn, wiq_ref[...], preferred_element_type=F32)
    for pair in range(IDX_HEADS // 2):
        qi_ref[0, pair] = qi[:, pair * LANES:(pair + 1) * LANES].astype(qi_ref.dtype)
    small = jnp.dot(u_ref[...], wsm_ref[...], preferred_element_type=F32)
    klo_ref[0] = small[:, :LANES].astype(klo_ref.dtype)
    khi_ref[0] = small[:, LANES:2 * LANES].astype(khi_ref.dtype)
    wi_ref[0] = small[:, 2 * LANES:] * (IDX_HEADS ** -0.5)


def _bprep(proj, u, qg, kvg, wqb, wiq, wuk_bd, wsmall, layer, n_batch, lp, tm=QB_B):
    nt = lp // tm
    row = lambda b, i: b * nt + i
    const2 = lambda b, i: (0, 0)
    lay2 = lambda b, i: (layer, 0, 0)
    return pl.pallas_call(
        _bprep_kernel,
        grid=(n_batch, nt),
        in_specs=[
            pl.BlockSpec((tm, B_Q_RANK), lambda b, i: (row(b, i), C_CQ // B_Q_RANK)),
            pl.BlockSpec((tm, B_KV_RANK), lambda b, i: (row(b, i), C_CKV // B_KV_RANK)),
            pl.BlockSpec((tm, D_MODEL), lambda b, i: (row(b, i), 0)),
            pl.BlockSpec((1, B_Q_RANK), const2),
            pl.BlockSpec((1, B_KV_RANK), const2),
            pl.BlockSpec((None, B_Q_RANK, B_WIDTH), lay2),
            pl.BlockSpec((None, B_Q_RANK, IDX_HEADS * IDX_DIM), lay2),
            pl.BlockSpec((None, B_HEADS // 2, LANES, 2 * B_KV_RANK), lambda b, i: (layer, 0, 0, 0)),
            pl.BlockSpec((None, D_MODEL, 3 * LANES), lay2),
        ],
        out_specs=[
            pl.BlockSpec((1, B_HEADS, tm, B_KV_RANK), lambda b, i: (b, 0, i, 0)),
            pl.BlockSpec((1, IDX_HEADS // 2, tm, LANES), lambda b, i: (b, 0, i, 0)),
            pl.BlockSpec((1, tm, LANES), lambda b, i: (b, i, 0)),
            pl.BlockSpec((1, tm, B_KV_RANK), lambda b, i: (b, i, 0)),
            pl.BlockSpec((1, tm, LANES), lambda b, i: (b, i, 0)),
            pl.BlockSpec((1, tm, LANES), lambda b, i: (b, i, 0)),
            pl.BlockSpec((1, 1, 8, LANES), lambda b, i: (b, i, 0, 0)),
        ],
        out_shape=[
            jax.ShapeDtypeStruct((n_batch, B_HEADS, lp, B_KV_RANK), BF16),
            jax.ShapeDtypeStruct((n_batch, IDX_HEADS // 2, lp, LANES), BF16),
            jax.ShapeDtypeStruct((n_batch, lp, LANES), F32),
            jax.ShapeDtypeStruct((n_batch, lp, B_KV_RANK), BF16),
            jax.ShapeDtypeStruct((n_batch, lp, LANES), BF16),
            jax.ShapeDtypeStruct((n_batch, lp, LANES), BF16),
            jax.ShapeDtypeStruct((n_batch, nt, 8, LANES), F32),
        ],
        compiler_params=_cparams(2),
        name="bprep",
    )(proj, proj, u, qg, kvg, wqb, wiq, wuk_bd, wsmall)


def _mixer_b_kernel(bmax_ref, ql_ref, qi_ref, wi_ref, ckv_ref, klo_ref, khi_ref, kn2_ref, zb_ref, bias_ref,
                    wuv_ref, o_ref, flag_ref, key_scr, s_scr, acc_scr, m_scr, l_scr, alpha_scr, p_scr,
                    *, bound):
    jq = pl.program_id(1)
    nch = jq + 1
    qb, kc = QB_B, QB_B
    rows = B_HEADS * qb
    kf = float(TOPK_MAX)

    qi = qi_ref[0].reshape(IDX_HEADS // 2 * qb, LANES)
    wit = wi_ref[0].T
    krow = lax.broadcasted_iota(jnp.int32, (qb, kc), 0)
    krow_sub = lax.broadcasted_iota(jnp.int32, (SUB_B, kc), 0)
    qpos = jq * qb + lax.broadcasted_iota(jnp.int32, (qb, kc), 1)

    def idx_body(c, carry):
        start = pl.multiple_of(c * kc, kc)
        sc = (lax.dot_general(klo_ref[0, pl.ds(start, kc), :], qi, _NT, preferred_element_type=F32),
              lax.dot_general(khi_ref[0, pl.ds(start, kc), :], qi, _NT, preferred_element_type=F32))
        isc = jnp.zeros((kc, qb), F32)
        for pair in range(IDX_HEADS // 2):
            for par in range(2):
                hd = 2 * pair + par
                isc = isc + wit[hd:hd + 1, :] * jnp.maximum(sc[par][:, pair * qb:(pair + 1) * qb], 0.0)
        bits = lax.bitcast_convert_type(isc, jnp.int32)
        key = bits ^ ((bits >> 31) & INT_MAX)
        key_scr[c] = jnp.where(c * kc + krow <= qpos, key, INT_MIN)
        return carry

    lax.fori_loop(0, nch, idx_body, 0)

    def reduce_chunks(tile_fn, init, combine):
        def body(c, acc):
            for r0 in range(0, kc, SUB_B):
                t = tile_fn(c, r0, key_scr[c, r0:r0 + SUB_B, :])
                t = t.reshape(SUB_B // 8, 8, qb)
                for i in range(SUB_B // 8):
                    acc = combine(acc, t[i])
            return acc
        return lax.fori_loop(0, nch, body, init)

    def count(pred_fn):
        acc = reduce_chunks(lambda c, r0, k: jnp.where(pred_fn(c, r0, k), 1.0, 0.0),
                            jnp.zeros((8, qb), F32), lambda a, b: a + b)
        return jnp.sum(acc, axis=0, keepdims=True)

    def unkey(k):
        return lax.bitcast_convert_type(k ^ ((k >> 31) & INT_MAX), F32)

    def tokey(v):
        b = lax.bitcast_convert_type(v, jnp.int32)
        return b ^ ((b >> 31) & INT_MAX)

    def select_all():
        return jnp.full((1, qb), INT_MIN, jnp.int32), jnp.full((1, qb), -1, jnp.int32)

    def select_topk():
        kmax = reduce_chunks(lambda c, r0, k: k, jnp.full((8, qb), INT_MIN, jnp.int32), jnp.maximum)
        kmax = jnp.max(kmax, axis=0, keepdims=True)
        kmin = reduce_chunks(lambda c, r0, k: jnp.where(k == INT_MIN, INT_MAX, k),
                             jnp.full((8, qb), INT_MAX, jnp.int32), jnp.minimum)
        kmin = jnp.min(kmin, axis=0, keepdims=True)
        c0 = count(lambda c, r0, k: k >= 0)
        c0p = count(lambda c, r0, k: k >= 1)
        n_valid = (jq * qb + 1 + lax.broadcasted_iota(jnp.int32, (1, qb), 1)).astype(F32)
        pos = c0p >= kf
        zero = (c0 >= kf) & (c0p < kf)
        lo0 = jnp.where(pos, 1, jnp.where(zero, 0, kmin))
        hi0 = jnp.where(pos, kmax + 1, jnp.where(zero, 1, 0))
        c_lo0 = jnp.where(pos, c0p, jnp.where(zero, c0, n_valid))
        c_hi0 = jnp.where(pos, 0.0, jnp.where(zero, c0p, c0))

        def bis_cond(st):
            return (st[0] < 64) & (st[5] > 0)

        def bis_step(it, lo, hi, c_lo, c_hi):
            mid_i = (lo >> 1) + (hi >> 1) + (lo & hi & 1)
            mid_v = tokey(0.5 * unkey(lo) + 0.5 * unkey(hi))
            use_v = (it < 24) & (mid_v > lo) & (mid_v < hi)
            mid = jnp.where(use_v, mid_v, mid_i)
            cnt = count(lambda c, r0, k: k >= mid)
            ge = cnt >= kf
            return (jnp.where(ge, mid, lo), jnp.where(ge, hi, mid),
                    jnp.where(ge, cnt, c_lo), jnp.where(ge, c_hi, cnt))

        def bis_body(st):
            it, lo, hi, c_lo, c_hi, _ = st
            for _ in range(BIS_UNROLL):
                lo, hi, c_lo, c_hi = bis_step(it, lo, hi, c_lo, c_hi)
                it = it + 1
            active = (lo + 1 < hi) & (c_lo != kf)
            return it, lo, hi, c_lo, c_hi, jnp.max(jnp.where(active, 1, 0))

        active0 = (lo0 + 1 < hi0) & (c_lo0 != kf)
        st0 = (jnp.int32(0), lo0, hi0, c_lo0, c_hi0, jnp.max(jnp.where(active0, 1, 0)))
        _, lo, hi, c_lo, c_hi, _ = lax.while_loop(bis_cond, bis_body, st0)
        tie = c_lo != kf

        def tie_break():
            need = kf - c_hi

            def tie_body(_, st):
                jlo, jhi = st
                mid = (jlo + jhi) >> 1
                cnt = count(lambda c, r0, k: (k == lo) & (c * kc + r0 + krow_sub <= mid))
                ge = cnt >= need
                return jnp.where(ge, jlo, mid), jnp.where(ge, mid, jhi)

            _, jhi = lax.fori_loop(0, 13, tie_body, (jnp.full((1, qb), -1, jnp.int32),
                                                     jnp.full((1, qb), 8191, jnp.int32)))
            return jnp.where(tie, jhi, INT_MAX)

        jmax = lax.cond(jnp.max(jnp.where(tie, 1, 0)) > 0, tie_break,
                        lambda: jnp.full((1, qb), INT_MAX, jnp.int32))
        return lo, jmax

    thr, jmax = lax.cond(jq >= 1, select_topk, select_all)

    def sel_body(c, carry):
        k = key_scr[c]
        sel = (k > thr) | ((k == thr) & (c * kc + krow <= jmax))
        key_scr[c] = lax.bitcast_convert_type(jnp.where(sel, 0.0, NEG).T, jnp.int32)
        return carry

    lax.fori_loop(0, nch, sel_body, 0)

    l_scr[...] = jnp.zeros(l_scr.shape, F32)
    acc_scr[...] = jnp.zeros(acc_scr.shape, F32)

    def kv_chunk(c):
        return ckv_ref[0, pl.ds(pl.multiple_of(c * kc, kc), kc), :]

    def qk(c, slot):
        s_scr[slot] = lax.dot_general(ql_ref[0].reshape(rows, B_KV_RANK), kv_chunk(c), _NT,
                                      preferred_element_type=F32)

    def logits(c, slot, near, hd, r0):
        sh = s_scr[slot, hd * qb + r0:hd * qb + r0 + SUB_B]
        if near == "cur":
            t0 = bias_ref[hd, r0 % QB_A:r0 % QB_A + SUB_B, LANES:]
            t1 = bias_ref[hd, r0 % QB_A:r0 % QB_A + SUB_B, :LANES]
            sh = sh + jnp.concatenate([t0 if r0 < QB_A else t1, t0], axis=1)
        elif near == "prev" and r0 < QB_A:
            t1 = bias_ref[hd, r0:r0 + SUB_B, :LANES]
            sh = sh + jnp.concatenate([jnp.zeros_like(t1), t1], axis=1)
        return sh + lax.bitcast_convert_type(key_scr[c, r0:r0 + SUB_B, :], F32)

    def sub_tiles():
        for hd in range(B_HEADS):
            for r0 in range(0, qb, SUB_B):
                yield hd, r0, slice(hd * qb + r0, hd * qb + r0 + SUB_B)

    if bound:
        kn2 = lax.fori_loop(0, nch, lambda c, a: jnp.maximum(a, kn2_ref[0, c]), jnp.zeros((8, LANES), F32))
        bmax = bmax_ref[0]
        for hd, r0, sl in sub_tiles():
            x = ql_ref[0, hd, r0:r0 + SUB_B, :].astype(F32)
            qn2 = jnp.sum(x * x, axis=-1, keepdims=True)
            m_scr[sl] = jnp.sqrt(qn2 * kn2[:1]) * (1.0 + 2.0 ** -10) + bmax

        def sm(c, slot, near):
            for hd, r0, sl in sub_tiles():
                m = m_scr[sl]
                p = jnp.exp2(logits(c, slot, near, hd, r0) - jnp.concatenate([m, m], axis=1))
                l_scr[sl] += p[:, :LANES] + p[:, LANES:]
                p_scr[slot, sl] = p.astype(BF16)

        def pv(c, slot):
            o = jnp.dot(p_scr[slot], kv_chunk(c), preferred_element_type=F32)
            for r0 in range(0, rows, SUB_B):
                acc_scr[r0:r0 + SUB_B] += o[r0:r0 + SUB_B]

        def stage(t, slot):
            qk(t, slot)
            pv(t - 2, slot)
            sm(t - 1, 1 - slot, None)

        def tail(slot):
            qk(jq, slot)
            pv(jq - 2, slot)
            sm(jq - 1, 1 - slot, "prev")
            pv(jq - 1, 1 - slot)
            sm(jq, slot, "cur")
            pv(jq, slot)

        @pl.when(jq >= 2)
        def _():
            qk(0, 0)
            qk(1, 1)
            sm(0, 0, None)

            def two_stages(i, carry):
                stage(2 + 2 * i, 0)
                stage(3 + 2 * i, 1)
                return carry

            lax.fori_loop(0, (jq - 2) // 2, two_stages, 0)

            @pl.when(jq % 2 == 0)
            def _():
                tail(0)

            @pl.when(jq % 2 == 1)
            def _():
                stage(jq - 1, 0)
                tail(1)

        @pl.when(jq < 2)
        def _():
            @pl.when(jq == 1)
            def _():
                qk(0, 0)
                sm(0, 0, "prev")
                pv(0, 0)

            qk(jq, 1)
            sm(jq, 1, "cur")
            pv(jq, 1)
    else:
        m_scr[...] = jnp.full(m_scr.shape, -jnp.inf, F32)

        def attend(c, near):
            qk(c, 0)
            for hd, r0, sl in sub_tiles():
                m_old = m_scr[sl]
                m_new = jnp.maximum(m_old, jnp.max(logits(c, 0, near, hd, r0), axis=-1, keepdims=True))
                m_scr[sl] = m_new
                alpha_scr[sl] = jnp.exp2(m_old - m_new)
            for hd, r0, sl in sub_tiles():
                m_new = m_scr[sl]
                p = jnp.exp2(logits(c, 0, near, hd, r0) - jnp.concatenate([m_new, m_new], axis=1))
                l_scr[sl] = alpha_scr[sl] * l_scr[sl] + (p[:, :LANES] + p[:, LANES:])
                p_scr[0, sl] = p.astype(BF16)
            o = jnp.dot(p_scr[0], kv_chunk(c), preferred_element_type=F32)
            for r0 in range(0, rows, SUB_B):
                sl = slice(r0, r0 + SUB_B)
                alpha = alpha_scr[sl]
                acc_scr[sl] = acc_scr[sl] * jnp.concatenate([alpha, alpha], axis=1) + o[sl]

        def far_body(c, carry):
            attend(c, None)
            return carry

        lax.fori_loop(0, jq - 1, far_body, 0)

        @pl.when(jq >= 1)
        def _():
            attend(jq - 1, "prev")

        attend(jq, "cur")

    l_min = jnp.full((qb, 1), jnp.inf, F32)
    for pair in range(B_HEADS // 2):
        y2 = jnp.zeros((qb, LANES), F32)
        for par in range(2):
            hd = 2 * pair + par
            sl = slice(hd * qb, (hd + 1) * qb)
            l = jnp.sum(l_scr[sl], axis=-1, keepdims=True)
            l_min = jnp.minimum(l_min, l)
            o_lat = acc_scr[sl] / l
            y2 = y2 + jnp.dot(o_lat.astype(BF16), wuv_ref[pair, par], preferred_element_type=F32)
        z = zb_ref[:, pair * LANES:(pair + 1) * LANES]
        o_ref[:, pair * LANES:(pair + 1) * LANES] = (y2 * _silu(z)).astype(o_ref.dtype)
    safe = jnp.min(l_min) >= L_SAFE
    flag_ref[0, 0] = jnp.full(flag_ref.shape[2:], jnp.where(safe, 0.0, 1.0), F32)


def _mixer_b(bmax, ql, qi, wi, ckvn, klo, khi, kn2, proj, bias_b, wuv_pairs, layer, n_batch, lp, bound):
    nb = lp // QB_B
    rows = n_batch * lp
    whole = lambda b, j: (b, 0, 0)
    once = pl.Buffered(1)
    return pl.pallas_call(
        functools.partial(_mixer_b_kernel, bound=bound),
        grid=(n_batch, nb),
        in_specs=[
            pl.BlockSpec(memory_space=pltpu.SMEM),
            pl.BlockSpec((1, B_HEADS, QB_B, B_KV_RANK), lambda b, j: (b, 0, j, 0)),
            pl.BlockSpec((1, IDX_HEADS // 2, QB_B, LANES), lambda b, j: (b, 0, j, 0)),
            pl.BlockSpec((1, QB_B, LANES), lambda b, j: (b, j, 0)),
            pl.BlockSpec((1, lp, B_KV_RANK), whole, pipeline_mode=once),
            pl.BlockSpec((1, lp, LANES), whole, pipeline_mode=once),
            pl.BlockSpec((1, lp, LANES), whole, pipeline_mode=once),
            pl.BlockSpec((1, nb, 8, LANES), lambda b, j: (b, 0, 0, 0), pipeline_mode=once),
            pl.BlockSpec((QB_B, B_WIDTH), lambda b, j: (b * nb + j, C_ZB // B_WIDTH)),
            pl.BlockSpec((B_HEADS, QB_A, 2 * LANES), lambda b, j: (0, 0, 0), pipeline_mode=once),
            pl.BlockSpec((None, B_HEADS // 2, 2, B_KV_RANK, LANES), lambda b, j: (layer, 0, 0, 0, 0),
                         pipeline_mode=once),
        ],
        out_specs=[pl.BlockSpec((QB_B, B_WIDTH), lambda b, j: (b * nb + j, 0)),
                   pl.BlockSpec((1, 1, 8, LANES), lambda b, j: (b, j, 0, 0))],
        out_shape=[jax.ShapeDtypeStruct((rows, B_WIDTH), BF16),
                   jax.ShapeDtypeStruct((n_batch, nb, 8, LANES), F32)],
        scratch_shapes=[
            pltpu.VMEM((nb, QB_B, QB_B), jnp.int32),
            pltpu.VMEM((2, B_HEADS * QB_B, QB_B), F32),
            pltpu.VMEM((B_HEADS * QB_B, B_KV_RANK), F32),
            pltpu.VMEM((B_HEADS * QB_B, LANES), F32),
            pltpu.VMEM((B_HEADS * QB_B, LANES), F32),
            pltpu.VMEM((B_HEADS * QB_B, LANES), F32),
            pltpu.VMEM((2, B_HEADS * QB_B, QB_B), BF16),
        ],
        compiler_params=_cparams(2),
        name="mixer_b" if bound else "mixer_b_running_max",
    )(bmax, ql, qi, wi, ckvn, klo, khi, kn2, proj, bias_b, wuv_pairs)


def _merge_kernel(ya_ref, yb_ref, ga_ref, gb_ref, h_ref, wpa_ref, wpb_ref, wo_ref, g_ref, *out_refs,
                  last):
    pa = jnp.dot(ya_ref[...], wpa_ref[...], preferred_element_type=F32)
    pb = jnp.dot(yb_ref[...], wpb_ref[...], preferred_element_type=F32)
    merged = _sigmoid(ga_ref[...]) * pa + _sigmoid(gb_ref[...]) * pb
    h_new = h_ref[...] + jnp.dot(merged.astype(BF16), wo_ref[...], preferred_element_type=F32)
    normed = _rms(h_new, g_ref[...])
    if last:
        out_refs[0][...] = normed
    else:
        out_refs[0][...] = h_new
        out_refs[1][...] = normed.astype(BF16)


def _merge(ya, yb, gates, h, wpa, wpb, wo, g_next, layer, last, tm=256):
    rows, d = h.shape
    const = lambda i: (0, 0)
    lay = lambda i: (layer, 0, 0)
    once = pl.Buffered(1)
    if last:
        out_specs = [pl.BlockSpec((tm, d), lambda i: (i, 0))]
        out_shape = [jax.ShapeDtypeStruct((rows, d), F32)]
    else:
        out_specs = [pl.BlockSpec((tm, d), lambda i: (i, 0)), pl.BlockSpec((tm, d), lambda i: (i, 0))]
        out_shape = [jax.ShapeDtypeStruct((rows, d), F32), jax.ShapeDtypeStruct((rows, d), BF16)]
    return pl.pallas_call(
        functools.partial(_merge_kernel, last=last),
        grid=(rows // tm,),
        in_specs=[
            pl.BlockSpec((tm, A_WIDTH), lambda i: (i, 0)),
            pl.BlockSpec((tm, B_WIDTH), lambda i: (i, 0)),
            pl.BlockSpec((tm, d), lambda i: (i, 0)),
            pl.BlockSpec((tm, d), lambda i: (i, 1)),
            pl.BlockSpec((tm, d), lambda i: (i, 0)),
            pl.BlockSpec((None, A_WIDTH, d), lay, pipeline_mode=once),
            pl.BlockSpec((None, B_WIDTH, d), lay, pipeline_mode=once),
            pl.BlockSpec((None, d, d), lay, pipeline_mode=once),
            pl.BlockSpec((1, d), const),
        ],
        out_specs=out_specs,
        out_shape=out_shape,
        compiler_params=_cparams(1),
        name="merge_out",
    )(ya, yb, gates, gates, h, wpa, wpb, wo, g_next.reshape(1, d))


def _pack_weights(w_in, w_uk, w_uv, w_qb, w_iq, w_proj_a, w_proj_b, w_out):
    depth = w_in.shape[0]
    sec = lambda a, b: w_in[:, :, a:b]
    kidx, widx = sec(4096, 4160), sec(4160, 4168)
    z64 = jnp.zeros_like(kidx)
    zpad = jnp.zeros(widx.shape[:2] + (LANES - IDX_HEADS,), w_in.dtype)
    w_small = jnp.concatenate([kidx, z64, z64, kidx, widx, zpad], axis=-1).astype(BF16)
    uk = (w_uk * (HEAD_DIM ** -0.5 * LOG2E)).reshape(depth, B_HEADS // 2, 2, HEAD_DIM, B_KV_RANK)
    zuk = jnp.zeros_like(uk[:, :, 0])
    wuk_bd = jnp.concatenate([jnp.concatenate([uk[:, :, 0], zuk], axis=-1),
                              jnp.concatenate([zuk, uk[:, :, 1]], axis=-1)], axis=-2).astype(BF16)
    uv = w_uv.reshape(depth, B_HEADS // 2, 2, B_KV_RANK, HEAD_DIM)
    zuv = jnp.zeros_like(uv[:, :, 0])
    wuv_pairs = jnp.stack([jnp.concatenate([uv[:, :, 0], zuv], axis=-1),
                           jnp.concatenate([zuv, uv[:, :, 1]], axis=-1)], axis=2).astype(BF16)
    w_iq8 = (w_iq * (IDX_DIM ** -0.5)).astype(BF16)
    return (w_small, wuk_bd, wuv_pairs, w_qb.astype(BF16), w_iq8,
            w_proj_a.astype(BF16), w_proj_b.astype(BF16), w_out.astype(BF16))


def kernel(x, meta_tokens, bias_table, norm_g, w_in, q_norm_g, kv_norm_g, w_qb, w_iq, w_uk, w_uv,
           sinks, w_proj_a, w_proj_b, w_out, final_g):
    n_batch, seq, d = x.shape
    depth = w_in.shape[0]
    length = seq + N_META
    lp = -(-length // QB_B) * QB_B
    assert min(TOPK_MAX, seq // 4) == TOPK_MAX and d == D_MODEL

    (w_small, wuk_bd, wuv_pairs, wqb, wiq, wpa, wpb, wo) = _pack_weights(
        w_in, w_uk, w_uv, w_qb, w_iq, w_proj_a, w_proj_b, w_out)
    w_t = jnp.swapaxes(w_in, 1, 2)
    bias_a, bias_b = _bias_tiles(bias_table)
    bmax = jnp.max(jnp.abs(bias_b)).reshape(1)

    meta = jnp.broadcast_to(meta_tokens.astype(x.dtype)[None], (n_batch, N_META, d))
    pad = jnp.zeros((n_batch, lp - length, d), x.dtype)
    h = jnp.concatenate([meta, x, pad], axis=1).reshape(n_batch * lp, d)

    u = _rmsnorm(h, norm_g[0])
    out = None
    for l in range(depth):
        proj = _in_proj(u, w_t, l)
        gates = _gate_proj(u, w_t, l, GATE_COL0, 2 * D_MODEL)
        ya = _mixer_a(proj, sinks[l], bias_a, n_batch, lp)
        ql, qi, wi, ckvn, klo, khi, kn2 = _bprep(
            proj, u, q_norm_g[l].reshape(1, -1), kv_norm_g[l].reshape(1, -1),
            wqb, wiq, wuk_bd, w_small, l, n_batch, lp)
        b_args = (bmax, ql, qi, wi, ckvn, klo, khi, kn2, proj, bias_b, wuv_pairs, l, n_batch, lp)
        yb, flags = _mixer_b(*b_args, bound=True)
        yb = lax.cond(jnp.max(flags) > 0.0, lambda: _mixer_b(*b_args, bound=False)[0], lambda: yb)
        last = l == depth - 1
        g_next = final_g if last else norm_g[l + 1]
        res = _merge(ya, yb, gates, h, wpa, wpb, wo, g_next, l, last)
        if last:
            out = res[0]
        else:
            h, u = res
    return out.reshape(n_batch, lp, d)[:, N_META:length]
```

```python
import functools
import math

import numpy as np
import jax
import jax.numpy as jnp
from jax import lax
from jax.experimental import pallas as pl
from jax.experimental.pallas import tpu as pltpu

D_MODEL = 2048
N_META = 16
WINDOW = 128
HEAD_DIM = 64
A_HEADS = 16
A_WIDTH = 1024
B_HEADS = 16
B_WIDTH = 1024
B_Q_RANK = 512
B_KV_RANK = 256
IDX_HEADS = 8
IDX_DIM = 64
TOPK_MAX = 256
N_BUCKETS = 32
MAX_DISTANCE = 128
EPS = 1e-6
NEG = -1e30

LANES = 128
QB_A = 128
QB_B = 256
SUB_A = 32
SUB_B = 32
L_SAFE = 2.0 ** -100
BIS_UNROLL = 3
LOG2E = math.log2(math.e)
VMEM_LIMIT = 56 * 1024 * 1024
INT_MIN = -(2 ** 31)
INT_MAX = 2 ** 31 - 1

C_QA, C_ZA, C_ZB, C_CQ, C_CKV, C_KA, C_VA = (0, 1024, 2048, 3072, 3584, 3840, 3968)
GATE_COL0 = 4168

F32 = jnp.float32
BF16 = jnp.bfloat16
_NT = (((1,), (1,)), ((), ()))


def _cparams(n_grid):
    return pltpu.CompilerParams(
        dimension_semantics=("arbitrary",) * n_grid,
        vmem_limit_bytes=VMEM_LIMIT)


def _t5_bucket_np(d):
    max_exact = N_BUCKETS // 2
    nf = np.maximum(d, 1).astype(np.float32)
    large = max_exact + (np.log(nf / np.float32(max_exact)) / np.float32(math.log(MAX_DISTANCE / max_exact))
                         * np.float32(N_BUCKETS - max_exact)).astype(np.int32)
    large = np.minimum(large, N_BUCKETS - 1)
    return np.where(d < max_exact, d, large).astype(np.int32)


def _bias_tiles_kernel(tab_ref, idx_ref, out_a_ref, out_b_ref):
    h = pl.program_id(0)
    idx = idx_ref[...]
    acc_a = jnp.zeros(idx.shape, F32)
    acc_b = jnp.zeros(idx.shape, F32)
    far_b = tab_ref[N_BUCKETS - 1, A_HEADS + h]
    for b in range(N_BUCKETS):
        hit = idx == b
        acc_a = jnp.where(hit, tab_ref[b, h], acc_a)
        acc_b = jnp.where(hit, (tab_ref[b, A_HEADS + h] - far_b) * LOG2E, acc_b)
    out_a_ref[0] = acc_a
    out_b_ref[0] = acc_b[:, :2 * LANES]


def _bias_tiles(bias_table):
    r = np.arange(QB_A)[:, None]
    k = np.arange(QB_A)[None, :]
    prev = _t5_bucket_np(np.maximum(QB_A + r - k, 0))
    cur = _t5_bucket_np(np.maximum(r - k, 0))
    far = np.full((QB_A, QB_A), N_BUCKETS - 1, np.int32)
    idx = jnp.asarray(np.concatenate([prev, cur, far], axis=1))
    return pl.pallas_call(
        _bias_tiles_kernel,
        grid=(A_HEADS,),
        in_specs=[pl.BlockSpec(memory_space=pltpu.SMEM),
                  pl.BlockSpec((QB_A, 3 * LANES), lambda h: (0, 0))],
        out_specs=[pl.BlockSpec((1, QB_A, 3 * LANES), lambda h: (h, 0, 0)),
                   pl.BlockSpec((1, QB_A, 2 * LANES), lambda h: (h, 0, 0))],
        out_shape=[jax.ShapeDtypeStruct((A_HEADS, QB_A, 3 * LANES), F32),
                   jax.ShapeDtypeStruct((B_HEADS, QB_A, 2 * LANES), F32)],
        compiler_params=_cparams(1),
        name="bias_tiles",
    )(bias_table, idx)


def _rms(x, g):
    return x * lax.rsqrt(jnp.mean(x * x, axis=-1, keepdims=True) + EPS) * g


def _rmsnorm_kernel(h_ref, g_ref, u_ref):
    u_ref[...] = _rms(h_ref[...], g_ref[...]).astype(u_ref.dtype)


def _rmsnorm(h, g, tm=512):
    rows, d = h.shape
    return pl.pallas_call(
        _rmsnorm_kernel,
        grid=(rows // tm,),
        in_specs=[pl.BlockSpec((tm, d), lambda i: (i, 0)),
                  pl.BlockSpec((1, d), lambda i: (0, 0))],
        out_specs=pl.BlockSpec((tm, d), lambda i: (i, 0)),
        out_shape=jax.ShapeDtypeStruct((rows, d), BF16),
        compiler_params=_cparams(1),
        name="rmsnorm",
    )(h, g.reshape(1, d))


_IN_BLOCKS = ((0, 1, 2, 3), (5, 6, 7, 8), (12, 13, 14, 15), (9, 10, 11, 4))
IN_BLK = 256


def _in_proj_kernel(x_ref, *refs):
    w_refs, o_ref, wbf_scr = refs[:-2], refs[-2], refs[-1]

    @pl.when(pl.program_id(1) == 0)
    def _():
        r0 = 0
        for w_ref in w_refs:
            wbf_scr[r0:r0 + w_ref.shape[0], :] = w_ref[...].astype(BF16)
            r0 += w_ref.shape[0]

    o_ref[...] = lax.dot_general(x_ref[...], wbf_scr[...], _NT, preferred_element_type=F32)


def _in_proj(x, w_t, layer, tm=512):
    rows, kdim = x.shape
    tn = 4 * IN_BLK

    def w_spec(s):
        def index(j, i):
            blk = jnp.int32(_IN_BLOCKS[0][s])
            for t in range(1, len(_IN_BLOCKS)):
                blk = jnp.where(j == t, _IN_BLOCKS[t][s], blk)
            return layer, blk, 0
        return pl.BlockSpec((None, IN_BLK, kdim), index)

    return pl.pallas_call(
        _in_proj_kernel,
        grid=(len(_IN_BLOCKS), rows // tm),
        in_specs=[pl.BlockSpec((tm, kdim), lambda j, i: (i, 0))] + [w_spec(s) for s in range(4)],
        out_specs=pl.BlockSpec((tm, tn), lambda j, i: (i, j)),
        out_shape=jax.ShapeDtypeStruct((rows, len(_IN_BLOCKS) * tn), F32),
        scratch_shapes=[pltpu.VMEM((tn, kdim), BF16)],
        compiler_params=_cparams(2),
        name="in_proj",
    )(x, w_t, w_t, w_t, w_t)


def _gate_proj(x, w_t, layer, col0, n, tm=512, tn=1024):
    rows, kdim = x.shape
    return pl.pallas_call(
        _in_proj_kernel,
        grid=(n // tn, rows // tm),
        in_specs=[pl.BlockSpec((tm, kdim), lambda j, i: (i, 0)),
                  pl.BlockSpec((None, pl.Element(tn), pl.Element(kdim)),
                               lambda j, i: (layer, pl.multiple_of(col0 + j * tn, 8), 0))],
        out_specs=pl.BlockSpec((tm, tn), lambda j, i: (i, j)),
        out_shape=jax.ShapeDtypeStruct((rows, n), F32),
        scratch_shapes=[pltpu.VMEM((tn, kdim), BF16)],
        compiler_params=_cparams(2),
        name="gate_proj",
    )(x, w_t)


def _silu(z):
    return z / (1.0 + jnp.exp(-z))


def _sigmoid(z):
    return 1.0 / (1.0 + jnp.exp(-z))


def _mixer_a_kernel(sink_ref, q_ref, kc_ref, vc_ref, kp_ref, vp_ref, km_ref, vm_ref, za_ref,
                    bias_ref, o_ref, s_scr, p_scr, m_scr, d_scr, msk_scr):
    j = pl.program_id(1)
    nk = 3 * QB_A
    q = (q_ref[...] * (HEAD_DIM ** -0.5)).astype(BF16)
    kcat = jnp.concatenate([kp_ref[...], kc_ref[...], km_ref[...]], axis=0)
    vcat = jnp.concatenate([vp_ref[...], vc_ref[...], vm_ref[...]], axis=0)
    lane = lax.broadcasted_iota(jnp.int32, (nk, LANES), 1)
    lo = lane < HEAD_DIM
    kswap = pltpu.roll(kcat, HEAD_DIM, 1)
    vswap = pltpu.roll(vcat, HEAD_DIM, 1)
    zero = jnp.zeros_like(kcat)
    k_half = ((jnp.where(lo, kcat, zero).astype(BF16), jnp.where(lo, zero, kswap).astype(BF16)),
              (jnp.where(lo, kswap, zero).astype(BF16), jnp.where(lo, zero, kcat).astype(BF16)))
    v_half = ((jnp.where(lo, vcat, zero).astype(BF16), jnp.where(lo, zero, vswap).astype(BF16)),
              (jnp.where(lo, vswap, zero).astype(BF16), jnp.where(lo, zero, vcat).astype(BF16)))

    r = lax.broadcasted_iota(jnp.int32, (QB_A, nk), 0)
    c = lax.broadcasted_iota(jnp.int32, (QB_A, nk), 1)
    jv = jnp.zeros((QB_A, nk), jnp.int32) + j
    in_prev = (c < QB_A) & (jv >= 1) & ((c > r) | ((jv == 1) & (c < N_META)))
    in_cur = (c >= QB_A) & (c < 2 * QB_A) & (c - QB_A <= r)
    in_meta = (c >= 2 * QB_A) & (jv >= 2) & (c - 2 * QB_A < N_META)
    msk_scr[...] = jnp.where(in_prev | in_cur | in_meta, 0.0, NEG)
    half = lax.broadcasted_iota(jnp.int32, (QB_A, LANES), 1) < HEAD_DIM
    pairs = A_HEADS // 4

    def logits(g, i, par, r0):
        hd = 2 * (g * pairs + i) + par
        s = s_scr[i * QB_A + r0:i * QB_A + r0 + SUB_A, par * nk:(par + 1) * nk]
        return s + bias_ref[hd, r0:r0 + SUB_A, :] + msk_scr[r0:r0 + SUB_A, :], sink_ref[hd]

    for g in range(2):
        q4 = jnp.concatenate([q[:, (g * pairs + i) * LANES:(g * pairs + i + 1) * LANES]
                              for i in range(pairs)], axis=0)
        s_scr[...] = lax.dot_general(q4, jnp.concatenate(k_half[g], axis=0), _NT,
                                     preferred_element_type=F32)
        for i in range(pairs):
            for par in range(2):
                for r0 in range(0, QB_A, SUB_A):
                    lg, sink = logits(g, i, par, r0)
                    m = jnp.maximum(jnp.max(lg, axis=-1, keepdims=True), sink)
                    m_scr[par, i * QB_A + r0:i * QB_A + r0 + SUB_A] = jnp.broadcast_to(m, (SUB_A, LANES))
        for i in range(pairs):
            for par in range(2):
                for r0 in range(0, QB_A, SUB_A):
                    sl = slice(i * QB_A + r0, i * QB_A + r0 + SUB_A)
                    lg, sink = logits(g, i, par, r0)
                    m = m_scr[par, sl]
                    p = jnp.exp(lg - jnp.concatenate([m] * 3, axis=1))
                    d_scr[par, sl] = jnp.broadcast_to(
                        jnp.sum(p, axis=-1, keepdims=True), (SUB_A, LANES)) + jnp.exp(sink - m)
                    p_scr[sl, par * nk:(par + 1) * nk] = p.astype(BF16)
        pv = jnp.dot(p_scr[...], jnp.concatenate(v_half[g], axis=0), preferred_element_type=F32)
        for i in range(pairs):
            sl = slice(i * QB_A, (i + 1) * QB_A)
            col = slice((g * pairs + i) * LANES, (g * pairs + i + 1) * LANES)
            o2 = pv[sl] / jnp.where(half, d_scr[0, sl], d_scr[1, sl])
            o_ref[:, col] = (o2 * _silu(za_ref[:, col])).astype(o_ref.dtype)


def _mixer_a(proj, sinks, bias_a, n_batch, lp):
    nb = lp // QB_A
    rows = n_batch * lp
    ck, cv = C_KA // LANES, C_VA // LANES
    cur = lambda b, j: b * nb + j
    prev = lambda b, j: b * nb + jnp.maximum(j - 1, 0)
    first = lambda b, j: b * nb
    return pl.pallas_call(
        _mixer_a_kernel,
        grid=(n_batch, nb),
        in_specs=[
            pl.BlockSpec(memory_space=pltpu.SMEM),
            pl.BlockSpec((QB_A, A_WIDTH), lambda b, j: (cur(b, j), C_QA // A_WIDTH)),
            pl.BlockSpec((QB_A, LANES), lambda b, j: (cur(b, j), ck)),
            pl.BlockSpec((QB_A, LANES), lambda b, j: (cur(b, j), cv)),
            pl.BlockSpec((QB_A, LANES), lambda b, j: (prev(b, j), ck)),
            pl.BlockSpec((QB_A, LANES), lambda b, j: (prev(b, j), cv)),
            pl.BlockSpec((QB_A, LANES), lambda b, j: (first(b, j), ck)),
            pl.BlockSpec((QB_A, LANES), lambda b, j: (first(b, j), cv)),
            pl.BlockSpec((QB_A, A_WIDTH), lambda b, j: (cur(b, j), C_ZA // A_WIDTH)),
            pl.BlockSpec((A_HEADS, QB_A, 3 * LANES), lambda b, j: (0, 0, 0)),
        ],
        out_specs=pl.BlockSpec((QB_A, A_WIDTH), lambda b, j: (cur(b, j), 0)),
        out_shape=jax.ShapeDtypeStruct((rows, A_WIDTH), BF16),
        scratch_shapes=[
            pltpu.VMEM((A_HEADS // 4 * QB_A, 6 * QB_A), F32),
            pltpu.VMEM((A_HEADS // 4 * QB_A, 6 * QB_A), BF16),
            pltpu.VMEM((2, A_HEADS // 4 * QB_A, LANES), F32),
            pltpu.VMEM((2, A_HEADS // 4 * QB_A, LANES), F32),
            pltpu.VMEM((QB_A, 3 * QB_A), F32),
        ],
        compiler_params=_cparams(2),
        name="mixer_a",
    )(sinks, proj, proj, proj, proj, proj, proj, proj, proj, bias_a)


def _bprep_kernel(cq_ref, ckv_ref, u_ref, qg_ref, kvg_ref, wqb_ref, wiq_ref, wuk_ref, wsm_ref,
                  ql_ref, qi_ref, wi_ref, ckvn_ref, klo_ref, khi_ref, kn2_ref):
    cqn = _rms(cq_ref[...], qg_ref[...]).astype(BF16)
    ckvn = _rms(ckv_ref[...], kvg_ref[...]).astype(ckvn_ref.dtype)
    ckvn_ref[0] = ckvn
    kn2 = jnp.sum(jnp.square(ckvn.astype(F32)), axis=-1, keepdims=True)
    kn2_ref[0, 0] = jnp.broadcast_to(jnp.max(kn2, axis=0, keepdims=True), kn2_ref.shape[2:])
    qb = jnp.dot(cqn, wqb_ref[...], preferred_element_type=F32).astype(BF16)
    for pair in range(B_HEADS // 2):
        ql2 = jnp.dot(qb[:, pair * LANES:(pair + 1) * LANES], wuk_ref[pair],
                      preferred_element_type=F32)
        ql_ref[0, 2 * pair] = ql2[:, :B_KV_RANK].astype(ql_ref.dtype)
        ql_ref[0, 2 * pair + 1] = ql2[:, B_KV_RANK:].astype(ql_ref.dtype)
    qi = jnp.dot(cqn, wiq_ref[...], preferred_element_type=F32)
    for pair in range(IDX_HEADS // 2):
        qi_ref[0, pair] = qi[:, pair * LANES:(pair + 1) * LANES].astype(qi_ref.dtype)
    small = jnp.dot(u_ref[...], wsm_ref[...], preferred_element_type=F32)
    klo_ref[0] = small[:, :LANES].astype(klo_ref.dtype)
    khi_ref[0] = small[:, LANES:2 * LANES].astype(khi_ref.dtype)
    wi_ref[0] = small[:, 2 * LANES:] * (IDX_HEADS ** -0.5)


def _bprep(proj, u, qg, kvg, wqb, wiq, wuk_bd, wsmall, layer, n_batch, lp, tm=QB_B):
    nt = lp // tm
    row = lambda b, i: b * nt + i
    const2 = lambda b, i: (0, 0)
    lay2 = lambda b, i: (layer, 0, 0)
    return pl.pallas_call(
        _bprep_kernel,
        grid=(n_batch, nt),
        in_specs=[
            pl.BlockSpec((tm, B_Q_RANK), lambda b, i: (row(b, i), C_CQ // B_Q_RANK)),
            pl.BlockSpec((tm, B_KV_RANK), lambda b, i: (row(b, i), C_CKV // B_KV_RANK)),
            pl.BlockSpec((tm, D_MODEL), lambda b, i: (row(b, i), 0)),
            pl.BlockSpec((1, B_Q_RANK), const2),
            pl.BlockSpec((1, B_KV_RANK), const2),
            pl.BlockSpec((None, B_Q_RANK, B_WIDTH), lay2),
            pl.BlockSpec((None, B_Q_RANK, IDX_HEADS * IDX_DIM), lay2),
            pl.BlockSpec((None, B_HEADS // 2, LANES, 2 * B_KV_RANK), lambda b, i: (layer, 0, 0, 0)),
            pl.BlockSpec((None, D_MODEL, 3 * LANES), lay2),
        ],
        out_specs=[
            pl.BlockSpec((1, B_HEADS, tm, B_KV_RANK), lambda b, i: (b, 0, i, 0)),
            pl.BlockSpec((1, IDX_HEADS // 2, tm, LANES), lambda b, i: (b, 0, i, 0)),
            pl.BlockSpec((1, tm, LANES), lambda b, i: (b, i, 0)),
            pl.BlockSpec((1, tm, B_KV_RANK), lambda b, i: (b, i, 0)),
            pl.BlockSpec((1, tm, LANES), lambda b, i: (b, i, 0)),
            pl.BlockSpec((1, tm, LANES), lambda b, i: (b, i, 0)),
            pl.BlockSpec((1, 1, 8, LANES), lambda b, i: (b, i, 0, 0)),
        ],
        out_shape=[
            jax.ShapeDtypeStruct((n_batch, B_HEADS, lp, B_KV_RANK), BF16),
            jax.ShapeDtypeStruct((n_batch, IDX_HEADS // 2, lp, LANES), BF16),
            jax.ShapeDtypeStruct((n_batch, lp, LANES), F32),
            jax.ShapeDtypeStruct((n_batch, lp, B_KV_RANK), BF16),
            jax.ShapeDtypeStruct((n_batch, lp, LANES), BF16),
            jax.ShapeDtypeStruct((n_batch, lp, LANES), BF16),
            jax.ShapeDtypeStruct((n_batch, nt, 8, LANES), F32),
        ],
        compiler_params=_cparams(2),
        name="bprep",
    )(proj, proj, u, qg, kvg, wqb, wiq, wuk_bd, wsmall)


def _mixer_b_kernel(bmax_ref, ql_ref, qi_ref, wi_ref, ckv_ref, klo_ref, khi_ref, kn2_ref, zb_ref, bias_ref,
                    wuv_ref, o_ref, flag_ref, key_scr, s_scr, acc_scr, m_scr, l_scr, alpha_scr, p_scr,
                    *, bound):
    jq = pl.program_id(1)
    nch = jq + 1
    qb, kc = QB_B, QB_B
    rows = B_HEADS * qb
    kf = float(TOPK_MAX)

    qi = qi_ref[0].reshape(IDX_HEADS // 2 * qb, LANES)
    wit = wi_ref[0].T
    krow = lax.broadcasted_iota(jnp.int32, (qb, kc), 0)
    krow_sub = lax.broadcasted_iota(jnp.int32, (SUB_B, kc), 0)
    qpos = jq * qb + lax.broadcasted_iota(jnp.int32, (qb, kc), 1)

    def idx_body(c, carry):
        start = pl.multiple_of(c * kc, kc)
        sc = (lax.dot_general(klo_ref[0, pl.ds(start, kc), :], qi, _NT, preferred_element_type=F32),
              lax.dot_general(khi_ref[0, pl.ds(start, kc), :], qi, _NT, preferred_element_type=F32))
        isc = jnp.zeros((kc, qb), F32)
        for pair in range(IDX_HEADS // 2):
            for par in range(2):
                hd = 2 * pair + par
                isc = isc + wit[hd:hd + 1, :] * jnp.maximum(sc[par][:, pair * qb:(pair + 1) * qb], 0.0)
        isc = jnp.where(isc == 0.0, 0.0, isc)
        bits = lax.bitcast_convert_type(isc, jnp.int32)
        key = bits ^ ((bits >> 31) & INT_MAX)
        key_scr[c] = jnp.where(c * kc + krow <= qpos, key, INT_MIN)
        return carry

    lax.fori_loop(0, nch, idx_body, 0)

    def reduce_chunks(tile_fn, init, combine):
        n_acc = SUB_B // 8

        def body(c, accs):
            accs = list(accs)
            for r0 in range(0, kc, SUB_B):
                t = tile_fn(c, r0, key_scr[c, r0:r0 + SUB_B, :])
                t = t.reshape(n_acc, 8, qb)
                for i in range(n_acc):
                    accs[i] = combine(accs[i], t[i])
            return tuple(accs)

        accs = lax.fori_loop(0, nch, body, (init,) * n_acc)
        acc = accs[0]
        for a in accs[1:]:
            acc = combine(acc, a)
        return acc

    def count(pred_fn):
        acc = reduce_chunks(lambda c, r0, k: jnp.where(pred_fn(c, r0, k), 1.0, 0.0),
                            jnp.zeros((8, qb), F32), lambda a, b: a + b)
        return jnp.sum(acc, axis=0, keepdims=True)

    def unkey(k):
        return lax.bitcast_convert_type(k ^ ((k >> 31) & INT_MAX), F32)

    def tokey(v):
        b = lax.bitcast_convert_type(v, jnp.int32)
        return b ^ ((b >> 31) & INT_MAX)

    def select_all():
        return jnp.full((1, qb), INT_MIN, jnp.int32), jnp.full((1, qb), -1, jnp.int32)

    def select_topk():
        kmax = reduce_chunks(lambda c, r0, k: k, jnp.full((8, qb), INT_MIN, jnp.int32), jnp.maximum)
        kmax = jnp.max(kmax, axis=0, keepdims=True)
        kmin = reduce_chunks(lambda c, r0, k: jnp.where(k == INT_MIN, INT_MAX, k),
                             jnp.full((8, qb), INT_MAX, jnp.int32), jnp.minimum)
        kmin = jnp.min(kmin, axis=0, keepdims=True)
        c0 = count(lambda c, r0, k: k >= 0)
        c0p = count(lambda c, r0, k: k >= 1)
        n_valid = (jq * qb + 1 + lax.broadcasted_iota(jnp.int32, (1, qb), 1)).astype(F32)
        pos = c0p >= kf
        zero = (c0 >= kf) & (c0p < kf)
        lo0 = jnp.where(pos, 1, jnp.where(zero, 0, kmin))
        hi0 = jnp.where(pos, kmax + 1, jnp.where(zero, 1, 0))
        c_lo0 = jnp.where(pos, c0p, jnp.where(zero, c0, n_valid))
        c_hi0 = jnp.where(pos, 0.0, jnp.where(zero, c0p, c0))

        def bis_cond(st):
            return (st[0] < 64) & (st[5] > 0)

        def bis_step(it, lo, hi, c_lo, c_hi):
            mid_i = (lo >> 1) + (hi >> 1) + (lo & hi & 1)
            mid_v = tokey(0.5 * unkey(lo) + 0.5 * unkey(hi))
            use_v = (it < 24) & (mid_v > lo) & (mid_v < hi)
            mid = jnp.where(use_v, mid_v, mid_i)
            cnt = count(lambda c, r0, k: k >= mid)
            ge = cnt >= kf
            return (jnp.where(ge, mid, lo), jnp.where(ge, hi, mid),
                    jnp.where(ge, cnt, c_lo), jnp.where(ge, c_hi, cnt))

        def bis_body(st):
            it, lo, hi, c_lo, c_hi, _ = st
            for _ in range(BIS_UNROLL):
                lo, hi, c_lo, c_hi = bis_step(it, lo, hi, c_lo, c_hi)
                it = it + 1
            active = (lo + 1 < hi) & (c_lo != kf)
            return it, lo, hi, c_lo, c_hi, jnp.max(jnp.where(active, 1, 0))

        active0 = (lo0 + 1 < hi0) & (c_lo0 != kf)
        st0 = (jnp.int32(0), lo0, hi0, c_lo0, c_hi0, jnp.max(jnp.where(active0, 1, 0)))
        _, lo, hi, c_lo, c_hi, _ = lax.while_loop(bis_cond, bis_body, st0)
        tie = c_lo != kf

        def tie_break():
            need = kf - c_hi

            def tie_body(_, st):
                jlo, jhi = st
                mid = (jlo + jhi) >> 1
                cnt = count(lambda c, r0, k: (k == lo) & (c * kc + r0 + krow_sub <= mid))
                ge = cnt >= need
                return jnp.where(ge, jlo, mid), jnp.where(ge, mid, jhi)

            _, jhi = lax.fori_loop(0, 13, tie_body, (jnp.full((1, qb), -1, jnp.int32),
                                                     jnp.full((1, qb), 8191, jnp.int32)))
            return jnp.where(tie, jhi, INT_MAX)

        jmax = lax.cond(jnp.max(jnp.where(tie, 1, 0)) > 0, tie_break,
                        lambda: jnp.full((1, qb), INT_MAX, jnp.int32))
        return lo, jmax

    thr, jmax = lax.cond(jq >= 1, select_topk, select_all)

    def sel_body(c, carry):
        k = key_scr[c]
        sel = (k > thr) | ((k == thr) & (c * kc + krow <= jmax))
        key_scr[c] = lax.bitcast_convert_type(jnp.where(sel, 0.0, NEG).T, jnp.int32)
        return carry

    lax.fori_loop(0, nch, sel_body, 0)

    l_scr[...] = jnp.zeros(l_scr.shape, F32)
    acc_scr[...] = jnp.zeros(acc_scr.shape, F32)

    def kv_chunk(c):
        return ckv_ref[0, pl.ds(pl.multiple_of(c * kc, kc), kc), :]

    def qk(c, slot):
        s_scr[slot] = lax.dot_general(ql_ref[0].reshape(rows, B_KV_RANK), kv_chunk(c), _NT,
                                      preferred_element_type=F32)

    def logits(c, slot, near, hd, r0):
        sh = s_scr[slot, hd * qb + r0:hd * qb + r0 + SUB_B]
        if near == "cur":
            t0 = bias_ref[hd, r0 % QB_A:r0 % QB_A + SUB_B, LANES:]
            t1 = bias_ref[hd, r0 % QB_A:r0 % QB_A + SUB_B, :LANES]
            sh = sh + jnp.concatenate([t0 if r0 < QB_A else t1, t0], axis=1)
        elif near == "prev" and r0 < QB_A:
            t1 = bias_ref[hd, r0:r0 + SUB_B, :LANES]
            sh = sh + jnp.concatenate([jnp.zeros_like(t1), t1], axis=1)
        return sh + lax.bitcast_convert_type(key_scr[c, r0:r0 + SUB_B, :], F32)

    def sub_tiles():
        for hd in range(B_HEADS):
            for r0 in range(0, qb, SUB_B):
                yield hd, r0, slice(hd * qb + r0, hd * qb + r0 + SUB_B)

    if bound:
        kn2 = lax.fori_loop(0, nch, lambda c, a: jnp.maximum(a, kn2_ref[0, c]), jnp.zeros((8, LANES), F32))
        bmax = bmax_ref[0]
        for hd, r0, sl in sub_tiles():
            x = ql_ref[0, hd, r0:r0 + SUB_B, :].astype(F32)
            qn2 = jnp.sum(x * x, axis=-1, keepdims=True)
            m_scr[sl] = jnp.sqrt(qn2 * kn2[:1]) * (1.0 + 2.0 ** -10) + bmax

        def sm(c, slot, near):
            for hd, r0, sl in sub_tiles():
                m = m_scr[sl]
                p = jnp.exp2(logits(c, slot, near, hd, r0) - jnp.concatenate([m, m], axis=1))
                l_scr[sl] += p[:, :LANES] + p[:, LANES:]
                p_scr[slot, sl] = p.astype(BF16)

        def pv(c, slot):
            o = jnp.dot(p_scr[slot], kv_chunk(c), preferred_element_type=F32)
            for r0 in range(0, rows, SUB_B):
                acc_scr[r0:r0 + SUB_B] += o[r0:r0 + SUB_B]

        def stage(t, slot):
            qk(t, slot)
            pv(t - 2, slot)
            sm(t - 1, 1 - slot, None)

        def tail(slot):
            qk(jq, slot)
            pv(jq - 2, slot)
            sm(jq - 1, 1 - slot, "prev")
            pv(jq - 1, 1 - slot)
            sm(jq, slot, "cur")
            pv(jq, slot)

        @pl.when(jq >= 2)
        def _():
            qk(0, 0)
            qk(1, 1)
            sm(0, 0, None)

            def two_stages(i, carry):
                stage(2 + 2 * i, 0)
                stage(3 + 2 * i, 1)
                return carry

            lax.fori_loop(0, (jq - 2) // 2, two_stages, 0)

            @pl.when(jq % 2 == 0)
            def _():
                tail(0)

            @pl.when(jq % 2 == 1)
            def _():
                stage(jq - 1, 0)
                tail(1)

        @pl.when(jq < 2)
        def _():
            @pl.when(jq == 1)
            def _():
                qk(0, 0)
                sm(0, 0, "prev")
                pv(0, 0)

            qk(jq, 1)
            sm(jq, 1, "cur")
            pv(jq, 1)
    else:
        m_scr[...] = jnp.full(m_scr.shape, -jnp.inf, F32)

        def attend(c, near):
            qk(c, 0)
            for hd, r0, sl in sub_tiles():
                m_old = m_scr[sl]
                m_new = jnp.maximum(m_old, jnp.max(logits(c, 0, near, hd, r0), axis=-1, keepdims=True))
                m_scr[sl] = m_new
                alpha_scr[sl] = jnp.exp2(m_old - m_new)
            for hd, r0, sl in sub_tiles():
                m_new = m_scr[sl]
                p = jnp.exp2(logits(c, 0, near, hd, r0) - jnp.concatenate([m_new, m_new], axis=1))
                l_scr[sl] = alpha_scr[sl] * l_scr[sl] + (p[:, :LANES] + p[:, LANES:])
                p_scr[0, sl] = p.astype(BF16)
            o = jnp.dot(p_scr[0], kv_chunk(c), preferred_element_type=F32)
            for r0 in range(0, rows, SUB_B):
                sl = slice(r0, r0 + SUB_B)
                alpha = alpha_scr[sl]
                acc_scr[sl] = acc_scr[sl] * jnp.concatenate([alpha, alpha], axis=1) + o[sl]

        def far_body(c, carry):
            attend(c, None)
            return carry

        lax.fori_loop(0, jq - 1, far_body, 0)

        @pl.when(jq >= 1)
        def _():
            attend(jq - 1, "prev")

        attend(jq, "cur")

    l_min = jnp.full((qb, 1), jnp.inf, F32)
    for pair in range(B_HEADS // 2):
        y2 = jnp.zeros((qb, LANES), F32)
        for par in range(2):
            hd = 2 * pair + par
            sl = slice(hd * qb, (hd + 1) * qb)
            l = jnp.sum(l_scr[sl], axis=-1, keepdims=True)
            l_min = jnp.minimum(l_min, l)
            o_lat = acc_scr[sl] / l
            y2 = y2 + jnp.dot(o_lat.astype(BF16), wuv_ref[pair, par], preferred_element_type=F32)
        z = zb_ref[:, pair * LANES:(pair + 1) * LANES]
        o_ref[:, pair * LANES:(pair + 1) * LANES] = (y2 * _silu(z)).astype(o_ref.dtype)
    safe = jnp.min(l_min) >= L_SAFE
    flag_ref[0, 0] = jnp.full(flag_ref.shape[2:], jnp.where(safe, 0.0, 1.0), F32)


def _mixer_b(bmax, ql, qi, wi, ckvn, klo, khi, kn2, proj, bias_b, wuv_pairs, layer, n_batch, lp, bound):
    nb = lp // QB_B
    rows = n_batch * lp
    whole = lambda b, j: (b, 0, 0)
    once = pl.Buffered(1)
    return pl.pallas_call(
        functools.partial(_mixer_b_kernel, bound=bound),
        grid=(n_batch, nb),
        in_specs=[
            pl.BlockSpec(memory_space=pltpu.SMEM),
            pl.BlockSpec((1, B_HEADS, QB_B, B_KV_RANK), lambda b, j: (b, 0, j, 0)),
            pl.BlockSpec((1, IDX_HEADS // 2, QB_B, LANES), lambda b, j: (b, 0, j, 0)),
            pl.BlockSpec((1, QB_B, LANES), lambda b, j: (b, j, 0)),
            pl.BlockSpec((1, lp, B_KV_RANK), whole, pipeline_mode=once),
            pl.BlockSpec((1, lp, LANES), whole, pipeline_mode=once),
            pl.BlockSpec((1, lp, LANES), whole, pipeline_mode=once),
            pl.BlockSpec((1, nb, 8, LANES), lambda b, j: (b, 0, 0, 0), pipeline_mode=once),
            pl.BlockSpec((QB_B, B_WIDTH), lambda b, j: (b * nb + j, C_ZB // B_WIDTH)),
            pl.BlockSpec((B_HEADS, QB_A, 2 * LANES), lambda b, j: (0, 0, 0), pipeline_mode=once),
            pl.BlockSpec((None, B_HEADS // 2, 2, B_KV_RANK, LANES), lambda b, j: (layer, 0, 0, 0, 0),
                         pipeline_mode=once),
        ],
        out_specs=[pl.BlockSpec((QB_B, B_WIDTH), lambda b, j: (b * nb + j, 0)),
                   pl.BlockSpec((1, 1, 8, LANES), lambda b, j: (b, j, 0, 0))],
        out_shape=[jax.ShapeDtypeStruct((rows, B_WIDTH), BF16),
                   jax.ShapeDtypeStruct((n_batch, nb, 8, LANES), F32)],
        scratch_shapes=[
            pltpu.VMEM((nb, QB_B, QB_B), jnp.int32),
            pltpu.VMEM((2, B_HEADS * QB_B, QB_B), F32),
            pltpu.VMEM((B_HEADS * QB_B, B_KV_RANK), F32),
            pltpu.VMEM((B_HEADS * QB_B, LANES), F32),
            pltpu.VMEM((B_HEADS * QB_B, LANES), F32),
            pltpu.VMEM((B_HEADS * QB_B, LANES), F32),
            pltpu.VMEM((2, B_HEADS * QB_B, QB_B), BF16),
        ],
        compiler_params=_cparams(2),
        name="mixer_b" if bound else "mixer_b_running_max",
    )(bmax, ql, qi, wi, ckvn, klo, khi, kn2, proj, bias_b, wuv_pairs)


def _merge_kernel(ya_ref, yb_ref, ga_ref, gb_ref, h_ref, wpa_ref, wpb_ref, wo_ref, g_ref, *out_refs,
                  last):
    pa = jnp.dot(ya_ref[...], wpa_ref[...], preferred_element_type=F32)
    pb = jnp.dot(yb_ref[...], wpb_ref[...], preferred_element_type=F32)
    merged = _sigmoid(ga_ref[...]) * pa + _sigmoid(gb_ref[...]) * pb
    h_new = h_ref[...] + jnp.dot(merged.astype(BF16), wo_ref[...], preferred_element_type=F32)
    normed = _rms(h_new, g_ref[...])
    if last:
        out_refs[0][...] = normed
    else:
        out_refs[0][...] = h_new
        out_refs[1][...] = normed.astype(BF16)


def _merge(ya, yb, gates, h, wpa, wpb, wo, g_next, layer, last, tm=256):
    rows, d = h.shape
    const = lambda i: (0, 0)
    lay = lambda i: (layer, 0, 0)
    once = pl.Buffered(1)
    if last:
        out_specs = [pl.BlockSpec((tm, d), lambda i: (i, 0))]
        out_shape = [jax.ShapeDtypeStruct((rows, d), F32)]
    else:
        out_specs = [pl.BlockSpec((tm, d), lambda i: (i, 0)), pl.BlockSpec((tm, d), lambda i: (i, 0))]
        out_shape = [jax.ShapeDtypeStruct((rows, d), F32), jax.ShapeDtypeStruct((rows, d), BF16)]
    return pl.pallas_call(
        functools.partial(_merge_kernel, last=last),
        grid=(rows // tm,),
        in_specs=[
            pl.BlockSpec((tm, A_WIDTH), lambda i: (i, 0)),
            pl.BlockSpec((tm, B_WIDTH), lambda i: (i, 0)),
            pl.BlockSpec((tm, d), lambda i: (i, 0)),
            pl.BlockSpec((tm, d), lambda i: (i, 1)),
            pl.BlockSpec((tm, d), lambda i: (i, 0)),
            pl.BlockSpec((None, A_WIDTH, d), lay, pipeline_mode=once),
            pl.BlockSpec((None, B_WIDTH, d), lay, pipeline_mode=once),
            pl.BlockSpec((None, d, d), lay, pipeline_mode=once),
            pl.BlockSpec((1, d), const),
        ],
        out_specs=out_specs,
        out_shape=out_shape,
        compiler_params=_cparams(1),
        name="merge_out",
    )(ya, yb, gates, gates, h, wpa, wpb, wo, g_next.reshape(1, d))


def _pack_weights(w_in, w_uk, w_uv, w_qb, w_iq, w_proj_a, w_proj_b, w_out):
    depth = w_in.shape[0]
    sec = lambda a, b: w_in[:, :, a:b]
    kidx, widx = sec(4096, 4160), sec(4160, 4168)
    z64 = jnp.zeros_like(kidx)
    zpad = jnp.zeros(widx.shape[:2] + (LANES - IDX_HEADS,), w_in.dtype)
    w_small = jnp.concatenate([kidx, z64, z64, kidx, widx, zpad], axis=-1).astype(BF16)
    uk = (w_uk * (HEAD_DIM ** -0.5 * LOG2E)).reshape(depth, B_HEADS // 2, 2, HEAD_DIM, B_KV_RANK)
    zuk = jnp.zeros_like(uk[:, :, 0])
    wuk_bd = jnp.concatenate([jnp.concatenate([uk[:, :, 0], zuk], axis=-1),
                              jnp.concatenate([zuk, uk[:, :, 1]], axis=-1)], axis=-2).astype(BF16)
    uv = w_uv.reshape(depth, B_HEADS // 2, 2, B_KV_RANK, HEAD_DIM)
    zuv = jnp.zeros_like(uv[:, :, 0])
    wuv_pairs = jnp.stack([jnp.concatenate([uv[:, :, 0], zuv], axis=-1),
                           jnp.concatenate([zuv, uv[:, :, 1]], axis=-1)], axis=2).astype(BF16)
    w_iq8 = (w_iq * (IDX_DIM ** -0.5)).astype(BF16)
    return (w_small, wuk_bd, wuv_pairs, w_qb.astype(BF16), w_iq8,
            w_proj_a.astype(BF16), w_proj_b.astype(BF16), w_out.astype(BF16))


def kernel(x, meta_tokens, bias_table, norm_g, w_in, q_norm_g, kv_norm_g, w_qb, w_iq, w_uk, w_uv,
           sinks, w_proj_a, w_proj_b, w_out, final_g):
    n_batch, seq, d = x.shape
    depth = w_in.shape[0]
    length = seq + N_META
    lp = -(-length // QB_B) * QB_B
    assert min(TOPK_MAX, seq // 4) == TOPK_MAX and d == D_MODEL

    (w_small, wuk_bd, wuv_pairs, wqb, wiq, wpa, wpb, wo) = _pack_weights(
        w_in, w_uk, w_uv, w_qb, w_iq, w_proj_a, w_proj_b, w_out)
    w_t = jnp.swapaxes(w_in, 1, 2)
    bias_a, bias_b = _bias_tiles(bias_table)
    bmax = jnp.max(jnp.abs(bias_b)).reshape(1)

    meta = jnp.broadcast_to(meta_tokens.astype(x.dtype)[None], (n_batch, N_META, d))
    pad = jnp.zeros((n_batch, lp - length, d), x.dtype)
    h = jnp.concatenate([meta, x, pad], axis=1).reshape(n_batch * lp, d)

    u = _rmsnorm(h, norm_g[0])
    out = None
    for l in range(depth):
        proj = _in_proj(u, w_t, l)
        gates = _gate_proj(u, w_t, l, GATE_COL0, 2 * D_MODEL)
        ya = _mixer_a(proj, sinks[l], bias_a, n_batch, lp)
        ql, qi, wi, ckvn, klo, khi, kn2 = _bprep(
            proj, u, q_norm_g[l].reshape(1, -1), kv_norm_g[l].reshape(1, -1),
            wqb, wiq, wuk_bd, w_small, l, n_batch, lp)
        b_args = (bmax, ql, qi, wi, ckvn, klo, khi, kn2, proj, bias_b, wuv_pairs, l, n_batch, lp)
        yb, flags = _mixer_b(*b_args, bound=True)
        yb = lax.cond(jnp.max(flags) > 0.0, lambda: _mixer_b(*b_args, bound=False)[0], lambda: yb)
        last = l == depth - 1
        g_next = final_g if last else norm_g[l + 1]
        res = _merge(ya, yb, gates, h, wpa, wpb, wo, g_next, l, last)
        if last:
            out = res[0]
        else:
            h, u = res
    return out.reshape(n_batch, lp, d)[:, N_META:length]
```

```python
import functools
import math

import numpy as np
import jax
import jax.numpy as jnp
from jax import lax
from jax.experimental import pallas as pl
from jax.experimental.pallas import tpu as pltpu

D_MODEL = 2048
N_META = 16
WINDOW = 128
HEAD_DIM = 64
A_HEADS = 16
A_WIDTH = 1024
B_HEADS = 16
B_WIDTH = 1024
B_Q_RANK = 512
B_KV_RANK = 256
IDX_HEADS = 8
IDX_DIM = 64
TOPK_MAX = 256
N_BUCKETS = 32
MAX_DISTANCE = 128
EPS = 1e-6
NEG = -1e30

LANES = 128
PROJ_ROW_TILES = 8
QB_A = 128
QB_B = 256
SUB_A = 32
SUB_B = 32
L_SAFE = 2.0 ** -100
BIS_UNROLL = 3
LOG2E = math.log2(math.e)
VMEM_LIMIT = 56 * 1024 * 1024
INT_MIN = -(2 ** 31)
INT_MAX = 2 ** 31 - 1

C_QA, C_ZA, C_ZB, C_CQ, C_CKV, C_KA, C_VA = (0, 1024, 2048, 3072, 3584, 3840, 3968)
GATE_COL0 = 4168

F32 = jnp.float32
BF16 = jnp.bfloat16
_NT = (((1,), (1,)), ((), ()))


def _cparams(n_grid):
    return pltpu.CompilerParams(
        dimension_semantics=("arbitrary",) * n_grid,
        vmem_limit_bytes=VMEM_LIMIT)


def _t5_bucket_np(d):
    max_exact = N_BUCKETS // 2
    nf = np.maximum(d, 1).astype(np.float32)
    large = max_exact + (np.log(nf / np.float32(max_exact)) / np.float32(math.log(MAX_DISTANCE / max_exact))
                         * np.float32(N_BUCKETS - max_exact)).astype(np.int32)
    large = np.minimum(large, N_BUCKETS - 1)
    return np.where(d < max_exact, d, large).astype(np.int32)


def _bias_tiles_kernel(tab_ref, idx_ref, out_a_ref, out_b_ref):
    h = pl.program_id(0)
    idx = idx_ref[...]
    acc_a = jnp.zeros(idx.shape, F32)
    acc_b = jnp.zeros(idx.shape, F32)
    far_b = tab_ref[N_BUCKETS - 1, A_HEADS + h]
    for b in range(N_BUCKETS):
        hit = idx == b
        acc_a = jnp.where(hit, tab_ref[b, h] * LOG2E, acc_a)
        acc_b = jnp.where(hit, (tab_ref[b, A_HEADS + h] - far_b) * LOG2E, acc_b)
    out_a_ref[0] = acc_a
    out_b_ref[0] = acc_b[:, :2 * LANES]


def _bias_tiles(bias_table):
    r = np.arange(QB_A)[:, None]
    k = np.arange(QB_A)[None, :]
    prev = _t5_bucket_np(np.maximum(QB_A + r - k, 0))
    cur = _t5_bucket_np(np.maximum(r - k, 0))
    far = np.full((QB_A, QB_A), N_BUCKETS - 1, np.int32)
    idx = jnp.asarray(np.concatenate([prev, cur, far], axis=1))
    return pl.pallas_call(
        _bias_tiles_kernel,
        grid=(A_HEADS,),
        in_specs=[pl.BlockSpec(memory_space=pltpu.SMEM),
                  pl.BlockSpec((QB_A, 3 * LANES), lambda h: (0, 0))],
        out_specs=[pl.BlockSpec((1, QB_A, 3 * LANES), lambda h: (h, 0, 0)),
                   pl.BlockSpec((1, QB_A, 2 * LANES), lambda h: (h, 0, 0))],
        out_shape=[jax.ShapeDtypeStruct((A_HEADS, QB_A, 3 * LANES), F32),
                   jax.ShapeDtypeStruct((B_HEADS, QB_A, 2 * LANES), F32)],
        compiler_params=_cparams(1),
        name="bias_tiles",
    )(bias_table, idx)


def _rms(x, g):
    return x * lax.rsqrt(jnp.mean(x * x, axis=-1, keepdims=True) + EPS) * g


def _rmsnorm_kernel(h_ref, g_ref, u_ref):
    u_ref[...] = _rms(h_ref[...], g_ref[...]).astype(u_ref.dtype)


def _rmsnorm(h, g, tm=512):
    rows, d = h.shape
    return pl.pallas_call(
        _rmsnorm_kernel,
        grid=(rows // tm,),
        in_specs=[pl.BlockSpec((tm, d), lambda i: (i, 0)),
                  pl.BlockSpec((1, d), lambda i: (0, 0))],
        out_specs=pl.BlockSpec((tm, d), lambda i: (i, 0)),
        out_shape=jax.ShapeDtypeStruct((rows, d), BF16),
        compiler_params=_cparams(1),
        name="rmsnorm",
    )(h, g.reshape(1, d))


_IN_BLOCKS = ((0, 1, 2, 3), (5, 6, 7, 8), (12, 13, 14, 15), (9, 10, 11, 4))
IN_BLK = 256


def _in_proj_kernel(x_ref, *refs):
    w_refs, o_ref, wbf_scr = refs[:-2], refs[-2], refs[-1]

    @pl.when(pl.program_id(1) == 0)
    def _():
        r0 = 0
        for w_ref in w_refs:
            wbf_scr[r0:r0 + w_ref.shape[0], :] = w_ref[...].astype(BF16)
            r0 += w_ref.shape[0]

    o_ref[...] = lax.dot_general(x_ref[...], wbf_scr[...], _NT, preferred_element_type=F32)


def _in_proj(x, w_t, layer):
    rows, kdim = x.shape
    tm = rows // PROJ_ROW_TILES
    tn = 4 * IN_BLK

    def w_spec(s):
        def index(j, i):
            blk = jnp.int32(_IN_BLOCKS[0][s])
            for t in range(1, len(_IN_BLOCKS)):
                blk = jnp.where(j == t, _IN_BLOCKS[t][s], blk)
            return layer, blk, 0
        return pl.BlockSpec((None, IN_BLK, kdim), index)

    return pl.pallas_call(
        _in_proj_kernel,
        grid=(len(_IN_BLOCKS), rows // tm),
        in_specs=[pl.BlockSpec((tm, kdim), lambda j, i: (i, 0))] + [w_spec(s) for s in range(4)],
        out_specs=pl.BlockSpec((tm, tn), lambda j, i: (i, j)),
        out_shape=jax.ShapeDtypeStruct((rows, len(_IN_BLOCKS) * tn), F32),
        scratch_shapes=[pltpu.VMEM((tn, kdim), BF16)],
        compiler_params=_cparams(2),
        name="in_proj",
    )(x, w_t, w_t, w_t, w_t)


def _gate_proj(x, w_t, layer, col0, n, tn=1024):
    rows, kdim = x.shape
    tm = rows // PROJ_ROW_TILES
    return pl.pallas_call(
        _in_proj_kernel,
        grid=(n // tn, rows // tm),
        in_specs=[pl.BlockSpec((tm, kdim), lambda j, i: (i, 0)),
                  pl.BlockSpec((None, pl.Element(tn), pl.Element(kdim)),
                               lambda j, i: (layer, pl.multiple_of(col0 + j * tn, 8), 0))],
        out_specs=pl.BlockSpec((tm, tn), lambda j, i: (i, j)),
        out_shape=jax.ShapeDtypeStruct((rows, n), F32),
        scratch_shapes=[pltpu.VMEM((tn, kdim), BF16)],
        compiler_params=_cparams(2),
        name="gate_proj",
    )(x, w_t)


def _silu(z):
    return z / (1.0 + jnp.exp(-z))


def _sigmoid(z):
    return 1.0 / (1.0 + jnp.exp(-z))


def _mixer_a_kernel(sink_ref, q_ref, kc_ref, vc_ref, kp_ref, vp_ref, km_ref, vm_ref, za_ref,
                    bias_ref, o_ref, s_scr, p_scr, m_scr, d_scr, msk_scr):
    j = pl.program_id(1)
    nk = 3 * QB_A
    q = (q_ref[...] * (HEAD_DIM ** -0.5 * LOG2E)).astype(BF16)
    kcat = jnp.concatenate([kp_ref[...], kc_ref[...], km_ref[...]], axis=0)
    vcat = jnp.concatenate([vp_ref[...], vc_ref[...], vm_ref[...]], axis=0)
    lane = lax.broadcasted_iota(jnp.int32, (nk, LANES), 1)
    lo = lane < HEAD_DIM
    kswap = pltpu.roll(kcat, HEAD_DIM, 1)
    vswap = pltpu.roll(vcat, HEAD_DIM, 1)
    zero = jnp.zeros_like(kcat)
    k_half = ((jnp.where(lo, kcat, zero).astype(BF16), jnp.where(lo, zero, kswap).astype(BF16)),
              (jnp.where(lo, kswap, zero).astype(BF16), jnp.where(lo, zero, kcat).astype(BF16)))
    v_half = ((jnp.where(lo, vcat, zero).astype(BF16), jnp.where(lo, zero, vswap).astype(BF16)),
              (jnp.where(lo, vswap, zero).astype(BF16), jnp.where(lo, zero, vcat).astype(BF16)))

    r = lax.broadcasted_iota(jnp.int32, (QB_A, nk), 0)
    c = lax.broadcasted_iota(jnp.int32, (QB_A, nk), 1)
    jv = jnp.zeros((QB_A, nk), jnp.int32) + j
    in_prev = (c < QB_A) & (jv >= 1) & ((c > r) | ((jv == 1) & (c < N_META)))
    in_cur = (c >= QB_A) & (c < 2 * QB_A) & (c - QB_A <= r)
    in_meta = (c >= 2 * QB_A) & (jv >= 2) & (c - 2 * QB_A < N_META)
    msk_scr[...] = jnp.where(in_prev | in_cur | in_meta, 0.0, NEG)
    half = lax.broadcasted_iota(jnp.int32, (QB_A, LANES), 1) < HEAD_DIM
    pairs = A_HEADS // 4

    def logits(g, i, par, r0):
        hd = 2 * (g * pairs + i) + par
        s = s_scr[i * QB_A + r0:i * QB_A + r0 + SUB_A, par * nk:(par + 1) * nk]
        return s + bias_ref[hd, r0:r0 + SUB_A, :] + msk_scr[r0:r0 + SUB_A, :], sink_ref[hd] * LOG2E

    for g in range(2):
        q4 = jnp.concatenate([q[:, (g * pairs + i) * LANES:(g * pairs + i + 1) * LANES]
                              for i in range(pairs)], axis=0)
        s_scr[...] = lax.dot_general(q4, jnp.concatenate(k_half[g], axis=0), _NT,
                                     preferred_element_type=F32)
        for i in range(pairs):
            for par in range(2):
                for r0 in range(0, QB_A, SUB_A):
                    lg, sink = logits(g, i, par, r0)
                    m = jnp.maximum(jnp.max(lg, axis=-1, keepdims=True), sink)
                    m_scr[par, i * QB_A + r0:i * QB_A + r0 + SUB_A] = jnp.broadcast_to(m, (SUB_A, LANES))
        for i in range(pairs):
            for par in range(2):
                for r0 in range(0, QB_A, SUB_A):
                    sl = slice(i * QB_A + r0, i * QB_A + r0 + SUB_A)
                    lg, sink = logits(g, i, par, r0)
                    m = m_scr[par, sl]
                    p = jnp.exp2(lg - jnp.concatenate([m] * 3, axis=1))
                    d_scr[par, sl] = jnp.broadcast_to(
                        jnp.sum(p, axis=-1, keepdims=True), (SUB_A, LANES)) + jnp.exp2(sink - m)
                    p_scr[sl, par * nk:(par + 1) * nk] = p.astype(BF16)
        pv = jnp.dot(p_scr[...], jnp.concatenate(v_half[g], axis=0), preferred_element_type=F32)
        for i in range(pairs):
            sl = slice(i * QB_A, (i + 1) * QB_A)
            col = slice((g * pairs + i) * LANES, (g * pairs + i + 1) * LANES)
            o2 = pv[sl] / jnp.where(half, d_scr[0, sl], d_scr[1, sl])
            o_ref[:, col] = (o2 * _silu(za_ref[:, col])).astype(o_ref.dtype)


def _mixer_a(proj, sinks, bias_a, n_batch, lp):
    nb = lp // QB_A
    rows = n_batch * lp
    ck, cv = C_KA // LANES, C_VA // LANES
    cur = lambda b, j: b * nb + j
    prev = lambda b, j: b * nb + jnp.maximum(j - 1, 0)
    first = lambda b, j: b * nb
    return pl.pallas_call(
        _mixer_a_kernel,
        grid=(n_batch, nb),
        in_specs=[
            pl.BlockSpec(memory_space=pltpu.SMEM),
            pl.BlockSpec((QB_A, A_WIDTH), lambda b, j: (cur(b, j), C_QA // A_WIDTH)),
            pl.BlockSpec((QB_A, LANES), lambda b, j: (cur(b, j), ck)),
            pl.BlockSpec((QB_A, LANES), lambda b, j: (cur(b, j), cv)),
            pl.BlockSpec((QB_A, LANES), lambda b, j: (prev(b, j), ck)),
            pl.BlockSpec((QB_A, LANES), lambda b, j: (prev(b, j), cv)),
            pl.BlockSpec((QB_A, LANES), lambda b, j: (first(b, j), ck)),
            pl.BlockSpec((QB_A, LANES), lambda b, j: (first(b, j), cv)),
            pl.BlockSpec((QB_A, A_WIDTH), lambda b, j: (cur(b, j), C_ZA // A_WIDTH)),
            pl.BlockSpec((A_HEADS, QB_A, 3 * LANES), lambda b, j: (0, 0, 0)),
        ],
        out_specs=pl.BlockSpec((QB_A, A_WIDTH), lambda b, j: (cur(b, j), 0)),
        out_shape=jax.ShapeDtypeStruct((rows, A_WIDTH), BF16),
        scratch_shapes=[
            pltpu.VMEM((A_HEADS // 4 * QB_A, 6 * QB_A), F32),
            pltpu.VMEM((A_HEADS // 4 * QB_A, 6 * QB_A), BF16),
            pltpu.VMEM((2, A_HEADS // 4 * QB_A, LANES), F32),
            pltpu.VMEM((2, A_HEADS // 4 * QB_A, LANES), F32),
            pltpu.VMEM((QB_A, 3 * QB_A), F32),
        ],
        compiler_params=_cparams(2),
        name="mixer_a",
    )(sinks, proj, proj, proj, proj, proj, proj, proj, proj, bias_a)


def _bprep_kernel(cq_ref, ckv_ref, u_ref, qg_ref, kvg_ref, wqb_ref, wiq_ref, wuk_ref, wsm_ref,
                  ql_ref, qi_ref, wi_ref, ckvn_ref, klo_ref, khi_ref, kn2_ref):
    cqn = _rms(cq_ref[...], qg_ref[...]).astype(BF16)
    ckvn = _rms(ckv_ref[...], kvg_ref[...]).astype(ckvn_ref.dtype)
    ckvn_ref[0] = ckvn
    kn2 = jnp.sum(jnp.square(ckvn.astype(F32)), axis=-1, keepdims=True)
    kn2_ref[0, 0] = jnp.broadcast_to(jnp.max(kn2, axis=0, keepdims=True), kn2_ref.shape[2:])
    qb = jnp.dot(cqn, wqb_ref[...], preferred_element_type=F32).astype(BF16)
    for pair in range(B_HEADS // 2):
        ql2 = jnp.dot(qb[:, pair * LANES:(pair + 1) * LANES], wuk_ref[pair],
                      preferred_element_type=F32)
        ql_ref[0, 2 * pair] = ql2[:, :B_KV_RANK].astype(ql_ref.dtype)
        ql_ref[0, 2 * pair + 1] = ql2[:, B_KV_RANK:].astype(ql_ref.dtype)
    qi = jnp.dot(cqn, wiq_ref[...], preferred_element_type=F32)
    for pair in range(IDX_HEADS // 2):
        qi_ref[0, pair] = qi[:, pair * LANES:(pair + 1) * LANES].astype(qi_ref.dtype)
    small = jnp.dot(u_ref[...], wsm_ref[...], preferred_element_type=F32)
    klo_ref[0] = small[:, :LANES].astype(klo_ref.dtype)
    khi_ref[0] = small[:, LANES:2 * LANES].astype(khi_ref.dtype)
    wi_ref[0] = small[:, 2 * LANES:] * (IDX_HEADS ** -0.5)


def _bprep(proj, u, qg, kvg, wqb, wiq, wuk_bd, wsmall, layer, n_batch, lp, tm=QB_B):
    nt = lp // tm
    row = lambda b, i: b * nt + i
    const2 = lambda b, i: (0, 0)
    lay2 = lambda b, i: (layer, 0, 0)
    return pl.pallas_call(
        _bprep_kernel,
        grid=(n_batch, nt),
        in_specs=[
            pl.BlockSpec((tm, B_Q_RANK), lambda b, i: (row(b, i), C_CQ // B_Q_RANK)),
            pl.BlockSpec((tm, B_KV_RANK), lambda b, i: (row(b, i), C_CKV // B_KV_RANK)),
            pl.BlockSpec((tm, D_MODEL), lambda b, i: (row(b, i), 0)),
            pl.BlockSpec((1, B_Q_RANK), const2),
            pl.BlockSpec((1, B_KV_RANK), const2),
            pl.BlockSpec((None, B_Q_RANK, B_WIDTH), lay2),
            pl.BlockSpec((None, B_Q_RANK, IDX_HEADS * IDX_DIM), lay2),
            pl.BlockSpec((None, B_HEADS // 2, LANES, 2 * B_KV_RANK), lambda b, i: (layer, 0, 0, 0)),
            pl.BlockSpec((None, D_MODEL, 3 * LANES), lay2),
        ],
        out_specs=[
            pl.BlockSpec((1, B_HEADS, tm, B_KV_RANK), lambda b, i: (b, 0, i, 0)),
            pl.BlockSpec((1, IDX_HEADS // 2, tm, LANES), lambda b, i: (b, 0, i, 0)),
            pl.BlockSpec((1, tm, LANES), lambda b, i: (b, i, 0)),
            pl.BlockSpec((1, tm, B_KV_RANK), lambda b, i: (b, i, 0)),
            pl.BlockSpec((1, tm, LANES), lambda b, i: (b, i, 0)),
            pl.BlockSpec((1, tm, LANES), lambda b, i: (b, i, 0)),
            pl.BlockSpec((1, 1, 8, LANES), lambda b, i: (b, i, 0, 0)),
        ],
        out_shape=[
            jax.ShapeDtypeStruct((n_batch, B_HEADS, lp, B_KV_RANK), BF16),
            jax.ShapeDtypeStruct((n_batch, IDX_HEADS // 2, lp, LANES), BF16),
            jax.ShapeDtypeStruct((n_batch, lp, LANES), F32),
            jax.ShapeDtypeStruct((n_batch, lp, B_KV_RANK), BF16),
            jax.ShapeDtypeStruct((n_batch, lp, LANES), BF16),
            jax.ShapeDtypeStruct((n_batch, lp, LANES), BF16),
            jax.ShapeDtypeStruct((n_batch, nt, 8, LANES), F32),
        ],
        compiler_params=_cparams(2),
        name="bprep",
    )(proj, proj, u, qg, kvg, wqb, wiq, wuk_bd, wsmall)


def _mixer_b_kernel(bmax_ref, ql_ref, qi_ref, wi_ref, ckv_ref, klo_ref, khi_ref, kn2_ref, zb_ref, bias_ref,
                    wuv_ref, o_ref, flag_ref, key_scr, s_scr, acc_scr, m_scr, l_scr, alpha_scr, p_scr,
                    *, bound):
    jq = pl.program_id(1)
    nch = jq + 1
    qb, kc = QB_B, QB_B
    rows = B_HEADS * qb
    kf = float(TOPK_MAX)

    qi = qi_ref[0].reshape(IDX_HEADS // 2 * qb, LANES)
    wit = wi_ref[0].T
    krow = lax.broadcasted_iota(jnp.int32, (qb, kc), 0)
    krow_sub = lax.broadcasted_iota(jnp.int32, (SUB_B, kc), 0)
    qpos = jq * qb + lax.broadcasted_iota(jnp.int32, (qb, kc), 1)

    def idx_body(c, carry):
        start = pl.multiple_of(c * kc, kc)
        sc = (lax.dot_general(klo_ref[0, pl.ds(start, kc), :], qi, _NT, preferred_element_type=F32),
              lax.dot_general(khi_ref[0, pl.ds(start, kc), :], qi, _NT, preferred_element_type=F32))
        isc = jnp.zeros((kc, qb), F32)
        for pair in range(IDX_HEADS // 2):
            for par in range(2):
                hd = 2 * pair + par
                isc = isc + wit[hd:hd + 1, :] * jnp.maximum(sc[par][:, pair * qb:(pair + 1) * qb], 0.0)
        isc = jnp.where(isc == 0.0, 0.0, isc)
        bits = lax.bitcast_convert_type(isc, jnp.int32)
        key = bits ^ ((bits >> 31) & INT_MAX)
        key_scr[c] = jnp.where(c * kc + krow <= qpos, key, INT_MIN)
        return carry

    lax.fori_loop(0, nch, idx_body, 0)

    def reduce_chunks(tile_fn, init, combine):
        n_acc = SUB_B // 8

        def body(c, accs):
            accs = list(accs)
            for r0 in range(0, kc, SUB_B):
                t = tile_fn(c, r0, key_scr[c, r0:r0 + SUB_B, :])
                t = t.reshape(n_acc, 8, qb)
                for i in range(n_acc):
                    accs[i] = combine(accs[i], t[i])
            return tuple(accs)

        accs = lax.fori_loop(0, nch, body, (init,) * n_acc)
        acc = accs[0]
        for a in accs[1:]:
            acc = combine(acc, a)
        return acc

    def count(pred_fn):
        acc = reduce_chunks(lambda c, r0, k: jnp.where(pred_fn(c, r0, k), 1.0, 0.0),
                            jnp.zeros((8, qb), F32), lambda a, b: a + b)
        return jnp.sum(acc, axis=0, keepdims=True)

    def unkey(k):
        return lax.bitcast_convert_type(k ^ ((k >> 31) & INT_MAX), F32)

    def tokey(v):
        b = lax.bitcast_convert_type(v, jnp.int32)
        return b ^ ((b >> 31) & INT_MAX)

    def select_all():
        return jnp.full((1, qb), INT_MIN, jnp.int32), jnp.full((1, qb), -1, jnp.int32)

    def select_topk():
        kmax = reduce_chunks(lambda c, r0, k: k, jnp.full((8, qb), INT_MIN, jnp.int32), jnp.maximum)
        kmax = jnp.max(kmax, axis=0, keepdims=True)
        kmin = reduce_chunks(lambda c, r0, k: jnp.where(k == INT_MIN, INT_MAX, k),
                             jnp.full((8, qb), INT_MAX, jnp.int32), jnp.minimum)
        kmin = jnp.min(kmin, axis=0, keepdims=True)
        c0 = count(lambda c, r0, k: k >= 0)
        c0p = count(lambda c, r0, k: k >= 1)
        n_valid = (jq * qb + 1 + lax.broadcasted_iota(jnp.int32, (1, qb), 1)).astype(F32)
        pos = c0p >= kf
        zero = (c0 >= kf) & (c0p < kf)
        lo0 = jnp.where(pos, 1, jnp.where(zero, 0, kmin))
        hi0 = jnp.where(pos, kmax + 1, jnp.where(zero, 1, 0))
        c_lo0 = jnp.where(pos, c0p, jnp.where(zero, c0, n_valid))
        c_hi0 = jnp.where(pos, 0.0, jnp.where(zero, c0p, c0))

        def bis_cond(st):
            return (st[0] < 64) & (st[5] > 0)

        def bis_step(it, lo, hi, c_lo, c_hi):
            mid_i = (lo >> 1) + (hi >> 1) + (lo & hi & 1)
            mid_v = tokey(0.5 * unkey(lo) + 0.5 * unkey(hi))
            use_v = (it < 24) & (mid_v > lo) & (mid_v < hi)
            mid = jnp.where(use_v, mid_v, mid_i)
            cnt = count(lambda c, r0, k: k >= mid)
            ge = cnt >= kf
            return (jnp.where(ge, mid, lo), jnp.where(ge, hi, mid),
                    jnp.where(ge, cnt, c_lo), jnp.where(ge, c_hi, cnt))

        def bis_body(st):
            it, lo, hi, c_lo, c_hi, _ = st
            for _ in range(BIS_UNROLL):
                lo, hi, c_lo, c_hi = bis_step(it, lo, hi, c_lo, c_hi)
                it = it + 1
            active = (lo + 1 < hi) & (c_lo != kf)
            return it, lo, hi, c_lo, c_hi, jnp.max(jnp.where(active, 1, 0))

        active0 = (lo0 + 1 < hi0) & (c_lo0 != kf)
        st0 = (jnp.int32(0), lo0, hi0, c_lo0, c_hi0, jnp.max(jnp.where(active0, 1, 0)))
        _, lo, hi, c_lo, c_hi, _ = lax.while_loop(bis_cond, bis_body, st0)
        tie = c_lo != kf

        def tie_break():
            need = kf - c_hi

            def tie_body(_, st):
                jlo, jhi = st
                mid = (jlo + jhi) >> 1
                cnt = count(lambda c, r0, k: (k == lo) & (c * kc + r0 + krow_sub <= mid))
                ge = cnt >= need
                return jnp.where(ge, jlo, mid), jnp.where(ge, mid, jhi)

            _, jhi = lax.fori_loop(0, 13, tie_body, (jnp.full((1, qb), -1, jnp.int32),
                                                     jnp.full((1, qb), 8191, jnp.int32)))
            return jnp.where(tie, jhi, INT_MAX)

        jmax = lax.cond(jnp.max(jnp.where(tie, 1, 0)) > 0, tie_break,
                        lambda: jnp.full((1, qb), INT_MAX, jnp.int32))
        return lo, jmax

    thr, jmax = lax.cond(jq >= 1, select_topk, select_all)

    def sel_body(c, carry):
        k = key_scr[c]
        sel = (k > thr) | ((k == thr) & (c * kc + krow <= jmax))
        key_scr[c] = lax.bitcast_convert_type(jnp.where(sel, 0.0, NEG).T, jnp.int32)
        return carry

    lax.fori_loop(0, nch, sel_body, 0)

    l_scr[...] = jnp.zeros(l_scr.shape, F32)
    acc_scr[...] = jnp.zeros(acc_scr.shape, F32)

    def kv_chunk(c):
        return ckv_ref[0, pl.ds(pl.multiple_of(c * kc, kc), kc), :]

    def qk(c, slot):
        s_scr[slot] = lax.dot_general(ql_ref[0].reshape(rows, B_KV_RANK), kv_chunk(c), _NT,
                                      preferred_element_type=F32)

    def logits(c, slot, near, hd, r0):
        sh = s_scr[slot, hd * qb + r0:hd * qb + r0 + SUB_B]
        if near == "cur":
            t0 = bias_ref[hd, r0 % QB_A:r0 % QB_A + SUB_B, LANES:]
            t1 = bias_ref[hd, r0 % QB_A:r0 % QB_A + SUB_B, :LANES]
            sh = sh + jnp.concatenate([t0 if r0 < QB_A else t1, t0], axis=1)
        elif near == "prev" and r0 < QB_A:
            t1 = bias_ref[hd, r0:r0 + SUB_B, :LANES]
            sh = sh + jnp.concatenate([jnp.zeros_like(t1), t1], axis=1)
        return sh + lax.bitcast_convert_type(key_scr[c, r0:r0 + SUB_B, :], F32)

    def sub_tiles():
        for hd in range(B_HEADS):
            for r0 in range(0, qb, SUB_B):
                yield hd, r0, slice(hd * qb + r0, hd * qb + r0 + SUB_B)

    if bound:
        kn2 = lax.fori_loop(0, nch, lambda c, a: jnp.maximum(a, kn2_ref[0, c]), jnp.zeros((8, LANES), F32))
        bmax = bmax_ref[0]
        for hd, r0, sl in sub_tiles():
            x = ql_ref[0, hd, r0:r0 + SUB_B, :].astype(F32)
            qn2 = jnp.sum(x * x, axis=-1, keepdims=True)
            m_scr[sl] = jnp.sqrt(qn2 * kn2[:1]) * (1.0 + 2.0 ** -10) + bmax

        def sm(c, slot, near):
            for hd, r0, sl in sub_tiles():
                m = m_scr[sl]
                p = jnp.exp2(logits(c, slot, near, hd, r0) - jnp.concatenate([m, m], axis=1))
                l_scr[sl] += p[:, :LANES] + p[:, LANES:]
                p_scr[slot, sl] = p.astype(BF16)

        def pv(c, slot):
            o = jnp.dot(p_scr[slot], kv_chunk(c), preferred_element_type=F32)
            for r0 in range(0, rows, SUB_B):
                acc_scr[r0:r0 + SUB_B] += o[r0:r0 + SUB_B]

        def stage(t, slot):
            qk(t, slot)
            pv(t - 2, slot)
            sm(t - 1, 1 - slot, None)

        def tail(slot):
            qk(jq, slot)
            pv(jq - 2, slot)
            sm(jq - 1, 1 - slot, "prev")
            pv(jq - 1, 1 - slot)
            sm(jq, slot, "cur")
            pv(jq, slot)

        @pl.when(jq >= 2)
        def _():
            qk(0, 0)
            qk(1, 1)
            sm(0, 0, None)

            def two_stages(i, carry):
                stage(2 + 2 * i, 0)
                stage(3 + 2 * i, 1)
                return carry

            lax.fori_loop(0, (jq - 2) // 2, two_stages, 0)

            @pl.when(jq % 2 == 0)
            def _():
                tail(0)

            @pl.when(jq % 2 == 1)
            def _():
                stage(jq - 1, 0)
                tail(1)

        @pl.when(jq < 2)
        def _():
            @pl.when(jq == 1)
            def _():
                qk(0, 0)
                sm(0, 0, "prev")
                pv(0, 0)

            qk(jq, 1)
            sm(jq, 1, "cur")
            pv(jq, 1)
    else:
        m_scr[...] = jnp.full(m_scr.shape, -jnp.inf, F32)

        def attend(c, near):
            qk(c, 0)
            for hd, r0, sl in sub_tiles():
                m_old = m_scr[sl]
                m_new = jnp.maximum(m_old, jnp.max(logits(c, 0, near, hd, r0), axis=-1, keepdims=True))
                m_scr[sl] = m_new
                alpha_scr[sl] = jnp.exp2(m_old - m_new)
            for hd, r0, sl in sub_tiles():
                m_new = m_scr[sl]
                p = jnp.exp2(logits(c, 0, near, hd, r0) - jnp.concatenate([m_new, m_new], axis=1))
                l_scr[sl] = alpha_scr[sl] * l_scr[sl] + (p[:, :LANES] + p[:, LANES:])
                p_scr[0, sl] = p.astype(BF16)
            o = jnp.dot(p_scr[0], kv_chunk(c), preferred_element_type=F32)
            for r0 in range(0, rows, SUB_B):
                sl = slice(r0, r0 + SUB_B)
                alpha = alpha_scr[sl]
                acc_scr[sl] = acc_scr[sl] * jnp.concatenate([alpha, alpha], axis=1) + o[sl]

        def far_body(c, carry):
            attend(c, None)
            return carry

        lax.fori_loop(0, jq - 1, far_body, 0)

        @pl.when(jq >= 1)
        def _():
            attend(jq - 1, "prev")

        attend(jq, "cur")

    l_min = jnp.full((qb, 1), jnp.inf, F32)
    for pair in range(B_HEADS // 2):
        y2 = jnp.zeros((qb, LANES), F32)
        for par in range(2):
            hd = 2 * pair + par
            sl = slice(hd * qb, (hd + 1) * qb)
            l = jnp.sum(l_scr[sl], axis=-1, keepdims=True)
            l_min = jnp.minimum(l_min, l)
            o_lat = acc_scr[sl] / l
            y2 = y2 + jnp.dot(o_lat.astype(BF16), wuv_ref[pair, par], preferred_element_type=F32)
        z = zb_ref[:, pair * LANES:(pair + 1) * LANES]
        o_ref[:, pair * LANES:(pair + 1) * LANES] = (y2 * _silu(z)).astype(o_ref.dtype)
    safe = jnp.min(l_min) >= L_SAFE
    flag_ref[0, 0] = jnp.full(flag_ref.shape[2:], jnp.where(safe, 0.0, 1.0), F32)


def _mixer_b(bmax, ql, qi, wi, ckvn, klo, khi, kn2, proj, bias_b, wuv_pairs, layer, n_batch, lp, bound):
    nb = lp // QB_B
    rows = n_batch * lp
    whole = lambda b, j: (b, 0, 0)
    once = pl.Buffered(1)
    return pl.pallas_call(
        functools.partial(_mixer_b_kernel, bound=bound),
        grid=(n_batch, nb),
        in_specs=[
            pl.BlockSpec(memory_space=pltpu.SMEM),
            pl.BlockSpec((1, B_HEADS, QB_B, B_KV_RANK), lambda b, j: (b, 0, j, 0)),
            pl.BlockSpec((1, IDX_HEADS // 2, QB_B, LANES), lambda b, j: (b, 0, j, 0)),
            pl.BlockSpec((1, QB_B, LANES), lambda b, j: (b, j, 0)),
            pl.BlockSpec((1, lp, B_KV_RANK), whole, pipeline_mode=once),
            pl.BlockSpec((1, lp, LANES), whole, pipeline_mode=once),
            pl.BlockSpec((1, lp, LANES), whole, pipeline_mode=once),
            pl.BlockSpec((1, nb, 8, LANES), lambda b, j: (b, 0, 0, 0), pipeline_mode=once),
            pl.BlockSpec((QB_B, B_WIDTH), lambda b, j: (b * nb + j, C_ZB // B_WIDTH)),
            pl.BlockSpec((B_HEADS, QB_A, 2 * LANES), lambda b, j: (0, 0, 0), pipeline_mode=once),
            pl.BlockSpec((None, B_HEADS // 2, 2, B_KV_RANK, LANES), lambda b, j: (layer, 0, 0, 0, 0),
                         pipeline_mode=once),
        ],
        out_specs=[pl.BlockSpec((QB_B, B_WIDTH), lambda b, j: (b * nb + j, 0)),
                   pl.BlockSpec((1, 1, 8, LANES), lambda b, j: (b, j, 0, 0))],
        out_shape=[jax.ShapeDtypeStruct((rows, B_WIDTH), BF16),
                   jax.ShapeDtypeStruct((n_batch, nb, 8, LANES), F32)],
        scratch_shapes=[
            pltpu.VMEM((nb, QB_B, QB_B), jnp.int32),
            pltpu.VMEM((2, B_HEADS * QB_B, QB_B), F32),
            pltpu.VMEM((B_HEADS * QB_B, B_KV_RANK), F32),
            pltpu.VMEM((B_HEADS * QB_B, LANES), F32),
            pltpu.VMEM((B_HEADS * QB_B, LANES), F32),
            pltpu.VMEM((B_HEADS * QB_B, LANES), F32),
            pltpu.VMEM((2, B_HEADS * QB_B, QB_B), BF16),
        ],
        compiler_params=_cparams(2),
        name="mixer_b" if bound else "mixer_b_running_max",
    )(bmax, ql, qi, wi, ckvn, klo, khi, kn2, proj, bias_b, wuv_pairs)


def _merge_kernel(ya_ref, yb_ref, ga_ref, gb_ref, h_ref, wpa_ref, wpb_ref, wo_ref, g_ref, *out_refs,
                  last):
    pa = jnp.dot(ya_ref[...], wpa_ref[...], preferred_element_type=F32)
    pb = jnp.dot(yb_ref[...], wpb_ref[...], preferred_element_type=F32)
    merged = _sigmoid(ga_ref[...]) * pa + _sigmoid(gb_ref[...]) * pb
    h_new = h_ref[...] + jnp.dot(merged.astype(BF16), wo_ref[...], preferred_element_type=F32)
    normed = _rms(h_new, g_ref[...])
    if last:
        out_refs[0][...] = normed
    else:
        out_refs[0][...] = h_new
        out_refs[1][...] = normed.astype(BF16)


def _merge(ya, yb, gates, h, wpa, wpb, wo, g_next, layer, last, tm=256):
    rows, d = h.shape
    const = lambda i: (0, 0)
    lay = lambda i: (layer, 0, 0)
    once = pl.Buffered(1)
    if last:
        out_specs = [pl.BlockSpec((tm, d), lambda i: (i, 0))]
        out_shape = [jax.ShapeDtypeStruct((rows, d), F32)]
    else:
        out_specs = [pl.BlockSpec((tm, d), lambda i: (i, 0)), pl.BlockSpec((tm, d), lambda i: (i, 0))]
        out_shape = [jax.ShapeDtypeStruct((rows, d), F32), jax.ShapeDtypeStruct((rows, d), BF16)]
    return pl.pallas_call(
        functools.partial(_merge_kernel, last=last),
        grid=(rows // tm,),
        in_specs=[
            pl.BlockSpec((tm, A_WIDTH), lambda i: (i, 0)),
            pl.BlockSpec((tm, B_WIDTH), lambda i: (i, 0)),
            pl.BlockSpec((tm, d), lambda i: (i, 0)),
            pl.BlockSpec((tm, d), lambda i: (i, 1)),
            pl.BlockSpec((tm, d), lambda i: (i, 0)),
            pl.BlockSpec((None, A_WIDTH, d), lay, pipeline_mode=once),
            pl.BlockSpec((None, B_WIDTH, d), lay, pipeline_mode=once),
            pl.BlockSpec((None, d, d), lay, pipeline_mode=once),
            pl.BlockSpec((1, d), const),
        ],
        out_specs=out_specs,
        out_shape=out_shape,
        compiler_params=_cparams(1),
        name="merge_out",
    )(ya, yb, gates, gates, h, wpa, wpb, wo, g_next.reshape(1, d))


def _pack_weights(w_in, w_uk, w_uv, w_qb, w_iq, w_proj_a, w_proj_b, w_out):
    depth = w_in.shape[0]
    sec = lambda a, b: w_in[:, :, a:b]
    kidx, widx = sec(4096, 4160), sec(4160, 4168)
    z64 = jnp.zeros_like(kidx)
    zpad = jnp.zeros(widx.shape[:2] + (LANES - IDX_HEADS,), w_in.dtype)
    w_small = jnp.concatenate([kidx, z64, z64, kidx, widx, zpad], axis=-1).astype(BF16)
    uk = (w_uk * (HEAD_DIM ** -0.5 * LOG2E)).reshape(depth, B_HEADS // 2, 2, HEAD_DIM, B_KV_RANK)
    zuk = jnp.zeros_like(uk[:, :, 0])
    wuk_bd = jnp.concatenate([jnp.concatenate([uk[:, :, 0], zuk], axis=-1),
                              jnp.concatenate([zuk, uk[:, :, 1]], axis=-1)], axis=-2).astype(BF16)
    uv = w_uv.reshape(depth, B_HEADS // 2, 2, B_KV_RANK, HEAD_DIM)
    zuv = jnp.zeros_like(uv[:, :, 0])
    wuv_pairs = jnp.stack([jnp.concatenate([uv[:, :, 0], zuv], axis=-1),
                           jnp.concatenate([zuv, uv[:, :, 1]], axis=-1)], axis=2).astype(BF16)
    w_iq8 = (w_iq * (IDX_DIM ** -0.5)).astype(BF16)
    return (w_small, wuk_bd, wuv_pairs, w_qb.astype(BF16), w_iq8,
            w_proj_a.astype(BF16), w_proj_b.astype(BF16), w_out.astype(BF16))


def kernel(x, meta_tokens, bias_table, norm_g, w_in, q_norm_g, kv_norm_g, w_qb, w_iq, w_uk, w_uv,
           sinks, w_proj_a, w_proj_b, w_out, final_g):
    n_batch, seq, d = x.shape
    depth = w_in.shape[0]
    length = seq + N_META
    lp = -(-length // QB_B) * QB_B
    assert min(TOPK_MAX, seq // 4) == TOPK_MAX and d == D_MODEL

    (w_small, wuk_bd, wuv_pairs, wqb, wiq, wpa, wpb, wo) = _pack_weights(
        w_in, w_uk, w_uv, w_qb, w_iq, w_proj_a, w_proj_b, w_out)
    w_t = jnp.swapaxes(w_in, 1, 2)
    bias_a, bias_b = _bias_tiles(bias_table)
    bmax = jnp.max(jnp.abs(bias_b)).reshape(1)

    meta = jnp.broadcast_to(meta_tokens.astype(x.dtype)[None], (n_batch, N_META, d))
    pad = jnp.zeros((n_batch, lp - length, d), x.dtype)
    h = jnp.concatenate([meta, x, pad], axis=1).reshape(n_batch * lp, d)

    u = _rmsnorm(h, norm_g[0])
    out = None
    for l in range(depth):
        proj = _in_proj(u, w_t, l)
        gates = _gate_proj(u, w_t, l, GATE_COL0, 2 * D_MODEL)
        ya = _mixer_a(proj, sinks[l], bias_a, n_batch, lp)
        ql, qi, wi, ckvn, klo, khi, kn2 = _bprep(
            proj, u, q_norm_g[l].reshape(1, -1), kv_norm_g[l].reshape(1, -1),
            wqb, wiq, wuk_bd, w_small, l, n_batch, lp)
        b_args = (bmax, ql, qi, wi, ckvn, klo, khi, kn2, proj, bias_b, wuv_pairs, l, n_batch, lp)
        yb, flags = _mixer_b(*b_args, bound=True)
        yb = lax.cond(jnp.max(flags) > 0.0, lambda: _mixer_b(*b_args, bound=False)[0], lambda: yb)
        last = l == depth - 1
        g_next = final_g if last else norm_g[l + 1]
        res = _merge(ya, yb, gates, h, wpa, wpb, wo, g_next, l, last)
        if last:
            out = res[0]
        else:
            h, u = res
    return out.reshape(n_batch, lp, d)[:, N_META:length]
```

```python
import functools
import math

import numpy as np
import jax
import jax.numpy as jnp
from jax import lax
from jax.experimental import pallas as pl
from jax.experimental.pallas import tpu as pltpu

D_MODEL = 2048
N_META = 16
WINDOW = 128
HEAD_DIM = 64
A_HEADS = 16
A_WIDTH = 1024
B_HEADS = 16
B_WIDTH = 1024
B_Q_RANK = 512
B_KV_RANK = 256
IDX_HEADS = 8
IDX_DIM = 64
TOPK_MAX = 256
N_BUCKETS = 32
MAX_DISTANCE = 128
EPS = 1e-6
NEG = -1e30

LANES = 128
PROJ_ROW_TILES = 8
QB_A = 128
QB_B = 256
SUB_A = 32
SUB_B = 32
L_SAFE = 2.0 ** -100
BIS_UNROLL = 3
LOG2E = math.log2(math.e)
VMEM_LIMIT = 56 * 1024 * 1024
INT_MIN = -(2 ** 31)
INT_MAX = 2 ** 31 - 1

C_QA, C_ZA, C_ZB, C_CQ, C_CKV, C_KA, C_VA = (0, 1024, 2048, 3072, 3584, 3840, 3968)
GATE_COL0 = 4168

F32 = jnp.float32
BF16 = jnp.bfloat16
_NT = (((1,), (1,)), ((), ()))


def _cparams(n_grid):
    return pltpu.CompilerParams(
        dimension_semantics=("arbitrary",) * n_grid,
        vmem_limit_bytes=VMEM_LIMIT)


def _t5_bucket_np(d):
    max_exact = N_BUCKETS // 2
    nf = np.maximum(d, 1).astype(np.float32)
    large = max_exact + (np.log(nf / np.float32(max_exact)) / np.float32(math.log(MAX_DISTANCE / max_exact))
                         * np.float32(N_BUCKETS - max_exact)).astype(np.int32)
    large = np.minimum(large, N_BUCKETS - 1)
    return np.where(d < max_exact, d, large).astype(np.int32)


def _bias_tiles_kernel(tab_ref, idx_ref, out_a_ref, out_b_ref):
    h = pl.program_id(0)
    idx = idx_ref[...]
    acc_a = jnp.zeros(idx.shape, F32)
    acc_b = jnp.zeros(idx.shape, F32)
    far_b = tab_ref[N_BUCKETS - 1, A_HEADS + h]
    for b in range(N_BUCKETS):
        hit = idx == b
        acc_a = jnp.where(hit, tab_ref[b, h] * LOG2E, acc_a)
        acc_b = jnp.where(hit, (tab_ref[b, A_HEADS + h] - far_b) * LOG2E, acc_b)
    out_a_ref[0] = acc_a
    out_b_ref[0] = acc_b[:, :2 * LANES]


def _bias_tiles(bias_table):
    r = np.arange(QB_A)[:, None]
    k = np.arange(QB_A)[None, :]
    prev = _t5_bucket_np(np.maximum(QB_A + r - k, 0))
    cur = _t5_bucket_np(np.maximum(r - k, 0))
    far = np.full((QB_A, QB_A), N_BUCKETS - 1, np.int32)
    idx = jnp.asarray(np.concatenate([prev, cur, far], axis=1))
    return pl.pallas_call(
        _bias_tiles_kernel,
        grid=(A_HEADS,),
        in_specs=[pl.BlockSpec(memory_space=pltpu.SMEM),
                  pl.BlockSpec((QB_A, 3 * LANES), lambda h: (0, 0))],
        out_specs=[pl.BlockSpec((1, QB_A, 3 * LANES), lambda h: (h, 0, 0)),
                   pl.BlockSpec((1, QB_A, 2 * LANES), lambda h: (h, 0, 0))],
        out_shape=[jax.ShapeDtypeStruct((A_HEADS, QB_A, 3 * LANES), F32),
                   jax.ShapeDtypeStruct((B_HEADS, QB_A, 2 * LANES), F32)],
        compiler_params=_cparams(1),
        name="bias_tiles",
    )(bias_table, idx)


def _rms(x, g):
    return x * lax.rsqrt(jnp.mean(x * x, axis=-1, keepdims=True) + EPS) * g


def _rmsnorm_kernel(h_ref, g_ref, u_ref):
    u_ref[...] = _rms(h_ref[...], g_ref[...]).astype(u_ref.dtype)


def _rmsnorm(h, g, tm=512):
    rows, d = h.shape
    return pl.pallas_call(
        _rmsnorm_kernel,
        grid=(rows // tm,),
        in_specs=[pl.BlockSpec((tm, d), lambda i: (i, 0)),
                  pl.BlockSpec((1, d), lambda i: (0, 0))],
        out_specs=pl.BlockSpec((tm, d), lambda i: (i, 0)),
        out_shape=jax.ShapeDtypeStruct((rows, d), BF16),
        compiler_params=_cparams(1),
        name="rmsnorm",
    )(h, g.reshape(1, d))


_IN_BLOCKS = ((0, 1, 2, 3), (5, 6, 7, 8), (12, 13, 14, 15), (9, 10, 11, 4))
IN_BLK = 256


def _in_proj_kernel(x_ref, *refs):
    w_refs, o_ref, wbf_scr = refs[:-2], refs[-2], refs[-1]

    @pl.when(pl.program_id(1) == 0)
    def _():
        r0 = 0
        for w_ref in w_refs:
            wbf_scr[r0:r0 + w_ref.shape[0], :] = w_ref[...].astype(BF16)
            r0 += w_ref.shape[0]

    o_ref[...] = lax.dot_general(x_ref[...], wbf_scr[...], _NT, preferred_element_type=F32)


def _in_proj(x, w_t, layer):
    rows, kdim = x.shape
    tm = rows // PROJ_ROW_TILES
    tn = 4 * IN_BLK

    def w_spec(s):
        def index(j, i):
            blk = jnp.int32(_IN_BLOCKS[0][s])
            for t in range(1, len(_IN_BLOCKS)):
                blk = jnp.where(j == t, _IN_BLOCKS[t][s], blk)
            return layer, blk, 0
        return pl.BlockSpec((None, IN_BLK, kdim), index)

    return pl.pallas_call(
        _in_proj_kernel,
        grid=(len(_IN_BLOCKS), rows // tm),
        in_specs=[pl.BlockSpec((tm, kdim), lambda j, i: (i, 0))] + [w_spec(s) for s in range(4)],
        out_specs=pl.BlockSpec((tm, tn), lambda j, i: (i, j)),
        out_shape=jax.ShapeDtypeStruct((rows, len(_IN_BLOCKS) * tn), F32),
        scratch_shapes=[pltpu.VMEM((tn, kdim), BF16)],
        compiler_params=_cparams(2),
        name="in_proj",
    )(x, w_t, w_t, w_t, w_t)


def _gate_proj(x, w_t, layer, col0, n, tn=1024):
    rows, kdim = x.shape
    tm = rows // PROJ_ROW_TILES
    return pl.pallas_call(
        _in_proj_kernel,
        grid=(n // tn, rows // tm),
        in_specs=[pl.BlockSpec((tm, kdim), lambda j, i: (i, 0)),
                  pl.BlockSpec((None, pl.Element(tn), pl.Element(kdim)),
                               lambda j, i: (layer, pl.multiple_of(col0 + j * tn, 8), 0))],
        out_specs=pl.BlockSpec((tm, tn), lambda j, i: (i, j)),
        out_shape=jax.ShapeDtypeStruct((rows, n), F32),
        scratch_shapes=[pltpu.VMEM((tn, kdim), BF16)],
        compiler_params=_cparams(2),
        name="gate_proj",
    )(x, w_t)


def _silu(z):
    return z / (1.0 + jnp.exp(-z))


def _sigmoid(z):
    return 1.0 / (1.0 + jnp.exp(-z))


def _mixer_a_kernel(sink_ref, q_ref, kc_ref, vc_ref, kp_ref, vp_ref, km_ref, vm_ref, za_ref,
                    bias_ref, o_ref, s_scr, p_scr, m_scr, d_scr, msk_scr):
    j = pl.program_id(1)
    nk = 3 * QB_A
    q = (q_ref[...] * (HEAD_DIM ** -0.5 * LOG2E)).astype(BF16)
    kcat = jnp.concatenate([kp_ref[...], kc_ref[...], km_ref[...]], axis=0)
    vcat = jnp.concatenate([vp_ref[...], vc_ref[...], vm_ref[...]], axis=0)
    lane = lax.broadcasted_iota(jnp.int32, (nk, LANES), 1)
    lo = lane < HEAD_DIM
    kswap = pltpu.roll(kcat, HEAD_DIM, 1)
    vswap = pltpu.roll(vcat, HEAD_DIM, 1)
    zero = jnp.zeros_like(kcat)
    k_half = ((jnp.where(lo, kcat, zero).astype(BF16), jnp.where(lo, zero, kswap).astype(BF16)),
              (jnp.where(lo, kswap, zero).astype(BF16), jnp.where(lo, zero, kcat).astype(BF16)))
    v_half = ((jnp.where(lo, vcat, zero).astype(BF16), jnp.where(lo, zero, vswap).astype(BF16)),
              (jnp.where(lo, vswap, zero).astype(BF16), jnp.where(lo, zero, vcat).astype(BF16)))

    r = lax.broadcasted_iota(jnp.int32, (QB_A, nk), 0)
    c = lax.broadcasted_iota(jnp.int32, (QB_A, nk), 1)
    jv = jnp.zeros((QB_A, nk), jnp.int32) + j
    in_prev = (c < QB_A) & (jv >= 1) & ((c > r) | ((jv == 1) & (c < N_META)))
    in_cur = (c >= QB_A) & (c < 2 * QB_A) & (c - QB_A <= r)
    in_meta = (c >= 2 * QB_A) & (jv >= 2) & (c - 2 * QB_A < N_META)
    msk_scr[...] = jnp.where(in_prev | in_cur | in_meta, 0.0, NEG)
    half = lax.broadcasted_iota(jnp.int32, (QB_A, LANES), 1) < HEAD_DIM
    pairs = A_HEADS // 4

    def logits(g, i, par, r0):
        hd = 2 * (g * pairs + i) + par
        s = s_scr[i * QB_A + r0:i * QB_A + r0 + SUB_A, par * nk:(par + 1) * nk]
        return s + bias_ref[hd, r0:r0 + SUB_A, :] + msk_scr[r0:r0 + SUB_A, :], sink_ref[hd] * LOG2E

    for g in range(2):
        q4 = jnp.concatenate([q[:, (g * pairs + i) * LANES:(g * pairs + i + 1) * LANES]
                              for i in range(pairs)], axis=0)
        s_scr[...] = lax.dot_general(q4, jnp.concatenate(k_half[g], axis=0), _NT,
                                     preferred_element_type=F32)
        for i in range(pairs):
            for par in range(2):
                for r0 in range(0, QB_A, SUB_A):
                    lg, sink = logits(g, i, par, r0)
                    m = jnp.maximum(jnp.max(lg, axis=-1, keepdims=True), sink)
                    m_scr[par, i * QB_A + r0:i * QB_A + r0 + SUB_A] = jnp.broadcast_to(m, (SUB_A, LANES))
        for i in range(pairs):
            for par in range(2):
                for r0 in range(0, QB_A, SUB_A):
                    sl = slice(i * QB_A + r0, i * QB_A + r0 + SUB_A)
                    lg, sink = logits(g, i, par, r0)
                    m = m_scr[par, sl]
                    p = jnp.exp2(lg - jnp.concatenate([m] * 3, axis=1))
                    d_scr[par, sl] = jnp.broadcast_to(
                        jnp.sum(p, axis=-1, keepdims=True), (SUB_A, LANES)) + jnp.exp2(sink - m)
                    p_scr[sl, par * nk:(par + 1) * nk] = p.astype(BF16)
        pv = jnp.dot(p_scr[...], jnp.concatenate(v_half[g], axis=0), preferred_element_type=F32)
        for i in range(pairs):
            sl = slice(i * QB_A, (i + 1) * QB_A)
            col = slice((g * pairs + i) * LANES, (g * pairs + i + 1) * LANES)
            o2 = pv[sl] / jnp.where(half, d_scr[0, sl], d_scr[1, sl])
            o_ref[:, col] = (o2 * _silu(za_ref[:, col])).astype(o_ref.dtype)


def _mixer_a(proj, sinks, bias_a, n_batch, lp):
    nb = lp // QB_A
    rows = n_batch * lp
    ck, cv = C_KA // LANES, C_VA // LANES
    cur = lambda b, j: b * nb + j
    prev = lambda b, j: b * nb + jnp.maximum(j - 1, 0)
    first = lambda b, j: b * nb
    return pl.pallas_call(
        _mixer_a_kernel,
        grid=(n_batch, nb),
        in_specs=[
            pl.BlockSpec(memory_space=pltpu.SMEM),
            pl.BlockSpec((QB_A, A_WIDTH), lambda b, j: (cur(b, j), C_QA // A_WIDTH)),
            pl.BlockSpec((QB_A, LANES), lambda b, j: (cur(b, j), ck)),
            pl.BlockSpec((QB_A, LANES), lambda b, j: (cur(b, j), cv)),
            pl.BlockSpec((QB_A, LANES), lambda b, j: (prev(b, j), ck)),
            pl.BlockSpec((QB_A, LANES), lambda b, j: (prev(b, j), cv)),
            pl.BlockSpec((QB_A, LANES), lambda b, j: (first(b, j), ck)),
            pl.BlockSpec((QB_A, LANES), lambda b, j: (first(b, j), cv)),
            pl.BlockSpec((QB_A, A_WIDTH), lambda b, j: (cur(b, j), C_ZA // A_WIDTH)),
            pl.BlockSpec((A_HEADS, QB_A, 3 * LANES), lambda b, j: (0, 0, 0)),
        ],
        out_specs=pl.BlockSpec((QB_A, A_WIDTH), lambda b, j: (cur(b, j), 0)),
        out_shape=jax.ShapeDtypeStruct((rows, A_WIDTH), BF16),
        scratch_shapes=[
            pltpu.VMEM((A_HEADS // 4 * QB_A, 6 * QB_A), F32),
            pltpu.VMEM((A_HEADS // 4 * QB_A, 6 * QB_A), BF16),
            pltpu.VMEM((2, A_HEADS // 4 * QB_A, LANES), F32),
            pltpu.VMEM((2, A_HEADS // 4 * QB_A, LANES), F32),
            pltpu.VMEM((QB_A, 3 * QB_A), F32),
        ],
        compiler_params=_cparams(2),
        name="mixer_a",
    )(sinks, proj, proj, proj, proj, proj, proj, proj, proj, bias_a)


def _bprep_kernel(cq_ref, ckv_ref, u_ref, qg_ref, kvg_ref, wqb_ref, wiq_ref, wuk_ref, wsm_ref,
                  ql_ref, qi_ref, wi_ref, ckvn_ref, klo_ref, khi_ref, kn2_ref):
    cqn = _rms(cq_ref[...], qg_ref[...]).astype(BF16)
    ckvn = _rms(ckv_ref[...], kvg_ref[...]).astype(ckvn_ref.dtype)
    ckvn_ref[0] = ckvn
    kn2 = jnp.sum(jnp.square(ckvn.astype(F32)), axis=-1, keepdims=True)
    kn2_ref[0, 0] = jnp.broadcast_to(jnp.max(kn2, axis=0, keepdims=True), kn2_ref.shape[2:])
    qb = jnp.dot(cqn, wqb_ref[...], preferred_element_type=F32).astype(BF16)
    for pair in range(B_HEADS // 2):
        ql2 = jnp.dot(qb[:, pair * LANES:(pair + 1) * LANES], wuk_ref[pair],
                      preferred_element_type=F32)
        ql_ref[0, 2 * pair] = ql2[:, :B_KV_RANK].astype(ql_ref.dtype)
        ql_ref[0, 2 * pair + 1] = ql2[:, B_KV_RANK:].astype(ql_ref.dtype)
    qi = jnp.dot(cqn, wiq_ref[...], preferred_element_type=F32)
    for pair in range(IDX_HEADS // 2):
        qi_ref[0, pair] = qi[:, pair * LANES:(pair + 1) * LANES].astype(qi_ref.dtype)
    small = jnp.dot(u_ref[...], wsm_ref[...], preferred_element_type=F32)
    klo_ref[0] = small[:, :LANES].astype(klo_ref.dtype)
    khi_ref[0] = small[:, LANES:2 * LANES].astype(khi_ref.dtype)
    wi_ref[0] = small[:, 2 * LANES:] * (IDX_HEADS ** -0.5)


def _bprep(proj, u, qg, kvg, wqb, wiq, wuk_bd, wsmall, layer, n_batch, lp, tm=QB_B):
    nt = lp // tm
    row = lambda b, i: b * nt + i
    const2 = lambda b, i: (0, 0)
    lay2 = lambda b, i: (layer, 0, 0)
    return pl.pallas_call(
        _bprep_kernel,
        grid=(n_batch, nt),
        in_specs=[
            pl.BlockSpec((tm, B_Q_RANK), lambda b, i: (row(b, i), C_CQ // B_Q_RANK)),
            pl.BlockSpec((tm, B_KV_RANK), lambda b, i: (row(b, i), C_CKV // B_KV_RANK)),
            pl.BlockSpec((tm, D_MODEL), lambda b, i: (row(b, i), 0)),
            pl.BlockSpec((1, B_Q_RANK), const2),
            pl.BlockSpec((1, B_KV_RANK), const2),
            pl.BlockSpec((None, B_Q_RANK, B_WIDTH), lay2),
            pl.BlockSpec((None, B_Q_RANK, IDX_HEADS * IDX_DIM), lay2),
            pl.BlockSpec((None, B_HEADS // 2, LANES, 2 * B_KV_RANK), lambda b, i: (layer, 0, 0, 0)),
            pl.BlockSpec((None, D_MODEL, 3 * LANES), lay2),
        ],
        out_specs=[
            pl.BlockSpec((1, B_HEADS, tm, B_KV_RANK), lambda b, i: (b, 0, i, 0)),
            pl.BlockSpec((1, IDX_HEADS // 2, tm, LANES), lambda b, i: (b, 0, i, 0)),
            pl.BlockSpec((1, tm, LANES), lambda b, i: (b, i, 0)),
            pl.BlockSpec((1, tm, B_KV_RANK), lambda b, i: (b, i, 0)),
            pl.BlockSpec((1, tm, LANES), lambda b, i: (b, i, 0)),
            pl.BlockSpec((1, tm, LANES), lambda b, i: (b, i, 0)),
            pl.BlockSpec((1, 1, 8, LANES), lambda b, i: (b, i, 0, 0)),
        ],
        out_shape=[
            jax.ShapeDtypeStruct((n_batch, B_HEADS, lp, B_KV_RANK), BF16),
            jax.ShapeDtypeStruct((n_batch, IDX_HEADS // 2, lp, LANES), BF16),
            jax.ShapeDtypeStruct((n_batch, lp, LANES), F32),
            jax.ShapeDtypeStruct((n_batch, lp, B_KV_RANK), BF16),
            jax.ShapeDtypeStruct((n_batch, lp, LANES), BF16),
            jax.ShapeDtypeStruct((n_batch, lp, LANES), BF16),
            jax.ShapeDtypeStruct((n_batch, nt, 8, LANES), F32),
        ],
        compiler_params=_cparams(2),
        name="bprep",
    )(proj, proj, u, qg, kvg, wqb, wiq, wuk_bd, wsmall)


def _mixer_b_kernel(bmax_ref, ql_ref, qi_ref, wi_ref, ckv_ref, klo_ref, khi_ref, kn2_ref, zb_ref, bias_ref,
                    wuv_ref, o_ref, flag_ref, key_scr, s_scr, acc_scr, m_scr, l_scr, alpha_scr, p_scr,
                    *, bound):
    jq = pl.program_id(1)
    nch = jq + 1
    qb, kc = QB_B, QB_B
    rows = B_HEADS * qb
    kf = float(TOPK_MAX)

    qi = qi_ref[0].reshape(IDX_HEADS // 2 * qb, LANES)
    wit = wi_ref[0].T
    krow = lax.broadcasted_iota(jnp.int32, (qb, kc), 0)
    krow_sub = lax.broadcasted_iota(jnp.int32, (SUB_B, kc), 0)
    qpos = jq * qb + lax.broadcasted_iota(jnp.int32, (qb, kc), 1)

    def idx_body(c, carry):
        start = pl.multiple_of(c * kc, kc)
        sc = (lax.dot_general(klo_ref[0, pl.ds(start, kc), :], qi, _NT, preferred_element_type=F32),
              lax.dot_general(khi_ref[0, pl.ds(start, kc), :], qi, _NT, preferred_element_type=F32))
        isc = jnp.zeros((kc, qb), F32)
        for pair in range(IDX_HEADS // 2):
            for par in range(2):
                hd = 2 * pair + par
                isc = isc + wit[hd:hd + 1, :] * jnp.maximum(sc[par][:, pair * qb:(pair + 1) * qb], 0.0)
        isc = jnp.where(isc == 0.0, 0.0, isc)
        bits = lax.bitcast_convert_type(isc, jnp.int32)
        key = bits ^ ((bits >> 31) & INT_MAX)
        key_scr[c] = jnp.where(c * kc + krow <= qpos, key, INT_MIN)
        return carry

    lax.fori_loop(0, nch, idx_body, 0)

    def reduce_chunks(tile_fn, init, combine):
        n_acc = SUB_B // 8

        def body(c, accs):
            accs = list(accs)
            for r0 in range(0, kc, SUB_B):
                t = tile_fn(c, r0, key_scr[c, r0:r0 + SUB_B, :])
                t = t.reshape(n_acc, 8, qb)
                for i in range(n_acc):
                    accs[i] = combine(accs[i], t[i])
            return tuple(accs)

        accs = lax.fori_loop(0, nch, body, (init,) * n_acc)
        acc = accs[0]
        for a in accs[1:]:
            acc = combine(acc, a)
        return acc

    def count(pred_fn):
        acc = reduce_chunks(lambda c, r0, k: jnp.where(pred_fn(c, r0, k), 1.0, 0.0),
                            jnp.zeros((8, qb), F32), lambda a, b: a + b)
        return jnp.sum(acc, axis=0, keepdims=True)

    def unkey(k):
        return lax.bitcast_convert_type(k ^ ((k >> 31) & INT_MAX), F32)

    def tokey(v):
        b = lax.bitcast_convert_type(v, jnp.int32)
        return b ^ ((b >> 31) & INT_MAX)

    def select_all():
        return jnp.full((1, qb), INT_MIN, jnp.int32), jnp.full((1, qb), -1, jnp.int32)

    def select_topk():
        kmax = reduce_chunks(lambda c, r0, k: k, jnp.full((8, qb), INT_MIN, jnp.int32), jnp.maximum)
        kmax = jnp.max(kmax, axis=0, keepdims=True)
        kmin = reduce_chunks(lambda c, r0, k: jnp.where(k == INT_MIN, INT_MAX, k),
                             jnp.full((8, qb), INT_MAX, jnp.int32), jnp.minimum)
        kmin = jnp.min(kmin, axis=0, keepdims=True)
        c0 = count(lambda c, r0, k: k >= 0)
        c0p = count(lambda c, r0, k: k >= 1)
        n_valid = (jq * qb + 1 + lax.broadcasted_iota(jnp.int32, (1, qb), 1)).astype(F32)
        pos = c0p >= kf
        zero = (c0 >= kf) & (c0p < kf)
        lo0 = jnp.where(pos, 1, jnp.where(zero, 0, kmin))
        hi0 = jnp.where(pos, kmax + 1, jnp.where(zero, 1, 0))
        c_lo0 = jnp.where(pos, c0p, jnp.where(zero, c0, n_valid))
        c_hi0 = jnp.where(pos, 0.0, jnp.where(zero, c0p, c0))

        def bis_cond(st):
            return (st[0] < 64) & (st[5] > 0)

        def bis_step(it, lo, hi, c_lo, c_hi):
            mid_i = (lo >> 1) + (hi >> 1) + (lo & hi & 1)
            mid_v = tokey(0.5 * unkey(lo) + 0.5 * unkey(hi))
            use_v = (it < 24) & (mid_v > lo) & (mid_v < hi)
            mid = jnp.where(use_v, mid_v, mid_i)
            cnt = count(lambda c, r0, k: k >= mid)
            ge = cnt >= kf
            return (jnp.where(ge, mid, lo), jnp.where(ge, hi, mid),
                    jnp.where(ge, cnt, c_lo), jnp.where(ge, c_hi, cnt))

        def bis_body(st):
            it, lo, hi, c_lo, c_hi, _ = st
            for _ in range(BIS_UNROLL):
                lo, hi, c_lo, c_hi = bis_step(it, lo, hi, c_lo, c_hi)
                it = it + 1
            return it, lo, hi, c_lo, c_hi, jnp.max(jnp.where(is_active(lo, hi, c_lo, c_hi), 1, 0))

        def is_active(lo, hi, c_lo, c_hi):
            return (lo + 1 < hi) & (c_lo != kf) & (c_lo - c_hi > 2.0)

        st0 = (jnp.int32(0), lo0, hi0, c_lo0, c_hi0,
               jnp.max(jnp.where(is_active(lo0, hi0, c_lo0, c_hi0), 1, 0)))
        _, lo, hi, c_lo, c_hi, _ = lax.while_loop(bis_cond, bis_body, st0)
        open_rows = c_lo != kf

        def finish():
            top = reduce_chunks(lambda c, r0, k: jnp.where(k < hi, k, INT_MIN),
                                jnp.full((8, qb), INT_MIN, jnp.int32), jnp.maximum)
            top = jnp.max(top, axis=0, keepdims=True)
            bot = reduce_chunks(lambda c, r0, k: jnp.where(k >= lo, k, INT_MAX),
                                jnp.full((8, qb), INT_MAX, jnp.int32), jnp.minimum)
            bot = jnp.min(bot, axis=0, keepdims=True)
            thr = jnp.where(open_rows, top, lo)
            tie = open_rows & (top == bot)
            need = kf - c_hi

            def tie_break():
                def tie_body(_, st):
                    jlo, jhi = st
                    mid = (jlo + jhi) >> 1
                    cnt = count(lambda c, r0, k: (k == thr) & (c * kc + r0 + krow_sub <= mid))
                    ge = cnt >= need
                    return jnp.where(ge, jlo, mid), jnp.where(ge, mid, jhi)

                _, jhi = lax.fori_loop(0, 13, tie_body, (jnp.full((1, qb), -1, jnp.int32),
                                                         jnp.full((1, qb), 8191, jnp.int32)))
                return jnp.where(tie, jhi, INT_MAX)

            jmax = lax.cond(jnp.max(jnp.where(tie, 1, 0)) > 0, tie_break,
                            lambda: jnp.full((1, qb), INT_MAX, jnp.int32))
            return thr, jmax

        return lax.cond(jnp.max(jnp.where(open_rows, 1, 0)) > 0, finish,
                        lambda: (lo, jnp.full((1, qb), INT_MAX, jnp.int32)))

    thr, jmax = lax.cond(jq >= 1, select_topk, select_all)

    def sel_body(c, carry):
        k = key_scr[c]
        sel = (k > thr) | ((k == thr) & (c * kc + krow <= jmax))
        key_scr[c] = lax.bitcast_convert_type(jnp.where(sel, 0.0, NEG).T, jnp.int32)
        return carry

    lax.fori_loop(0, nch, sel_body, 0)

    l_scr[...] = jnp.zeros(l_scr.shape, F32)
    acc_scr[...] = jnp.zeros(acc_scr.shape, F32)

    def kv_chunk(c):
        return ckv_ref[0, pl.ds(pl.multiple_of(c * kc, kc), kc), :]

    def qk(c, slot):
        s_scr[slot] = lax.dot_general(ql_ref[0].reshape(rows, B_KV_RANK), kv_chunk(c), _NT,
                                      preferred_element_type=F32)

    def logits(c, slot, near, hd, r0):
        sh = s_scr[slot, hd * qb + r0:hd * qb + r0 + SUB_B]
        if near == "cur":
            t0 = bias_ref[hd, r0 % QB_A:r0 % QB_A + SUB_B, LANES:]
            t1 = bias_ref[hd, r0 % QB_A:r0 % QB_A + SUB_B, :LANES]
            sh = sh + jnp.concatenate([t0 if r0 < QB_A else t1, t0], axis=1)
        elif near == "prev" and r0 < QB_A:
            t1 = bias_ref[hd, r0:r0 + SUB_B, :LANES]
            sh = sh + jnp.concatenate([jnp.zeros_like(t1), t1], axis=1)
        return sh + lax.bitcast_convert_type(key_scr[c, r0:r0 + SUB_B, :], F32)

    def sub_tiles():
        for hd in range(B_HEADS):
            for r0 in range(0, qb, SUB_B):
                yield hd, r0, slice(hd * qb + r0, hd * qb + r0 + SUB_B)

    if bound:
        kn2 = lax.fori_loop(0, nch, lambda c, a: jnp.maximum(a, kn2_ref[0, c]), jnp.zeros((8, LANES), F32))
        bmax = bmax_ref[0]
        for hd, r0, sl in sub_tiles():
            x = ql_ref[0, hd, r0:r0 + SUB_B, :].astype(F32)
            qn2 = jnp.sum(x * x, axis=-1, keepdims=True)
            m_scr[sl] = jnp.sqrt(qn2 * kn2[:1]) * (1.0 + 2.0 ** -10) + bmax

        def sm(c, slot, near):
            for hd, r0, sl in sub_tiles():
                m = m_scr[sl]
                p = jnp.exp2(logits(c, slot, near, hd, r0) - jnp.concatenate([m, m], axis=1))
                l_scr[sl] += p[:, :LANES] + p[:, LANES:]
                p_scr[slot, sl] = p.astype(BF16)

        def pv(c, slot):
            o = jnp.dot(p_scr[slot], kv_chunk(c), preferred_element_type=F32)
            for r0 in range(0, rows, SUB_B):
                acc_scr[r0:r0 + SUB_B] += o[r0:r0 + SUB_B]

        def stage(t, slot):
            qk(t, slot)
            pv(t - 2, slot)
            sm(t - 1, 1 - slot, None)

        def tail(slot):
            qk(jq, slot)
            pv(jq - 2, slot)
            sm(jq - 1, 1 - slot, "prev")
            pv(jq - 1, 1 - slot)
            sm(jq, slot, "cur")
            pv(jq, slot)

        @pl.when(jq >= 2)
        def _():
            qk(0, 0)
            qk(1, 1)
            sm(0, 0, None)

            def two_stages(i, carry):
                stage(2 + 2 * i, 0)
                stage(3 + 2 * i, 1)
                return carry

            lax.fori_loop(0, (jq - 2) // 2, two_stages, 0)

            @pl.when(jq % 2 == 0)
            def _():
                tail(0)

            @pl.when(jq % 2 == 1)
            def _():
                stage(jq - 1, 0)
                tail(1)

        @pl.when(jq < 2)
        def _():
            @pl.when(jq == 1)
            def _():
                qk(0, 0)
                sm(0, 0, "prev")
                pv(0, 0)

            qk(jq, 1)
            sm(jq, 1, "cur")
            pv(jq, 1)
    else:
        m_scr[...] = jnp.full(m_scr.shape, -jnp.inf, F32)

        def attend(c, near):
            qk(c, 0)
            for hd, r0, sl in sub_tiles():
                m_old = m_scr[sl]
                m_new = jnp.maximum(m_old, jnp.max(logits(c, 0, near, hd, r0), axis=-1, keepdims=True))
                m_scr[sl] = m_new
                alpha_scr[sl] = jnp.exp2(m_old - m_new)
            for hd, r0, sl in sub_tiles():
                m_new = m_scr[sl]
                p = jnp.exp2(logits(c, 0, near, hd, r0) - jnp.concatenate([m_new, m_new], axis=1))
                l_scr[sl] = alpha_scr[sl] * l_scr[sl] + (p[:, :LANES] + p[:, LANES:])
                p_scr[0, sl] = p.astype(BF16)
            o = jnp.dot(p_scr[0], kv_chunk(c), preferred_element_type=F32)
            for r0 in range(0, rows, SUB_B):
                sl = slice(r0, r0 + SUB_B)
                alpha = alpha_scr[sl]
                acc_scr[sl] = acc_scr[sl] * jnp.concatenate([alpha, alpha], axis=1) + o[sl]

        def far_body(c, carry):
            attend(c, None)
            return carry

        lax.fori_loop(0, jq - 1, far_body, 0)

        @pl.when(jq >= 1)
        def _():
            attend(jq - 1, "prev")

        attend(jq, "cur")

    l_min = jnp.full((qb, 1), jnp.inf, F32)
    for pair in range(B_HEADS // 2):
        y2 = jnp.zeros((qb, LANES), F32)
        for par in range(2):
            hd = 2 * pair + par
            sl = slice(hd * qb, (hd + 1) * qb)
            l = jnp.sum(l_scr[sl], axis=-1, keepdims=True)
            l_min = jnp.minimum(l_min, l)
            o_lat = acc_scr[sl] / l
            y2 = y2 + jnp.dot(o_lat.astype(BF16), wuv_ref[pair, par], preferred_element_type=F32)
        z = zb_ref[:, pair * LANES:(pair + 1) * LANES]
        o_ref[:, pair * LANES:(pair + 1) * LANES] = (y2 * _silu(z)).astype(o_ref.dtype)
    safe = jnp.min(l_min) >= L_SAFE
    flag_ref[0, 0] = jnp.full(flag_ref.shape[2:], jnp.where(safe, 0.0, 1.0), F32)


def _mixer_b(bmax, ql, qi, wi, ckvn, klo, khi, kn2, proj, bias_b, wuv_pairs, layer, n_batch, lp, bound):
    nb = lp // QB_B
    rows = n_batch * lp
    whole = lambda b, j: (b, 0, 0)
    once = pl.Buffered(1)
    return pl.pallas_call(
        functools.partial(_mixer_b_kernel, bound=bound),
        grid=(n_batch, nb),
        in_specs=[
            pl.BlockSpec(memory_space=pltpu.SMEM),
            pl.BlockSpec((1, B_HEADS, QB_B, B_KV_RANK), lambda b, j: (b, 0, j, 0)),
            pl.BlockSpec((1, IDX_HEADS // 2, QB_B, LANES), lambda b, j: (b, 0, j, 0)),
            pl.BlockSpec((1, QB_B, LANES), lambda b, j: (b, j, 0)),
            pl.BlockSpec((1, lp, B_KV_RANK), whole, pipeline_mode=once),
            pl.BlockSpec((1, lp, LANES), whole, pipeline_mode=once),
            pl.BlockSpec((1, lp, LANES), whole, pipeline_mode=once),
            pl.BlockSpec((1, nb, 8, LANES), lambda b, j: (b, 0, 0, 0), pipeline_mode=once),
            pl.BlockSpec((QB_B, B_WIDTH), lambda b, j: (b * nb + j, C_ZB // B_WIDTH)),
            pl.BlockSpec((B_HEADS, QB_A, 2 * LANES), lambda b, j: (0, 0, 0), pipeline_mode=once),
            pl.BlockSpec((None, B_HEADS // 2, 2, B_KV_RANK, LANES), lambda b, j: (layer, 0, 0, 0, 0),
                         pipeline_mode=once),
        ],
        out_specs=[pl.BlockSpec((QB_B, B_WIDTH), lambda b, j: (b * nb + j, 0)),
                   pl.BlockSpec((1, 1, 8, LANES), lambda b, j: (b, j, 0, 0))],
        out_shape=[jax.ShapeDtypeStruct((rows, B_WIDTH), BF16),
                   jax.ShapeDtypeStruct((n_batch, nb, 8, LANES), F32)],
        scratch_shapes=[
            pltpu.VMEM((nb, QB_B, QB_B), jnp.int32),
            pltpu.VMEM((2, B_HEADS * QB_B, QB_B), F32),
            pltpu.VMEM((B_HEADS * QB_B, B_KV_RANK), F32),
            pltpu.VMEM((B_HEADS * QB_B, LANES), F32),
            pltpu.VMEM((B_HEADS * QB_B, LANES), F32),
            pltpu.VMEM((B_HEADS * QB_B, LANES), F32),
            pltpu.VMEM((2, B_HEADS * QB_B, QB_B), BF16),
        ],
        compiler_params=_cparams(2),
        name="mixer_b" if bound else "mixer_b_running_max",
    )(bmax, ql, qi, wi, ckvn, klo, khi, kn2, proj, bias_b, wuv_pairs)


def _merge_kernel(ya_ref, yb_ref, ga_ref, gb_ref, h_ref, wpa_ref, wpb_ref, wo_ref, g_ref, *out_refs,
                  last):
    pa = jnp.dot(ya_ref[...], wpa_ref[...], preferred_element_type=F32)
    pb = jnp.dot(yb_ref[...], wpb_ref[...], preferred_element_type=F32)
    merged = _sigmoid(ga_ref[...]) * pa + _sigmoid(gb_ref[...]) * pb
    h_new = h_ref[...] + jnp.dot(merged.astype(BF16), wo_ref[...], preferred_element_type=F32)
    normed = _rms(h_new, g_ref[...])
    if last:
        out_refs[0][...] = normed
    else:
        out_refs[0][...] = h_new
        out_refs[1][...] = normed.astype(BF16)


def _merge(ya, yb, gates, h, wpa, wpb, wo, g_next, layer, last, tm=256):
    rows, d = h.shape
    const = lambda i: (0, 0)
    lay = lambda i: (layer, 0, 0)
    once = pl.Buffered(1)
    if last:
        out_specs = [pl.BlockSpec((tm, d), lambda i: (i, 0))]
        out_shape = [jax.ShapeDtypeStruct((rows, d), F32)]
    else:
        out_specs = [pl.BlockSpec((tm, d), lambda i: (i, 0)), pl.BlockSpec((tm, d), lambda i: (i, 0))]
        out_shape = [jax.ShapeDtypeStruct((rows, d), F32), jax.ShapeDtypeStruct((rows, d), BF16)]
    return pl.pallas_call(
        functools.partial(_merge_kernel, last=last),
        grid=(rows // tm,),
        in_specs=[
            pl.BlockSpec((tm, A_WIDTH), lambda i: (i, 0)),
            pl.BlockSpec((tm, B_WIDTH), lambda i: (i, 0)),
            pl.BlockSpec((tm, d), lambda i: (i, 0)),
            pl.BlockSpec((tm, d), lambda i: (i, 1)),
            pl.BlockSpec((tm, d), lambda i: (i, 0)),
            pl.BlockSpec((None, A_WIDTH, d), lay, pipeline_mode=once),
            pl.BlockSpec((None, B_WIDTH, d), lay, pipeline_mode=once),
            pl.BlockSpec((None, d, d), lay, pipeline_mode=once),
            pl.BlockSpec((1, d), const),
        ],
        out_specs=out_specs,
        out_shape=out_shape,
        compiler_params=_cparams(1),
        name="merge_out",
    )(ya, yb, gates, gates, h, wpa, wpb, wo, g_next.reshape(1, d))


def _pack_weights(w_in, w_uk, w_uv, w_qb, w_iq, w_proj_a, w_proj_b, w_out):
    depth = w_in.shape[0]
    sec = lambda a, b: w_in[:, :, a:b]
    kidx, widx = sec(4096, 4160), sec(4160, 4168)
    z64 = jnp.zeros_like(kidx)
    zpad = jnp.zeros(widx.shape[:2] + (LANES - IDX_HEADS,), w_in.dtype)
    w_small = jnp.concatenate([kidx, z64, z64, kidx, widx, zpad], axis=-1).astype(BF16)
    uk = (w_uk * (HEAD_DIM ** -0.5 * LOG2E)).reshape(depth, B_HEADS // 2, 2, HEAD_DIM, B_KV_RANK)
    zuk = jnp.zeros_like(uk[:, :, 0])
    wuk_bd = jnp.concatenate([jnp.concatenate([uk[:, :, 0], zuk], axis=-1),
                              jnp.concatenate([zuk, uk[:, :, 1]], axis=-1)], axis=-2).astype(BF16)
    uv = w_uv.reshape(depth, B_HEADS // 2, 2, B_KV_RANK, HEAD_DIM)
    zuv = jnp.zeros_like(uv[:, :, 0])
    wuv_pairs = jnp.stack([jnp.concatenate([uv[:, :, 0], zuv], axis=-1),
                           jnp.concatenate([zuv, uv[:, :, 1]], axis=-1)], axis=2).astype(BF16)
    w_iq8 = (w_iq * (IDX_DIM ** -0.5)).astype(BF16)
    return (w_small, wuk_bd, wuv_pairs, w_qb.astype(BF16), w_iq8,
            w_proj_a.astype(BF16), w_proj_b.astype(BF16), w_out.astype(BF16))


def kernel(x, meta_tokens, bias_table, norm_g, w_in, q_norm_g, kv_norm_g, w_qb, w_iq, w_uk, w_uv,
           sinks, w_proj_a, w_proj_b, w_out, final_g):
    n_batch, seq, d = x.shape
    depth = w_in.shape[0]
    length = seq + N_META
    lp = -(-length // QB_B) * QB_B
    assert min(TOPK_MAX, seq // 4) == TOPK_MAX and d == D_MODEL

    (w_small, wuk_bd, wuv_pairs, wqb, wiq, wpa, wpb, wo) = _pack_weights(
        w_in, w_uk, w_uv, w_qb, w_iq, w_proj_a, w_proj_b, w_out)
    w_t = jnp.swapaxes(w_in, 1, 2)
    bias_a, bias_b = _bias_tiles(bias_table)
    bmax = jnp.max(jnp.abs(bias_b)).reshape(1)

    meta = jnp.broadcast_to(meta_tokens.astype(x.dtype)[None], (n_batch, N_META, d))
    pad = jnp.zeros((n_batch, lp - length, d), x.dtype)
    h = jnp.concatenate([meta, x, pad], axis=1).reshape(n_batch * lp, d)

    u = _rmsnorm(h, norm_g[0])
    out = None
    for l in range(depth):
        proj = _in_proj(u, w_t, l)
        gates = _gate_proj(u, w_t, l, GATE_COL0, 2 * D_MODEL)
        ya = _mixer_a(proj, sinks[l], bias_a, n_batch, lp)
        ql, qi, wi, ckvn, klo, khi, kn2 = _bprep(
            proj, u, q_norm_g[l].reshape(1, -1), kv_norm_g[l].reshape(1, -1),
            wqb, wiq, wuk_bd, w_small, l, n_batch, lp)
        b_args = (bmax, ql, qi, wi, ckvn, klo, khi, kn2, proj, bias_b, wuv_pairs, l, n_batch, lp)
        yb, flags = _mixer_b(*b_args, bound=True)
        yb = lax.cond(jnp.max(flags) > 0.0, lambda: _mixer_b(*b_args, bound=False)[0], lambda: yb)
        last = l == depth - 1
        g_next = final_g if last else norm_g[l + 1]
        res = _merge(ya, yb, gates, h, wpa, wpb, wo, g_next, l, last)
        if last:
            out = res[0]
        else:
            h, u = res
    return out.reshape(n_batch, lp, d)[:, N_META:length]
```

```python
import functools
import math

import numpy as np
import jax
import jax.numpy as jnp
from jax import lax
from jax.experimental import pallas as pl
from jax.experimental.pallas import tpu as pltpu

D_MODEL = 2048
N_META = 16
WINDOW = 128
HEAD_DIM = 64
A_HEADS = 16
A_WIDTH = 1024
B_HEADS = 16
B_WIDTH = 1024
B_Q_RANK = 512
B_KV_RANK = 256
IDX_HEADS = 8
IDX_DIM = 64
TOPK_MAX = 256
N_BUCKETS = 32
MAX_DISTANCE = 128
EPS = 1e-6
NEG = -1e30

LANES = 128
PROJ_ROW_TILES = 8
QB_A = 128
QB_B = 256
SUB_A = 32
SUB_B = 32
L_SAFE = 2.0 ** -100
BIS_UNROLL = 3
LOG2E = math.log2(math.e)
VMEM_LIMIT = 56 * 1024 * 1024
INT_MIN = -(2 ** 31)
INT_MAX = 2 ** 31 - 1

C_QA, C_ZA, C_ZB, C_CQ, C_CKV, C_KA, C_VA = (0, 1024, 2048, 3072, 3584, 3840, 3968)
GATE_COL0 = 4168

F32 = jnp.float32
BF16 = jnp.bfloat16
_NT = (((1,), (1,)), ((), ()))


def _cparams(n_grid):
    return pltpu.CompilerParams(
        dimension_semantics=("arbitrary",) * n_grid,
        vmem_limit_bytes=VMEM_LIMIT)


def _t5_bucket_np(d):
    max_exact = N_BUCKETS // 2
    nf = np.maximum(d, 1).astype(np.float32)
    large = max_exact + (np.log(nf / np.float32(max_exact)) / np.float32(math.log(MAX_DISTANCE / max_exact))
                         * np.float32(N_BUCKETS - max_exact)).astype(np.int32)
    large = np.minimum(large, N_BUCKETS - 1)
    return np.where(d < max_exact, d, large).astype(np.int32)


def _bias_tiles_kernel(tab_ref, idx_ref, out_a_ref, out_b_ref):
    h = pl.program_id(0)
    idx = idx_ref[...]
    acc_a = jnp.zeros(idx.shape, F32)
    acc_b = jnp.zeros(idx.shape, F32)
    far_b = tab_ref[N_BUCKETS - 1, A_HEADS + h]
    for b in range(N_BUCKETS):
        hit = idx == b
        acc_a = jnp.where(hit, tab_ref[b, h] * LOG2E, acc_a)
        acc_b = jnp.where(hit, (tab_ref[b, A_HEADS + h] - far_b) * LOG2E, acc_b)
    out_a_ref[0] = acc_a
    out_b_ref[0] = acc_b[:, :2 * LANES]


def _bias_tiles(bias_table):
    r = np.arange(QB_A)[:, None]
    k = np.arange(QB_A)[None, :]
    prev = _t5_bucket_np(np.maximum(QB_A + r - k, 0))
    cur = _t5_bucket_np(np.maximum(r - k, 0))
    far = np.full((QB_A, QB_A), N_BUCKETS - 1, np.int32)
    idx = jnp.asarray(np.concatenate([prev, cur, far], axis=1))
    return pl.pallas_call(
        _bias_tiles_kernel,
        grid=(A_HEADS,),
        in_specs=[pl.BlockSpec(memory_space=pltpu.SMEM),
                  pl.BlockSpec((QB_A, 3 * LANES), lambda h: (0, 0))],
        out_specs=[pl.BlockSpec((1, QB_A, 3 * LANES), lambda h: (h, 0, 0)),
                   pl.BlockSpec((1, QB_A, 2 * LANES), lambda h: (h, 0, 0))],
        out_shape=[jax.ShapeDtypeStruct((A_HEADS, QB_A, 3 * LANES), F32),
                   jax.ShapeDtypeStruct((B_HEADS, QB_A, 2 * LANES), F32)],
        compiler_params=_cparams(1),
        name="bias_tiles",
    )(bias_table, idx)


def _rms(x, g):
    return x * lax.rsqrt(jnp.mean(x * x, axis=-1, keepdims=True) + EPS) * g


def _rmsnorm_kernel(h_ref, g_ref, u_ref):
    u_ref[...] = _rms(h_ref[...], g_ref[...]).astype(u_ref.dtype)


def _rmsnorm(h, g, tm=512):
    rows, d = h.shape
    return pl.pallas_call(
        _rmsnorm_kernel,
        grid=(rows // tm,),
        in_specs=[pl.BlockSpec((tm, d), lambda i: (i, 0)),
                  pl.BlockSpec((1, d), lambda i: (0, 0))],
        out_specs=pl.BlockSpec((tm, d), lambda i: (i, 0)),
        out_shape=jax.ShapeDtypeStruct((rows, d), BF16),
        compiler_params=_cparams(1),
        name="rmsnorm",
    )(h, g.reshape(1, d))


_IN_BLOCKS = ((0, 1, 2, 3), (5, 6, 7, 8), (12, 13, 14, 15), (9, 10, 11, 4))
IN_BLK = 256


def _in_proj_kernel(x_ref, *refs):
    w_refs, o_ref, wbf_scr = refs[:-2], refs[-2], refs[-1]

    @pl.when(pl.program_id(1) == 0)
    def _():
        r0 = 0
        for w_ref in w_refs:
            wbf_scr[r0:r0 + w_ref.shape[0], :] = w_ref[...].astype(BF16)
            r0 += w_ref.shape[0]

    o_ref[...] = lax.dot_general(x_ref[...], wbf_scr[...], _NT,
                                 preferred_element_type=F32).astype(o_ref.dtype)


def _in_proj(x, w_t, layer):
    rows, kdim = x.shape
    tm = rows // PROJ_ROW_TILES
    tn = 4 * IN_BLK

    def w_spec(s):
        def index(j, i):
            blk = jnp.int32(_IN_BLOCKS[0][s])
            for t in range(1, len(_IN_BLOCKS)):
                blk = jnp.where(j == t, _IN_BLOCKS[t][s], blk)
            return layer, blk, 0
        return pl.BlockSpec((None, IN_BLK, kdim), index)

    return pl.pallas_call(
        _in_proj_kernel,
        grid=(len(_IN_BLOCKS), rows // tm),
        in_specs=[pl.BlockSpec((tm, kdim), lambda j, i: (i, 0))] + [w_spec(s) for s in range(4)],
        out_specs=pl.BlockSpec((tm, tn), lambda j, i: (i, j)),
        out_shape=jax.ShapeDtypeStruct((rows, len(_IN_BLOCKS) * tn), F32),
        scratch_shapes=[pltpu.VMEM((tn, kdim), BF16)],
        compiler_params=_cparams(2),
        name="in_proj",
    )(x, w_t, w_t, w_t, w_t)


def _gate_proj(x, w_t, layer, col0, n, tn=1024):
    rows, kdim = x.shape
    tm = rows // PROJ_ROW_TILES
    return pl.pallas_call(
        _in_proj_kernel,
        grid=(n // tn, rows // tm),
        in_specs=[pl.BlockSpec((tm, kdim), lambda j, i: (i, 0)),
                  pl.BlockSpec((None, pl.Element(tn), pl.Element(kdim)),
                               lambda j, i: (layer, pl.multiple_of(col0 + j * tn, 8), 0))],
        out_specs=pl.BlockSpec((tm, tn), lambda j, i: (i, j)),
        out_shape=jax.ShapeDtypeStruct((rows, n), BF16),
        scratch_shapes=[pltpu.VMEM((tn, kdim), BF16)],
        compiler_params=_cparams(2),
        name="gate_proj",
    )(x, w_t)


def _silu(z):
    return z / (1.0 + jnp.exp(-z))


def _sigmoid(z):
    return 1.0 / (1.0 + jnp.exp(-z))


def _mixer_a_kernel(sink_ref, q_ref, kc_ref, vc_ref, kp_ref, vp_ref, km_ref, vm_ref, za_ref,
                    bias_ref, o_ref, s_scr, p_scr, m_scr, d_scr, msk_scr):
    j = pl.program_id(1)
    nk = 3 * QB_A
    q = (q_ref[...] * (HEAD_DIM ** -0.5 * LOG2E)).astype(BF16)
    kcat = jnp.concatenate([kp_ref[...], kc_ref[...], km_ref[...]], axis=0)
    vcat = jnp.concatenate([vp_ref[...], vc_ref[...], vm_ref[...]], axis=0)
    lane = lax.broadcasted_iota(jnp.int32, (nk, LANES), 1)
    lo = lane < HEAD_DIM
    kswap = pltpu.roll(kcat, HEAD_DIM, 1)
    vswap = pltpu.roll(vcat, HEAD_DIM, 1)
    zero = jnp.zeros_like(kcat)
    k_half = ((jnp.where(lo, kcat, zero).astype(BF16), jnp.where(lo, zero, kswap).astype(BF16)),
              (jnp.where(lo, kswap, zero).astype(BF16), jnp.where(lo, zero, kcat).astype(BF16)))
    v_half = ((jnp.where(lo, vcat, zero).astype(BF16), jnp.where(lo, zero, vswap).astype(BF16)),
              (jnp.where(lo, vswap, zero).astype(BF16), jnp.where(lo, zero, vcat).astype(BF16)))

    r = lax.broadcasted_iota(jnp.int32, (QB_A, nk), 0)
    c = lax.broadcasted_iota(jnp.int32, (QB_A, nk), 1)
    jv = jnp.zeros((QB_A, nk), jnp.int32) + j
    in_prev = (c < QB_A) & (jv >= 1) & ((c > r) | ((jv == 1) & (c < N_META)))
    in_cur = (c >= QB_A) & (c < 2 * QB_A) & (c - QB_A <= r)
    in_meta = (c >= 2 * QB_A) & (jv >= 2) & (c - 2 * QB_A < N_META)
    msk_scr[...] = jnp.where(in_prev | in_cur | in_meta, 0.0, NEG)
    half = lax.broadcasted_iota(jnp.int32, (QB_A, LANES), 1) < HEAD_DIM
    pairs = A_HEADS // 4

    def logits(g, i, par, r0):
        hd = 2 * (g * pairs + i) + par
        s = s_scr[i * QB_A + r0:i * QB_A + r0 + SUB_A, par * nk:(par + 1) * nk]
        return s + bias_ref[hd, r0:r0 + SUB_A, :] + msk_scr[r0:r0 + SUB_A, :], sink_ref[hd] * LOG2E

    for g in range(2):
        q4 = jnp.concatenate([q[:, (g * pairs + i) * LANES:(g * pairs + i + 1) * LANES]
                              for i in range(pairs)], axis=0)
        s_scr[...] = lax.dot_general(q4, jnp.concatenate(k_half[g], axis=0), _NT,
                                     preferred_element_type=F32)
        for i in range(pairs):
            for par in range(2):
                for r0 in range(0, QB_A, SUB_A):
                    lg, sink = logits(g, i, par, r0)
                    m = jnp.maximum(jnp.max(lg, axis=-1, keepdims=True), sink)
                    m_scr[par, i * QB_A + r0:i * QB_A + r0 + SUB_A] = jnp.broadcast_to(m, (SUB_A, LANES))
        for i in range(pairs):
            for par in range(2):
                for r0 in range(0, QB_A, SUB_A):
                    sl = slice(i * QB_A + r0, i * QB_A + r0 + SUB_A)
                    lg, sink = logits(g, i, par, r0)
                    m = m_scr[par, sl]
                    p = jnp.exp2(lg - jnp.concatenate([m] * 3, axis=1))
                    d_scr[par, sl] = jnp.broadcast_to(
                        jnp.sum(p, axis=-1, keepdims=True), (SUB_A, LANES)) + jnp.exp2(sink - m)
                    p_scr[sl, par * nk:(par + 1) * nk] = p.astype(BF16)
        pv = jnp.dot(p_scr[...], jnp.concatenate(v_half[g], axis=0), preferred_element_type=F32)
        for i in range(pairs):
            sl = slice(i * QB_A, (i + 1) * QB_A)
            col = slice((g * pairs + i) * LANES, (g * pairs + i + 1) * LANES)
            o2 = pv[sl] / jnp.where(half, d_scr[0, sl], d_scr[1, sl])
            o_ref[:, col] = (o2 * _silu(za_ref[:, col])).astype(o_ref.dtype)


def _mixer_a(proj, sinks, bias_a, n_batch, lp):
    nb = lp // QB_A
    rows = n_batch * lp
    ck, cv = C_KA // LANES, C_VA // LANES
    cur = lambda b, j: b * nb + j
    prev = lambda b, j: b * nb + jnp.maximum(j - 1, 0)
    first = lambda b, j: b * nb
    return pl.pallas_call(
        _mixer_a_kernel,
        grid=(n_batch, nb),
        in_specs=[
            pl.BlockSpec(memory_space=pltpu.SMEM),
            pl.BlockSpec((QB_A, A_WIDTH), lambda b, j: (cur(b, j), C_QA // A_WIDTH)),
            pl.BlockSpec((QB_A, LANES), lambda b, j: (cur(b, j), ck)),
            pl.BlockSpec((QB_A, LANES), lambda b, j: (cur(b, j), cv)),
            pl.BlockSpec((QB_A, LANES), lambda b, j: (prev(b, j), ck)),
            pl.BlockSpec((QB_A, LANES), lambda b, j: (prev(b, j), cv)),
            pl.BlockSpec((QB_A, LANES), lambda b, j: (first(b, j), ck)),
            pl.BlockSpec((QB_A, LANES), lambda b, j: (first(b, j), cv)),
            pl.BlockSpec((QB_A, A_WIDTH), lambda b, j: (cur(b, j), C_ZA // A_WIDTH)),
            pl.BlockSpec((A_HEADS, QB_A, 3 * LANES), lambda b, j: (0, 0, 0)),
        ],
        out_specs=pl.BlockSpec((QB_A, A_WIDTH), lambda b, j: (cur(b, j), 0)),
        out_shape=jax.ShapeDtypeStruct((rows, A_WIDTH), BF16),
        scratch_shapes=[
            pltpu.VMEM((A_HEADS // 4 * QB_A, 6 * QB_A), F32),
            pltpu.VMEM((A_HEADS // 4 * QB_A, 6 * QB_A), BF16),
            pltpu.VMEM((2, A_HEADS // 4 * QB_A, LANES), F32),
            pltpu.VMEM((2, A_HEADS // 4 * QB_A, LANES), F32),
            pltpu.VMEM((QB_A, 3 * QB_A), F32),
        ],
        compiler_params=_cparams(2),
        name="mixer_a",
    )(sinks, proj, proj, proj, proj, proj, proj, proj, proj, bias_a)


def _bprep_kernel(cq_ref, ckv_ref, u_ref, qg_ref, kvg_ref, wqb_ref, wiq_ref, wuk_ref, wsm_ref,
                  ql_ref, qi_ref, wi_ref, ckvn_ref, klo_ref, khi_ref, kn2_ref):
    cqn = _rms(cq_ref[...], qg_ref[...]).astype(BF16)
    ckvn = _rms(ckv_ref[...], kvg_ref[...]).astype(ckvn_ref.dtype)
    ckvn_ref[0] = ckvn
    kn2 = jnp.sum(jnp.square(ckvn.astype(F32)), axis=-1, keepdims=True)
    kn2_ref[0, 0] = jnp.broadcast_to(jnp.max(kn2, axis=0, keepdims=True), kn2_ref.shape[2:])
    qb = jnp.dot(cqn, wqb_ref[...], preferred_element_type=F32).astype(BF16)
    for pair in range(B_HEADS // 2):
        ql2 = jnp.dot(qb[:, pair * LANES:(pair + 1) * LANES], wuk_ref[pair],
                      preferred_element_type=F32)
        ql_ref[0, 2 * pair] = ql2[:, :B_KV_RANK].astype(ql_ref.dtype)
        ql_ref[0, 2 * pair + 1] = ql2[:, B_KV_RANK:].astype(ql_ref.dtype)
    qi = jnp.dot(cqn, wiq_ref[...], preferred_element_type=F32)
    for pair in range(IDX_HEADS // 2):
        qi_ref[0, pair] = qi[:, pair * LANES:(pair + 1) * LANES].astype(qi_ref.dtype)
    small = jnp.dot(u_ref[...], wsm_ref[...], preferred_element_type=F32)
    klo_ref[0] = small[:, :LANES].astype(klo_ref.dtype)
    khi_ref[0] = small[:, LANES:2 * LANES].astype(khi_ref.dtype)
    wi_ref[0] = small[:, 2 * LANES:] * (IDX_HEADS ** -0.5)


def _bprep(proj, u, qg, kvg, wqb, wiq, wuk_bd, wsmall, layer, n_batch, lp, tm=QB_B):
    nt = lp // tm
    row = lambda b, i: b * nt + i
    const2 = lambda b, i: (0, 0)
    lay2 = lambda b, i: (layer, 0, 0)
    return pl.pallas_call(
        _bprep_kernel,
        grid=(n_batch, nt),
        in_specs=[
            pl.BlockSpec((tm, B_Q_RANK), lambda b, i: (row(b, i), C_CQ // B_Q_RANK)),
            pl.BlockSpec((tm, B_KV_RANK), lambda b, i: (row(b, i), C_CKV // B_KV_RANK)),
            pl.BlockSpec((tm, D_MODEL), lambda b, i: (row(b, i), 0)),
            pl.BlockSpec((1, B_Q_RANK), const2),
            pl.BlockSpec((1, B_KV_RANK), const2),
            pl.BlockSpec((None, B_Q_RANK, B_WIDTH), lay2),
            pl.BlockSpec((None, B_Q_RANK, IDX_HEADS * IDX_DIM), lay2),
            pl.BlockSpec((None, B_HEADS // 2, LANES, 2 * B_KV_RANK), lambda b, i: (layer, 0, 0, 0)),
            pl.BlockSpec((None, D_MODEL, 3 * LANES), lay2),
        ],
        out_specs=[
            pl.BlockSpec((1, B_HEADS, tm, B_KV_RANK), lambda b, i: (b, 0, i, 0)),
            pl.BlockSpec((1, IDX_HEADS // 2, tm, LANES), lambda b, i: (b, 0, i, 0)),
            pl.BlockSpec((1, tm, LANES), lambda b, i: (b, i, 0)),
            pl.BlockSpec((1, tm, B_KV_RANK), lambda b, i: (b, i, 0)),
            pl.BlockSpec((1, tm, LANES), lambda b, i: (b, i, 0)),
            pl.BlockSpec((1, tm, LANES), lambda b, i: (b, i, 0)),
            pl.BlockSpec((1, 1, 8, LANES), lambda b, i: (b, i, 0, 0)),
        ],
        out_shape=[
            jax.ShapeDtypeStruct((n_batch, B_HEADS, lp, B_KV_RANK), BF16),
            jax.ShapeDtypeStruct((n_batch, IDX_HEADS // 2, lp, LANES), BF16),
            jax.ShapeDtypeStruct((n_batch, lp, LANES), F32),
            jax.ShapeDtypeStruct((n_batch, lp, B_KV_RANK), BF16),
            jax.ShapeDtypeStruct((n_batch, lp, LANES), BF16),
            jax.ShapeDtypeStruct((n_batch, lp, LANES), BF16),
            jax.ShapeDtypeStruct((n_batch, nt, 8, LANES), F32),
        ],
        compiler_params=_cparams(2),
        name="bprep",
    )(proj, proj, u, qg, kvg, wqb, wiq, wuk_bd, wsmall)


def _mixer_b_kernel(bmax_ref, ql_ref, qi_ref, wi_ref, ckv_ref, klo_ref, khi_ref, kn2_ref, zb_ref, bias_ref,
                    wuv_ref, o_ref, flag_ref, key_scr, s_scr, acc_scr, m_scr, l_scr, alpha_scr, p_scr,
                    *, bound):
    jq = pl.program_id(1)
    nch = jq + 1
    qb, kc = QB_B, QB_B
    rows = B_HEADS * qb
    kf = float(TOPK_MAX)

    qi = qi_ref[0].reshape(IDX_HEADS // 2 * qb, LANES)
    wit = wi_ref[0].T
    krow = lax.broadcasted_iota(jnp.int32, (qb, kc), 0)
    krow_sub = lax.broadcasted_iota(jnp.int32, (SUB_B, kc), 0)
    qpos = jq * qb + lax.broadcasted_iota(jnp.int32, (qb, kc), 1)

    def idx_body(c, carry):
        start = pl.multiple_of(c * kc, kc)
        sc = (lax.dot_general(klo_ref[0, pl.ds(start, kc), :], qi, _NT, preferred_element_type=F32),
              lax.dot_general(khi_ref[0, pl.ds(start, kc), :], qi, _NT, preferred_element_type=F32))
        isc = jnp.zeros((kc, qb), F32)
        for pair in range(IDX_HEADS // 2):
            for par in range(2):
                hd = 2 * pair + par
                isc = isc + wit[hd:hd + 1, :] * jnp.maximum(sc[par][:, pair * qb:(pair + 1) * qb], 0.0)
        isc = jnp.where(isc == 0.0, 0.0, isc)
        bits = lax.bitcast_convert_type(isc, jnp.int32)
        key = bits ^ ((bits >> 31) & INT_MAX)
        key_scr[c] = jnp.where(c * kc + krow <= qpos, key, INT_MIN)
        return carry

    lax.fori_loop(0, nch, idx_body, 0)

    def reduce_chunks(tile_fn, init, combine):
        n_acc = SUB_B // 8

        def body(c, accs):
            accs = list(accs)
            for r0 in range(0, kc, SUB_B):
                t = tile_fn(c, r0, key_scr[c, r0:r0 + SUB_B, :])
                t = t.reshape(n_acc, 8, qb)
                for i in range(n_acc):
                    accs[i] = combine(accs[i], t[i])
            return tuple(accs)

        accs = lax.fori_loop(0, nch, body, (init,) * n_acc)
        acc = accs[0]
        for a in accs[1:]:
            acc = combine(acc, a)
        return acc

    def count(pred_fn):
        acc = reduce_chunks(lambda c, r0, k: jnp.where(pred_fn(c, r0, k), 1.0, 0.0),
                            jnp.zeros((8, qb), F32), lambda a, b: a + b)
        return jnp.sum(acc, axis=0, keepdims=True)

    def unkey(k):
        return lax.bitcast_convert_type(k ^ ((k >> 31) & INT_MAX), F32)

    def tokey(v):
        b = lax.bitcast_convert_type(v, jnp.int32)
        return b ^ ((b >> 31) & INT_MAX)

    def select_all():
        return jnp.full((1, qb), INT_MIN, jnp.int32), jnp.full((1, qb), -1, jnp.int32)

    def select_topk():
        kmax = reduce_chunks(lambda c, r0, k: k, jnp.full((8, qb), INT_MIN, jnp.int32), jnp.maximum)
        kmax = jnp.max(kmax, axis=0, keepdims=True)
        kmin = reduce_chunks(lambda c, r0, k: jnp.where(k == INT_MIN, INT_MAX, k),
                             jnp.full((8, qb), INT_MAX, jnp.int32), jnp.minimum)
        kmin = jnp.min(kmin, axis=0, keepdims=True)
        c0 = count(lambda c, r0, k: k >= 0)
        c0p = count(lambda c, r0, k: k >= 1)
        n_valid = (jq * qb + 1 + lax.broadcasted_iota(jnp.int32, (1, qb), 1)).astype(F32)
        pos = c0p >= kf
        zero = (c0 >= kf) & (c0p < kf)
        lo0 = jnp.where(pos, 1, jnp.where(zero, 0, kmin))
        hi0 = jnp.where(pos, kmax + 1, jnp.where(zero, 1, 0))
        c_lo0 = jnp.where(pos, c0p, jnp.where(zero, c0, n_valid))
        c_hi0 = jnp.where(pos, 0.0, jnp.where(zero, c0p, c0))

        def bis_cond(st):
            return (st[0] < 64) & (st[5] > 0)

        def bis_step(it, lo, hi, c_lo, c_hi):
            mid_i = (lo >> 1) + (hi >> 1) + (lo & hi & 1)
            mid_v = tokey(0.5 * unkey(lo) + 0.5 * unkey(hi))
            use_v = (it < 24) & (mid_v > lo) & (mid_v < hi)
            mid = jnp.where(use_v, mid_v, mid_i)
            cnt = count(lambda c, r0, k: k >= mid)
            ge = cnt >= kf
            return (jnp.where(ge, mid, lo), jnp.where(ge, hi, mid),
                    jnp.where(ge, cnt, c_lo), jnp.where(ge, c_hi, cnt))

        def bis_body(st):
            it, lo, hi, c_lo, c_hi, _ = st
            for _ in range(BIS_UNROLL):
                lo, hi, c_lo, c_hi = bis_step(it, lo, hi, c_lo, c_hi)
                it = it + 1
            return it, lo, hi, c_lo, c_hi, jnp.max(jnp.where(is_active(lo, hi, c_lo, c_hi), 1, 0))

        def is_active(lo, hi, c_lo, c_hi):
            return (lo + 1 < hi) & (c_lo != kf) & (c_lo - c_hi > 2.0)

        st0 = (jnp.int32(0), lo0, hi0, c_lo0, c_hi0,
               jnp.max(jnp.where(is_active(lo0, hi0, c_lo0, c_hi0), 1, 0)))
        _, lo, hi, c_lo, c_hi, _ = lax.while_loop(bis_cond, bis_body, st0)
        open_rows = c_lo != kf

        def finish():
            top = reduce_chunks(lambda c, r0, k: jnp.where(k < hi, k, INT_MIN),
                                jnp.full((8, qb), INT_MIN, jnp.int32), jnp.maximum)
            top = jnp.max(top, axis=0, keepdims=True)
            bot = reduce_chunks(lambda c, r0, k: jnp.where(k >= lo, k, INT_MAX),
                                jnp.full((8, qb), INT_MAX, jnp.int32), jnp.minimum)
            bot = jnp.min(bot, axis=0, keepdims=True)
            thr = jnp.where(open_rows, top, lo)
            tie = open_rows & (top == bot)
            need = kf - c_hi

            def tie_break():
                def tie_body(_, st):
                    jlo, jhi = st
                    mid = (jlo + jhi) >> 1
                    cnt = count(lambda c, r0, k: (k == thr) & (c * kc + r0 + krow_sub <= mid))
                    ge = cnt >= need
                    return jnp.where(ge, jlo, mid), jnp.where(ge, mid, jhi)

                _, jhi = lax.fori_loop(0, 13, tie_body, (jnp.full((1, qb), -1, jnp.int32),
                                                         jnp.full((1, qb), 8191, jnp.int32)))
                return jnp.where(tie, jhi, INT_MAX)

            jmax = lax.cond(jnp.max(jnp.where(tie, 1, 0)) > 0, tie_break,
                            lambda: jnp.full((1, qb), INT_MAX, jnp.int32))
            return thr, jmax

        return lax.cond(jnp.max(jnp.where(open_rows, 1, 0)) > 0, finish,
                        lambda: (lo, jnp.full((1, qb), INT_MAX, jnp.int32)))

    thr, jmax = lax.cond(jq >= 1, select_topk, select_all)

    def sel_body(c, carry):
        k = key_scr[c]
        sel = (k > thr) | ((k == thr) & (c * kc + krow <= jmax))
        key_scr[c] = lax.bitcast_convert_type(jnp.where(sel, 0.0, NEG).T, jnp.int32)
        return carry

    lax.fori_loop(0, nch, sel_body, 0)

    l_scr[...] = jnp.zeros(l_scr.shape, F32)
    acc_scr[...] = jnp.zeros(acc_scr.shape, F32)

    def kv_chunk(c):
        return ckv_ref[0, pl.ds(pl.multiple_of(c * kc, kc), kc), :]

    def qk(c, slot):
        s_scr[slot] = lax.dot_general(ql_ref[0].reshape(rows, B_KV_RANK), kv_chunk(c), _NT,
                                      preferred_element_type=F32)

    def logits(c, slot, near, hd, r0):
        sh = s_scr[slot, hd * qb + r0:hd * qb + r0 + SUB_B]
        if near == "cur":
            t0 = bias_ref[hd, r0 % QB_A:r0 % QB_A + SUB_B, LANES:]
            t1 = bias_ref[hd, r0 % QB_A:r0 % QB_A + SUB_B, :LANES]
            sh = sh + jnp.concatenate([t0 if r0 < QB_A else t1, t0], axis=1)
        elif near == "prev" and r0 < QB_A:
            t1 = bias_ref[hd, r0:r0 + SUB_B, :LANES]
            sh = sh + jnp.concatenate([jnp.zeros_like(t1), t1], axis=1)
        return sh + lax.bitcast_convert_type(key_scr[c, r0:r0 + SUB_B, :], F32)

    def sub_tiles():
        for hd in range(B_HEADS):
            for r0 in range(0, qb, SUB_B):
                yield hd, r0, slice(hd * qb + r0, hd * qb + r0 + SUB_B)

    if bound:
        kn2 = lax.fori_loop(0, nch, lambda c, a: jnp.maximum(a, kn2_ref[0, c]), jnp.zeros((8, LANES), F32))
        bmax = bmax_ref[0]
        for hd, r0, sl in sub_tiles():
            x = ql_ref[0, hd, r0:r0 + SUB_B, :].astype(F32)
            qn2 = jnp.sum(x * x, axis=-1, keepdims=True)
            m_scr[sl] = jnp.sqrt(qn2 * kn2[:1]) * (1.0 + 2.0 ** -10) + bmax

        def sm(c, slot, near):
            for hd, r0, sl in sub_tiles():
                m = m_scr[sl]
                p = jnp.exp2(logits(c, slot, near, hd, r0) - jnp.concatenate([m, m], axis=1))
                l_scr[sl] += p[:, :LANES] + p[:, LANES:]
                p_scr[slot, sl] = p.astype(BF16)

        def pv(c, slot):
            o = jnp.dot(p_scr[slot], kv_chunk(c), preferred_element_type=F32)
            for r0 in range(0, rows, SUB_B):
                acc_scr[r0:r0 + SUB_B] += o[r0:r0 + SUB_B]

        def stage(t, slot):
            qk(t, slot)
            pv(t - 2, slot)
            sm(t - 1, 1 - slot, None)

        def tail(slot):
            qk(jq, slot)
            pv(jq - 2, slot)
            sm(jq - 1, 1 - slot, "prev")
            pv(jq - 1, 1 - slot)
            sm(jq, slot, "cur")
            pv(jq, slot)

        @pl.when(jq >= 2)
        def _():
            qk(0, 0)
            qk(1, 1)
            sm(0, 0, None)

            def two_stages(i, carry):
                stage(2 + 2 * i, 0)
                stage(3 + 2 * i, 1)
                return carry

            lax.fori_loop(0, (jq - 2) // 2, two_stages, 0)

            @pl.when(jq % 2 == 0)
            def _():
                tail(0)

            @pl.when(jq % 2 == 1)
            def _():
                stage(jq - 1, 0)
                tail(1)

        @pl.when(jq < 2)
        def _():
            @pl.when(jq == 1)
            def _():
                qk(0, 0)
                sm(0, 0, "prev")
                pv(0, 0)

            qk(jq, 1)
            sm(jq, 1, "cur")
            pv(jq, 1)
    else:
        m_scr[...] = jnp.full(m_scr.shape, -jnp.inf, F32)

        def attend(c, near):
            qk(c, 0)
            for hd, r0, sl in sub_tiles():
                m_old = m_scr[sl]
                m_new = jnp.maximum(m_old, jnp.max(logits(c, 0, near, hd, r0), axis=-1, keepdims=True))
                m_scr[sl] = m_new
                alpha_scr[sl] = jnp.exp2(m_old - m_new)
            for hd, r0, sl in sub_tiles():
                m_new = m_scr[sl]
                p = jnp.exp2(logits(c, 0, near, hd, r0) - jnp.concatenate([m_new, m_new], axis=1))
                l_scr[sl] = alpha_scr[sl] * l_scr[sl] + (p[:, :LANES] + p[:, LANES:])
                p_scr[0, sl] = p.astype(BF16)
            o = jnp.dot(p_scr[0], kv_chunk(c), preferred_element_type=F32)
            for r0 in range(0, rows, SUB_B):
                sl = slice(r0, r0 + SUB_B)
                alpha = alpha_scr[sl]
                acc_scr[sl] = acc_scr[sl] * jnp.concatenate([alpha, alpha], axis=1) + o[sl]

        def far_body(c, carry):
            attend(c, None)
            return carry

        lax.fori_loop(0, jq - 1, far_body, 0)

        @pl.when(jq >= 1)
        def _():
            attend(jq - 1, "prev")

        attend(jq, "cur")

    l_min = jnp.full((qb, 1), jnp.inf, F32)
    for pair in range(B_HEADS // 2):
        y2 = jnp.zeros((qb, LANES), F32)
        for par in range(2):
            hd = 2 * pair + par
            sl = slice(hd * qb, (hd + 1) * qb)
            l = jnp.sum(l_scr[sl], axis=-1, keepdims=True)
            l_min = jnp.minimum(l_min, l)
            o_lat = acc_scr[sl] / l
            y2 = y2 + jnp.dot(o_lat.astype(BF16), wuv_ref[pair, par], preferred_element_type=F32)
        z = zb_ref[:, pair * LANES:(pair + 1) * LANES]
        o_ref[:, pair * LANES:(pair + 1) * LANES] = (y2 * _silu(z)).astype(o_ref.dtype)
    safe = jnp.min(l_min) >= L_SAFE
    flag_ref[0, 0] = jnp.full(flag_ref.shape[2:], jnp.where(safe, 0.0, 1.0), F32)


def _mixer_b(bmax, ql, qi, wi, ckvn, klo, khi, kn2, proj, bias_b, wuv_pairs, layer, n_batch, lp, bound):
    nb = lp // QB_B
    rows = n_batch * lp
    whole = lambda b, j: (b, 0, 0)
    once = pl.Buffered(1)
    return pl.pallas_call(
        functools.partial(_mixer_b_kernel, bound=bound),
        grid=(n_batch, nb),
        in_specs=[
            pl.BlockSpec(memory_space=pltpu.SMEM),
            pl.BlockSpec((1, B_HEADS, QB_B, B_KV_RANK), lambda b, j: (b, 0, j, 0)),
            pl.BlockSpec((1, IDX_HEADS // 2, QB_B, LANES), lambda b, j: (b, 0, j, 0)),
            pl.BlockSpec((1, QB_B, LANES), lambda b, j: (b, j, 0)),
            pl.BlockSpec((1, lp, B_KV_RANK), whole, pipeline_mode=once),
            pl.BlockSpec((1, lp, LANES), whole, pipeline_mode=once),
            pl.BlockSpec((1, lp, LANES), whole, pipeline_mode=once),
            pl.BlockSpec((1, nb, 8, LANES), lambda b, j: (b, 0, 0, 0), pipeline_mode=once),
            pl.BlockSpec((QB_B, B_WIDTH), lambda b, j: (b * nb + j, C_ZB // B_WIDTH)),
            pl.BlockSpec((B_HEADS, QB_A, 2 * LANES), lambda b, j: (0, 0, 0), pipeline_mode=once),
            pl.BlockSpec((None, B_HEADS // 2, 2, B_KV_RANK, LANES), lambda b, j: (layer, 0, 0, 0, 0),
                         pipeline_mode=once),
        ],
        out_specs=[pl.BlockSpec((QB_B, B_WIDTH), lambda b, j: (b * nb + j, 0)),
                   pl.BlockSpec((1, 1, 8, LANES), lambda b, j: (b, j, 0, 0))],
        out_shape=[jax.ShapeDtypeStruct((rows, B_WIDTH), BF16),
                   jax.ShapeDtypeStruct((n_batch, nb, 8, LANES), F32)],
        scratch_shapes=[
            pltpu.VMEM((nb, QB_B, QB_B), jnp.int32),
            pltpu.VMEM((2, B_HEADS * QB_B, QB_B), F32),
            pltpu.VMEM((B_HEADS * QB_B, B_KV_RANK), F32),
            pltpu.VMEM((B_HEADS * QB_B, LANES), F32),
            pltpu.VMEM((B_HEADS * QB_B, LANES), F32),
            pltpu.VMEM((B_HEADS * QB_B, LANES), F32),
            pltpu.VMEM((2, B_HEADS * QB_B, QB_B), BF16),
        ],
        compiler_params=_cparams(2),
        name="mixer_b" if bound else "mixer_b_running_max",
    )(bmax, ql, qi, wi, ckvn, klo, khi, kn2, proj, bias_b, wuv_pairs)


def _merge_kernel(ya_ref, yb_ref, ga_ref, gb_ref, h_ref, wpa_ref, wpb_ref, wo_ref, g_ref, *out_refs,
                  last):
    pa = jnp.dot(ya_ref[...], wpa_ref[...], preferred_element_type=F32)
    pb = jnp.dot(yb_ref[...], wpb_ref[...], preferred_element_type=F32)
    merged = _sigmoid(ga_ref[...].astype(F32)) * pa + _sigmoid(gb_ref[...].astype(F32)) * pb
    h_new = h_ref[...] + jnp.dot(merged.astype(BF16), wo_ref[...], preferred_element_type=F32)
    normed = _rms(h_new, g_ref[...])
    if last:
        out_refs[0][...] = normed
    else:
        out_refs[0][...] = h_new
        out_refs[1][...] = normed.astype(BF16)


def _merge(ya, yb, gates, h, wpa, wpb, wo, g_next, layer, last, tm=256):
    rows, d = h.shape
    const = lambda i: (0, 0)
    lay = lambda i: (layer, 0, 0)
    once = pl.Buffered(1)
    if last:
        out_specs = [pl.BlockSpec((tm, d), lambda i: (i, 0))]
        out_shape = [jax.ShapeDtypeStruct((rows, d), F32)]
    else:
        out_specs = [pl.BlockSpec((tm, d), lambda i: (i, 0)), pl.BlockSpec((tm, d), lambda i: (i, 0))]
        out_shape = [jax.ShapeDtypeStruct((rows, d), F32), jax.ShapeDtypeStruct((rows, d), BF16)]
    return pl.pallas_call(
        functools.partial(_merge_kernel, last=last),
        grid=(rows // tm,),
        in_specs=[
            pl.BlockSpec((tm, A_WIDTH), lambda i: (i, 0)),
            pl.BlockSpec((tm, B_WIDTH), lambda i: (i, 0)),
            pl.BlockSpec((tm, d), lambda i: (i, 0)),
            pl.BlockSpec((tm, d), lambda i: (i, 1)),
            pl.BlockSpec((tm, d), lambda i: (i, 0)),
            pl.BlockSpec((None, A_WIDTH, d), lay, pipeline_mode=once),
            pl.BlockSpec((None, B_WIDTH, d), lay, pipeline_mode=once),
            pl.BlockSpec((None, d, d), lay, pipeline_mode=once),
            pl.BlockSpec((1, d), const),
        ],
        out_specs=out_specs,
        out_shape=out_shape,
        compiler_params=_cparams(1),
        name="merge_out",
    )(ya, yb, gates, gates, h, wpa, wpb, wo, g_next.reshape(1, d))


def _pack_weights(w_in, w_uk, w_uv, w_qb, w_iq, w_proj_a, w_proj_b, w_out):
    depth = w_in.shape[0]
    sec = lambda a, b: w_in[:, :, a:b]
    kidx, widx = sec(4096, 4160), sec(4160, 4168)
    z64 = jnp.zeros_like(kidx)
    zpad = jnp.zeros(widx.shape[:2] + (LANES - IDX_HEADS,), w_in.dtype)
    w_small = jnp.concatenate([kidx, z64, z64, kidx, widx, zpad], axis=-1).astype(BF16)
    uk = (w_uk * (HEAD_DIM ** -0.5 * LOG2E)).reshape(depth, B_HEADS // 2, 2, HEAD_DIM, B_KV_RANK)
    zuk = jnp.zeros_like(uk[:, :, 0])
    wuk_bd = jnp.concatenate([jnp.concatenate([uk[:, :, 0], zuk], axis=-1),
                              jnp.concatenate([zuk, uk[:, :, 1]], axis=-1)], axis=-2).astype(BF16)
    uv = w_uv.reshape(depth, B_HEADS // 2, 2, B_KV_RANK, HEAD_DIM)
    zuv = jnp.zeros_like(uv[:, :, 0])
    wuv_pairs = jnp.stack([jnp.concatenate([uv[:, :, 0], zuv], axis=-1),
                           jnp.concatenate([zuv, uv[:, :, 1]], axis=-1)], axis=2).astype(BF16)
    w_iq8 = (w_iq * (IDX_DIM ** -0.5)).astype(BF16)
    return (w_small, wuk_bd, wuv_pairs, w_qb.astype(BF16), w_iq8,
            w_proj_a.astype(BF16), w_proj_b.astype(BF16), w_out.astype(BF16))


def kernel(x, meta_tokens, bias_table, norm_g, w_in, q_norm_g, kv_norm_g, w_qb, w_iq, w_uk, w_uv,
           sinks, w_proj_a, w_proj_b, w_out, final_g):
    n_batch, seq, d = x.shape
    depth = w_in.shape[0]
    length = seq + N_META
    lp = -(-length // QB_B) * QB_B
    assert min(TOPK_MAX, seq // 4) == TOPK_MAX and d == D_MODEL

    (w_small, wuk_bd, wuv_pairs, wqb, wiq, wpa, wpb, wo) = _pack_weights(
        w_in, w_uk, w_uv, w_qb, w_iq, w_proj_a, w_proj_b, w_out)
    w_t = jnp.swapaxes(w_in, 1, 2)
    bias_a, bias_b = _bias_tiles(bias_table)
    bmax = jnp.max(jnp.abs(bias_b)).reshape(1)

    meta = jnp.broadcast_to(meta_tokens.astype(x.dtype)[None], (n_batch, N_META, d))
    pad = jnp.zeros((n_batch, lp - length, d), x.dtype)
    h = jnp.concatenate([meta, x, pad], axis=1).reshape(n_batch * lp, d)

    u = _rmsnorm(h, norm_g[0])
    out = None
    for l in range(depth):
        proj = _in_proj(u, w_t, l)
        gates = _gate_proj(u, w_t, l, GATE_COL0, 2 * D_MODEL)
        ya = _mixer_a(proj, sinks[l], bias_a, n_batch, lp)
        ql, qi, wi, ckvn, klo, khi, kn2 = _bprep(
            proj, u, q_norm_g[l].reshape(1, -1), kv_norm_g[l].reshape(1, -1),
            wqb, wiq, wuk_bd, w_small, l, n_batch, lp)
        b_args = (bmax, ql, qi, wi, ckvn, klo, khi, kn2, proj, bias_b, wuv_pairs, l, n_batch, lp)
        yb, flags = _mixer_b(*b_args, bound=True)
        yb = lax.cond(jnp.max(flags) > 0.0, lambda: _mixer_b(*b_args, bound=False)[0], lambda: yb)
        last = l == depth - 1
        g_next = final_g if last else norm_g[l + 1]
        res = _merge(ya, yb, gates, h, wpa, wpb, wo, g_next, l, last)
        if last:
            out = res[0]
        else:
            h, u = res
    return out.reshape(n_batch, lp, d)[:, N_META:length]
```

```python
import functools
import math

import numpy as np
import jax
import jax.numpy as jnp
from jax import lax
from jax.experimental import pallas as pl
from jax.experimental.pallas import tpu as pltpu

D_MODEL = 2048
N_META = 16
WINDOW = 128
HEAD_DIM = 64
A_HEADS = 16
A_WIDTH = 1024
B_HEADS = 16
B_WIDTH = 1024
B_Q_RANK = 512
B_KV_RANK = 256
IDX_HEADS = 8
IDX_DIM = 64
TOPK_MAX = 256
N_BUCKETS = 32
MAX_DISTANCE = 128
EPS = 1e-6
NEG = -1e30

LANES = 128
PROJ_ROW_TILES = 8
QB_A = 128
QB_B = 256
SUB_A = 32
SUB_B = 32
L_SAFE = 2.0 ** -100
BIS_UNROLL = 3
LOG2E = math.log2(math.e)
VMEM_LIMIT = 56 * 1024 * 1024
INT_MIN = -(2 ** 31)
INT_MAX = 2 ** 31 - 1

C_QA, C_ZA, C_ZB, C_CQ, C_CKV, C_KA, C_VA = (0, 1024, 2048, 3072, 3584, 3840, 3968)
GATE_COL0 = 4168

F32 = jnp.float32
BF16 = jnp.bfloat16
_NT = (((1,), (1,)), ((), ()))


def _cparams(n_grid):
    return pltpu.CompilerParams(
        dimension_semantics=("arbitrary",) * n_grid,
        vmem_limit_bytes=VMEM_LIMIT)


def _t5_bucket_np(d):
    max_exact = N_BUCKETS // 2
    nf = np.maximum(d, 1).astype(np.float32)
    large = max_exact + (np.log(nf / np.float32(max_exact)) / np.float32(math.log(MAX_DISTANCE / max_exact))
                         * np.float32(N_BUCKETS - max_exact)).astype(np.int32)
    large = np.minimum(large, N_BUCKETS - 1)
    return np.where(d < max_exact, d, large).astype(np.int32)


def _bias_tiles_kernel(tab_ref, idx_ref, out_a_ref, out_b_ref):
    h = pl.program_id(0)
    idx = idx_ref[...]
    acc_a = jnp.zeros(idx.shape, F32)
    acc_b = jnp.zeros(idx.shape, F32)
    far_b = tab_ref[N_BUCKETS - 1, A_HEADS + h]
    for b in range(N_BUCKETS):
        hit = idx == b
        acc_a = jnp.where(hit, tab_ref[b, h] * LOG2E, acc_a)
        acc_b = jnp.where(hit, (tab_ref[b, A_HEADS + h] - far_b) * LOG2E, acc_b)
    out_a_ref[0] = acc_a
    out_b_ref[0] = acc_b[:, :2 * LANES]


def _bias_tiles(bias_table):
    r = np.arange(QB_A)[:, None]
    k = np.arange(QB_A)[None, :]
    prev = _t5_bucket_np(np.maximum(QB_A + r - k, 0))
    cur = _t5_bucket_np(np.maximum(r - k, 0))
    far = np.full((QB_A, QB_A), N_BUCKETS - 1, np.int32)
    idx = jnp.asarray(np.concatenate([prev, cur, far], axis=1))
    return pl.pallas_call(
        _bias_tiles_kernel,
        grid=(A_HEADS,),
        in_specs=[pl.BlockSpec(memory_space=pltpu.SMEM),
                  pl.BlockSpec((QB_A, 3 * LANES), lambda h: (0, 0))],
        out_specs=[pl.BlockSpec((1, QB_A, 3 * LANES), lambda h: (h, 0, 0)),
                   pl.BlockSpec((1, QB_A, 2 * LANES), lambda h: (h, 0, 0))],
        out_shape=[jax.ShapeDtypeStruct((A_HEADS, QB_A, 3 * LANES), F32),
                   jax.ShapeDtypeStruct((B_HEADS, QB_A, 2 * LANES), F32)],
        compiler_params=_cparams(1),
        name="bias_tiles",
    )(bias_table, idx)


def _rms(x, g):
    return x * lax.rsqrt(jnp.mean(x * x, axis=-1, keepdims=True) + EPS) * g


def _rmsnorm_kernel(h_ref, g_ref, u_ref):
    u_ref[...] = _rms(h_ref[...], g_ref[...]).astype(u_ref.dtype)


def _rmsnorm(h, g, tm=512):
    rows, d = h.shape
    return pl.pallas_call(
        _rmsnorm_kernel,
        grid=(rows // tm,),
        in_specs=[pl.BlockSpec((tm, d), lambda i: (i, 0)),
                  pl.BlockSpec((1, d), lambda i: (0, 0))],
        out_specs=pl.BlockSpec((tm, d), lambda i: (i, 0)),
        out_shape=jax.ShapeDtypeStruct((rows, d), BF16),
        compiler_params=_cparams(1),
        name="rmsnorm",
    )(h, g.reshape(1, d))


_IN_BLOCKS = ((0, 1, 2, 3), (5, 6, 7, 8), (12, 13, 14, 15), (9, 10, 11, 4))
IN_BLK = 256


def _in_proj_kernel(x_ref, *refs):
    w_refs, o_ref, wbf_scr = refs[:-2], refs[-2], refs[-1]

    @pl.when(pl.program_id(1) == 0)
    def _():
        r0 = 0
        for w_ref in w_refs:
            wbf_scr[r0:r0 + w_ref.shape[0], :] = w_ref[...].astype(BF16)
            r0 += w_ref.shape[0]

    o_ref[...] = lax.dot_general(x_ref[...], wbf_scr[...], _NT, preferred_element_type=F32)


def _in_proj(x, w_t, layer):
    rows, kdim = x.shape
    tm = rows // PROJ_ROW_TILES
    tn = 4 * IN_BLK

    def w_spec(s):
        def index(j, i):
            blk = jnp.int32(_IN_BLOCKS[0][s])
            for t in range(1, len(_IN_BLOCKS)):
                blk = jnp.where(j == t, _IN_BLOCKS[t][s], blk)
            return layer, blk, 0
        return pl.BlockSpec((None, IN_BLK, kdim), index)

    return pl.pallas_call(
        _in_proj_kernel,
        grid=(len(_IN_BLOCKS), rows // tm),
        in_specs=[pl.BlockSpec((tm, kdim), lambda j, i: (i, 0))] + [w_spec(s) for s in range(4)],
        out_specs=pl.BlockSpec((tm, tn), lambda j, i: (i, j)),
        out_shape=jax.ShapeDtypeStruct((rows, len(_IN_BLOCKS) * tn), F32),
        scratch_shapes=[pltpu.VMEM((tn, kdim), BF16)],
        compiler_params=_cparams(2),
        name="in_proj",
    )(x, w_t, w_t, w_t, w_t)


def _gate_proj(x, w_t, layer, col0, n, tn=1024):
    rows, kdim = x.shape
    tm = rows // PROJ_ROW_TILES
    return pl.pallas_call(
        _in_proj_kernel,
        grid=(n // tn, rows // tm),
        in_specs=[pl.BlockSpec((tm, kdim), lambda j, i: (i, 0)),
                  pl.BlockSpec((None, pl.Element(tn), pl.Element(kdim)),
                               lambda j, i: (layer, pl.multiple_of(col0 + j * tn, 8), 0))],
        out_specs=pl.BlockSpec((tm, tn), lambda j, i: (i, j)),
        out_shape=jax.ShapeDtypeStruct((rows, n), F32),
        scratch_shapes=[pltpu.VMEM((tn, kdim), BF16)],
        compiler_params=_cparams(2),
        name="gate_proj",
    )(x, w_t)


def _silu(z):
    return z / (1.0 + jnp.exp(-z))


def _sigmoid(z):
    return 1.0 / (1.0 + jnp.exp(-z))


def _mixer_a_kernel(sink_ref, q_ref, kc_ref, vc_ref, kp_ref, vp_ref, km_ref, vm_ref, za_ref,
                    bias_ref, o_ref, s_scr, p_scr, m_scr, d_scr, msk_scr):
    j = pl.program_id(1)
    nk = 3 * QB_A
    q = (q_ref[...] * (HEAD_DIM ** -0.5 * LOG2E)).astype(BF16)
    kcat = jnp.concatenate([kp_ref[...], kc_ref[...], km_ref[...]], axis=0)
    vcat = jnp.concatenate([vp_ref[...], vc_ref[...], vm_ref[...]], axis=0)
    lane = lax.broadcasted_iota(jnp.int32, (nk, LANES), 1)
    lo = lane < HEAD_DIM
    kswap = pltpu.roll(kcat, HEAD_DIM, 1)
    vswap = pltpu.roll(vcat, HEAD_DIM, 1)
    zero = jnp.zeros_like(kcat)
    k_half = ((jnp.where(lo, kcat, zero).astype(BF16), jnp.where(lo, zero, kswap).astype(BF16)),
              (jnp.where(lo, kswap, zero).astype(BF16), jnp.where(lo, zero, kcat).astype(BF16)))
    v_half = ((jnp.where(lo, vcat, zero).astype(BF16), jnp.where(lo, zero, vswap).astype(BF16)),
              (jnp.where(lo, vswap, zero).astype(BF16), jnp.where(lo, zero, vcat).astype(BF16)))

    r = lax.broadcasted_iota(jnp.int32, (QB_A, nk), 0)
    c = lax.broadcasted_iota(jnp.int32, (QB_A, nk), 1)
    jv = jnp.zeros((QB_A, nk), jnp.int32) + j
    in_prev = (c < QB_A) & (jv >= 1) & ((c > r) | ((jv == 1) & (c < N_META)))
    in_cur = (c >= QB_A) & (c < 2 * QB_A) & (c - QB_A <= r)
    in_meta = (c >= 2 * QB_A) & (jv >= 2) & (c - 2 * QB_A < N_META)
    msk_scr[...] = jnp.where(in_prev | in_cur | in_meta, 0.0, NEG)
    half = lax.broadcasted_iota(jnp.int32, (QB_A, LANES), 1) < HEAD_DIM
    pairs = A_HEADS // 4

    def logits(g, i, par, r0):
        hd = 2 * (g * pairs + i) + par
        s = s_scr[i * QB_A + r0:i * QB_A + r0 + SUB_A, par * nk:(par + 1) * nk]
        return s + bias_ref[hd, r0:r0 + SUB_A, :] + msk_scr[r0:r0 + SUB_A, :], sink_ref[hd] * LOG2E

    for g in range(2):
        q4 = jnp.concatenate([q[:, (g * pairs + i) * LANES:(g * pairs + i + 1) * LANES]
                              for i in range(pairs)], axis=0)
        s_scr[...] = lax.dot_general(q4, jnp.concatenate(k_half[g], axis=0), _NT,
                                     preferred_element_type=F32)
        for i in range(pairs):
            for par in range(2):
                for r0 in range(0, QB_A, SUB_A):
                    lg, sink = logits(g, i, par, r0)
                    m = jnp.maximum(jnp.max(lg, axis=-1, keepdims=True), sink)
                    m_scr[par, i * QB_A + r0:i * QB_A + r0 + SUB_A] = jnp.broadcast_to(m, (SUB_A, LANES))
        for i in range(pairs):
            for par in range(2):
                for r0 in range(0, QB_A, SUB_A):
                    sl = slice(i * QB_A + r0, i * QB_A + r0 + SUB_A)
                    lg, sink = logits(g, i, par, r0)
                    m = m_scr[par, sl]
                    p = jnp.exp2(lg - jnp.concatenate([m] * 3, axis=1))
                    d_scr[par, sl] = jnp.broadcast_to(
                        jnp.sum(p, axis=-1, keepdims=True), (SUB_A, LANES)) + jnp.exp2(sink - m)
                    p_scr[sl, par * nk:(par + 1) * nk] = p.astype(BF16)
        pv = jnp.dot(p_scr[...], jnp.concatenate(v_half[g], axis=0), preferred_element_type=F32)
        for i in range(pairs):
            sl = slice(i * QB_A, (i + 1) * QB_A)
            col = slice((g * pairs + i) * LANES, (g * pairs + i + 1) * LANES)
            o2 = pv[sl] / jnp.where(half, d_scr[0, sl], d_scr[1, sl])
            o_ref[:, col] = (o2 * _silu(za_ref[:, col])).astype(o_ref.dtype)


def _mixer_a(proj, sinks, bias_a, n_batch, lp):
    nb = lp // QB_A
    rows = n_batch * lp
    ck, cv = C_KA // LANES, C_VA // LANES
    cur = lambda b, j: b * nb + j
    prev = lambda b, j: b * nb + jnp.maximum(j - 1, 0)
    first = lambda b, j: b * nb
    return pl.pallas_call(
        _mixer_a_kernel,
        grid=(n_batch, nb),
        in_specs=[
            pl.BlockSpec(memory_space=pltpu.SMEM),
            pl.BlockSpec((QB_A, A_WIDTH), lambda b, j: (cur(b, j), C_QA // A_WIDTH)),
            pl.BlockSpec((QB_A, LANES), lambda b, j: (cur(b, j), ck)),
            pl.BlockSpec((QB_A, LANES), lambda b, j: (cur(b, j), cv)),
            pl.BlockSpec((QB_A, LANES), lambda b, j: (prev(b, j), ck)),
            pl.BlockSpec((QB_A, LANES), lambda b, j: (prev(b, j), cv)),
            pl.BlockSpec((QB_A, LANES), lambda b, j: (first(b, j), ck)),
            pl.BlockSpec((QB_A, LANES), lambda b, j: (first(b, j), cv)),
            pl.BlockSpec((QB_A, A_WIDTH), lambda b, j: (cur(b, j), C_ZA // A_WIDTH)),
            pl.BlockSpec((A_HEADS, QB_A, 3 * LANES), lambda b, j: (0, 0, 0)),
        ],
        out_specs=pl.BlockSpec((QB_A, A_WIDTH), lambda b, j: (cur(b, j), 0)),
        out_shape=jax.ShapeDtypeStruct((rows, A_WIDTH), BF16),
        scratch_shapes=[
            pltpu.VMEM((A_HEADS // 4 * QB_A, 6 * QB_A), F32),
            pltpu.VMEM((A_HEADS // 4 * QB_A, 6 * QB_A), BF16),
            pltpu.VMEM((2, A_HEADS // 4 * QB_A, LANES), F32),
            pltpu.VMEM((2, A_HEADS // 4 * QB_A, LANES), F32),
            pltpu.VMEM((QB_A, 3 * QB_A), F32),
        ],
        compiler_params=_cparams(2),
        name="mixer_a",
    )(sinks, proj, proj, proj, proj, proj, proj, proj, proj, bias_a)


def _bprep_kernel(cq_ref, ckv_ref, u_ref, qg_ref, kvg_ref, wqb_ref, wiq_ref, wuk_ref, wsm_ref,
                  ql_ref, qi_ref, wi_ref, ckvn_ref, klo_ref, khi_ref, kn2_ref):
    cqn = _rms(cq_ref[...], qg_ref[...]).astype(BF16)
    ckvn = _rms(ckv_ref[...], kvg_ref[...]).astype(ckvn_ref.dtype)
    ckvn_ref[0] = ckvn
    kn2 = jnp.sum(jnp.square(ckvn.astype(F32)), axis=-1, keepdims=True)
    kn2_ref[0, 0] = jnp.broadcast_to(jnp.max(kn2, axis=0, keepdims=True), kn2_ref.shape[2:])
    qb = jnp.dot(cqn, wqb_ref[...], preferred_element_type=F32).astype(BF16)
    for pair in range(B_HEADS // 2):
        ql2 = jnp.dot(qb[:, pair * LANES:(pair + 1) * LANES], wuk_ref[pair],
                      preferred_element_type=F32)
        ql_ref[0, 2 * pair] = ql2[:, :B_KV_RANK].astype(ql_ref.dtype)
        ql_ref[0, 2 * pair + 1] = ql2[:, B_KV_RANK:].astype(ql_ref.dtype)
    qi = jnp.dot(cqn, wiq_ref[...], preferred_element_type=F32)
    for pair in range(IDX_HEADS // 2):
        qi_ref[0, pair] = qi[:, pair * LANES:(pair + 1) * LANES].astype(qi_ref.dtype)
    small = jnp.dot(u_ref[...], wsm_ref[...], preferred_element_type=F32)
    klo_ref[0] = small[:, :LANES].astype(klo_ref.dtype)
    khi_ref[0] = small[:, LANES:2 * LANES].astype(khi_ref.dtype)
    wi_ref[0] = small[:, 2 * LANES:] * (IDX_HEADS ** -0.5)


def _bprep(proj, u, qg, kvg, wqb, wiq, wuk_bd, wsmall, layer, n_batch, lp, tm=QB_B):
    nt = lp // tm
    row = lambda b, i: b * nt + i
    const2 = lambda b, i: (0, 0)
    lay2 = lambda b, i: (layer, 0, 0)
    return pl.pallas_call(
        _bprep_kernel,
        grid=(n_batch, nt),
        in_specs=[
            pl.BlockSpec((tm, B_Q_RANK), lambda b, i: (row(b, i), C_CQ // B_Q_RANK)),
            pl.BlockSpec((tm, B_KV_RANK), lambda b, i: (row(b, i), C_CKV // B_KV_RANK)),
            pl.BlockSpec((tm, D_MODEL), lambda b, i: (row(b, i), 0)),
            pl.BlockSpec((1, B_Q_RANK), const2),
            pl.BlockSpec((1, B_KV_RANK), const2),
            pl.BlockSpec((None, B_Q_RANK, B_WIDTH), lay2),
            pl.BlockSpec((None, B_Q_RANK, IDX_HEADS * IDX_DIM), lay2),
            pl.BlockSpec((None, B_HEADS // 2, LANES, 2 * B_KV_RANK), lambda b, i: (layer, 0, 0, 0)),
            pl.BlockSpec((None, D_MODEL, 3 * LANES), lay2),
        ],
        out_specs=[
            pl.BlockSpec((1, B_HEADS, tm, B_KV_RANK), lambda b, i: (b, 0, i, 0)),
            pl.BlockSpec((1, IDX_HEADS // 2, tm, LANES), lambda b, i: (b, 0, i, 0)),
            pl.BlockSpec((1, tm, LANES), lambda b, i: (b, i, 0)),
            pl.BlockSpec((1, tm, B_KV_RANK), lambda b, i: (b, i, 0)),
            pl.BlockSpec((1, tm, LANES), lambda b, i: (b, i, 0)),
            pl.BlockSpec((1, tm, LANES), lambda b, i: (b, i, 0)),
            pl.BlockSpec((1, 1, 8, LANES), lambda b, i: (b, i, 0, 0)),
        ],
        out_shape=[
            jax.ShapeDtypeStruct((n_batch, B_HEADS, lp, B_KV_RANK), BF16),
            jax.ShapeDtypeStruct((n_batch, IDX_HEADS // 2, lp, LANES), BF16),
            jax.ShapeDtypeStruct((n_batch, lp, LANES), F32),
            jax.ShapeDtypeStruct((n_batch, lp, B_KV_RANK), BF16),
            jax.ShapeDtypeStruct((n_batch, lp, LANES), BF16),
            jax.ShapeDtypeStruct((n_batch, lp, LANES), BF16),
            jax.ShapeDtypeStruct((n_batch, nt, 8, LANES), F32),
        ],
        compiler_params=_cparams(2),
        name="bprep",
    )(proj, proj, u, qg, kvg, wqb, wiq, wuk_bd, wsmall)


def _mixer_b_kernel(bmax_ref, ql_ref, qi_ref, wi_ref, ckv_ref, klo_ref, khi_ref, kn2_ref, zb_ref, bias_ref,
                    wuv_ref, o_ref, flag_ref, key_scr, s_scr, acc_scr, m_scr, l_scr, alpha_scr, p_scr,
                    *, bound, tail_q):
    jq = pl.program_id(1)
    nch = jq + 1
    qb, kc = QB_B, QB_B
    rows = B_HEADS * qb
    kf = float(TOPK_MAX)

    qi = qi_ref[0].reshape(IDX_HEADS // 2 * qb, LANES)
    wit = wi_ref[0].T
    krow = lax.broadcasted_iota(jnp.int32, (qb, kc), 0)
    krow_sub = lax.broadcasted_iota(jnp.int32, (SUB_B, kc), 0)
    qpos = jq * qb + lax.broadcasted_iota(jnp.int32, (qb, kc), 1)

    def idx_body(c, carry):
        start = pl.multiple_of(c * kc, kc)
        sc = (lax.dot_general(klo_ref[0, pl.ds(start, kc), :], qi, _NT, preferred_element_type=F32),
              lax.dot_general(khi_ref[0, pl.ds(start, kc), :], qi, _NT, preferred_element_type=F32))
        isc = jnp.zeros((kc, qb), F32)
        for pair in range(IDX_HEADS // 2):
            for par in range(2):
                hd = 2 * pair + par
                isc = isc + wit[hd:hd + 1, :] * jnp.maximum(sc[par][:, pair * qb:(pair + 1) * qb], 0.0)
        isc = jnp.where(isc == 0.0, 0.0, isc)
        bits = lax.bitcast_convert_type(isc, jnp.int32)
        key = bits ^ ((bits >> 31) & INT_MAX)
        key_scr[c] = jnp.where(c * kc + krow <= qpos, key, INT_MIN)
        return carry

    lax.fori_loop(0, nch, idx_body, 0)

    def reduce_chunks(tile_fn, init, combine):
        n_acc = SUB_B // 8

        def body(c, accs):
            accs = list(accs)
            for r0 in range(0, kc, SUB_B):
                t = tile_fn(c, r0, key_scr[c, r0:r0 + SUB_B, :])
                t = t.reshape(n_acc, 8, qb)
                for i in range(n_acc):
                    accs[i] = combine(accs[i], t[i])
            return tuple(accs)

        accs = lax.fori_loop(0, nch, body, (init,) * n_acc)
        acc = accs[0]
        for a in accs[1:]:
            acc = combine(acc, a)
        return acc

    def count(pred_fn):
        acc = reduce_chunks(lambda c, r0, k: jnp.where(pred_fn(c, r0, k), 1.0, 0.0),
                            jnp.zeros((8, qb), F32), lambda a, b: a + b)
        return jnp.sum(acc, axis=0, keepdims=True)

    def unkey(k):
        return lax.bitcast_convert_type(k ^ ((k >> 31) & INT_MAX), F32)

    def tokey(v):
        b = lax.bitcast_convert_type(v, jnp.int32)
        return b ^ ((b >> 31) & INT_MAX)

    def select_all():
        return jnp.full((1, qb), INT_MIN, jnp.int32), jnp.full((1, qb), -1, jnp.int32)

    def select_topk():
        kmax = reduce_chunks(lambda c, r0, k: k, jnp.full((8, qb), INT_MIN, jnp.int32), jnp.maximum)
        kmax = jnp.max(kmax, axis=0, keepdims=True)
        kmin = reduce_chunks(lambda c, r0, k: jnp.where(k == INT_MIN, INT_MAX, k),
                             jnp.full((8, qb), INT_MAX, jnp.int32), jnp.minimum)
        kmin = jnp.min(kmin, axis=0, keepdims=True)
        c0 = count(lambda c, r0, k: k >= 0)
        c0p = count(lambda c, r0, k: k >= 1)
        n_valid = (jq * qb + 1 + lax.broadcasted_iota(jnp.int32, (1, qb), 1)).astype(F32)
        pos = c0p >= kf
        zero = (c0 >= kf) & (c0p < kf)
        lo0 = jnp.where(pos, 1, jnp.where(zero, 0, kmin))
        hi0 = jnp.where(pos, kmax + 1, jnp.where(zero, 1, 0))
        c_lo0 = jnp.where(pos, c0p, jnp.where(zero, c0, n_valid))
        c_hi0 = jnp.where(pos, 0.0, jnp.where(zero, c0p, c0))

        def bis_cond(st):
            return (st[0] < 64) & (st[5] > 0)

        def bis_step(it, lo, hi, c_lo, c_hi):
            mid_i = (lo >> 1) + (hi >> 1) + (lo & hi & 1)
            mid_v = tokey(0.5 * unkey(lo) + 0.5 * unkey(hi))
            use_v = (it < 24) & (mid_v > lo) & (mid_v < hi)
            mid = jnp.where(use_v, mid_v, mid_i)
            cnt = count(lambda c, r0, k: k >= mid)
            ge = cnt >= kf
            return (jnp.where(ge, mid, lo), jnp.where(ge, hi, mid),
                    jnp.where(ge, cnt, c_lo), jnp.where(ge, c_hi, cnt))

        def bis_body(st):
            it, lo, hi, c_lo, c_hi, _ = st
            for _ in range(BIS_UNROLL):
                lo, hi, c_lo, c_hi = bis_step(it, lo, hi, c_lo, c_hi)
                it = it + 1
            return it, lo, hi, c_lo, c_hi, jnp.max(jnp.where(is_active(lo, hi, c_lo, c_hi), 1, 0))

        def is_active(lo, hi, c_lo, c_hi):
            return (lo + 1 < hi) & (c_lo != kf) & (c_lo - c_hi > 2.0)

        st0 = (jnp.int32(0), lo0, hi0, c_lo0, c_hi0,
               jnp.max(jnp.where(is_active(lo0, hi0, c_lo0, c_hi0), 1, 0)))
        _, lo, hi, c_lo, c_hi, _ = lax.while_loop(bis_cond, bis_body, st0)
        open_rows = c_lo != kf

        def finish():
            top = reduce_chunks(lambda c, r0, k: jnp.where(k < hi, k, INT_MIN),
                                jnp.full((8, qb), INT_MIN, jnp.int32), jnp.maximum)
            top = jnp.max(top, axis=0, keepdims=True)
            bot = reduce_chunks(lambda c, r0, k: jnp.where(k >= lo, k, INT_MAX),
                                jnp.full((8, qb), INT_MAX, jnp.int32), jnp.minimum)
            bot = jnp.min(bot, axis=0, keepdims=True)
            thr = jnp.where(open_rows, top, lo)
            tie = open_rows & (top == bot)
            need = kf - c_hi

            def tie_break():
                def tie_body(_, st):
                    jlo, jhi = st
                    mid = (jlo + jhi) >> 1
                    cnt = count(lambda c, r0, k: (k == thr) & (c * kc + r0 + krow_sub <= mid))
                    ge = cnt >= need
                    return jnp.where(ge, jlo, mid), jnp.where(ge, mid, jhi)

                _, jhi = lax.fori_loop(0, 13, tie_body, (jnp.full((1, qb), -1, jnp.int32),
                                                         jnp.full((1, qb), 8191, jnp.int32)))
                return jnp.where(tie, jhi, INT_MAX)

            jmax = lax.cond(jnp.max(jnp.where(tie, 1, 0)) > 0, tie_break,
                            lambda: jnp.full((1, qb), INT_MAX, jnp.int32))
            return thr, jmax

        return lax.cond(jnp.max(jnp.where(open_rows, 1, 0)) > 0, finish,
                        lambda: (lo, jnp.full((1, qb), INT_MAX, jnp.int32)))

    thr, jmax = lax.cond(jq >= 1, select_topk, select_all)

    def sel_body(c, carry):
        k = key_scr[c]
        sel = (k > thr) | ((k == thr) & (c * kc + krow <= jmax))
        key_scr[c] = lax.bitcast_convert_type(jnp.where(sel, 0.0, NEG).T, jnp.int32)
        return carry

    lax.fori_loop(0, nch, sel_body, 0)

    def kv_chunk(c):
        return ckv_ref[0, pl.ds(pl.multiple_of(c * kc, kc), kc), :]

    def qk(nq, c, slot):
        s_scr[slot, :B_HEADS * nq] = lax.dot_general(
            ql_ref[0, :, :nq, :].reshape(B_HEADS * nq, B_KV_RANK), kv_chunk(c), _NT,
            preferred_element_type=F32)

    def logits(nq, c, slot, near, hd, r0):
        sub = min(SUB_B, nq)
        sh = s_scr[slot, hd * nq + r0:hd * nq + r0 + sub]
        if near == "cur":
            t0 = bias_ref[hd, r0 % QB_A:r0 % QB_A + sub, LANES:]
            t1 = bias_ref[hd, r0 % QB_A:r0 % QB_A + sub, :LANES]
            sh = sh + jnp.concatenate([t0 if r0 < QB_A else t1, t0], axis=1)
        elif near == "prev" and r0 < QB_A:
            t1 = bias_ref[hd, r0:r0 + sub, :LANES]
            sh = sh + jnp.concatenate([jnp.zeros_like(t1), t1], axis=1)
        return sh + lax.bitcast_convert_type(key_scr[c, r0:r0 + sub, :], F32)

    def sub_tiles(nq):
        sub = min(SUB_B, nq)
        for hd in range(B_HEADS):
            for r0 in range(0, nq, sub):
                yield hd, r0, slice(hd * nq + r0, hd * nq + r0 + sub)

    def sm(nq, c, slot, near):
        for hd, r0, sl in sub_tiles(nq):
            m = m_scr[sl]
            p = jnp.exp2(logits(nq, c, slot, near, hd, r0) - jnp.concatenate([m, m], axis=1))
            l_scr[sl] += p[:, :LANES] + p[:, LANES:]
            p_scr[slot, sl] = p.astype(BF16)

    def pv(nq, c, slot):
        o = jnp.dot(p_scr[slot, :B_HEADS * nq], kv_chunk(c), preferred_element_type=F32)
        sub = min(SUB_B, nq)
        for r0 in range(0, B_HEADS * nq, sub):
            acc_scr[r0:r0 + sub] += o[r0:r0 + sub]

    def bound_attention(nq):
        l_scr[...] = jnp.zeros(l_scr.shape, F32)
        acc_scr[...] = jnp.zeros(acc_scr.shape, F32)
        kn2 = lax.fori_loop(0, nch, lambda c, a: jnp.maximum(a, kn2_ref[0, c]), jnp.zeros((8, LANES), F32))
        bmax = bmax_ref[0]
        for hd, r0, sl in sub_tiles(nq):
            x = ql_ref[0, hd, r0:r0 + min(SUB_B, nq), :].astype(F32)
            qn2 = jnp.sum(x * x, axis=-1, keepdims=True)
            m_scr[sl] = jnp.sqrt(qn2 * kn2[:1]) * (1.0 + 2.0 ** -10) + bmax

        def stage(t, slot):
            qk(nq, t, slot)
            pv(nq, t - 2, slot)
            sm(nq, t - 1, 1 - slot, None)

        def tail(slot):
            qk(nq, jq, slot)
            pv(nq, jq - 2, slot)
            sm(nq, jq - 1, 1 - slot, "prev")
            pv(nq, jq - 1, 1 - slot)
            sm(nq, jq, slot, "cur")
            pv(nq, jq, slot)

        @pl.when(jq >= 2)
        def _():
            qk(nq, 0, 0)
            qk(nq, 1, 1)
            sm(nq, 0, 0, None)

            def two_stages(i, carry):
                stage(2 + 2 * i, 0)
                stage(3 + 2 * i, 1)
                return carry

            lax.fori_loop(0, (jq - 2) // 2, two_stages, 0)

            @pl.when(jq % 2 == 0)
            def _():
                tail(0)

            @pl.when(jq % 2 == 1)
            def _():
                stage(jq - 1, 0)
                tail(1)

        @pl.when(jq < 2)
        def _():
            @pl.when(jq == 1)
            def _():
                qk(nq, 0, 0)
                sm(nq, 0, 0, "prev")
                pv(nq, 0, 0)

            qk(nq, jq, 1)
            sm(nq, jq, 1, "cur")
            pv(nq, jq, 1)

    def running_max_attention():
        l_scr[...] = jnp.zeros(l_scr.shape, F32)
        acc_scr[...] = jnp.zeros(acc_scr.shape, F32)
        m_scr[...] = jnp.full(m_scr.shape, -jnp.inf, F32)

        def attend(c, near):
            qk(qb, c, 0)
            for hd, r0, sl in sub_tiles(qb):
                m_old = m_scr[sl]
                m_new = jnp.maximum(m_old, jnp.max(logits(qb, c, 0, near, hd, r0), axis=-1, keepdims=True))
                m_scr[sl] = m_new
                alpha_scr[sl] = jnp.exp2(m_old - m_new)
            for hd, r0, sl in sub_tiles(qb):
                m_new = m_scr[sl]
                p = jnp.exp2(logits(qb, c, 0, near, hd, r0) - jnp.concatenate([m_new, m_new], axis=1))
                l_scr[sl] = alpha_scr[sl] * l_scr[sl] + (p[:, :LANES] + p[:, LANES:])
                p_scr[0, sl] = p.astype(BF16)
            o = jnp.dot(p_scr[0], kv_chunk(c), preferred_element_type=F32)
            for r0 in range(0, rows, SUB_B):
                sl = slice(r0, r0 + SUB_B)
                alpha = alpha_scr[sl]
                acc_scr[sl] = acc_scr[sl] * jnp.concatenate([alpha, alpha], axis=1) + o[sl]

        def far_body(c, carry):
            attend(c, None)
            return carry

        lax.fori_loop(0, jq - 1, far_body, 0)

        @pl.when(jq >= 1)
        def _():
            attend(jq - 1, "prev")

        attend(jq, "cur")

    def write_output(nq):
        l_min = jnp.full((nq, 1), jnp.inf, F32)
        for pair in range(B_HEADS // 2):
            col = slice(pair * LANES, (pair + 1) * LANES)
            y2 = jnp.zeros((nq, LANES), F32)
            for par in range(2):
                hd = 2 * pair + par
                sl = slice(hd * nq, (hd + 1) * nq)
                l = jnp.sum(l_scr[sl], axis=-1, keepdims=True)
                l_min = jnp.minimum(l_min, l)
                o_lat = acc_scr[sl] / l
                y2 = y2 + jnp.dot(o_lat.astype(BF16), wuv_ref[pair, par], preferred_element_type=F32)
            o_ref[:nq, col] = (y2 * _silu(zb_ref[:nq, col])).astype(o_ref.dtype)
            if nq < qb:
                o_ref[nq:, col] = jnp.zeros((qb - nq, LANES), o_ref.dtype)
        safe = jnp.min(l_min) >= L_SAFE
        flag_ref[0, 0] = jnp.full(flag_ref.shape[2:], jnp.where(safe, 0.0, 1.0), F32)

    if not bound:
        running_max_attention()
        write_output(qb)
    elif tail_q == qb:
        bound_attention(qb)
        write_output(qb)
    else:
        last = pl.num_programs(1) - 1

        @pl.when(jq < last)
        def _():
            bound_attention(qb)
            write_output(qb)

        @pl.when(jq == last)
        def _():
            bound_attention(tail_q)
            write_output(tail_q)


def _mixer_b(bmax, ql, qi, wi, ckvn, klo, khi, kn2, proj, bias_b, wuv_pairs, layer, n_batch, lp, length,
             bound):
    nb = lp // QB_B
    tail_q = min(QB_B, -(-(length - (nb - 1) * QB_B) // 16) * 16)
    rows = n_batch * lp
    whole = lambda b, j: (b, 0, 0)
    once = pl.Buffered(1)
    return pl.pallas_call(
        functools.partial(_mixer_b_kernel, bound=bound, tail_q=tail_q),
        grid=(n_batch, nb),
        in_specs=[
            pl.BlockSpec(memory_space=pltpu.SMEM),
            pl.BlockSpec((1, B_HEADS, QB_B, B_KV_RANK), lambda b, j: (b, 0, j, 0)),
            pl.BlockSpec((1, IDX_HEADS // 2, QB_B, LANES), lambda b, j: (b, 0, j, 0)),
            pl.BlockSpec((1, QB_B, LANES), lambda b, j: (b, j, 0)),
            pl.BlockSpec((1, lp, B_KV_RANK), whole, pipeline_mode=once),
            pl.BlockSpec((1, lp, LANES), whole, pipeline_mode=once),
            pl.BlockSpec((1, lp, LANES), whole, pipeline_mode=once),
            pl.BlockSpec((1, nb, 8, LANES), lambda b, j: (b, 0, 0, 0), pipeline_mode=once),
            pl.BlockSpec((QB_B, B_WIDTH), lambda b, j: (b * nb + j, C_ZB // B_WIDTH)),
            pl.BlockSpec((B_HEADS, QB_A, 2 * LANES), lambda b, j: (0, 0, 0), pipeline_mode=once),
            pl.BlockSpec((None, B_HEADS // 2, 2, B_KV_RANK, LANES), lambda b, j: (layer, 0, 0, 0, 0),
                         pipeline_mode=once),
        ],
        out_specs=[pl.BlockSpec((QB_B, B_WIDTH), lambda b, j: (b * nb + j, 0)),
                   pl.BlockSpec((1, 1, 8, LANES), lambda b, j: (b, j, 0, 0))],
        out_shape=[jax.ShapeDtypeStruct((rows, B_WIDTH), BF16),
                   jax.ShapeDtypeStruct((n_batch, nb, 8, LANES), F32)],
        scratch_shapes=[
            pltpu.VMEM((nb, QB_B, QB_B), jnp.int32),
            pltpu.VMEM((2, B_HEADS * QB_B, QB_B), F32),
            pltpu.VMEM((B_HEADS * QB_B, B_KV_RANK), F32),
            pltpu.VMEM((B_HEADS * QB_B, LANES), F32),
            pltpu.VMEM((B_HEADS * QB_B, LANES), F32),
            pltpu.VMEM((B_HEADS * QB_B, LANES), F32),
            pltpu.VMEM((2, B_HEADS * QB_B, QB_B), BF16),
        ],
        compiler_params=_cparams(2),
        name="mixer_b" if bound else "mixer_b_running_max",
    )(bmax, ql, qi, wi, ckvn, klo, khi, kn2, proj, bias_b, wuv_pairs)


def _merge_kernel(ya_ref, yb_ref, ga_ref, gb_ref, h_ref, wpa_ref, wpb_ref, wo_ref, g_ref, *out_refs,
                  last):
    pa = jnp.dot(ya_ref[...], wpa_ref[...], preferred_element_type=F32)
    pb = jnp.dot(yb_ref[...], wpb_ref[...], preferred_element_type=F32)
    merged = _sigmoid(ga_ref[...]) * pa + _sigmoid(gb_ref[...]) * pb
    h_new = h_ref[...] + jnp.dot(merged.astype(BF16), wo_ref[...], preferred_element_type=F32)
    normed = _rms(h_new, g_ref[...])
    if last:
        out_refs[0][...] = normed
    else:
        out_refs[0][...] = h_new
        out_refs[1][...] = normed.astype(BF16)


def _merge(ya, yb, gates, h, wpa, wpb, wo, g_next, layer, last, tm=256):
    rows, d = h.shape
    const = lambda i: (0, 0)
    lay = lambda i: (layer, 0, 0)
    once = pl.Buffered(1)
    if last:
        out_specs = [pl.BlockSpec((tm, d), lambda i: (i, 0))]
        out_shape = [jax.ShapeDtypeStruct((rows, d), F32)]
    else:
        out_specs = [pl.BlockSpec((tm, d), lambda i: (i, 0)), pl.BlockSpec((tm, d), lambda i: (i, 0))]
        out_shape = [jax.ShapeDtypeStruct((rows, d), F32), jax.ShapeDtypeStruct((rows, d), BF16)]
    return pl.pallas_call(
        functools.partial(_merge_kernel, last=last),
        grid=(rows // tm,),
        in_specs=[
            pl.BlockSpec((tm, A_WIDTH), lambda i: (i, 0)),
            pl.BlockSpec((tm, B_WIDTH), lambda i: (i, 0)),
            pl.BlockSpec((tm, d), lambda i: (i, 0)),
            pl.BlockSpec((tm, d), lambda i: (i, 1)),
            pl.BlockSpec((tm, d), lambda i: (i, 0)),
            pl.BlockSpec((None, A_WIDTH, d), lay, pipeline_mode=once),
            pl.BlockSpec((None, B_WIDTH, d), lay, pipeline_mode=once),
            pl.BlockSpec((None, d, d), lay, pipeline_mode=once),
            pl.BlockSpec((1, d), const),
        ],
        out_specs=out_specs,
        out_shape=out_shape,
        compiler_params=_cparams(1),
        name="merge_out",
    )(ya, yb, gates, gates, h, wpa, wpb, wo, g_next.reshape(1, d))


def _pack_weights(w_in, w_uk, w_uv, w_qb, w_iq, w_proj_a, w_proj_b, w_out):
    depth = w_in.shape[0]
    sec = lambda a, b: w_in[:, :, a:b]
    kidx, widx = sec(4096, 4160), sec(4160, 4168)
    z64 = jnp.zeros_like(kidx)
    zpad = jnp.zeros(widx.shape[:2] + (LANES - IDX_HEADS,), w_in.dtype)
    w_small = jnp.concatenate([kidx, z64, z64, kidx, widx, zpad], axis=-1).astype(BF16)
    uk = (w_uk * (HEAD_DIM ** -0.5 * LOG2E)).reshape(depth, B_HEADS // 2, 2, HEAD_DIM, B_KV_RANK)
    zuk = jnp.zeros_like(uk[:, :, 0])
    wuk_bd = jnp.concatenate([jnp.concatenate([uk[:, :, 0], zuk], axis=-1),
                              jnp.concatenate([zuk, uk[:, :, 1]], axis=-1)], axis=-2).astype(BF16)
    uv = w_uv.reshape(depth, B_HEADS // 2, 2, B_KV_RANK, HEAD_DIM)
    zuv = jnp.zeros_like(uv[:, :, 0])
    wuv_pairs = jnp.stack([jnp.concatenate([uv[:, :, 0], zuv], axis=-1),
                           jnp.concatenate([zuv, uv[:, :, 1]], axis=-1)], axis=2).astype(BF16)
    w_iq8 = (w_iq * (IDX_DIM ** -0.5)).astype(BF16)
    return (w_small, wuk_bd, wuv_pairs, w_qb.astype(BF16), w_iq8,
            w_proj_a.astype(BF16), w_proj_b.astype(BF16), w_out.astype(BF16))


def kernel(x, meta_tokens, bias_table, norm_g, w_in, q_norm_g, kv_norm_g, w_qb, w_iq, w_uk, w_uv,
           sinks, w_proj_a, w_proj_b, w_out, final_g):
    n_batch, seq, d = x.shape
    depth = w_in.shape[0]
    length = seq + N_META
    lp = -(-length // QB_B) * QB_B
    assert min(TOPK_MAX, seq // 4) == TOPK_MAX and d == D_MODEL

    (w_small, wuk_bd, wuv_pairs, wqb, wiq, wpa, wpb, wo) = _pack_weights(
        w_in, w_uk, w_uv, w_qb, w_iq, w_proj_a, w_proj_b, w_out)
    w_t = jnp.swapaxes(w_in, 1, 2)
    bias_a, bias_b = _bias_tiles(bias_table)
    bmax = jnp.max(jnp.abs(bias_b)).reshape(1)

    meta = jnp.broadcast_to(meta_tokens.astype(x.dtype)[None], (n_batch, N_META, d))
    pad = jnp.zeros((n_batch, lp - length, d), x.dtype)
    h = jnp.concatenate([meta, x, pad], axis=1).reshape(n_batch * lp, d)

    u = _rmsnorm(h, norm_g[0])
    out = None
    for l in range(depth):
        proj = _in_proj(u, w_t, l)
        gates = _gate_proj(u, w_t, l, GATE_COL0, 2 * D_MODEL)
        ya = _mixer_a(proj, sinks[l], bias_a, n_batch, lp)
        ql, qi, wi, ckvn, klo, khi, kn2 = _bprep(
            proj, u, q_norm_g[l].reshape(1, -1), kv_norm_g[l].reshape(1, -1),
            wqb, wiq, wuk_bd, w_small, l, n_batch, lp)
        b_args = (bmax, ql, qi, wi, ckvn, klo, khi, kn2, proj, bias_b, wuv_pairs, l, n_batch, lp, length)
        yb, flags = _mixer_b(*b_args, bound=True)
        yb = lax.cond(jnp.max(flags) > 0.0, lambda: _mixer_b(*b_args, bound=False)[0], lambda: yb)
        last = l == depth - 1
        g_next = final_g if last else norm_g[l + 1]
        res = _merge(ya, yb, gates, h, wpa, wpb, wo, g_next, l, last)
        if last:
            out = res[0]
        else:
            h, u = res
    return out.reshape(n_batch, lp, d)[:, N_META:length]
```

```python
import functools
import math

import numpy as np
import jax
import jax.numpy as jnp
from jax import lax
from jax.experimental import pallas as pl
from jax.experimental.pallas import tpu as pltpu

D_MODEL = 2048
N_META = 16
WINDOW = 128
HEAD_DIM = 64
A_HEADS = 16
A_WIDTH = 1024
B_HEADS = 16
B_WIDTH = 1024
B_Q_RANK = 512
B_KV_RANK = 256
IDX_HEADS = 8
IDX_DIM = 64
TOPK_MAX = 256
N_BUCKETS = 32
MAX_DISTANCE = 128
EPS = 1e-6
NEG = -1e30

LANES = 128
PROJ_ROW_TILES = 8
QB_A = 128
QB_B = 256
SUB_A = 32
SUB_B = 32
L_SAFE = 2.0 ** -100
BIS_UNROLL = 3
LOG2E = math.log2(math.e)
VMEM_LIMIT = 56 * 1024 * 1024
INT_MAX = 2 ** 31 - 1
F32_TINY = 2.0 ** -126

C_QA, C_ZA, C_ZB, C_CQ, C_CKV, C_KA, C_VA = (0, 1024, 2048, 3072, 3584, 3840, 3968)
GATE_COL0 = 4168

F32 = jnp.float32
BF16 = jnp.bfloat16
_NT = (((1,), (1,)), ((), ()))


def _cparams(n_grid):
    return pltpu.CompilerParams(
        dimension_semantics=("arbitrary",) * n_grid,
        vmem_limit_bytes=VMEM_LIMIT)


def _t5_bucket_np(d):
    max_exact = N_BUCKETS // 2
    nf = np.maximum(d, 1).astype(np.float32)
    large = max_exact + (np.log(nf / np.float32(max_exact)) / np.float32(math.log(MAX_DISTANCE / max_exact))
                         * np.float32(N_BUCKETS - max_exact)).astype(np.int32)
    large = np.minimum(large, N_BUCKETS - 1)
    return np.where(d < max_exact, d, large).astype(np.int32)


def _bias_tiles_kernel(tab_ref, idx_ref, out_a_ref, out_b_ref):
    h = pl.program_id(0)
    idx = idx_ref[...]
    acc_a = jnp.zeros(idx.shape, F32)
    acc_b = jnp.zeros(idx.shape, F32)
    far_b = tab_ref[N_BUCKETS - 1, A_HEADS + h]
    for b in range(N_BUCKETS):
        hit = idx == b
        acc_a = jnp.where(hit, tab_ref[b, h] * LOG2E, acc_a)
        acc_b = jnp.where(hit, (tab_ref[b, A_HEADS + h] - far_b) * LOG2E, acc_b)
    out_a_ref[0] = acc_a
    out_b_ref[0] = acc_b[:, :2 * LANES]


def _bias_tiles(bias_table):
    r = np.arange(QB_A)[:, None]
    k = np.arange(QB_A)[None, :]
    prev = _t5_bucket_np(np.maximum(QB_A + r - k, 0))
    cur = _t5_bucket_np(np.maximum(r - k, 0))
    far = np.full((QB_A, QB_A), N_BUCKETS - 1, np.int32)
    idx = jnp.asarray(np.concatenate([prev, cur, far], axis=1))
    return pl.pallas_call(
        _bias_tiles_kernel,
        grid=(A_HEADS,),
        in_specs=[pl.BlockSpec(memory_space=pltpu.SMEM),
                  pl.BlockSpec((QB_A, 3 * LANES), lambda h: (0, 0))],
        out_specs=[pl.BlockSpec((1, QB_A, 3 * LANES), lambda h: (h, 0, 0)),
                   pl.BlockSpec((1, QB_A, 2 * LANES), lambda h: (h, 0, 0))],
        out_shape=[jax.ShapeDtypeStruct((A_HEADS, QB_A, 3 * LANES), F32),
                   jax.ShapeDtypeStruct((B_HEADS, QB_A, 2 * LANES), F32)],
        compiler_params=_cparams(1),
        name="bias_tiles",
    )(bias_table, idx)


def _rms(x, g):
    return x * lax.rsqrt(jnp.mean(x * x, axis=-1, keepdims=True) + EPS) * g


def _rmsnorm_kernel(h_ref, g_ref, u_ref):
    u_ref[...] = _rms(h_ref[...], g_ref[...]).astype(u_ref.dtype)


def _rmsnorm(h, g, tm=512):
    rows, d = h.shape
    return pl.pallas_call(
        _rmsnorm_kernel,
        grid=(rows // tm,),
        in_specs=[pl.BlockSpec((tm, d), lambda i: (i, 0)),
                  pl.BlockSpec((1, d), lambda i: (0, 0))],
        out_specs=pl.BlockSpec((tm, d), lambda i: (i, 0)),
        out_shape=jax.ShapeDtypeStruct((rows, d), BF16),
        compiler_params=_cparams(1),
        name="rmsnorm",
    )(h, g.reshape(1, d))


_IN_BLOCKS = ((0, 1, 2, 3), (5, 6, 7, 8), (12, 13, 14, 15), (9, 10, 11, 4))
IN_BLK = 256


def _in_proj_kernel(x_ref, *refs):
    w_refs, o_ref, wbf_scr = refs[:-2], refs[-2], refs[-1]

    @pl.when(pl.program_id(1) == 0)
    def _():
        r0 = 0
        for w_ref in w_refs:
            wbf_scr[r0:r0 + w_ref.shape[0], :] = w_ref[...].astype(BF16)
            r0 += w_ref.shape[0]

    o_ref[...] = lax.dot_general(x_ref[...], wbf_scr[...], _NT, preferred_element_type=F32)


def _in_proj(x, w_t, layer):
    rows, kdim = x.shape
    tm = rows // PROJ_ROW_TILES
    tn = 4 * IN_BLK

    def w_spec(s):
        def index(j, i):
            blk = jnp.int32(_IN_BLOCKS[0][s])
            for t in range(1, len(_IN_BLOCKS)):
                blk = jnp.where(j == t, _IN_BLOCKS[t][s], blk)
            return layer, blk, 0
        return pl.BlockSpec((None, IN_BLK, kdim), index)

    return pl.pallas_call(
        _in_proj_kernel,
        grid=(len(_IN_BLOCKS), rows // tm),
        in_specs=[pl.BlockSpec((tm, kdim), lambda j, i: (i, 0))] + [w_spec(s) for s in range(4)],
        out_specs=pl.BlockSpec((tm, tn), lambda j, i: (i, j)),
        out_shape=jax.ShapeDtypeStruct((rows, len(_IN_BLOCKS) * tn), F32),
        scratch_shapes=[pltpu.VMEM((tn, kdim), BF16)],
        compiler_params=_cparams(2),
        name="in_proj",
    )(x, w_t, w_t, w_t, w_t)


def _gate_proj(x, w_t, layer, col0, n, tn=1024):
    rows, kdim = x.shape
    tm = rows // PROJ_ROW_TILES
    return pl.pallas_call(
        _in_proj_kernel,
        grid=(n // tn, rows // tm),
        in_specs=[pl.BlockSpec((tm, kdim), lambda j, i: (i, 0)),
                  pl.BlockSpec((None, pl.Element(tn), pl.Element(kdim)),
                               lambda j, i: (layer, pl.multiple_of(col0 + j * tn, 8), 0))],
        out_specs=pl.BlockSpec((tm, tn), lambda j, i: (i, j)),
        out_shape=jax.ShapeDtypeStruct((rows, n), F32),
        scratch_shapes=[pltpu.VMEM((tn, kdim), BF16)],
        compiler_params=_cparams(2),
        name="gate_proj",
    )(x, w_t)


def _silu(z):
    return z / (1.0 + jnp.exp(-z))


def _sigmoid(z):
    return 1.0 / (1.0 + jnp.exp(-z))


def _mixer_a_kernel(sink_ref, q_ref, kc_ref, vc_ref, kp_ref, vp_ref, km_ref, vm_ref, za_ref,
                    bias_ref, o_ref, s_scr, p_scr, m_scr, d_scr, msk_scr):
    j = pl.program_id(1)
    nk = 3 * QB_A
    q = (q_ref[...] * (HEAD_DIM ** -0.5 * LOG2E)).astype(BF16)
    kcat = jnp.concatenate([kp_ref[...], kc_ref[...], km_ref[...]], axis=0)
    vcat = jnp.concatenate([vp_ref[...], vc_ref[...], vm_ref[...]], axis=0)
    lane = lax.broadcasted_iota(jnp.int32, (nk, LANES), 1)
    lo = lane < HEAD_DIM
    kswap = pltpu.roll(kcat, HEAD_DIM, 1)
    vswap = pltpu.roll(vcat, HEAD_DIM, 1)
    zero = jnp.zeros_like(kcat)
    k_half = ((jnp.where(lo, kcat, zero).astype(BF16), jnp.where(lo, zero, kswap).astype(BF16)),
              (jnp.where(lo, kswap, zero).astype(BF16), jnp.where(lo, zero, kcat).astype(BF16)))
    v_half = ((jnp.where(lo, vcat, zero).astype(BF16), jnp.where(lo, zero, vswap).astype(BF16)),
              (jnp.where(lo, vswap, zero).astype(BF16), jnp.where(lo, zero, vcat).astype(BF16)))

    r = lax.broadcasted_iota(jnp.int32, (QB_A, nk), 0)
    c = lax.broadcasted_iota(jnp.int32, (QB_A, nk), 1)
    jv = jnp.zeros((QB_A, nk), jnp.int32) + j
    in_prev = (c < QB_A) & (jv >= 1) & ((c > r) | ((jv == 1) & (c < N_META)))
    in_cur = (c >= QB_A) & (c < 2 * QB_A) & (c - QB_A <= r)
    in_meta = (c >= 2 * QB_A) & (jv >= 2) & (c - 2 * QB_A < N_META)
    msk_scr[...] = jnp.where(in_prev | in_cur | in_meta, 0.0, NEG)
    half = lax.broadcasted_iota(jnp.int32, (QB_A, LANES), 1) < HEAD_DIM
    pairs = A_HEADS // 4

    def logits(g, i, par, r0):
        hd = 2 * (g * pairs + i) + par
        s = s_scr[i * QB_A + r0:i * QB_A + r0 + SUB_A, par * nk:(par + 1) * nk]
        return s + bias_ref[hd, r0:r0 + SUB_A, :] + msk_scr[r0:r0 + SUB_A, :], sink_ref[hd] * LOG2E

    for g in range(2):
        q4 = jnp.concatenate([q[:, (g * pairs + i) * LANES:(g * pairs + i + 1) * LANES]
                              for i in range(pairs)], axis=0)
        s_scr[...] = lax.dot_general(q4, jnp.concatenate(k_half[g], axis=0), _NT,
                                     preferred_element_type=F32)
        for i in range(pairs):
            for par in range(2):
                for r0 in range(0, QB_A, SUB_A):
                    lg, sink = logits(g, i, par, r0)
                    m = jnp.maximum(jnp.max(lg, axis=-1, keepdims=True), sink)
                    m_scr[par, i * QB_A + r0:i * QB_A + r0 + SUB_A] = jnp.broadcast_to(m, (SUB_A, LANES))
        for i in range(pairs):
            for par in range(2):
                for r0 in range(0, QB_A, SUB_A):
                    sl = slice(i * QB_A + r0, i * QB_A + r0 + SUB_A)
                    lg, sink = logits(g, i, par, r0)
                    m = m_scr[par, sl]
                    p = jnp.exp2(lg - jnp.concatenate([m] * 3, axis=1))
                    d_scr[par, sl] = jnp.broadcast_to(
                        jnp.sum(p, axis=-1, keepdims=True), (SUB_A, LANES)) + jnp.exp2(sink - m)
                    p_scr[sl, par * nk:(par + 1) * nk] = p.astype(BF16)
        pv = jnp.dot(p_scr[...], jnp.concatenate(v_half[g], axis=0), preferred_element_type=F32)
        for i in range(pairs):
            sl = slice(i * QB_A, (i + 1) * QB_A)
            col = slice((g * pairs + i) * LANES, (g * pairs + i + 1) * LANES)
            o2 = pv[sl] / jnp.where(half, d_scr[0, sl], d_scr[1, sl])
            o_ref[:, col] = (o2 * _silu(za_ref[:, col])).astype(o_ref.dtype)


def _mixer_a(proj, sinks, bias_a, n_batch, lp):
    nb = lp // QB_A
    rows = n_batch * lp
    ck, cv = C_KA // LANES, C_VA // LANES
    cur = lambda b, j: b * nb + j
    prev = lambda b, j: b * nb + jnp.maximum(j - 1, 0)
    first = lambda b, j: b * nb
    return pl.pallas_call(
        _mixer_a_kernel,
        grid=(n_batch, nb),
        in_specs=[
            pl.BlockSpec(memory_space=pltpu.SMEM),
            pl.BlockSpec((QB_A, A_WIDTH), lambda b, j: (cur(b, j), C_QA // A_WIDTH)),
            pl.BlockSpec((QB_A, LANES), lambda b, j: (cur(b, j), ck)),
            pl.BlockSpec((QB_A, LANES), lambda b, j: (cur(b, j), cv)),
            pl.BlockSpec((QB_A, LANES), lambda b, j: (prev(b, j), ck)),
            pl.BlockSpec((QB_A, LANES), lambda b, j: (prev(b, j), cv)),
            pl.BlockSpec((QB_A, LANES), lambda b, j: (first(b, j), ck)),
            pl.BlockSpec((QB_A, LANES), lambda b, j: (first(b, j), cv)),
            pl.BlockSpec((QB_A, A_WIDTH), lambda b, j: (cur(b, j), C_ZA // A_WIDTH)),
            pl.BlockSpec((A_HEADS, QB_A, 3 * LANES), lambda b, j: (0, 0, 0)),
        ],
        out_specs=pl.BlockSpec((QB_A, A_WIDTH), lambda b, j: (cur(b, j), 0)),
        out_shape=jax.ShapeDtypeStruct((rows, A_WIDTH), BF16),
        scratch_shapes=[
            pltpu.VMEM((A_HEADS // 4 * QB_A, 6 * QB_A), F32),
            pltpu.VMEM((A_HEADS // 4 * QB_A, 6 * QB_A), BF16),
            pltpu.VMEM((2, A_HEADS // 4 * QB_A, LANES), F32),
            pltpu.VMEM((2, A_HEADS // 4 * QB_A, LANES), F32),
            pltpu.VMEM((QB_A, 3 * QB_A), F32),
        ],
        compiler_params=_cparams(2),
        name="mixer_a",
    )(sinks, proj, proj, proj, proj, proj, proj, proj, proj, bias_a)


def _bprep_kernel(cq_ref, ckv_ref, u_ref, qg_ref, kvg_ref, wqb_ref, wiq_ref, wuk_ref, wsm_ref,
                  ql_ref, qi_ref, wi_ref, ckvn_ref, klo_ref, khi_ref, kn2_ref):
    cqn = _rms(cq_ref[...], qg_ref[...]).astype(BF16)
    ckvn = _rms(ckv_ref[...], kvg_ref[...]).astype(ckvn_ref.dtype)
    ckvn_ref[0] = ckvn
    kn2 = jnp.sum(jnp.square(ckvn.astype(F32)), axis=-1, keepdims=True)
    kn2_ref[0, 0] = jnp.broadcast_to(jnp.max(kn2, axis=0, keepdims=True), kn2_ref.shape[2:])
    qb = jnp.dot(cqn, wqb_ref[...], preferred_element_type=F32).astype(BF16)
    for pair in range(B_HEADS // 2):
        ql2 = jnp.dot(qb[:, pair * LANES:(pair + 1) * LANES], wuk_ref[pair],
                      preferred_element_type=F32)
        ql_ref[0, 2 * pair] = ql2[:, :B_KV_RANK].astype(ql_ref.dtype)
        ql_ref[0, 2 * pair + 1] = ql2[:, B_KV_RANK:].astype(ql_ref.dtype)
    qi = jnp.dot(cqn, wiq_ref[...], preferred_element_type=F32)
    for pair in range(IDX_HEADS // 2):
        qi_ref[0, pair] = qi[:, pair * LANES:(pair + 1) * LANES].astype(qi_ref.dtype)
    small = jnp.dot(u_ref[...], wsm_ref[...], preferred_element_type=F32)
    klo_ref[0] = small[:, :LANES].astype(klo_ref.dtype)
    khi_ref[0] = small[:, LANES:2 * LANES].astype(khi_ref.dtype)
    wi_ref[0] = small[:, 2 * LANES:] * (IDX_HEADS ** -0.5)


def _bprep(proj, u, qg, kvg, wqb, wiq, wuk_bd, wsmall, layer, n_batch, lp, tm=QB_B):
    nt = lp // tm
    row = lambda b, i: b * nt + i
    const2 = lambda b, i: (0, 0)
    lay2 = lambda b, i: (layer, 0, 0)
    return pl.pallas_call(
        _bprep_kernel,
        grid=(n_batch, nt),
        in_specs=[
            pl.BlockSpec((tm, B_Q_RANK), lambda b, i: (row(b, i), C_CQ // B_Q_RANK)),
            pl.BlockSpec((tm, B_KV_RANK), lambda b, i: (row(b, i), C_CKV // B_KV_RANK)),
            pl.BlockSpec((tm, D_MODEL), lambda b, i: (row(b, i), 0)),
            pl.BlockSpec((1, B_Q_RANK), const2),
            pl.BlockSpec((1, B_KV_RANK), const2),
            pl.BlockSpec((None, B_Q_RANK, B_WIDTH), lay2),
            pl.BlockSpec((None, B_Q_RANK, IDX_HEADS * IDX_DIM), lay2),
            pl.BlockSpec((None, B_HEADS // 2, LANES, 2 * B_KV_RANK), lambda b, i: (layer, 0, 0, 0)),
            pl.BlockSpec((None, D_MODEL, 3 * LANES), lay2),
        ],
        out_specs=[
            pl.BlockSpec((1, B_HEADS, tm, B_KV_RANK), lambda b, i: (b, 0, i, 0)),
            pl.BlockSpec((1, IDX_HEADS // 2, tm, LANES), lambda b, i: (b, 0, i, 0)),
            pl.BlockSpec((1, tm, LANES), lambda b, i: (b, i, 0)),
            pl.BlockSpec((1, tm, B_KV_RANK), lambda b, i: (b, i, 0)),
            pl.BlockSpec((1, tm, LANES), lambda b, i: (b, i, 0)),
            pl.BlockSpec((1, tm, LANES), lambda b, i: (b, i, 0)),
            pl.BlockSpec((1, 1, 8, LANES), lambda b, i: (b, i, 0, 0)),
        ],
        out_shape=[
            jax.ShapeDtypeStruct((n_batch, B_HEADS, lp, B_KV_RANK), BF16),
            jax.ShapeDtypeStruct((n_batch, IDX_HEADS // 2, lp, LANES), BF16),
            jax.ShapeDtypeStruct((n_batch, lp, LANES), F32),
            jax.ShapeDtypeStruct((n_batch, lp, B_KV_RANK), BF16),
            jax.ShapeDtypeStruct((n_batch, lp, LANES), BF16),
            jax.ShapeDtypeStruct((n_batch, lp, LANES), BF16),
            jax.ShapeDtypeStruct((n_batch, nt, 8, LANES), F32),
        ],
        compiler_params=_cparams(2),
        name="bprep",
    )(proj, proj, u, qg, kvg, wqb, wiq, wuk_bd, wsmall)


def _mixer_b_kernel(bmax_ref, ql_ref, qi_ref, wi_ref, ckv_ref, klo_ref, khi_ref, kn2_ref, zb_ref, bias_ref,
                    wuv_ref, o_ref, flag_ref, key_scr, s_scr, acc_scr, m_scr, l_scr, alpha_scr, p_scr,
                    *, bound, tail_q):
    jq = pl.program_id(1)
    nch = jq + 1
    qb, kc = QB_B, QB_B
    rows = B_HEADS * qb
    kf = float(TOPK_MAX)

    qi = qi_ref[0].reshape(IDX_HEADS // 2 * qb, LANES)
    wit = wi_ref[0].T
    krow = lax.broadcasted_iota(jnp.int32, (qb, kc), 0)
    krow_sub = lax.broadcasted_iota(jnp.int32, (SUB_B, kc), 0)
    qpos = jq * qb + lax.broadcasted_iota(jnp.int32, (qb, kc), 1)

    def idx_body(i, carry):
        c0 = jnp.minimum(2 * i, n_chunks_total - 2)
        start = pl.multiple_of(c0 * kc, kc)
        sc = (lax.dot_general(klo_ref[0, pl.ds(start, 2 * kc), :], qi, _NT, preferred_element_type=F32),
              lax.dot_general(khi_ref[0, pl.ds(start, 2 * kc), :], qi, _NT, preferred_element_type=F32))
        for half in range(2):
            rs = slice(half * kc, (half + 1) * kc)
            isc = jnp.zeros((kc, qb), F32)
            for pair in range(IDX_HEADS // 2):
                for par in range(2):
                    hd = 2 * pair + par
                    isc = isc + wit[hd:hd + 1, :] * jnp.maximum(sc[par][rs, pair * qb:(pair + 1) * qb], 0.0)
            key_scr[c0 + half] = jnp.where((c0 + half) * kc + krow <= qpos, isc, -jnp.inf)
        return carry

    n_chunks_total = key_scr.shape[0]
    lax.fori_loop(0, (nch + 1) // 2, idx_body, 0)

    def reduce_chunks(tile_fn, init, combine):
        n_acc = SUB_B // 8

        def body(c, accs):
            accs = list(accs)
            for r0 in range(0, kc, SUB_B):
                t = tile_fn(c, r0, key_scr[c, r0:r0 + SUB_B, :])
                t = t.reshape(n_acc, 8, qb)
                for i in range(n_acc):
                    accs[i] = combine(accs[i], t[i])
            return tuple(accs)

        accs = lax.fori_loop(0, nch, body, (init,) * n_acc)
        acc = accs[0]
        for a in accs[1:]:
            acc = combine(acc, a)
        return acc

    def count(pred_fn):
        acc = reduce_chunks(lambda c, r0, k: jnp.where(pred_fn(c, r0, k), 1.0, 0.0),
                            jnp.zeros((8, qb), F32), lambda a, b: a + b)
        return jnp.sum(acc, axis=0, keepdims=True)

    def unkey(k):
        return lax.bitcast_convert_type(k ^ ((k >> 31) & INT_MAX), F32)

    def tokey(v):
        b = lax.bitcast_convert_type(v, jnp.int32)
        return b ^ ((b >> 31) & INT_MAX)

    def select_all():
        return jnp.full((1, qb), -jnp.inf, F32), jnp.full((1, qb), -1, jnp.int32)

    def select_topk():
        kmax = reduce_chunks(lambda c, r0, k: k, jnp.full((8, qb), -jnp.inf, F32), jnp.maximum)
        kmax = jnp.max(kmax, axis=0, keepdims=True)
        kmin = reduce_chunks(lambda c, r0, k: jnp.where(k == -jnp.inf, jnp.inf, k),
                             jnp.full((8, qb), jnp.inf, F32), jnp.minimum)
        kmin = jnp.min(kmin, axis=0, keepdims=True)
        c0 = count(lambda c, r0, k: k >= 0.0)
        c0p = count(lambda c, r0, k: k >= F32_TINY)
        n_valid = (jq * qb + 1 + lax.broadcasted_iota(jnp.int32, (1, qb), 1)).astype(F32)
        pos = c0p >= kf
        zero = (c0 >= kf) & (c0p < kf)
        lo0 = jnp.where(pos, F32_TINY, jnp.where(zero, 0.0, kmin))
        hi0 = jnp.where(pos, unkey(tokey(kmax) + 1), jnp.where(zero, F32_TINY, 0.0))
        c_lo0 = jnp.where(pos, c0p, jnp.where(zero, c0, n_valid))
        c_hi0 = jnp.where(pos, 0.0, jnp.where(zero, c0p, c0))

        def bis_cond(st):
            return (st[0] < 64) & (st[5] > 0)

        def bis_step(it, lo, hi, c_lo, c_hi):
            klo, khi = tokey(lo), tokey(hi)
            mid_i = unkey((klo >> 1) + (khi >> 1) + (klo & khi & 1))
            mid_v = 0.5 * lo + 0.5 * hi
            use_v = (it < 24) & (mid_v > lo) & (mid_v < hi)
            mid = jnp.where(use_v, mid_v, mid_i)
            mid = jnp.where((mid > -F32_TINY) & (mid < 0.0), -F32_TINY, mid)
            cnt = count(lambda c, r0, k: k >= mid)
            ge = cnt >= kf
            return (jnp.where(ge, mid, lo), jnp.where(ge, hi, mid),
                    jnp.where(ge, cnt, c_lo), jnp.where(ge, c_hi, cnt))

        def bis_body(st):
            it, lo, hi, c_lo, c_hi, _ = st
            for _ in range(BIS_UNROLL):
                lo, hi, c_lo, c_hi = bis_step(it, lo, hi, c_lo, c_hi)
                it = it + 1
            return it, lo, hi, c_lo, c_hi, jnp.max(jnp.where(is_active(lo, hi, c_lo, c_hi), 1, 0))

        def is_active(lo, hi, c_lo, c_hi):
            gap = (lo == 0.0) | ((hi == 0.0) & (lo >= -F32_TINY))
            return (tokey(lo) + 1 < tokey(hi)) & (c_lo != kf) & (c_lo - c_hi > 2.0) & jnp.logical_not(gap)

        st0 = (jnp.int32(0), lo0, hi0, c_lo0, c_hi0,
               jnp.max(jnp.where(is_active(lo0, hi0, c_lo0, c_hi0), 1, 0)))
        _, lo, hi, c_lo, c_hi, _ = lax.while_loop(bis_cond, bis_body, st0)
        open_rows = c_lo != kf

        def finish():
            top = reduce_chunks(lambda c, r0, k: jnp.where(k < hi, k, -jnp.inf),
                                jnp.full((8, qb), -jnp.inf, F32), jnp.maximum)
            top = jnp.max(top, axis=0, keepdims=True)
            bot = reduce_chunks(lambda c, r0, k: jnp.where(k >= lo, k, jnp.inf),
                                jnp.full((8, qb), jnp.inf, F32), jnp.minimum)
            bot = jnp.min(bot, axis=0, keepdims=True)
            thr = jnp.where(open_rows, top, lo)
            tie = open_rows & (top == bot)
            need = kf - c_hi

            def tie_break():
                def tie_body(_, st):
                    jlo, jhi = st
                    mid = (jlo + jhi) >> 1
                    cnt = count(lambda c, r0, k: (k == thr) & (c * kc + r0 + krow_sub <= mid))
                    ge = cnt >= need
                    return jnp.where(ge, jlo, mid), jnp.where(ge, mid, jhi)

                _, jhi = lax.fori_loop(0, 13, tie_body, (jnp.full((1, qb), -1, jnp.int32),
                                                         jnp.full((1, qb), 8191, jnp.int32)))
                return jnp.where(tie, jhi, INT_MAX)

            jmax = lax.cond(jnp.max(jnp.where(tie, 1, 0)) > 0, tie_break,
                            lambda: jnp.full((1, qb), INT_MAX, jnp.int32))
            return thr, jmax

        return lax.cond(jnp.max(jnp.where(open_rows, 1, 0)) > 0, finish,
                        lambda: (lo, jnp.full((1, qb), INT_MAX, jnp.int32)))

    thr, jmax = lax.cond(jq >= 1, select_topk, select_all)

    def sel_body(c, carry):
        k = key_scr[c]
        sel = (k > thr) | ((k == thr) & (c * kc + krow <= jmax))
        key_scr[c] = jnp.where(sel, 0.0, NEG).T
        return carry

    lax.fori_loop(0, nch, sel_body, 0)

    def kv_chunk(c):
        return ckv_ref[0, pl.ds(pl.multiple_of(c * kc, kc), kc), :]

    def qk(nq, c, slot):
        s_scr[slot, :B_HEADS * nq] = lax.dot_general(
            ql_ref[0, :, :nq, :].reshape(B_HEADS * nq, B_KV_RANK), kv_chunk(c), _NT,
            preferred_element_type=F32)

    def logits(nq, c, slot, near, hd, r0):
        sub = min(SUB_B, nq)
        sh = s_scr[slot, hd * nq + r0:hd * nq + r0 + sub]
        if near == "cur":
            t0 = bias_ref[hd, r0 % QB_A:r0 % QB_A + sub, LANES:]
            t1 = bias_ref[hd, r0 % QB_A:r0 % QB_A + sub, :LANES]
            sh = sh + jnp.concatenate([t0 if r0 < QB_A else t1, t0], axis=1)
        elif near == "prev" and r0 < QB_A:
            t1 = bias_ref[hd, r0:r0 + sub, :LANES]
            sh = sh + jnp.concatenate([jnp.zeros_like(t1), t1], axis=1)
        return sh + key_scr[c, r0:r0 + sub, :]

    def sub_tiles(nq):
        sub = min(SUB_B, nq)
        for hd in range(B_HEADS):
            for r0 in range(0, nq, sub):
                yield hd, r0, slice(hd * nq + r0, hd * nq + r0 + sub)

    def sm(nq, c, slot, near):
        for hd, r0, sl in sub_tiles(nq):
            m = m_scr[sl]
            p = jnp.exp2(logits(nq, c, slot, near, hd, r0) - jnp.concatenate([m, m], axis=1))
            l_scr[sl] += p[:, :LANES] + p[:, LANES:]
            p_scr[slot, sl] = p.astype(BF16)

    def pv(nq, c, slot):
        o = jnp.dot(p_scr[slot, :B_HEADS * nq], kv_chunk(c), preferred_element_type=F32)
        sub = min(SUB_B, nq)
        for r0 in range(0, B_HEADS * nq, sub):
            acc_scr[r0:r0 + sub] += o[r0:r0 + sub]

    def bound_attention(nq):
        l_scr[...] = jnp.zeros(l_scr.shape, F32)
        acc_scr[...] = jnp.zeros(acc_scr.shape, F32)
        kn2 = lax.fori_loop(0, nch, lambda c, a: jnp.maximum(a, kn2_ref[0, c]), jnp.zeros((8, LANES), F32))
        bmax = bmax_ref[0]
        for hd, r0, sl in sub_tiles(nq):
            x = ql_ref[0, hd, r0:r0 + min(SUB_B, nq), :].astype(F32)
            qn2 = jnp.sum(x * x, axis=-1, keepdims=True)
            m_scr[sl] = jnp.sqrt(qn2 * kn2[:1]) * (1.0 + 2.0 ** -10) + bmax

        def stage(t, slot):
            qk(nq, t, slot)
            pv(nq, t - 2, slot)
            sm(nq, t - 1, 1 - slot, None)

        def tail(slot):
            qk(nq, jq, slot)
            pv(nq, jq - 2, slot)
            sm(nq, jq - 1, 1 - slot, "prev")
            pv(nq, jq - 1, 1 - slot)
            sm(nq, jq, slot, "cur")
            pv(nq, jq, slot)

        @pl.when(jq >= 2)
        def _():
            qk(nq, 0, 0)
            qk(nq, 1, 1)
            sm(nq, 0, 0, None)

            def two_stages(i, carry):
                stage(2 + 2 * i, 0)
                stage(3 + 2 * i, 1)
                return carry

            lax.fori_loop(0, (jq - 2) // 2, two_stages, 0)

            @pl.when(jq % 2 == 0)
            def _():
                tail(0)

            @pl.when(jq % 2 == 1)
            def _():
                stage(jq - 1, 0)
                tail(1)

        @pl.when(jq < 2)
        def _():
            @pl.when(jq == 1)
            def _():
                qk(nq, 0, 0)
                sm(nq, 0, 0, "prev")
                pv(nq, 0, 0)

            qk(nq, jq, 1)
            sm(nq, jq, 1, "cur")
            pv(nq, jq, 1)

    def running_max_attention():
        l_scr[...] = jnp.zeros(l_scr.shape, F32)
        acc_scr[...] = jnp.zeros(acc_scr.shape, F32)
        m_scr[...] = jnp.full(m_scr.shape, -jnp.inf, F32)

        def attend(c, near):
            qk(qb, c, 0)
            for hd, r0, sl in sub_tiles(qb):
                m_old = m_scr[sl]
                m_new = jnp.maximum(m_old, jnp.max(logits(qb, c, 0, near, hd, r0), axis=-1, keepdims=True))
                m_scr[sl] = m_new
                alpha_scr[sl] = jnp.exp2(m_old - m_new)
            for hd, r0, sl in sub_tiles(qb):
                m_new = m_scr[sl]
                p = jnp.exp2(logits(qb, c, 0, near, hd, r0) - jnp.concatenate([m_new, m_new], axis=1))
                l_scr[sl] = alpha_scr[sl] * l_scr[sl] + (p[:, :LANES] + p[:, LANES:])
                p_scr[0, sl] = p.astype(BF16)
            o = jnp.dot(p_scr[0], kv_chunk(c), preferred_element_type=F32)
            for r0 in range(0, rows, SUB_B):
                sl = slice(r0, r0 + SUB_B)
                alpha = alpha_scr[sl]
                acc_scr[sl] = acc_scr[sl] * jnp.concatenate([alpha, alpha], axis=1) + o[sl]

        def far_body(c, carry):
            attend(c, None)
            return carry

        lax.fori_loop(0, jq - 1, far_body, 0)

        @pl.when(jq >= 1)
        def _():
            attend(jq - 1, "prev")

        attend(jq, "cur")

    def write_output(nq):
        l_min = jnp.full((nq, 1), jnp.inf, F32)
        for pair in range(B_HEADS // 2):
            col = slice(pair * LANES, (pair + 1) * LANES)
            y2 = jnp.zeros((nq, LANES), F32)
            for par in range(2):
                hd = 2 * pair + par
                sl = slice(hd * nq, (hd + 1) * nq)
                l = jnp.sum(l_scr[sl], axis=-1, keepdims=True)
                l_min = jnp.minimum(l_min, l)
                o_lat = acc_scr[sl] / l
                y2 = y2 + jnp.dot(o_lat.astype(BF16), wuv_ref[pair, par], preferred_element_type=F32)
            o_ref[:nq, col] = (y2 * _silu(zb_ref[:nq, col])).astype(o_ref.dtype)
            if nq < qb:
                o_ref[nq:, col] = jnp.zeros((qb - nq, LANES), o_ref.dtype)
        safe = jnp.min(l_min) >= L_SAFE
        flag_ref[0, 0] = jnp.full(flag_ref.shape[2:], jnp.where(safe, 0.0, 1.0), F32)

    if not bound:
        running_max_attention()
        write_output(qb)
    elif tail_q == qb:
        bound_attention(qb)
        write_output(qb)
    else:
        last = pl.num_programs(1) - 1

        @pl.when(jq < last)
        def _():
            bound_attention(qb)
            write_output(qb)

        @pl.when(jq == last)
        def _():
            bound_attention(tail_q)
            write_output(tail_q)


def _mixer_b(bmax, ql, qi, wi, ckvn, klo, khi, kn2, proj, bias_b, wuv_pairs, layer, n_batch, lp, length,
             bound):
    nb = lp // QB_B
    tail_q = min(QB_B, -(-(length - (nb - 1) * QB_B) // 16) * 16)
    rows = n_batch * lp
    whole = lambda b, j: (b, 0, 0)
    once = pl.Buffered(1)
    return pl.pallas_call(
        functools.partial(_mixer_b_kernel, bound=bound, tail_q=tail_q),
        grid=(n_batch, nb),
        in_specs=[
            pl.BlockSpec(memory_space=pltpu.SMEM),
            pl.BlockSpec((1, B_HEADS, QB_B, B_KV_RANK), lambda b, j: (b, 0, j, 0)),
            pl.BlockSpec((1, IDX_HEADS // 2, QB_B, LANES), lambda b, j: (b, 0, j, 0)),
            pl.BlockSpec((1, QB_B, LANES), lambda b, j: (b, j, 0)),
            pl.BlockSpec((1, lp, B_KV_RANK), whole, pipeline_mode=once),
            pl.BlockSpec((1, lp, LANES), whole, pipeline_mode=once),
            pl.BlockSpec((1, lp, LANES), whole, pipeline_mode=once),
            pl.BlockSpec((1, nb, 8, LANES), lambda b, j: (b, 0, 0, 0), pipeline_mode=once),
            pl.BlockSpec((QB_B, B_WIDTH), lambda b, j: (b * nb + j, C_ZB // B_WIDTH)),
            pl.BlockSpec((B_HEADS, QB_A, 2 * LANES), lambda b, j: (0, 0, 0), pipeline_mode=once),
            pl.BlockSpec((None, B_HEADS // 2, 2, B_KV_RANK, LANES), lambda b, j: (layer, 0, 0, 0, 0),
                         pipeline_mode=once),
        ],
        out_specs=[pl.BlockSpec((QB_B, B_WIDTH), lambda b, j: (b * nb + j, 0)),
                   pl.BlockSpec((1, 1, 8, LANES), lambda b, j: (b, j, 0, 0))],
        out_shape=[jax.ShapeDtypeStruct((rows, B_WIDTH), BF16),
                   jax.ShapeDtypeStruct((n_batch, nb, 8, LANES), F32)],
        scratch_shapes=[
            pltpu.VMEM((nb, QB_B, QB_B), F32),
            pltpu.VMEM((2, B_HEADS * QB_B, QB_B), F32),
            pltpu.VMEM((B_HEADS * QB_B, B_KV_RANK), F32),
            pltpu.VMEM((B_HEADS * QB_B, LANES), F32),
            pltpu.VMEM((B_HEADS * QB_B, LANES), F32),
            pltpu.VMEM((B_HEADS * QB_B, LANES), F32),
            pltpu.VMEM((2, B_HEADS * QB_B, QB_B), BF16),
        ],
        compiler_params=_cparams(2),
        name="mixer_b" if bound else "mixer_b_running_max",
    )(bmax, ql, qi, wi, ckvn, klo, khi, kn2, proj, bias_b, wuv_pairs)


def _merge_kernel(ya_ref, yb_ref, ga_ref, gb_ref, h_ref, wpa_ref, wpb_ref, wo_ref, g_ref, *out_refs,
                  last):
    pa = jnp.dot(ya_ref[...], wpa_ref[...], preferred_element_type=F32)
    pb = jnp.dot(yb_ref[...], wpb_ref[...], preferred_element_type=F32)
    merged = _sigmoid(ga_ref[...]) * pa + _sigmoid(gb_ref[...]) * pb
    h_new = h_ref[...] + jnp.dot(merged.astype(BF16), wo_ref[...], preferred_element_type=F32)
    normed = _rms(h_new, g_ref[...])
    if last:
        out_refs[0][...] = normed
    else:
        out_refs[0][...] = h_new
        out_refs[1][...] = normed.astype(BF16)


def _merge(ya, yb, gates, h, wpa, wpb, wo, g_next, layer, last, tm=256):
    rows, d = h.shape
    const = lambda i: (0, 0)
    lay = lambda i: (layer, 0, 0)
    once = pl.Buffered(1)
    if last:
        out_specs = [pl.BlockSpec((tm, d), lambda i: (i, 0))]
        out_shape = [jax.ShapeDtypeStruct((rows, d), F32)]
    else:
        out_specs = [pl.BlockSpec((tm, d), lambda i: (i, 0)), pl.BlockSpec((tm, d), lambda i: (i, 0))]
        out_shape = [jax.ShapeDtypeStruct((rows, d), F32), jax.ShapeDtypeStruct((rows, d), BF16)]
    return pl.pallas_call(
        functools.partial(_merge_kernel, last=last),
        grid=(rows // tm,),
        in_specs=[
            pl.BlockSpec((tm, A_WIDTH), lambda i: (i, 0)),
            pl.BlockSpec((tm, B_WIDTH), lambda i: (i, 0)),
            pl.BlockSpec((tm, d), lambda i: (i, 0)),
            pl.BlockSpec((tm, d), lambda i: (i, 1)),
            pl.BlockSpec((tm, d), lambda i: (i, 0)),
            pl.BlockSpec((None, A_WIDTH, d), lay, pipeline_mode=once),
            pl.BlockSpec((None, B_WIDTH, d), lay, pipeline_mode=once),
            pl.BlockSpec((None, d, d), lay, pipeline_mode=once),
            pl.BlockSpec((1, d), const),
        ],
        out_specs=out_specs,
        out_shape=out_shape,
        compiler_params=_cparams(1),
        name="merge_out",
    )(ya, yb, gates, gates, h, wpa, wpb, wo, g_next.reshape(1, d))


def _pack_weights(w_in, w_uk, w_uv, w_qb, w_iq, w_proj_a, w_proj_b, w_out):
    depth = w_in.shape[0]
    sec = lambda a, b: w_in[:, :, a:b]
    kidx, widx = sec(4096, 4160), sec(4160, 4168)
    z64 = jnp.zeros_like(kidx)
    zpad = jnp.zeros(widx.shape[:2] + (LANES - IDX_HEADS,), w_in.dtype)
    w_small = jnp.concatenate([kidx, z64, z64, kidx, widx, zpad], axis=-1).astype(BF16)
    uk = (w_uk * (HEAD_DIM ** -0.5 * LOG2E)).reshape(depth, B_HEADS // 2, 2, HEAD_DIM, B_KV_RANK)
    zuk = jnp.zeros_like(uk[:, :, 0])
    wuk_bd = jnp.concatenate([jnp.concatenate([uk[:, :, 0], zuk], axis=-1),
                              jnp.concatenate([zuk, uk[:, :, 1]], axis=-1)], axis=-2).astype(BF16)
    uv = w_uv.reshape(depth, B_HEADS // 2, 2, B_KV_RANK, HEAD_DIM)
    zuv = jnp.zeros_like(uv[:, :, 0])
    wuv_pairs = jnp.stack([jnp.concatenate([uv[:, :, 0], zuv], axis=-1),
                           jnp.concatenate([zuv, uv[:, :, 1]], axis=-1)], axis=2).astype(BF16)
    w_iq8 = (w_iq * (IDX_DIM ** -0.5)).astype(BF16)
    return (w_small, wuk_bd, wuv_pairs, w_qb.astype(BF16), w_iq8,
            w_proj_a.astype(BF16), w_proj_b.astype(BF16), w_out.astype(BF16))


def kernel(x, meta_tokens, bias_table, norm_g, w_in, q_norm_g, kv_norm_g, w_qb, w_iq, w_uk, w_uv,
           sinks, w_proj_a, w_proj_b, w_out, final_g):
    n_batch, seq, d = x.shape
    depth = w_in.shape[0]
    length = seq + N_META
    lp = -(-length // QB_B) * QB_B
    assert min(TOPK_MAX, seq // 4) == TOPK_MAX and d == D_MODEL
    assert WINDOW == QB_A and lp // QB_B >= 3

    (w_small, wuk_bd, wuv_pairs, wqb, wiq, wpa, wpb, wo) = _pack_weights(
        w_in, w_uk, w_uv, w_qb, w_iq, w_proj_a, w_proj_b, w_out)
    w_t = jnp.swapaxes(w_in, 1, 2)
    bias_a, bias_b = _bias_tiles(bias_table)
    bmax = jnp.max(jnp.abs(bias_b)).reshape(1)

    meta = jnp.broadcast_to(meta_tokens.astype(x.dtype)[None], (n_batch, N_META, d))
    pad = jnp.zeros((n_batch, lp - length, d), x.dtype)
    h = jnp.concatenate([meta, x, pad], axis=1).reshape(n_batch * lp, d)

    u = _rmsnorm(h, norm_g[0])
    out = None
    for l in range(depth):
        proj = _in_proj(u, w_t, l)
        gates = _gate_proj(u, w_t, l, GATE_COL0, 2 * D_MODEL)
        ya = _mixer_a(proj, sinks[l], bias_a, n_batch, lp)
        ql, qi, wi, ckvn, klo, khi, kn2 = _bprep(
            proj, u, q_norm_g[l].reshape(1, -1), kv_norm_g[l].reshape(1, -1),
            wqb, wiq, wuk_bd, w_small, l, n_batch, lp)
        b_args = (bmax, ql, qi, wi, ckvn, klo, khi, kn2, proj, bias_b, wuv_pairs, l, n_batch, lp, length)
        yb, flags = _mixer_b(*b_args, bound=True)
        yb = lax.cond(jnp.max(flags) > 0.0, lambda: _mixer_b(*b_args, bound=False)[0], lambda: yb)
        last = l == depth - 1
        g_next = final_g if last else norm_g[l + 1]
        res = _merge(ya, yb, gates, h, wpa, wpb, wo, g_next, l, last)
        if last:
            out = res[0]
        else:
            h, u = res
    return out.reshape(n_batch, lp, d)[:, N_META:length]
```

```python
import functools
import math

import numpy as np
import jax
import jax.numpy as jnp
from jax import lax
from jax.experimental import pallas as pl
from jax.experimental.pallas import tpu as pltpu

D_MODEL = 2048
N_META = 16
WINDOW = 128
HEAD_DIM = 64
A_HEADS = 16
A_WIDTH = 1024
B_HEADS = 16
B_WIDTH = 1024
B_Q_RANK = 512
B_KV_RANK = 256
IDX_HEADS = 8
IDX_DIM = 64
TOPK_MAX = 256
N_BUCKETS = 32
MAX_DISTANCE = 128
EPS = 1e-6
NEG = -1e30

LANES = 128
SUBLANES = 8
PROJ_ROW_TILES = 8
QB_A = 128
QB_B = 256
SUB_A = 32
SUB_B = 32
L_SAFE = 2.0 ** -100
BIS_UNROLL = 3
LOG2E = math.log2(math.e)
VMEM_LIMIT = 56 * 1024 * 1024
INT_MAX = 2 ** 31 - 1
F32_TINY = 2.0 ** -126

C_QA, C_ZA, C_ZB, C_CQ, C_CKV, C_KA, C_VA = (0, 1024, 2048, 3072, 3584, 3840, 3968)
GATE_COL0 = 4168

F32 = jnp.float32
BF16 = jnp.bfloat16
_NT = (((1,), (1,)), ((), ()))


def _cparams(n_grid):
    return pltpu.CompilerParams(
        dimension_semantics=("arbitrary",) * n_grid,
        vmem_limit_bytes=VMEM_LIMIT)


def _t5_bucket_np(d):
    max_exact = N_BUCKETS // 2
    nf = np.maximum(d, 1).astype(np.float32)
    large = max_exact + (np.log(nf / np.float32(max_exact)) / np.float32(math.log(MAX_DISTANCE / max_exact))
                         * np.float32(N_BUCKETS - max_exact)).astype(np.int32)
    large = np.minimum(large, N_BUCKETS - 1)
    return np.where(d < max_exact, d, large).astype(np.int32)


def _bias_tiles_kernel(tab_ref, idx_ref, out_a_ref, out_b_ref):
    h = pl.program_id(0)
    idx = idx_ref[...]
    acc_a = jnp.zeros(idx.shape, F32)
    acc_b = jnp.zeros(idx.shape, F32)
    far_b = tab_ref[N_BUCKETS - 1, A_HEADS + h]
    for b in range(N_BUCKETS):
        hit = idx == b
        acc_a = jnp.where(hit, tab_ref[b, h] * LOG2E, acc_a)
        acc_b = jnp.where(hit, (tab_ref[b, A_HEADS + h] - far_b) * LOG2E, acc_b)
    out_a_ref[0] = acc_a
    out_b_ref[0] = acc_b[:, :2 * LANES]


def _bias_tiles(bias_table):
    r = np.arange(QB_A)[:, None]
    k = np.arange(QB_A)[None, :]
    prev = _t5_bucket_np(np.maximum(QB_A + r - k, 0))
    cur = _t5_bucket_np(np.maximum(r - k, 0))
    far = np.full((QB_A, QB_A), N_BUCKETS - 1, np.int32)
    idx = jnp.asarray(np.concatenate([prev, cur, far], axis=1))
    return pl.pallas_call(
        _bias_tiles_kernel,
        grid=(A_HEADS,),
        in_specs=[pl.BlockSpec(memory_space=pltpu.SMEM),
                  pl.BlockSpec((QB_A, 3 * LANES), lambda h: (0, 0))],
        out_specs=[pl.BlockSpec((1, QB_A, 3 * LANES), lambda h: (h, 0, 0)),
                   pl.BlockSpec((1, QB_A, 2 * LANES), lambda h: (h, 0, 0))],
        out_shape=[jax.ShapeDtypeStruct((A_HEADS, QB_A, 3 * LANES), F32),
                   jax.ShapeDtypeStruct((B_HEADS, QB_A, 2 * LANES), F32)],
        compiler_params=_cparams(1),
        name="bias_tiles",
    )(bias_table, idx)


def _rms(x, g):
    return x * lax.rsqrt(jnp.mean(x * x, axis=-1, keepdims=True) + EPS) * g


def _rmsnorm_kernel(h_ref, g_ref, u_ref):
    u_ref[...] = _rms(h_ref[...], g_ref[...]).astype(u_ref.dtype)


def _rmsnorm(h, g, tm=512):
    rows, d = h.shape
    return pl.pallas_call(
        _rmsnorm_kernel,
        grid=(rows // tm,),
        in_specs=[pl.BlockSpec((tm, d), lambda i: (i, 0)),
                  pl.BlockSpec((1, d), lambda i: (0, 0))],
        out_specs=pl.BlockSpec((tm, d), lambda i: (i, 0)),
        out_shape=jax.ShapeDtypeStruct((rows, d), BF16),
        compiler_params=_cparams(1),
        name="rmsnorm",
    )(h, g.reshape(1, d))


_IN_BLOCKS = ((0, 1, 2, 3), (5, 6, 7, 8), (12, 13, 14, 15), (9, 10, 11, 4))
IN_BLK = 256


def _in_proj_kernel(x_ref, *refs):
    w_refs, o_ref, wbf_scr = refs[:-2], refs[-2], refs[-1]

    @pl.when(pl.program_id(1) == 0)
    def _():
        r0 = 0
        for w_ref in w_refs:
            wbf_scr[r0:r0 + w_ref.shape[0], :] = w_ref[...].astype(BF16)
            r0 += w_ref.shape[0]

    o_ref[...] = lax.dot_general(x_ref[...], wbf_scr[...], _NT, preferred_element_type=F32)


def _in_proj(x, w_t, layer):
    rows, kdim = x.shape
    tm = rows // PROJ_ROW_TILES
    tn = 4 * IN_BLK

    def w_spec(s):
        def index(j, i):
            blk = jnp.int32(_IN_BLOCKS[0][s])
            for t in range(1, len(_IN_BLOCKS)):
                blk = jnp.where(j == t, _IN_BLOCKS[t][s], blk)
            return layer, blk, 0
        return pl.BlockSpec((None, IN_BLK, kdim), index)

    return pl.pallas_call(
        _in_proj_kernel,
        grid=(len(_IN_BLOCKS), rows // tm),
        in_specs=[pl.BlockSpec((tm, kdim), lambda j, i: (i, 0))] + [w_spec(s) for s in range(4)],
        out_specs=pl.BlockSpec((tm, tn), lambda j, i: (i, j)),
        out_shape=jax.ShapeDtypeStruct((rows, len(_IN_BLOCKS) * tn), F32),
        scratch_shapes=[pltpu.VMEM((tn, kdim), BF16)],
        compiler_params=_cparams(2),
        name="in_proj",
    )(x, w_t, w_t, w_t, w_t)


def _gate_proj(x, w_t, layer, col0, n, tn=1024):
    rows, kdim = x.shape
    tm = rows // PROJ_ROW_TILES
    return pl.pallas_call(
        _in_proj_kernel,
        grid=(n // tn, rows // tm),
        in_specs=[pl.BlockSpec((tm, kdim), lambda j, i: (i, 0)),
                  pl.BlockSpec((None, pl.Element(tn), pl.Element(kdim)),
                               lambda j, i: (layer, pl.multiple_of(col0 + j * tn, 8), 0))],
        out_specs=pl.BlockSpec((tm, tn), lambda j, i: (i, j)),
        out_shape=jax.ShapeDtypeStruct((rows, n), F32),
        scratch_shapes=[pltpu.VMEM((tn, kdim), BF16)],
        compiler_params=_cparams(2),
        name="gate_proj",
    )(x, w_t)


def _silu(z):
    return z / (1.0 + jnp.exp(-z))


def _sigmoid(z):
    return 1.0 / (1.0 + jnp.exp(-z))


def _mixer_a_kernel(sink_ref, q_ref, kc_ref, vc_ref, kp_ref, vp_ref, km_ref, vm_ref, za_ref,
                    bias_ref, o_ref, s_scr, p_scr, m_scr, d_scr, msk_scr):
    j = pl.program_id(1)
    nk = 3 * QB_A
    q = (q_ref[...] * (HEAD_DIM ** -0.5 * LOG2E)).astype(BF16)
    kcat = jnp.concatenate([kp_ref[...], kc_ref[...], km_ref[...]], axis=0)
    vcat = jnp.concatenate([vp_ref[...], vc_ref[...], vm_ref[...]], axis=0)
    lane = lax.broadcasted_iota(jnp.int32, (nk, LANES), 1)
    lo = lane < HEAD_DIM
    kswap = pltpu.roll(kcat, HEAD_DIM, 1)
    vswap = pltpu.roll(vcat, HEAD_DIM, 1)
    zero = jnp.zeros_like(kcat)
    k_half = ((jnp.where(lo, kcat, zero).astype(BF16), jnp.where(lo, zero, kswap).astype(BF16)),
              (jnp.where(lo, kswap, zero).astype(BF16), jnp.where(lo, zero, kcat).astype(BF16)))
    v_half = ((jnp.where(lo, vcat, zero).astype(BF16), jnp.where(lo, zero, vswap).astype(BF16)),
              (jnp.where(lo, vswap, zero).astype(BF16), jnp.where(lo, zero, vcat).astype(BF16)))

    r = lax.broadcasted_iota(jnp.int32, (QB_A, nk), 0)
    c = lax.broadcasted_iota(jnp.int32, (QB_A, nk), 1)
    jv = jnp.zeros((QB_A, nk), jnp.int32) + j
    in_prev = (c < QB_A) & (jv >= 1) & ((c > r) | ((jv == 1) & (c < N_META)))
    in_cur = (c >= QB_A) & (c < 2 * QB_A) & (c - QB_A <= r)
    in_meta = (c >= 2 * QB_A) & (jv >= 2) & (c - 2 * QB_A < N_META)
    msk_scr[...] = jnp.where(in_prev | in_cur | in_meta, 0.0, NEG)
    half = lax.broadcasted_iota(jnp.int32, (QB_A, LANES), 1) < HEAD_DIM
    pairs = A_HEADS // 4

    def logits(g, i, par, r0):
        hd = 2 * (g * pairs + i) + par
        s = s_scr[i * QB_A + r0:i * QB_A + r0 + SUB_A, par * nk:(par + 1) * nk]
        return s + bias_ref[hd, r0:r0 + SUB_A, :] + msk_scr[r0:r0 + SUB_A, :], sink_ref[hd] * LOG2E

    for g in range(2):
        q4 = jnp.concatenate([q[:, (g * pairs + i) * LANES:(g * pairs + i + 1) * LANES]
                              for i in range(pairs)], axis=0)
        s_scr[...] = lax.dot_general(q4, jnp.concatenate(k_half[g], axis=0), _NT,
                                     preferred_element_type=F32)
        for i in range(pairs):
            for par in range(2):
                for r0 in range(0, QB_A, SUB_A):
                    lg, sink = logits(g, i, par, r0)
                    m = jnp.maximum(jnp.max(lg, axis=-1, keepdims=True), sink)
                    m_scr[par, i * QB_A + r0:i * QB_A + r0 + SUB_A] = jnp.broadcast_to(m, (SUB_A, LANES))
        for i in range(pairs):
            for par in range(2):
                for r0 in range(0, QB_A, SUB_A):
                    sl = slice(i * QB_A + r0, i * QB_A + r0 + SUB_A)
                    lg, sink = logits(g, i, par, r0)
                    m = m_scr[par, sl]
                    p = jnp.exp2(lg - jnp.concatenate([m] * 3, axis=1))
                    d_scr[par, sl] = jnp.broadcast_to(
                        jnp.sum(p, axis=-1, keepdims=True), (SUB_A, LANES)) + jnp.exp2(sink - m)
                    p_scr[sl, par * nk:(par + 1) * nk] = p.astype(BF16)
        pv = jnp.dot(p_scr[...], jnp.concatenate(v_half[g], axis=0), preferred_element_type=F32)
        for i in range(pairs):
            sl = slice(i * QB_A, (i + 1) * QB_A)
            col = slice((g * pairs + i) * LANES, (g * pairs + i + 1) * LANES)
            o2 = pv[sl] / jnp.where(half, d_scr[0, sl], d_scr[1, sl])
            o_ref[:, col] = (o2 * _silu(za_ref[:, col])).astype(o_ref.dtype)


def _mixer_a(proj, sinks, bias_a, n_batch, lp):
    nb = lp // QB_A
    rows = n_batch * lp
    ck, cv = C_KA // LANES, C_VA // LANES
    cur = lambda b, j: b * nb + j
    prev = lambda b, j: b * nb + jnp.maximum(j - 1, 0)
    first = lambda b, j: b * nb
    return pl.pallas_call(
        _mixer_a_kernel,
        grid=(n_batch, nb),
        in_specs=[
            pl.BlockSpec(memory_space=pltpu.SMEM),
            pl.BlockSpec((QB_A, A_WIDTH), lambda b, j: (cur(b, j), C_QA // A_WIDTH)),
            pl.BlockSpec((QB_A, LANES), lambda b, j: (cur(b, j), ck)),
            pl.BlockSpec((QB_A, LANES), lambda b, j: (cur(b, j), cv)),
            pl.BlockSpec((QB_A, LANES), lambda b, j: (prev(b, j), ck)),
            pl.BlockSpec((QB_A, LANES), lambda b, j: (prev(b, j), cv)),
            pl.BlockSpec((QB_A, LANES), lambda b, j: (first(b, j), ck)),
            pl.BlockSpec((QB_A, LANES), lambda b, j: (first(b, j), cv)),
            pl.BlockSpec((QB_A, A_WIDTH), lambda b, j: (cur(b, j), C_ZA // A_WIDTH)),
            pl.BlockSpec((A_HEADS, QB_A, 3 * LANES), lambda b, j: (0, 0, 0)),
        ],
        out_specs=pl.BlockSpec((QB_A, A_WIDTH), lambda b, j: (cur(b, j), 0)),
        out_shape=jax.ShapeDtypeStruct((rows, A_WIDTH), BF16),
        scratch_shapes=[
            pltpu.VMEM((A_HEADS // 4 * QB_A, 6 * QB_A), F32),
            pltpu.VMEM((A_HEADS // 4 * QB_A, 6 * QB_A), BF16),
            pltpu.VMEM((2, A_HEADS // 4 * QB_A, LANES), F32),
            pltpu.VMEM((2, A_HEADS // 4 * QB_A, LANES), F32),
            pltpu.VMEM((QB_A, 3 * QB_A), F32),
        ],
        compiler_params=_cparams(2),
        name="mixer_a",
    )(sinks, proj, proj, proj, proj, proj, proj, proj, proj, bias_a)


def _bprep_kernel(cq_ref, ckv_ref, u_ref, qg_ref, kvg_ref, wqb_ref, wiq_ref, wuk_ref, wsm_ref,
                  ql_ref, qi_ref, wi_ref, ckvn_ref, klo_ref, khi_ref, kn2_ref):
    cqn = _rms(cq_ref[...], qg_ref[...]).astype(BF16)
    ckvn = _rms(ckv_ref[...], kvg_ref[...]).astype(ckvn_ref.dtype)
    ckvn_ref[0] = ckvn
    kn2 = jnp.sum(jnp.square(ckvn.astype(F32)), axis=-1, keepdims=True)
    kn2_ref[0, 0] = jnp.broadcast_to(jnp.max(kn2, axis=0, keepdims=True), kn2_ref.shape[2:])
    qb = jnp.dot(cqn, wqb_ref[...], preferred_element_type=F32).astype(BF16)
    for pair in range(B_HEADS // 2):
        ql2 = jnp.dot(qb[:, pair * LANES:(pair + 1) * LANES], wuk_ref[pair],
                      preferred_element_type=F32)
        ql_ref[0, 2 * pair] = ql2[:, :B_KV_RANK].astype(ql_ref.dtype)
        ql_ref[0, 2 * pair + 1] = ql2[:, B_KV_RANK:].astype(ql_ref.dtype)
    qi = jnp.dot(cqn, wiq_ref[...], preferred_element_type=F32)
    for pair in range(IDX_HEADS // 2):
        qi_ref[0, pair] = qi[:, pair * LANES:(pair + 1) * LANES].astype(qi_ref.dtype)
    small = jnp.dot(u_ref[...], wsm_ref[...], preferred_element_type=F32)
    klo_ref[0] = small[:, :LANES].astype(klo_ref.dtype)
    khi_ref[0] = small[:, LANES:2 * LANES].astype(khi_ref.dtype)
    wi_ref[0] = small[:, 2 * LANES:] * (IDX_HEADS ** -0.5)


def _bprep(proj, u, qg, kvg, wqb, wiq, wuk_bd, wsmall, layer, n_batch, lp, tm=QB_B):
    nt = lp // tm
    row = lambda b, i: b * nt + i
    const2 = lambda b, i: (0, 0)
    lay2 = lambda b, i: (layer, 0, 0)
    return pl.pallas_call(
        _bprep_kernel,
        grid=(n_batch, nt),
        in_specs=[
            pl.BlockSpec((tm, B_Q_RANK), lambda b, i: (row(b, i), C_CQ // B_Q_RANK)),
            pl.BlockSpec((tm, B_KV_RANK), lambda b, i: (row(b, i), C_CKV // B_KV_RANK)),
            pl.BlockSpec((tm, D_MODEL), lambda b, i: (row(b, i), 0)),
            pl.BlockSpec((1, B_Q_RANK), const2),
            pl.BlockSpec((1, B_KV_RANK), const2),
            pl.BlockSpec((None, B_Q_RANK, B_WIDTH), lay2),
            pl.BlockSpec((None, B_Q_RANK, IDX_HEADS * IDX_DIM), lay2),
            pl.BlockSpec((None, B_HEADS // 2, LANES, 2 * B_KV_RANK), lambda b, i: (layer, 0, 0, 0)),
            pl.BlockSpec((None, D_MODEL, 3 * LANES), lay2),
        ],
        out_specs=[
            pl.BlockSpec((1, B_HEADS, tm, B_KV_RANK), lambda b, i: (b, 0, i, 0)),
            pl.BlockSpec((1, IDX_HEADS // 2, tm, LANES), lambda b, i: (b, 0, i, 0)),
            pl.BlockSpec((1, tm, LANES), lambda b, i: (b, i, 0)),
            pl.BlockSpec((1, tm, B_KV_RANK), lambda b, i: (b, i, 0)),
            pl.BlockSpec((1, tm, LANES), lambda b, i: (b, i, 0)),
            pl.BlockSpec((1, tm, LANES), lambda b, i: (b, i, 0)),
            pl.BlockSpec((1, 1, SUBLANES, LANES), lambda b, i: (b, i, 0, 0)),
        ],
        out_shape=[
            jax.ShapeDtypeStruct((n_batch, B_HEADS, lp, B_KV_RANK), BF16),
            jax.ShapeDtypeStruct((n_batch, IDX_HEADS // 2, lp, LANES), BF16),
            jax.ShapeDtypeStruct((n_batch, lp, LANES), F32),
            jax.ShapeDtypeStruct((n_batch, lp, B_KV_RANK), BF16),
            jax.ShapeDtypeStruct((n_batch, lp, LANES), BF16),
            jax.ShapeDtypeStruct((n_batch, lp, LANES), BF16),
            jax.ShapeDtypeStruct((n_batch, nt, SUBLANES, LANES), F32),
        ],
        compiler_params=_cparams(2),
        name="bprep",
    )(proj, proj, u, qg, kvg, wqb, wiq, wuk_bd, wsmall)


def _mixer_b_kernel(bmax_ref, ql_ref, qi_ref, wi_ref, ckv_ref, klo_ref, khi_ref, kn2_ref, zb_ref, bias_ref,
                    wuv_ref, o_ref, flag_ref, key_scr, s_scr, acc_scr, m_scr, l_scr, alpha_scr, p_scr,
                    *, bound, tail_q):
    jq = pl.program_id(1)
    nch = jq + 1
    qb, kc = QB_B, QB_B
    rows = B_HEADS * qb
    kf = float(TOPK_MAX)

    qi = qi_ref[0].reshape(IDX_HEADS // 2 * qb, LANES)
    wit = wi_ref[0].T
    krow = lax.broadcasted_iota(jnp.int32, (qb, kc), 0)
    krow_sub = lax.broadcasted_iota(jnp.int32, (SUB_B, kc), 0)
    qpos = jq * qb + lax.broadcasted_iota(jnp.int32, (qb, kc), 1)

    def idx_body(i, carry):
        c0 = jnp.minimum(2 * i, n_chunks_total - 2)
        start = pl.multiple_of(c0 * kc, kc)
        sc = (lax.dot_general(klo_ref[0, pl.ds(start, 2 * kc), :], qi, _NT, preferred_element_type=F32),
              lax.dot_general(khi_ref[0, pl.ds(start, 2 * kc), :], qi, _NT, preferred_element_type=F32))
        for half in range(2):
            rs = slice(half * kc, (half + 1) * kc)
            isc = jnp.zeros((kc, qb), F32)
            for pair in range(IDX_HEADS // 2):
                for par in range(2):
                    hd = 2 * pair + par
                    isc = isc + wit[hd:hd + 1, :] * jnp.maximum(sc[par][rs, pair * qb:(pair + 1) * qb], 0.0)
            key_scr[c0 + half] = jnp.where((c0 + half) * kc + krow <= qpos, isc, -jnp.inf)
        return carry

    n_chunks_total = key_scr.shape[0]
    lax.fori_loop(0, (nch + 1) // 2, idx_body, 0)

    def reduce_chunks(tile_fn, init, combine):
        n_acc = SUB_B // SUBLANES

        def body(c, accs):
            accs = list(accs)
            for r0 in range(0, kc, SUB_B):
                t = tile_fn(c, r0, key_scr[c, r0:r0 + SUB_B, :])
                t = t.reshape(n_acc, SUBLANES, qb)
                for i in range(n_acc):
                    accs[i] = combine(accs[i], t[i])
            return tuple(accs)

        accs = lax.fori_loop(0, nch, body, (init,) * n_acc)
        acc = accs[0]
        for a in accs[1:]:
            acc = combine(acc, a)
        return acc

    def count(pred_fn):
        acc = reduce_chunks(lambda c, r0, k: jnp.where(pred_fn(c, r0, k), 1.0, 0.0),
                            jnp.zeros((SUBLANES, qb), F32), lambda a, b: a + b)
        return jnp.sum(acc, axis=0, keepdims=True)

    def unkey(k):
        return lax.bitcast_convert_type(k ^ ((k >> 31) & INT_MAX), F32)

    def tokey(v):
        b = lax.bitcast_convert_type(v, jnp.int32)
        return b ^ ((b >> 31) & INT_MAX)

    def select_all():
        return jnp.full((1, qb), -jnp.inf, F32), jnp.full((1, qb), -1, jnp.int32)

    def select_topk():
        kmax = reduce_chunks(lambda c, r0, k: k, jnp.full((SUBLANES, qb), -jnp.inf, F32), jnp.maximum)
        kmax = jnp.max(kmax, axis=0, keepdims=True)
        kmin = reduce_chunks(lambda c, r0, k: jnp.where(k == -jnp.inf, jnp.inf, k),
                             jnp.full((SUBLANES, qb), jnp.inf, F32), jnp.minimum)
        kmin = jnp.min(kmin, axis=0, keepdims=True)
        c0 = count(lambda c, r0, k: k >= 0.0)
        c0p = count(lambda c, r0, k: k >= F32_TINY)
        n_valid = (jq * qb + 1 + lax.broadcasted_iota(jnp.int32, (1, qb), 1)).astype(F32)
        pos = c0p >= kf
        zero = (c0 >= kf) & (c0p < kf)
        lo0 = jnp.where(pos, F32_TINY, jnp.where(zero, 0.0, kmin))
        hi0 = jnp.where(pos, unkey(tokey(kmax) + 1), jnp.where(zero, F32_TINY, 0.0))
        c_lo0 = jnp.where(pos, c0p, jnp.where(zero, c0, n_valid))
        c_hi0 = jnp.where(pos, 0.0, jnp.where(zero, c0p, c0))

        def bis_cond(st):
            return (st[0] < 64) & (st[5] > 0)

        def bis_step(it, lo, hi, c_lo, c_hi):
            klo, khi = tokey(lo), tokey(hi)
            mid_i = unkey((klo >> 1) + (khi >> 1) + (klo & khi & 1))
            mid_v = 0.5 * lo + 0.5 * hi
            use_v = (it < 24) & (mid_v > lo) & (mid_v < hi)
            mid = jnp.where(use_v, mid_v, mid_i)
            mid = jnp.where((mid > -F32_TINY) & (mid < 0.0), -F32_TINY, mid)
            cnt = count(lambda c, r0, k: k >= mid)
            ge = cnt >= kf
            return (jnp.where(ge, mid, lo), jnp.where(ge, hi, mid),
                    jnp.where(ge, cnt, c_lo), jnp.where(ge, c_hi, cnt))

        def bis_body(st):
            it, lo, hi, c_lo, c_hi, _ = st
            for _ in range(BIS_UNROLL):
                lo, hi, c_lo, c_hi = bis_step(it, lo, hi, c_lo, c_hi)
                it = it + 1
            return it, lo, hi, c_lo, c_hi, jnp.max(jnp.where(is_active(lo, hi, c_lo, c_hi), 1, 0))

        def is_active(lo, hi, c_lo, c_hi):
            gap = (lo == 0.0) | ((hi == 0.0) & (lo >= -F32_TINY))
            return (tokey(lo) + 1 < tokey(hi)) & (c_lo != kf) & (c_lo - c_hi > 2.0) & jnp.logical_not(gap)

        st0 = (jnp.int32(0), lo0, hi0, c_lo0, c_hi0,
               jnp.max(jnp.where(is_active(lo0, hi0, c_lo0, c_hi0), 1, 0)))
        _, lo, hi, c_lo, c_hi, _ = lax.while_loop(bis_cond, bis_body, st0)
        open_rows = c_lo != kf

        def finish():
            top = reduce_chunks(lambda c, r0, k: jnp.where(k < hi, k, -jnp.inf),
                                jnp.full((SUBLANES, qb), -jnp.inf, F32), jnp.maximum)
            top = jnp.max(top, axis=0, keepdims=True)
            bot = reduce_chunks(lambda c, r0, k: jnp.where(k >= lo, k, jnp.inf),
                                jnp.full((SUBLANES, qb), jnp.inf, F32), jnp.minimum)
            bot = jnp.min(bot, axis=0, keepdims=True)
            thr = jnp.where(open_rows, top, lo)
            tie = open_rows & (top == bot)
            need = kf - c_hi

            def tie_break():
                def tie_body(_, st):
                    jlo, jhi = st
                    mid = (jlo + jhi) >> 1
                    cnt = count(lambda c, r0, k: (k == thr) & (c * kc + r0 + krow_sub <= mid))
                    ge = cnt >= need
                    return jnp.where(ge, jlo, mid), jnp.where(ge, mid, jhi)

                idx_bits = (n_chunks_total * kc - 1).bit_length()
                _, jhi = lax.fori_loop(0, idx_bits, tie_body,
                                       (jnp.full((1, qb), -1, jnp.int32),
                                        jnp.full((1, qb), 2 ** idx_bits - 1, jnp.int32)))
                return jnp.where(tie, jhi, INT_MAX)

            jmax = lax.cond(jnp.max(jnp.where(tie, 1, 0)) > 0, tie_break,
                            lambda: jnp.full((1, qb), INT_MAX, jnp.int32))
            return thr, jmax

        return lax.cond(jnp.max(jnp.where(open_rows, 1, 0)) > 0, finish,
                        lambda: (lo, jnp.full((1, qb), INT_MAX, jnp.int32)))

    thr, jmax = lax.cond(jq >= 1, select_topk, select_all)

    def sel_body(c, carry):
        k = key_scr[c]
        sel = (k > thr) | ((k == thr) & (c * kc + krow <= jmax))
        key_scr[c] = jnp.where(sel, 0.0, NEG).T
        return carry

    lax.fori_loop(0, nch, sel_body, 0)

    def kv_chunk(c):
        return ckv_ref[0, pl.ds(pl.multiple_of(c * kc, kc), kc), :]

    def qk(nq, c, slot):
        s_scr[slot, :B_HEADS * nq] = lax.dot_general(
            ql_ref[0, :, :nq, :].reshape(B_HEADS * nq, B_KV_RANK), kv_chunk(c), _NT,
            preferred_element_type=F32)

    def logits(nq, c, slot, near, hd, r0):
        sub = min(SUB_B, nq)
        sh = s_scr[slot, hd * nq + r0:hd * nq + r0 + sub]
        if near == "cur":
            t0 = bias_ref[hd, r0 % QB_A:r0 % QB_A + sub, LANES:]
            t1 = bias_ref[hd, r0 % QB_A:r0 % QB_A + sub, :LANES]
            sh = sh + jnp.concatenate([t0 if r0 < QB_A else t1, t0], axis=1)
        elif near == "prev" and r0 < QB_A:
            t1 = bias_ref[hd, r0:r0 + sub, :LANES]
            sh = sh + jnp.concatenate([jnp.zeros_like(t1), t1], axis=1)
        return sh + key_scr[c, r0:r0 + sub, :]

    def sub_tiles(nq):
        sub = min(SUB_B, nq)
        for hd in range(B_HEADS):
            for r0 in range(0, nq, sub):
                yield hd, r0, slice(hd * nq + r0, hd * nq + r0 + sub)

    def sm(nq, c, slot, near):
        for hd, r0, sl in sub_tiles(nq):
            m = m_scr[sl]
            p = jnp.exp2(logits(nq, c, slot, near, hd, r0) - jnp.concatenate([m, m], axis=1))
            l_scr[sl] += p[:, :LANES] + p[:, LANES:]
            p_scr[slot, sl] = p.astype(BF16)

    def pv(nq, c, slot):
        o = jnp.dot(p_scr[slot, :B_HEADS * nq], kv_chunk(c), preferred_element_type=F32)
        sub = min(SUB_B, nq)
        for r0 in range(0, B_HEADS * nq, sub):
            acc_scr[r0:r0 + sub] += o[r0:r0 + sub]

    def bound_attention(nq):
        l_scr[...] = jnp.zeros(l_scr.shape, F32)
        acc_scr[...] = jnp.zeros(acc_scr.shape, F32)
        kn2 = lax.fori_loop(0, nch, lambda c, a: jnp.maximum(a, kn2_ref[0, c]),
                            jnp.zeros((SUBLANES, LANES), F32))
        bmax = bmax_ref[0]
        for hd, r0, sl in sub_tiles(nq):
            x = ql_ref[0, hd, r0:r0 + min(SUB_B, nq), :].astype(F32)
            qn2 = jnp.sum(x * x, axis=-1, keepdims=True)
            m_scr[sl] = jnp.sqrt(qn2 * kn2[:1]) * (1.0 + 2.0 ** -10) + bmax

        def stage(t, slot):
            qk(nq, t, slot)
            pv(nq, t - 2, slot)
            sm(nq, t - 1, 1 - slot, None)

        def tail(slot):
            qk(nq, jq, slot)
            pv(nq, jq - 2, slot)
            sm(nq, jq - 1, 1 - slot, "prev")
            pv(nq, jq - 1, 1 - slot)
            sm(nq, jq, slot, "cur")
            pv(nq, jq, slot)

        @pl.when(jq >= 2)
        def _():
            qk(nq, 0, 0)
            qk(nq, 1, 1)
            sm(nq, 0, 0, None)

            def two_stages(i, carry):
                stage(2 + 2 * i, 0)
                stage(3 + 2 * i, 1)
                return carry

            lax.fori_loop(0, (jq - 2) // 2, two_stages, 0)

            @pl.when(jq % 2 == 0)
            def _():
                tail(0)

            @pl.when(jq % 2 == 1)
            def _():
                stage(jq - 1, 0)
                tail(1)

        @pl.when(jq < 2)
        def _():
            @pl.when(jq == 1)
            def _():
                qk(nq, 0, 0)
                sm(nq, 0, 0, "prev")
                pv(nq, 0, 0)

            qk(nq, jq, 1)
            sm(nq, jq, 1, "cur")
            pv(nq, jq, 1)

    def running_max_attention():
        l_scr[...] = jnp.zeros(l_scr.shape, F32)
        acc_scr[...] = jnp.zeros(acc_scr.shape, F32)
        m_scr[...] = jnp.full(m_scr.shape, -jnp.inf, F32)

        def attend(c, near):
            qk(qb, c, 0)
            for hd, r0, sl in sub_tiles(qb):
                m_old = m_scr[sl]
                m_new = jnp.maximum(m_old, jnp.max(logits(qb, c, 0, near, hd, r0), axis=-1, keepdims=True))
                m_scr[sl] = m_new
                alpha_scr[sl] = jnp.exp2(m_old - m_new)
            for hd, r0, sl in sub_tiles(qb):
                m_new = m_scr[sl]
                p = jnp.exp2(logits(qb, c, 0, near, hd, r0) - jnp.concatenate([m_new, m_new], axis=1))
                l_scr[sl] = alpha_scr[sl] * l_scr[sl] + (p[:, :LANES] + p[:, LANES:])
                p_scr[0, sl] = p.astype(BF16)
            o = jnp.dot(p_scr[0], kv_chunk(c), preferred_element_type=F32)
            for r0 in range(0, rows, SUB_B):
                sl = slice(r0, r0 + SUB_B)
                alpha = alpha_scr[sl]
                acc_scr[sl] = acc_scr[sl] * jnp.concatenate([alpha, alpha], axis=1) + o[sl]

        def far_body(c, carry):
            attend(c, None)
            return carry

        lax.fori_loop(0, jq - 1, far_body, 0)

        @pl.when(jq >= 1)
        def _():
            attend(jq - 1, "prev")

        attend(jq, "cur")

    def write_output(nq):
        l_min = jnp.full((nq, 1), jnp.inf, F32)
        for pair in range(B_HEADS // 2):
            col = slice(pair * LANES, (pair + 1) * LANES)
            y2 = jnp.zeros((nq, LANES), F32)
            for par in range(2):
                hd = 2 * pair + par
                sl = slice(hd * nq, (hd + 1) * nq)
                l = jnp.sum(l_scr[sl], axis=-1, keepdims=True)
                l_min = jnp.minimum(l_min, l)
                o_lat = acc_scr[sl] / l
                y2 = y2 + jnp.dot(o_lat.astype(BF16), wuv_ref[pair, par], preferred_element_type=F32)
            o_ref[:nq, col] = (y2 * _silu(zb_ref[:nq, col])).astype(o_ref.dtype)
            if nq < qb:
                o_ref[nq:, col] = jnp.zeros((qb - nq, LANES), o_ref.dtype)
        safe = jnp.min(l_min) >= L_SAFE
        flag_ref[0, 0] = jnp.full(flag_ref.shape[2:], jnp.where(safe, 0.0, 1.0), F32)

    if not bound:
        running_max_attention()
        write_output(qb)
    elif tail_q == qb:
        bound_attention(qb)
        write_output(qb)
    else:
        last = pl.num_programs(1) - 1

        @pl.when(jq < last)
        def _():
            bound_attention(qb)
            write_output(qb)

        @pl.when(jq == last)
        def _():
            bound_attention(tail_q)
            write_output(tail_q)


def _mixer_b(bmax, ql, qi, wi, ckvn, klo, khi, kn2, proj, bias_b, wuv_pairs, layer, n_batch, lp, length,
             bound):
    nb = lp // QB_B
    tail_q = min(QB_B, -(-(length - (nb - 1) * QB_B) // 16) * 16)
    rows = n_batch * lp
    whole = lambda b, j: (b, 0, 0)
    once = pl.Buffered(1)
    return pl.pallas_call(
        functools.partial(_mixer_b_kernel, bound=bound, tail_q=tail_q),
        grid=(n_batch, nb),
        in_specs=[
            pl.BlockSpec(memory_space=pltpu.SMEM),
            pl.BlockSpec((1, B_HEADS, QB_B, B_KV_RANK), lambda b, j: (b, 0, j, 0)),
            pl.BlockSpec((1, IDX_HEADS // 2, QB_B, LANES), lambda b, j: (b, 0, j, 0)),
            pl.BlockSpec((1, QB_B, LANES), lambda b, j: (b, j, 0)),
            pl.BlockSpec((1, lp, B_KV_RANK), whole, pipeline_mode=once),
            pl.BlockSpec((1, lp, LANES), whole, pipeline_mode=once),
            pl.BlockSpec((1, lp, LANES), whole, pipeline_mode=once),
            pl.BlockSpec((1, nb, SUBLANES, LANES), lambda b, j: (b, 0, 0, 0), pipeline_mode=once),
            pl.BlockSpec((QB_B, B_WIDTH), lambda b, j: (b * nb + j, C_ZB // B_WIDTH)),
            pl.BlockSpec((B_HEADS, QB_A, 2 * LANES), lambda b, j: (0, 0, 0), pipeline_mode=once),
            pl.BlockSpec((None, B_HEADS // 2, 2, B_KV_RANK, LANES), lambda b, j: (layer, 0, 0, 0, 0),
                         pipeline_mode=once),
        ],
        out_specs=[pl.BlockSpec((QB_B, B_WIDTH), lambda b, j: (b * nb + j, 0)),
                   pl.BlockSpec((1, 1, SUBLANES, LANES), lambda b, j: (b, j, 0, 0))],
        out_shape=[jax.ShapeDtypeStruct((rows, B_WIDTH), BF16),
                   jax.ShapeDtypeStruct((n_batch, nb, SUBLANES, LANES), F32)],
        scratch_shapes=[
            pltpu.VMEM((nb, QB_B, QB_B), F32),
            pltpu.VMEM((2, B_HEADS * QB_B, QB_B), F32),
            pltpu.VMEM((B_HEADS * QB_B, B_KV_RANK), F32),
            pltpu.VMEM((B_HEADS * QB_B, LANES), F32),
            pltpu.VMEM((B_HEADS * QB_B, LANES), F32),
            pltpu.VMEM((B_HEADS * QB_B, LANES), F32),
            pltpu.VMEM((2, B_HEADS * QB_B, QB_B), BF16),
        ],
        compiler_params=_cparams(2),
        name="mixer_b" if bound else "mixer_b_running_max",
    )(bmax, ql, qi, wi, ckvn, klo, khi, kn2, proj, bias_b, wuv_pairs)


def _merge_kernel(ya_ref, yb_ref, ga_ref, gb_ref, h_ref, wpa_ref, wpb_ref, wo_ref, g_ref, *out_refs,
                  last):
    pa = jnp.dot(ya_ref[...], wpa_ref[...], preferred_element_type=F32)
    pb = jnp.dot(yb_ref[...], wpb_ref[...], preferred_element_type=F32)
    merged = _sigmoid(ga_ref[...]) * pa + _sigmoid(gb_ref[...]) * pb
    h_new = h_ref[...] + jnp.dot(merged.astype(BF16), wo_ref[...], preferred_element_type=F32)
    normed = _rms(h_new, g_ref[...])
    if last:
        out_refs[0][...] = normed
    else:
        out_refs[0][...] = h_new
        out_refs[1][...] = normed.astype(BF16)


def _merge(ya, yb, gates, h, wpa, wpb, wo, g_next, layer, last, tm=256):
    rows, d = h.shape
    const = lambda i: (0, 0)
    lay = lambda i: (layer, 0, 0)
    once = pl.Buffered(1)
    if last:
        out_specs = [pl.BlockSpec((tm, d), lambda i: (i, 0))]
        out_shape = [jax.ShapeDtypeStruct((rows, d), F32)]
    else:
        out_specs = [pl.BlockSpec((tm, d), lambda i: (i, 0)), pl.BlockSpec((tm, d), lambda i: (i, 0))]
        out_shape = [jax.ShapeDtypeStruct((rows, d), F32), jax.ShapeDtypeStruct((rows, d), BF16)]
    return pl.pallas_call(
        functools.partial(_merge_kernel, last=last),
        grid=(rows // tm,),
        in_specs=[
            pl.BlockSpec((tm, A_WIDTH), lambda i: (i, 0)),
            pl.BlockSpec((tm, B_WIDTH), lambda i: (i, 0)),
            pl.BlockSpec((tm, d), lambda i: (i, 0)),
            pl.BlockSpec((tm, d), lambda i: (i, 1)),
            pl.BlockSpec((tm, d), lambda i: (i, 0)),
            pl.BlockSpec((None, A_WIDTH, d), lay, pipeline_mode=once),
            pl.BlockSpec((None, B_WIDTH, d), lay, pipeline_mode=once),
            pl.BlockSpec((None, d, d), lay, pipeline_mode=once),
            pl.BlockSpec((1, d), const),
        ],
        out_specs=out_specs,
        out_shape=out_shape,
        compiler_params=_cparams(1),
        name="merge_out",
    )(ya, yb, gates, gates, h, wpa, wpb, wo, g_next.reshape(1, d))


def _pack_weights(w_in, w_uk, w_uv, w_qb, w_iq, w_proj_a, w_proj_b, w_out):
    depth = w_in.shape[0]
    sec = lambda a, b: w_in[:, :, a:b]
    kidx, widx = sec(4096, 4160), sec(4160, 4168)
    z64 = jnp.zeros_like(kidx)
    zpad = jnp.zeros(widx.shape[:2] + (LANES - IDX_HEADS,), w_in.dtype)
    w_small = jnp.concatenate([kidx, z64, z64, kidx, widx, zpad], axis=-1).astype(BF16)
    uk = (w_uk * (HEAD_DIM ** -0.5 * LOG2E)).reshape(depth, B_HEADS // 2, 2, HEAD_DIM, B_KV_RANK)
    zuk = jnp.zeros_like(uk[:, :, 0])
    wuk_bd = jnp.concatenate([jnp.concatenate([uk[:, :, 0], zuk], axis=-1),
                              jnp.concatenate([zuk, uk[:, :, 1]], axis=-1)], axis=-2).astype(BF16)
    uv = w_uv.reshape(depth, B_HEADS // 2, 2, B_KV_RANK, HEAD_DIM)
    zuv = jnp.zeros_like(uv[:, :, 0])
    wuv_pairs = jnp.stack([jnp.concatenate([uv[:, :, 0], zuv], axis=-1),
                           jnp.concatenate([zuv, uv[:, :, 1]], axis=-1)], axis=2).astype(BF16)
    w_iq8 = (w_iq * (IDX_DIM ** -0.5)).astype(BF16)
    return (w_small, wuk_bd, wuv_pairs, w_qb.astype(BF16), w_iq8,
            w_proj_a.astype(BF16), w_proj_b.astype(BF16), w_out.astype(BF16))


def kernel(x, meta_tokens, bias_table, norm_g, w_in, q_norm_g, kv_norm_g, w_qb, w_iq, w_uk, w_uv,
           sinks, w_proj_a, w_proj_b, w_out, final_g):
    n_batch, seq, d = x.shape
    depth = w_in.shape[0]
    length = seq + N_META
    lp = -(-length // QB_B) * QB_B
    assert min(TOPK_MAX, seq // 4) == TOPK_MAX and d == D_MODEL
    assert WINDOW == QB_A and lp // QB_B >= 3

    (w_small, wuk_bd, wuv_pairs, wqb, wiq, wpa, wpb, wo) = _pack_weights(
        w_in, w_uk, w_uv, w_qb, w_iq, w_proj_a, w_proj_b, w_out)
    w_t = jnp.swapaxes(w_in, 1, 2)
    bias_a, bias_b = _bias_tiles(bias_table)
    bmax = jnp.max(jnp.abs(bias_b)).reshape(1)

    meta = jnp.broadcast_to(meta_tokens.astype(x.dtype)[None], (n_batch, N_META, d))
    pad = jnp.zeros((n_batch, lp - length, d), x.dtype)
    h = jnp.concatenate([meta, x, pad], axis=1).reshape(n_batch * lp, d)

    u = _rmsnorm(h, norm_g[0])
    out = None
    for l in range(depth):
        proj = _in_proj(u, w_t, l)
        gates = _gate_proj(u, w_t, l, GATE_COL0, 2 * D_MODEL)
        ya = _mixer_a(proj, sinks[l], bias_a, n_batch, lp)
        ql, qi, wi, ckvn, klo, khi, kn2 = _bprep(
            proj, u, q_norm_g[l].reshape(1, -1), kv_norm_g[l].reshape(1, -1),
            wqb, wiq, wuk_bd, w_small, l, n_batch, lp)
        b_args = (bmax, ql, qi, wi, ckvn, klo, khi, kn2, proj, bias_b, wuv_pairs, l, n_batch, lp, length)
        yb, flags = _mixer_b(*b_args, bound=True)
        yb = lax.cond(jnp.max(flags) > 0.0, lambda: _mixer_b(*b_args, bound=False)[0], lambda: yb)
        last = l == depth - 1
        g_next = final_g if last else norm_g[l + 1]
        res = _merge(ya, yb, gates, h, wpa, wpb, wo, g_next, l, last)
        if last:
            out = res[0]
        else:
            h, u = res
    return out.reshape(n_batch, lp, d)[:, N_META:length]
```

```python
import functools
import math

import numpy as np
import jax
import jax.numpy as jnp
from jax import lax
from jax.experimental import pallas as pl
from jax.experimental.pallas import tpu as pltpu

D_MODEL = 2048
N_META = 16
WINDOW = 128
HEAD_DIM = 64
A_HEADS = 16
A_WIDTH = 1024
B_HEADS = 16
B_WIDTH = 1024
B_Q_RANK = 512
B_KV_RANK = 256
IDX_HEADS = 8
IDX_DIM = 64
TOPK_MAX = 256
N_BUCKETS = 32
MAX_DISTANCE = 128
EPS = 1e-6
NEG = -1e30

LANES = 128
SUBLANES = 8
PROJ_ROW_TILES = 8
QB_A = 128
QB_B = 256
SUB_A = 32
SUB_B = 32
L_SAFE = 2.0 ** -100
BIS_UNROLL = 3
LOG2E = math.log2(math.e)
VMEM_LIMIT = 56 * 1024 * 1024
INT_MAX = 2 ** 31 - 1
F32_TINY = 2.0 ** -126

C_QA, C_ZA, C_ZB, C_CQ, C_CKV, C_KA, C_VA = (0, 1024, 2048, 3072, 3584, 3840, 3968)
GATE_COL0 = 4168

F32 = jnp.float32
BF16 = jnp.bfloat16
_NT = (((1,), (1,)), ((), ()))


def _cparams(n_grid):
    return pltpu.CompilerParams(
        dimension_semantics=("arbitrary",) * n_grid,
        vmem_limit_bytes=VMEM_LIMIT)


def _t5_bucket_np(d):
    max_exact = N_BUCKETS // 2
    nf = np.maximum(d, 1).astype(np.float32)
    large = max_exact + (np.log(nf / np.float32(max_exact)) / np.float32(math.log(MAX_DISTANCE / max_exact))
                         * np.float32(N_BUCKETS - max_exact)).astype(np.int32)
    large = np.minimum(large, N_BUCKETS - 1)
    return np.where(d < max_exact, d, large).astype(np.int32)


def _bias_tiles_kernel(tab_ref, idx_ref, out_a_ref, out_b_ref):
    h = pl.program_id(0)
    idx = idx_ref[...]
    acc_a = jnp.zeros(idx.shape, F32)
    acc_b = jnp.zeros(idx.shape, F32)
    far_b = tab_ref[N_BUCKETS - 1, A_HEADS + h]
    for b in range(N_BUCKETS):
        hit = idx == b
        acc_a = jnp.where(hit, tab_ref[b, h] * LOG2E, acc_a)
        acc_b = jnp.where(hit, (tab_ref[b, A_HEADS + h] - far_b) * LOG2E, acc_b)
    out_a_ref[0] = acc_a
    out_b_ref[0] = acc_b[:, :2 * LANES]


def _bias_tiles(bias_table):
    r = np.arange(QB_A)[:, None]
    k = np.arange(QB_A)[None, :]
    prev = _t5_bucket_np(np.maximum(QB_A + r - k, 0))
    cur = _t5_bucket_np(np.maximum(r - k, 0))
    far = np.full((QB_A, QB_A), N_BUCKETS - 1, np.int32)
    idx = jnp.asarray(np.concatenate([prev, cur, far], axis=1))
    return pl.pallas_call(
        _bias_tiles_kernel,
        grid=(A_HEADS,),
        in_specs=[pl.BlockSpec(memory_space=pltpu.SMEM),
                  pl.BlockSpec((QB_A, 3 * LANES), lambda h: (0, 0))],
        out_specs=[pl.BlockSpec((1, QB_A, 3 * LANES), lambda h: (h, 0, 0)),
                   pl.BlockSpec((1, QB_A, 2 * LANES), lambda h: (h, 0, 0))],
        out_shape=[jax.ShapeDtypeStruct((A_HEADS, QB_A, 3 * LANES), F32),
                   jax.ShapeDtypeStruct((B_HEADS, QB_A, 2 * LANES), F32)],
        compiler_params=_cparams(1),
        name="bias_tiles",
    )(bias_table, idx)


def _rms(x, g):
    return x * lax.rsqrt(jnp.mean(x * x, axis=-1, keepdims=True) + EPS) * g


def _rmsnorm_kernel(h_ref, g_ref, u_ref):
    u_ref[...] = _rms(h_ref[...], g_ref[...]).astype(u_ref.dtype)


def _rmsnorm(h, g, tm=512):
    rows, d = h.shape
    return pl.pallas_call(
        _rmsnorm_kernel,
        grid=(rows // tm,),
        in_specs=[pl.BlockSpec((tm, d), lambda i: (i, 0)),
                  pl.BlockSpec((1, d), lambda i: (0, 0))],
        out_specs=pl.BlockSpec((tm, d), lambda i: (i, 0)),
        out_shape=jax.ShapeDtypeStruct((rows, d), BF16),
        compiler_params=_cparams(1),
        name="rmsnorm",
    )(h, g.reshape(1, d))


_IN_BLOCKS = ((0, 1, 2, 3), (5, 6, 7, 8), (12, 13, 14, 15), (9, 10, 11, 4))
IN_BLK = 256


def _in_proj_kernel(x_ref, *refs):
    w_refs, o_ref, wbf_scr = refs[:-2], refs[-2], refs[-1]

    @pl.when(pl.program_id(1) == 0)
    def _():
        r0 = 0
        for w_ref in w_refs:
            wbf_scr[r0:r0 + w_ref.shape[0], :] = w_ref[...].astype(BF16)
            r0 += w_ref.shape[0]

    o_ref[...] = lax.dot_general(x_ref[...], wbf_scr[...], _NT, preferred_element_type=F32)


def _in_proj(x, w_t, layer):
    rows, kdim = x.shape
    tm = rows // PROJ_ROW_TILES
    tn = 4 * IN_BLK

    def w_spec(s):
        def index(j, i):
            blk = jnp.int32(_IN_BLOCKS[0][s])
            for t in range(1, len(_IN_BLOCKS)):
                blk = jnp.where(j == t, _IN_BLOCKS[t][s], blk)
            return layer, blk, 0
        return pl.BlockSpec((None, IN_BLK, kdim), index)

    return pl.pallas_call(
        _in_proj_kernel,
        grid=(len(_IN_BLOCKS), rows // tm),
        in_specs=[pl.BlockSpec((tm, kdim), lambda j, i: (i, 0))] + [w_spec(s) for s in range(4)],
        out_specs=pl.BlockSpec((tm, tn), lambda j, i: (i, j)),
        out_shape=jax.ShapeDtypeStruct((rows, len(_IN_BLOCKS) * tn), F32),
        scratch_shapes=[pltpu.VMEM((tn, kdim), BF16)],
        compiler_params=_cparams(2),
        name="in_proj",
    )(x, w_t, w_t, w_t, w_t)


def _gate_proj(x, w_t, layer, col0, n, tn=1024):
    rows, kdim = x.shape
    tm = rows // PROJ_ROW_TILES
    return pl.pallas_call(
        _in_proj_kernel,
        grid=(n // tn, rows // tm),
        in_specs=[pl.BlockSpec((tm, kdim), lambda j, i: (i, 0)),
                  pl.BlockSpec((None, pl.Element(tn), pl.Element(kdim)),
                               lambda j, i: (layer, pl.multiple_of(col0 + j * tn, 8), 0))],
        out_specs=pl.BlockSpec((tm, tn), lambda j, i: (i, j)),
        out_shape=jax.ShapeDtypeStruct((rows, n), F32),
        scratch_shapes=[pltpu.VMEM((tn, kdim), BF16)],
        compiler_params=_cparams(2),
        name="gate_proj",
    )(x, w_t)


def _silu(z):
    return z / (1.0 + jnp.exp(-z))


def _sigmoid(z):
    return 1.0 / (1.0 + jnp.exp(-z))


def _mixer_a_kernel(sink_ref, q_ref, kc_ref, vc_ref, kp_ref, vp_ref, km_ref, vm_ref, za_ref,
                    bias_ref, o_ref, s_scr, p_scr, m_scr, d_scr, msk_scr):
    j = pl.program_id(1)
    nk = 3 * QB_A
    q = (q_ref[...] * (HEAD_DIM ** -0.5 * LOG2E)).astype(BF16)
    kcat = jnp.concatenate([kp_ref[...], kc_ref[...], km_ref[...]], axis=0)
    vcat = jnp.concatenate([vp_ref[...], vc_ref[...], vm_ref[...]], axis=0)
    lane = lax.broadcasted_iota(jnp.int32, (nk, LANES), 1)
    lo = lane < HEAD_DIM
    kswap = pltpu.roll(kcat, HEAD_DIM, 1)
    vswap = pltpu.roll(vcat, HEAD_DIM, 1)
    zero = jnp.zeros_like(kcat)
    k_half = ((jnp.where(lo, kcat, zero).astype(BF16), jnp.where(lo, zero, kswap).astype(BF16)),
              (jnp.where(lo, kswap, zero).astype(BF16), jnp.where(lo, zero, kcat).astype(BF16)))
    v_half = ((jnp.where(lo, vcat, zero).astype(BF16), jnp.where(lo, zero, vswap).astype(BF16)),
              (jnp.where(lo, vswap, zero).astype(BF16), jnp.where(lo, zero, vcat).astype(BF16)))

    r = lax.broadcasted_iota(jnp.int32, (QB_A, nk), 0)
    c = lax.broadcasted_iota(jnp.int32, (QB_A, nk), 1)
    jv = jnp.zeros((QB_A, nk), jnp.int32) + j
    in_prev = (c < QB_A) & (jv >= 1) & ((c > r) | ((jv == 1) & (c < N_META)))
    in_cur = (c >= QB_A) & (c < 2 * QB_A) & (c - QB_A <= r)
    in_meta = (c >= 2 * QB_A) & (jv >= 2) & (c - 2 * QB_A < N_META)
    msk_scr[...] = jnp.where(in_prev | in_cur | in_meta, 0.0, NEG)
    half = lax.broadcasted_iota(jnp.int32, (QB_A, LANES), 1) < HEAD_DIM
    pairs = A_HEADS // 4

    def logits(g, i, par, r0):
        hd = 2 * (g * pairs + i) + par
        s = s_scr[i * QB_A + r0:i * QB_A + r0 + SUB_A, par * nk:(par + 1) * nk]
        return s + bias_ref[hd, r0:r0 + SUB_A, :] + msk_scr[r0:r0 + SUB_A, :], sink_ref[hd] * LOG2E

    for g in range(2):
        q4 = jnp.concatenate([q[:, (g * pairs + i) * LANES:(g * pairs + i + 1) * LANES]
                              for i in range(pairs)], axis=0)
        s_scr[...] = lax.dot_general(q4, jnp.concatenate(k_half[g], axis=0), _NT,
                                     preferred_element_type=F32)
        for i in range(pairs):
            for par in range(2):
                for r0 in range(0, QB_A, SUB_A):
                    lg, sink = logits(g, i, par, r0)
                    m = jnp.maximum(jnp.max(lg, axis=-1, keepdims=True), sink)
                    m_scr[par, i * QB_A + r0:i * QB_A + r0 + SUB_A] = jnp.broadcast_to(m, (SUB_A, LANES))
        for i in range(pairs):
            for par in range(2):
                for r0 in range(0, QB_A, SUB_A):
                    sl = slice(i * QB_A + r0, i * QB_A + r0 + SUB_A)
                    lg, sink = logits(g, i, par, r0)
                    m = m_scr[par, sl]
                    p = jnp.exp2(lg - jnp.concatenate([m] * 3, axis=1))
                    d_scr[par, sl] = jnp.broadcast_to(
                        jnp.sum(p, axis=-1, keepdims=True), (SUB_A, LANES)) + jnp.exp2(sink - m)
                    p_scr[sl, par * nk:(par + 1) * nk] = p.astype(BF16)
        pv = jnp.dot(p_scr[...], jnp.concatenate(v_half[g], axis=0), preferred_element_type=F32)
        for i in range(pairs):
            sl = slice(i * QB_A, (i + 1) * QB_A)
            col = slice((g * pairs + i) * LANES, (g * pairs + i + 1) * LANES)
            o2 = pv[sl] / jnp.where(half, d_scr[0, sl], d_scr[1, sl])
            o_ref[:, col] = (o2 * _silu(za_ref[:, col])).astype(o_ref.dtype)


def _mixer_a(proj, sinks, bias_a, n_batch, lp):
    nb = lp // QB_A
    rows = n_batch * lp
    ck, cv = C_KA // LANES, C_VA // LANES
    cur = lambda b, j: b * nb + j
    prev = lambda b, j: b * nb + jnp.maximum(j - 1, 0)
    first = lambda b, j: b * nb
    return pl.pallas_call(
        _mixer_a_kernel,
        grid=(n_batch, nb),
        in_specs=[
            pl.BlockSpec(memory_space=pltpu.SMEM),
            pl.BlockSpec((QB_A, A_WIDTH), lambda b, j: (cur(b, j), C_QA // A_WIDTH)),
            pl.BlockSpec((QB_A, LANES), lambda b, j: (cur(b, j), ck)),
            pl.BlockSpec((QB_A, LANES), lambda b, j: (cur(b, j), cv)),
            pl.BlockSpec((QB_A, LANES), lambda b, j: (prev(b, j), ck)),
            pl.BlockSpec((QB_A, LANES), lambda b, j: (prev(b, j), cv)),
            pl.BlockSpec((QB_A, LANES), lambda b, j: (first(b, j), ck)),
            pl.BlockSpec((QB_A, LANES), lambda b, j: (first(b, j), cv)),
            pl.BlockSpec((QB_A, A_WIDTH), lambda b, j: (cur(b, j), C_ZA // A_WIDTH)),
            pl.BlockSpec((A_HEADS, QB_A, 3 * LANES), lambda b, j: (0, 0, 0)),
        ],
        out_specs=pl.BlockSpec((QB_A, A_WIDTH), lambda b, j: (cur(b, j), 0)),
        out_shape=jax.ShapeDtypeStruct((rows, A_WIDTH), BF16),
        scratch_shapes=[
            pltpu.VMEM((A_HEADS // 4 * QB_A, 6 * QB_A), F32),
            pltpu.VMEM((A_HEADS // 4 * QB_A, 6 * QB_A), BF16),
            pltpu.VMEM((2, A_HEADS // 4 * QB_A, LANES), F32),
            pltpu.VMEM((2, A_HEADS // 4 * QB_A, LANES), F32),
            pltpu.VMEM((QB_A, 3 * QB_A), F32),
        ],
        compiler_params=_cparams(2),
        name="mixer_a",
    )(sinks, proj, proj, proj, proj, proj, proj, proj, proj, bias_a)


def _bprep_kernel(cq_ref, ckv_ref, u_ref, qg_ref, kvg_ref, wqb_ref, wiq_ref, wuk_ref, wsm_ref,
                  ql_ref, qi_ref, wi_ref, ckvn_ref, klo_ref, khi_ref, kn2_ref):
    nbt, tm = cq_ref.shape[0], cq_ref.shape[1]
    rows = nbt * tm
    per_batch = lambda x, b: x[b * tm:(b + 1) * tm]
    cqn = _rms(cq_ref[...].reshape(rows, -1), qg_ref[...]).astype(BF16)
    ckvn = _rms(ckv_ref[...].reshape(rows, -1), kvg_ref[...]).astype(ckvn_ref.dtype)
    kn2 = jnp.sum(jnp.square(ckvn.astype(F32)), axis=-1, keepdims=True)
    qb = jnp.dot(cqn, wqb_ref[...], preferred_element_type=F32).astype(BF16)
    qi = jnp.dot(cqn, wiq_ref[...], preferred_element_type=F32)
    small = jnp.dot(u_ref[...].reshape(rows, -1), wsm_ref[...], preferred_element_type=F32)
    for b in range(nbt):
        ckvn_ref[b] = per_batch(ckvn, b)
        kn2_ref[b, 0] = jnp.broadcast_to(jnp.max(per_batch(kn2, b), axis=0, keepdims=True), kn2_ref.shape[2:])
        for pair in range(IDX_HEADS // 2):
            qi_ref[b, pair] = per_batch(qi, b)[:, pair * LANES:(pair + 1) * LANES].astype(qi_ref.dtype)
        klo_ref[b] = per_batch(small, b)[:, :LANES].astype(klo_ref.dtype)
        khi_ref[b] = per_batch(small, b)[:, LANES:2 * LANES].astype(khi_ref.dtype)
        wi_ref[b] = per_batch(small, b)[:, 2 * LANES:] * (IDX_HEADS ** -0.5)
    for pair in range(B_HEADS // 2):
        ql2 = jnp.dot(qb[:, pair * LANES:(pair + 1) * LANES], wuk_ref[pair],
                      preferred_element_type=F32)
        for b in range(nbt):
            ql_ref[b, 2 * pair] = per_batch(ql2, b)[:, :B_KV_RANK].astype(ql_ref.dtype)
            ql_ref[b, 2 * pair + 1] = per_batch(ql2, b)[:, B_KV_RANK:].astype(ql_ref.dtype)


def _bprep(proj, u, qg, kvg, wqb, wiq, wuk_bd, wsmall, layer, n_batch, lp, tm=QB_B):
    nt = lp // tm
    const2 = lambda i: (0, 0)
    lay2 = lambda i: (layer, 0, 0)
    proj3 = proj.reshape(n_batch, lp, proj.shape[-1])
    return pl.pallas_call(
        _bprep_kernel,
        grid=(nt,),
        in_specs=[
            pl.BlockSpec((n_batch, tm, B_Q_RANK), lambda i: (0, i, C_CQ // B_Q_RANK)),
            pl.BlockSpec((n_batch, tm, B_KV_RANK), lambda i: (0, i, C_CKV // B_KV_RANK)),
            pl.BlockSpec((n_batch, tm, D_MODEL), lambda i: (0, i, 0)),
            pl.BlockSpec((1, B_Q_RANK), const2),
            pl.BlockSpec((1, B_KV_RANK), const2),
            pl.BlockSpec((None, B_Q_RANK, B_WIDTH), lay2),
            pl.BlockSpec((None, B_Q_RANK, IDX_HEADS * IDX_DIM), lay2),
            pl.BlockSpec((None, B_HEADS // 2, LANES, 2 * B_KV_RANK), lambda i: (layer, 0, 0, 0)),
            pl.BlockSpec((None, D_MODEL, 3 * LANES), lay2),
        ],
        out_specs=[
            pl.BlockSpec((n_batch, B_HEADS, tm, B_KV_RANK), lambda i: (0, 0, i, 0)),
            pl.BlockSpec((n_batch, IDX_HEADS // 2, tm, LANES), lambda i: (0, 0, i, 0)),
            pl.BlockSpec((n_batch, tm, LANES), lambda i: (0, i, 0)),
            pl.BlockSpec((n_batch, tm, B_KV_RANK), lambda i: (0, i, 0)),
            pl.BlockSpec((n_batch, tm, LANES), lambda i: (0, i, 0)),
            pl.BlockSpec((n_batch, tm, LANES), lambda i: (0, i, 0)),
            pl.BlockSpec((n_batch, 1, SUBLANES, LANES), lambda i: (0, i, 0, 0)),
        ],
        out_shape=[
            jax.ShapeDtypeStruct((n_batch, B_HEADS, lp, B_KV_RANK), BF16),
            jax.ShapeDtypeStruct((n_batch, IDX_HEADS // 2, lp, LANES), BF16),
            jax.ShapeDtypeStruct((n_batch, lp, LANES), F32),
            jax.ShapeDtypeStruct((n_batch, lp, B_KV_RANK), BF16),
            jax.ShapeDtypeStruct((n_batch, lp, LANES), BF16),
            jax.ShapeDtypeStruct((n_batch, lp, LANES), BF16),
            jax.ShapeDtypeStruct((n_batch, nt, SUBLANES, LANES), F32),
        ],
        compiler_params=_cparams(1),
        name="bprep",
    )(proj3, proj3, u.reshape(n_batch, lp, u.shape[-1]), qg, kvg, wqb, wiq, wuk_bd, wsmall)


def _mixer_b_kernel(bmax_ref, ql_ref, qi_ref, wi_ref, ckv_ref, klo_ref, khi_ref, kn2_ref, zb_ref, bias_ref,
                    wuv_ref, o_ref, flag_ref, key_scr, s_scr, acc_scr, m_scr, l_scr, alpha_scr, p_scr,
                    *, bound, tail_q):
    jq = pl.program_id(1)
    nch = jq + 1
    qb, kc = QB_B, QB_B
    rows = B_HEADS * qb
    kf = float(TOPK_MAX)

    qi = qi_ref[0].reshape(IDX_HEADS // 2 * qb, LANES)
    wit = wi_ref[0].T
    krow = lax.broadcasted_iota(jnp.int32, (qb, kc), 0)
    krow_sub = lax.broadcasted_iota(jnp.int32, (SUB_B, kc), 0)
    qpos = jq * qb + lax.broadcasted_iota(jnp.int32, (qb, kc), 1)

    def idx_body(i, carry):
        c0 = jnp.minimum(2 * i, n_chunks_total - 2)
        start = pl.multiple_of(c0 * kc, kc)
        sc = (lax.dot_general(klo_ref[0, pl.ds(start, 2 * kc), :], qi, _NT, preferred_element_type=F32),
              lax.dot_general(khi_ref[0, pl.ds(start, 2 * kc), :], qi, _NT, preferred_element_type=F32))
        for half in range(2):
            rs = slice(half * kc, (half + 1) * kc)
            isc = jnp.zeros((kc, qb), F32)
            for pair in range(IDX_HEADS // 2):
                for par in range(2):
                    hd = 2 * pair + par
                    isc = isc + wit[hd:hd + 1, :] * jnp.maximum(sc[par][rs, pair * qb:(pair + 1) * qb], 0.0)
            key_scr[c0 + half] = jnp.where((c0 + half) * kc + krow <= qpos, isc, -jnp.inf)
        return carry

    n_chunks_total = key_scr.shape[0]
    lax.fori_loop(0, (nch + 1) // 2, idx_body, 0)

    def reduce_chunks(tile_fn, init, combine):
        n_acc = SUB_B // SUBLANES

        def body(c, accs):
            accs = list(accs)
            for r0 in range(0, kc, SUB_B):
                t = tile_fn(c, r0, key_scr[c, r0:r0 + SUB_B, :])
                t = t.reshape(n_acc, SUBLANES, qb)
                for i in range(n_acc):
                    accs[i] = combine(accs[i], t[i])
            return tuple(accs)

        accs = lax.fori_loop(0, nch, body, (init,) * n_acc)
        acc = accs[0]
        for a in accs[1:]:
            acc = combine(acc, a)
        return acc

    def count(pred_fn):
        acc = reduce_chunks(lambda c, r0, k: jnp.where(pred_fn(c, r0, k), 1.0, 0.0),
                            jnp.zeros((SUBLANES, qb), F32), lambda a, b: a + b)
        return jnp.sum(acc, axis=0, keepdims=True)

    def unkey(k):
        return lax.bitcast_convert_type(k ^ ((k >> 31) & INT_MAX), F32)

    def tokey(v):
        b = lax.bitcast_convert_type(v, jnp.int32)
        return b ^ ((b >> 31) & INT_MAX)

    def select_all():
        return jnp.full((1, qb), -jnp.inf, F32), jnp.full((1, qb), -1, jnp.int32)

    def select_topk():
        kmax = reduce_chunks(lambda c, r0, k: k, jnp.full((SUBLANES, qb), -jnp.inf, F32), jnp.maximum)
        kmax = jnp.max(kmax, axis=0, keepdims=True)
        kmin = reduce_chunks(lambda c, r0, k: jnp.where(k == -jnp.inf, jnp.inf, k),
                             jnp.full((SUBLANES, qb), jnp.inf, F32), jnp.minimum)
        kmin = jnp.min(kmin, axis=0, keepdims=True)
        c0 = count(lambda c, r0, k: k >= 0.0)
        c0p = count(lambda c, r0, k: k >= F32_TINY)
        n_valid = (jq * qb + 1 + lax.broadcasted_iota(jnp.int32, (1, qb), 1)).astype(F32)
        pos = c0p >= kf
        zero = (c0 >= kf) & (c0p < kf)
        lo0 = jnp.where(pos, F32_TINY, jnp.where(zero, 0.0, kmin))
        hi0 = jnp.where(pos, unkey(tokey(kmax) + 1), jnp.where(zero, F32_TINY, 0.0))
        c_lo0 = jnp.where(pos, c0p, jnp.where(zero, c0, n_valid))
        c_hi0 = jnp.where(pos, 0.0, jnp.where(zero, c0p, c0))

        def bis_cond(st):
            return (st[0] < 64) & (st[5] > 0)

        def bis_step(it, lo, hi, c_lo, c_hi):
            klo, khi = tokey(lo), tokey(hi)
            mid_i = unkey((klo >> 1) + (khi >> 1) + (klo & khi & 1))
            mid_v = 0.5 * lo + 0.5 * hi
            use_v = (it < 24) & (mid_v > lo) & (mid_v < hi)
            mid = jnp.where(use_v, mid_v, mid_i)
            mid = jnp.where((mid > -F32_TINY) & (mid < 0.0), -F32_TINY, mid)
            cnt = count(lambda c, r0, k: k >= mid)
            ge = cnt >= kf
            return (jnp.where(ge, mid, lo), jnp.where(ge, hi, mid),
                    jnp.where(ge, cnt, c_lo), jnp.where(ge, c_hi, cnt))

        def bis_body(st):
            it, lo, hi, c_lo, c_hi, _ = st
            for _ in range(BIS_UNROLL):
                lo, hi, c_lo, c_hi = bis_step(it, lo, hi, c_lo, c_hi)
                it = it + 1
            return it, lo, hi, c_lo, c_hi, jnp.max(jnp.where(is_active(lo, hi, c_lo, c_hi), 1, 0))

        def is_active(lo, hi, c_lo, c_hi):
            gap = (lo == 0.0) | ((hi == 0.0) & (lo >= -F32_TINY))
            return (tokey(lo) + 1 < tokey(hi)) & (c_lo != kf) & (c_lo - c_hi > 2.0) & jnp.logical_not(gap)

        st0 = (jnp.int32(0), lo0, hi0, c_lo0, c_hi0,
               jnp.max(jnp.where(is_active(lo0, hi0, c_lo0, c_hi0), 1, 0)))
        _, lo, hi, c_lo, c_hi, _ = lax.while_loop(bis_cond, bis_body, st0)
        open_rows = c_lo != kf

        def finish():
            top = reduce_chunks(lambda c, r0, k: jnp.where(k < hi, k, -jnp.inf),
                                jnp.full((SUBLANES, qb), -jnp.inf, F32), jnp.maximum)
            top = jnp.max(top, axis=0, keepdims=True)
            bot = reduce_chunks(lambda c, r0, k: jnp.where(k >= lo, k, jnp.inf),
                                jnp.full((SUBLANES, qb), jnp.inf, F32), jnp.minimum)
            bot = jnp.min(bot, axis=0, keepdims=True)
            thr = jnp.where(open_rows, top, lo)
            tie = open_rows & (top == bot)
            need = kf - c_hi

            def tie_break():
                def tie_body(_, st):
                    jlo, jhi = st
                    mid = (jlo + jhi) >> 1
                    cnt = count(lambda c, r0, k: (k == thr) & (c * kc + r0 + krow_sub <= mid))
                    ge = cnt >= need
                    return jnp.where(ge, jlo, mid), jnp.where(ge, mid, jhi)

                idx_bits = (n_chunks_total * kc - 1).bit_length()
                _, jhi = lax.fori_loop(0, idx_bits, tie_body,
                                       (jnp.full((1, qb), -1, jnp.int32),
                                        jnp.full((1, qb), 2 ** idx_bits - 1, jnp.int32)))
                return jnp.where(tie, jhi, INT_MAX)

            jmax = lax.cond(jnp.max(jnp.where(tie, 1, 0)) > 0, tie_break,
                            lambda: jnp.full((1, qb), INT_MAX, jnp.int32))
            return thr, jmax

        return lax.cond(jnp.max(jnp.where(open_rows, 1, 0)) > 0, finish,
                        lambda: (lo, jnp.full((1, qb), INT_MAX, jnp.int32)))

    thr, jmax = lax.cond(jq >= 1, select_topk, select_all)

    def sel_body(c, carry):
        k = key_scr[c]
        sel = (k > thr) | ((k == thr) & (c * kc + krow <= jmax))
        key_scr[c] = jnp.where(sel, 0.0, NEG).T
        return carry

    lax.fori_loop(0, nch, sel_body, 0)

    def kv_chunk(c):
        return ckv_ref[0, pl.ds(pl.multiple_of(c * kc, kc), kc), :]

    def qk(nq, c, slot):
        s_scr[slot, :B_HEADS * nq] = lax.dot_general(
            ql_ref[0, :, :nq, :].reshape(B_HEADS * nq, B_KV_RANK), kv_chunk(c), _NT,
            preferred_element_type=F32)

    def logits(nq, c, slot, near, hd, r0):
        sub = min(SUB_B, nq)
        sh = s_scr[slot, hd * nq + r0:hd * nq + r0 + sub]
        if near == "cur":
            t0 = bias_ref[hd, r0 % QB_A:r0 % QB_A + sub, LANES:]
            t1 = bias_ref[hd, r0 % QB_A:r0 % QB_A + sub, :LANES]
            sh = sh + jnp.concatenate([t0 if r0 < QB_A else t1, t0], axis=1)
        elif near == "prev" and r0 < QB_A:
            t1 = bias_ref[hd, r0:r0 + sub, :LANES]
            sh = sh + jnp.concatenate([jnp.zeros_like(t1), t1], axis=1)
        return sh + key_scr[c, r0:r0 + sub, :]

    def sub_tiles(nq):
        sub = min(SUB_B, nq)
        for hd in range(B_HEADS):
            for r0 in range(0, nq, sub):
                yield hd, r0, slice(hd * nq + r0, hd * nq + r0 + sub)

    def sm(nq, c, slot, near):
        for hd, r0, sl in sub_tiles(nq):
            m = m_scr[sl]
            p = jnp.exp2(logits(nq, c, slot, near, hd, r0) - jnp.concatenate([m, m], axis=1))
            l_scr[sl] += p[:, :LANES] + p[:, LANES:]
            p_scr[slot, sl] = p.astype(BF16)

    def pv(nq, c, slot):
        o = jnp.dot(p_scr[slot, :B_HEADS * nq], kv_chunk(c), preferred_element_type=F32)
        sub = min(SUB_B, nq)
        for r0 in range(0, B_HEADS * nq, sub):
            acc_scr[r0:r0 + sub] += o[r0:r0 + sub]

    def bound_attention(nq):
        l_scr[...] = jnp.zeros(l_scr.shape, F32)
        acc_scr[...] = jnp.zeros(acc_scr.shape, F32)
        kn2 = lax.fori_loop(0, nch, lambda c, a: jnp.maximum(a, kn2_ref[0, c]),
                            jnp.zeros((SUBLANES, LANES), F32))
        bmax = bmax_ref[0]
        for hd, r0, sl in sub_tiles(nq):
            x = ql_ref[0, hd, r0:r0 + min(SUB_B, nq), :].astype(F32)
            qn2 = jnp.sum(x * x, axis=-1, keepdims=True)
            m_scr[sl] = jnp.sqrt(qn2 * kn2[:1]) * (1.0 + 2.0 ** -10) + bmax

        def stage(t, slot):
            qk(nq, t, slot)
            pv(nq, t - 2, slot)
            sm(nq, t - 1, 1 - slot, None)

        def tail(slot):
            qk(nq, jq, slot)
            pv(nq, jq - 2, slot)
            sm(nq, jq - 1, 1 - slot, "prev")
            pv(nq, jq - 1, 1 - slot)
            sm(nq, jq, slot, "cur")
            pv(nq, jq, slot)

        @pl.when(jq >= 2)
        def _():
            qk(nq, 0, 0)
            qk(nq, 1, 1)
            sm(nq, 0, 0, None)

            def two_stages(i, carry):
                stage(2 + 2 * i, 0)
                stage(3 + 2 * i, 1)
                return carry

            lax.fori_loop(0, (jq - 2) // 2, two_stages, 0)

            @pl.when(jq % 2 == 0)
            def _():
                tail(0)

            @pl.when(jq % 2 == 1)
            def _():
                stage(jq - 1, 0)
                tail(1)

        @pl.when(jq < 2)
        def _():
            @pl.when(jq == 1)
            def _():
                qk(nq, 0, 0)
                sm(nq, 0, 0, "prev")
                pv(nq, 0, 0)

            qk(nq, jq, 1)
            sm(nq, jq, 1, "cur")
            pv(nq, jq, 1)

    def running_max_attention():
        l_scr[...] = jnp.zeros(l_scr.shape, F32)
        acc_scr[...] = jnp.zeros(acc_scr.shape, F32)
        m_scr[...] = jnp.full(m_scr.shape, -jnp.inf, F32)

        def attend(c, near):
            qk(qb, c, 0)
            for hd, r0, sl in sub_tiles(qb):
                m_old = m_scr[sl]
                m_new = jnp.maximum(m_old, jnp.max(logits(qb, c, 0, near, hd, r0), axis=-1, keepdims=True))
                m_scr[sl] = m_new
                alpha_scr[sl] = jnp.exp2(m_old - m_new)
            for hd, r0, sl in sub_tiles(qb):
                m_new = m_scr[sl]
                p = jnp.exp2(logits(qb, c, 0, near, hd, r0) - jnp.concatenate([m_new, m_new], axis=1))
                l_scr[sl] = alpha_scr[sl] * l_scr[sl] + (p[:, :LANES] + p[:, LANES:])
                p_scr[0, sl] = p.astype(BF16)
            o = jnp.dot(p_scr[0], kv_chunk(c), preferred_element_type=F32)
            for r0 in range(0, rows, SUB_B):
                sl = slice(r0, r0 + SUB_B)
                alpha = alpha_scr[sl]
                acc_scr[sl] = acc_scr[sl] * jnp.concatenate([alpha, alpha], axis=1) + o[sl]

        def far_body(c, carry):
            attend(c, None)
            return carry

        lax.fori_loop(0, jq - 1, far_body, 0)

        @pl.when(jq >= 1)
        def _():
            attend(jq - 1, "prev")

        attend(jq, "cur")

    def write_output(nq):
        l_min = jnp.full((nq, 1), jnp.inf, F32)
        for pair in range(B_HEADS // 2):
            col = slice(pair * LANES, (pair + 1) * LANES)
            y2 = jnp.zeros((nq, LANES), F32)
            for par in range(2):
                hd = 2 * pair + par
                sl = slice(hd * nq, (hd + 1) * nq)
                l = jnp.sum(l_scr[sl], axis=-1, keepdims=True)
                l_min = jnp.minimum(l_min, l)
                o_lat = acc_scr[sl] / l
                y2 = y2 + jnp.dot(o_lat.astype(BF16), wuv_ref[pair, par], preferred_element_type=F32)
            o_ref[:nq, col] = (y2 * _silu(zb_ref[:nq, col])).astype(o_ref.dtype)
            if nq < qb:
                o_ref[nq:, col] = jnp.zeros((qb - nq, LANES), o_ref.dtype)
        safe = jnp.min(l_min) >= L_SAFE
        flag_ref[0, 0] = jnp.full(flag_ref.shape[2:], jnp.where(safe, 0.0, 1.0), F32)

    if not bound:
        running_max_attention()
        write_output(qb)
    elif tail_q == qb:
        bound_attention(qb)
        write_output(qb)
    else:
        last = pl.num_programs(1) - 1

        @pl.when(jq < last)
        def _():
            bound_attention(qb)
            write_output(qb)

        @pl.when(jq == last)
        def _():
            bound_attention(tail_q)
            write_output(tail_q)


def _mixer_b(bmax, ql, qi, wi, ckvn, klo, khi, kn2, proj, bias_b, wuv_pairs, layer, n_batch, lp, length,
             bound):
    nb = lp // QB_B
    tail_q = min(QB_B, -(-(length - (nb - 1) * QB_B) // 16) * 16)
    rows = n_batch * lp
    whole = lambda b, j: (b, 0, 0)
    once = pl.Buffered(1)
    return pl.pallas_call(
        functools.partial(_mixer_b_kernel, bound=bound, tail_q=tail_q),
        grid=(n_batch, nb),
        in_specs=[
            pl.BlockSpec(memory_space=pltpu.SMEM),
            pl.BlockSpec((1, B_HEADS, QB_B, B_KV_RANK), lambda b, j: (b, 0, j, 0)),
            pl.BlockSpec((1, IDX_HEADS // 2, QB_B, LANES), lambda b, j: (b, 0, j, 0)),
            pl.BlockSpec((1, QB_B, LANES), lambda b, j: (b, j, 0)),
            pl.BlockSpec((1, lp, B_KV_RANK), whole, pipeline_mode=once),
            pl.BlockSpec((1, lp, LANES), whole, pipeline_mode=once),
            pl.BlockSpec((1, lp, LANES), whole, pipeline_mode=once),
            pl.BlockSpec((1, nb, SUBLANES, LANES), lambda b, j: (b, 0, 0, 0), pipeline_mode=once),
            pl.BlockSpec((QB_B, B_WIDTH), lambda b, j: (b * nb + j, C_ZB // B_WIDTH)),
            pl.BlockSpec((B_HEADS, QB_A, 2 * LANES), lambda b, j: (0, 0, 0), pipeline_mode=once),
            pl.BlockSpec((None, B_HEADS // 2, 2, B_KV_RANK, LANES), lambda b, j: (layer, 0, 0, 0, 0),
                         pipeline_mode=once),
        ],
        out_specs=[pl.BlockSpec((QB_B, B_WIDTH), lambda b, j: (b * nb + j, 0)),
                   pl.BlockSpec((1, 1, SUBLANES, LANES), lambda b, j: (b, j, 0, 0))],
        out_shape=[jax.ShapeDtypeStruct((rows, B_WIDTH), BF16),
                   jax.ShapeDtypeStruct((n_batch, nb, SUBLANES, LANES), F32)],
        scratch_shapes=[
            pltpu.VMEM((nb, QB_B, QB_B), F32),
            pltpu.VMEM((2, B_HEADS * QB_B, QB_B), F32),
            pltpu.VMEM((B_HEADS * QB_B, B_KV_RANK), F32),
            pltpu.VMEM((B_HEADS * QB_B, LANES), F32),
            pltpu.VMEM((B_HEADS * QB_B, LANES), F32),
            pltpu.VMEM((B_HEADS * QB_B, LANES), F32),
            pltpu.VMEM((2, B_HEADS * QB_B, QB_B), BF16),
        ],
        compiler_params=_cparams(2),
        name="mixer_b" if bound else "mixer_b_running_max",
    )(bmax, ql, qi, wi, ckvn, klo, khi, kn2, proj, bias_b, wuv_pairs)


def _merge_kernel(ya_ref, yb_ref, ga_ref, gb_ref, h_ref, wpa_ref, wpb_ref, wo_ref, g_ref, *out_refs,
                  last):
    pa = jnp.dot(ya_ref[...], wpa_ref[...], preferred_element_type=F32)
    pb = jnp.dot(yb_ref[...], wpb_ref[...], preferred_element_type=F32)
    merged = _sigmoid(ga_ref[...]) * pa + _sigmoid(gb_ref[...]) * pb
    h_new = h_ref[...] + jnp.dot(merged.astype(BF16), wo_ref[...], preferred_element_type=F32)
    normed = _rms(h_new, g_ref[...])
    if last:
        out_refs[0][...] = normed
    else:
        out_refs[0][...] = h_new
        out_refs[1][...] = normed.astype(BF16)


def _merge(ya, yb, gates, h, wpa, wpb, wo, g_next, layer, last, tm=256):
    rows, d = h.shape
    const = lambda i: (0, 0)
    lay = lambda i: (layer, 0, 0)
    once = pl.Buffered(1)
    if last:
        out_specs = [pl.BlockSpec((tm, d), lambda i: (i, 0))]
        out_shape = [jax.ShapeDtypeStruct((rows, d), F32)]
    else:
        out_specs = [pl.BlockSpec((tm, d), lambda i: (i, 0)), pl.BlockSpec((tm, d), lambda i: (i, 0))]
        out_shape = [jax.ShapeDtypeStruct((rows, d), F32), jax.ShapeDtypeStruct((rows, d), BF16)]
    return pl.pallas_call(
        functools.partial(_merge_kernel, last=last),
        grid=(rows // tm,),
        in_specs=[
            pl.BlockSpec((tm, A_WIDTH), lambda i: (i, 0)),
            pl.BlockSpec((tm, B_WIDTH), lambda i: (i, 0)),
            pl.BlockSpec((tm, d), lambda i: (i, 0)),
            pl.BlockSpec((tm, d), lambda i: (i, 1)),
            pl.BlockSpec((tm, d), lambda i: (i, 0)),
            pl.BlockSpec((None, A_WIDTH, d), lay, pipeline_mode=once),
            pl.BlockSpec((None, B_WIDTH, d), lay, pipeline_mode=once),
            pl.BlockSpec((None, d, d), lay, pipeline_mode=once),
            pl.BlockSpec((1, d), const),
        ],
        out_specs=out_specs,
        out_shape=out_shape,
        compiler_params=_cparams(1),
        name="merge_out",
    )(ya, yb, gates, gates, h, wpa, wpb, wo, g_next.reshape(1, d))


def _pack_weights(w_in, w_uk, w_uv, w_qb, w_iq, w_proj_a, w_proj_b, w_out):
    depth = w_in.shape[0]
    sec = lambda a, b: w_in[:, :, a:b]
    kidx, widx = sec(4096, 4160), sec(4160, 4168)
    z64 = jnp.zeros_like(kidx)
    zpad = jnp.zeros(widx.shape[:2] + (LANES - IDX_HEADS,), w_in.dtype)
    w_small = jnp.concatenate([kidx, z64, z64, kidx, widx, zpad], axis=-1).astype(BF16)
    uk = (w_uk * (HEAD_DIM ** -0.5 * LOG2E)).reshape(depth, B_HEADS // 2, 2, HEAD_DIM, B_KV_RANK)
    zuk = jnp.zeros_like(uk[:, :, 0])
    wuk_bd = jnp.concatenate([jnp.concatenate([uk[:, :, 0], zuk], axis=-1),
                              jnp.concatenate([zuk, uk[:, :, 1]], axis=-1)], axis=-2).astype(BF16)
    uv = w_uv.reshape(depth, B_HEADS // 2, 2, B_KV_RANK, HEAD_DIM)
    zuv = jnp.zeros_like(uv[:, :, 0])
    wuv_pairs = jnp.stack([jnp.concatenate([uv[:, :, 0], zuv], axis=-1),
                           jnp.concatenate([zuv, uv[:, :, 1]], axis=-1)], axis=2).astype(BF16)
    w_iq8 = (w_iq * (IDX_DIM ** -0.5)).astype(BF16)
    return (w_small, wuk_bd, wuv_pairs, w_qb.astype(BF16), w_iq8,
            w_proj_a.astype(BF16), w_proj_b.astype(BF16), w_out.astype(BF16))


def kernel(x, meta_tokens, bias_table, norm_g, w_in, q_norm_g, kv_norm_g, w_qb, w_iq, w_uk, w_uv,
           sinks, w_proj_a, w_proj_b, w_out, final_g):
    n_batch, seq, d = x.shape
    depth = w_in.shape[0]
    length = seq + N_META
    lp = -(-length // QB_B) * QB_B
    assert min(TOPK_MAX, seq // 4) == TOPK_MAX and d == D_MODEL
    assert WINDOW == QB_A and lp // QB_B >= 3

    (w_small, wuk_bd, wuv_pairs, wqb, wiq, wpa, wpb, wo) = _pack_weights(
        w_in, w_uk, w_uv, w_qb, w_iq, w_proj_a, w_proj_b, w_out)
    w_t = jnp.swapaxes(w_in, 1, 2)
    bias_a, bias_b = _bias_tiles(bias_table)
    bmax = jnp.max(jnp.abs(bias_b)).reshape(1)

    meta = jnp.broadcast_to(meta_tokens.astype(x.dtype)[None], (n_batch, N_META, d))
    pad = jnp.zeros((n_batch, lp - length, d), x.dtype)
    h = jnp.concatenate([meta, x, pad], axis=1).reshape(n_batch * lp, d)

    u = _rmsnorm(h, norm_g[0])
    out = None
    for l in range(depth):
        proj = _in_proj(u, w_t, l)
        gates = _gate_proj(u, w_t, l, GATE_COL0, 2 * D_MODEL)
        ya = _mixer_a(proj, sinks[l], bias_a, n_batch, lp)
        ql, qi, wi, ckvn, klo, khi, kn2 = _bprep(
            proj, u, q_norm_g[l].reshape(1, -1), kv_norm_g[l].reshape(1, -1),
            wqb, wiq, wuk_bd, w_small, l, n_batch, lp)
        b_args = (bmax, ql, qi, wi, ckvn, klo, khi, kn2, proj, bias_b, wuv_pairs, l, n_batch, lp, length)
        yb, flags = _mixer_b(*b_args, bound=True)
        yb = lax.cond(jnp.max(flags) > 0.0, lambda: _mixer_b(*b_args, bound=False)[0], lambda: yb)
        last = l == depth - 1
        g_next = final_g if last else norm_g[l + 1]
        res = _merge(ya, yb, gates, h, wpa, wpb, wo, g_next, l, last)
        if last:
            out = res[0]
        else:
            h, u = res
    return out.reshape(n_batch, lp, d)[:, N_META:length]
```

```python
import functools
import math

import numpy as np
import jax
import jax.numpy as jnp
from jax import lax
from jax.experimental import pallas as pl
from jax.experimental.pallas import tpu as pltpu

D_MODEL = 2048
N_META = 16
WINDOW = 128
HEAD_DIM = 64
A_HEADS = 16
A_WIDTH = 1024
B_HEADS = 16
B_WIDTH = 1024
B_Q_RANK = 512
B_KV_RANK = 256
IDX_HEADS = 8
IDX_DIM = 64
TOPK_MAX = 256
N_BUCKETS = 32
MAX_DISTANCE = 128
EPS = 1e-6
NEG = -1e30

LANES = 128
SUBLANES = 8
PROJ_ROW_TILES = 8
QB_A = 128
QB_B = 256
SUB_A = 32
SUB_B = 32
L_SAFE = 2.0 ** -100
BIS_UNROLL = 3
LOG2E = math.log2(math.e)
VMEM_LIMIT = 56 * 1024 * 1024
INT_MAX = 2 ** 31 - 1
F32_TINY = 2.0 ** -126

C_QA, C_ZA, C_ZB, C_CQ, C_CKV, C_KA, C_VA = (0, 1024, 2048, 3072, 3584, 3840, 3968)
GATE_COL0 = 4168

F32 = jnp.float32
BF16 = jnp.bfloat16
_NT = (((1,), (1,)), ((), ()))


def _cparams(n_grid):
    return pltpu.CompilerParams(
        dimension_semantics=("arbitrary",) * n_grid,
        vmem_limit_bytes=VMEM_LIMIT)


def _t5_bucket_np(d):
    max_exact = N_BUCKETS // 2
    nf = np.maximum(d, 1).astype(np.float32)
    large = max_exact + (np.log(nf / np.float32(max_exact)) / np.float32(math.log(MAX_DISTANCE / max_exact))
                         * np.float32(N_BUCKETS - max_exact)).astype(np.int32)
    large = np.minimum(large, N_BUCKETS - 1)
    return np.where(d < max_exact, d, large).astype(np.int32)


def _bias_tiles_kernel(tab_ref, idx_ref, out_a_ref, out_b_ref):
    h = pl.program_id(0)
    idx = idx_ref[...]
    acc_a = jnp.zeros(idx.shape, F32)
    acc_b = jnp.zeros(idx.shape, F32)
    far_b = tab_ref[N_BUCKETS - 1, A_HEADS + h]
    for b in range(N_BUCKETS):
        hit = idx == b
        acc_a = jnp.where(hit, tab_ref[b, h] * LOG2E, acc_a)
        acc_b = jnp.where(hit, (tab_ref[b, A_HEADS + h] - far_b) * LOG2E, acc_b)
    out_a_ref[0] = acc_a
    out_b_ref[0] = acc_b[:, :2 * LANES]


def _bias_tiles(bias_table):
    r = np.arange(QB_A)[:, None]
    k = np.arange(QB_A)[None, :]
    prev = _t5_bucket_np(np.maximum(QB_A + r - k, 0))
    cur = _t5_bucket_np(np.maximum(r - k, 0))
    far = np.full((QB_A, QB_A), N_BUCKETS - 1, np.int32)
    idx = jnp.asarray(np.concatenate([prev, cur, far], axis=1))
    return pl.pallas_call(
        _bias_tiles_kernel,
        grid=(A_HEADS,),
        in_specs=[pl.BlockSpec(memory_space=pltpu.SMEM),
                  pl.BlockSpec((QB_A, 3 * LANES), lambda h: (0, 0))],
        out_specs=[pl.BlockSpec((1, QB_A, 3 * LANES), lambda h: (h, 0, 0)),
                   pl.BlockSpec((1, QB_A, 2 * LANES), lambda h: (h, 0, 0))],
        out_shape=[jax.ShapeDtypeStruct((A_HEADS, QB_A, 3 * LANES), F32),
                   jax.ShapeDtypeStruct((B_HEADS, QB_A, 2 * LANES), F32)],
        compiler_params=_cparams(1),
        name="bias_tiles",
    )(bias_table, idx)


def _rms(x, g):
    return x * lax.rsqrt(jnp.mean(x * x, axis=-1, keepdims=True) + EPS) * g


def _rmsnorm_kernel(h_ref, g_ref, u_ref):
    u_ref[...] = _rms(h_ref[...], g_ref[...]).astype(u_ref.dtype)


def _rmsnorm(h, g, tm=512):
    rows, d = h.shape
    return pl.pallas_call(
        _rmsnorm_kernel,
        grid=(rows // tm,),
        in_specs=[pl.BlockSpec((tm, d), lambda i: (i, 0)),
                  pl.BlockSpec((1, d), lambda i: (0, 0))],
        out_specs=pl.BlockSpec((tm, d), lambda i: (i, 0)),
        out_shape=jax.ShapeDtypeStruct((rows, d), BF16),
        compiler_params=_cparams(1),
        name="rmsnorm",
    )(h, g.reshape(1, d))


_IN_BLOCKS = ((0, 1, 2, 3), (5, 6, 7, 8), (12, 13, 14, 15), (9, 10, 11, 4))
IN_BLK = 256


def _in_proj_kernel(x_ref, *refs):
    w_refs, o_ref, wbf_scr = refs[:-2], refs[-2], refs[-1]

    @pl.when(pl.program_id(1) == 0)
    def _():
        r0 = 0
        for w_ref in w_refs:
            wbf_scr[r0:r0 + w_ref.shape[0], :] = w_ref[...].astype(BF16)
            r0 += w_ref.shape[0]

    o_ref[...] = lax.dot_general(x_ref[...], wbf_scr[...], _NT, preferred_element_type=F32)


def _in_proj(x, w_t, layer):
    rows, kdim = x.shape
    tm = rows // PROJ_ROW_TILES
    tn = 4 * IN_BLK

    n_packed = len(_IN_BLOCKS)
    n_gate = 2 * D_MODEL // tn

    def w_spec(s):
        def index(j, i):
            off = jnp.int32(IN_BLK * _IN_BLOCKS[0][s])
            for t in range(1, n_packed):
                off = jnp.where(j == t, IN_BLK * _IN_BLOCKS[t][s], off)
            off = jnp.where(j >= n_packed, GATE_COL0 + (j - n_packed) * tn + s * IN_BLK, off)
            return layer, pl.multiple_of(off, SUBLANES), 0
        return pl.BlockSpec((None, pl.Element(IN_BLK), pl.Element(kdim)), index)

    return pl.pallas_call(
        _in_proj_kernel,
        grid=(n_packed + n_gate, rows // tm),
        in_specs=[pl.BlockSpec((tm, kdim), lambda j, i: (i, 0))] + [w_spec(s) for s in range(4)],
        out_specs=pl.BlockSpec((tm, tn), lambda j, i: (i, j)),
        out_shape=jax.ShapeDtypeStruct((rows, (n_packed + n_gate) * tn), F32),
        scratch_shapes=[pltpu.VMEM((tn, kdim), BF16)],
        compiler_params=_cparams(2),
        name="in_proj",
    )(x, w_t, w_t, w_t, w_t)


def _gate_proj(x, w_t, layer, col0, n, tn=1024):
    rows, kdim = x.shape
    tm = rows // PROJ_ROW_TILES
    return pl.pallas_call(
        _in_proj_kernel,
        grid=(n // tn, rows // tm),
        in_specs=[pl.BlockSpec((tm, kdim), lambda j, i: (i, 0)),
                  pl.BlockSpec((None, pl.Element(tn), pl.Element(kdim)),
                               lambda j, i: (layer, pl.multiple_of(col0 + j * tn, 8), 0))],
        out_specs=pl.BlockSpec((tm, tn), lambda j, i: (i, j)),
        out_shape=jax.ShapeDtypeStruct((rows, n), F32),
        scratch_shapes=[pltpu.VMEM((tn, kdim), BF16)],
        compiler_params=_cparams(2),
        name="gate_proj",
    )(x, w_t)


def _silu(z):
    return z / (1.0 + jnp.exp(-z))


def _sigmoid(z):
    return 1.0 / (1.0 + jnp.exp(-z))


def _mixer_a_kernel(sink_ref, q_ref, kc_ref, vc_ref, kp_ref, vp_ref, km_ref, vm_ref, za_ref,
                    bias_ref, o_ref, s_scr, p_scr, m_scr, d_scr, msk_scr):
    j = pl.program_id(1)
    nk = 3 * QB_A
    q = (q_ref[...] * (HEAD_DIM ** -0.5 * LOG2E)).astype(BF16)
    kcat = jnp.concatenate([kp_ref[...], kc_ref[...], km_ref[...]], axis=0)
    vcat = jnp.concatenate([vp_ref[...], vc_ref[...], vm_ref[...]], axis=0)
    lane = lax.broadcasted_iota(jnp.int32, (nk, LANES), 1)
    lo = lane < HEAD_DIM
    kswap = pltpu.roll(kcat, HEAD_DIM, 1)
    vswap = pltpu.roll(vcat, HEAD_DIM, 1)
    zero = jnp.zeros_like(kcat)
    k_half = ((jnp.where(lo, kcat, zero).astype(BF16), jnp.where(lo, zero, kswap).astype(BF16)),
              (jnp.where(lo, kswap, zero).astype(BF16), jnp.where(lo, zero, kcat).astype(BF16)))
    v_half = ((jnp.where(lo, vcat, zero).astype(BF16), jnp.where(lo, zero, vswap).astype(BF16)),
              (jnp.where(lo, vswap, zero).astype(BF16), jnp.where(lo, zero, vcat).astype(BF16)))

    r = lax.broadcasted_iota(jnp.int32, (QB_A, nk), 0)
    c = lax.broadcasted_iota(jnp.int32, (QB_A, nk), 1)
    jv = jnp.zeros((QB_A, nk), jnp.int32) + j
    in_prev = (c < QB_A) & (jv >= 1) & ((c > r) | ((jv == 1) & (c < N_META)))
    in_cur = (c >= QB_A) & (c < 2 * QB_A) & (c - QB_A <= r)
    in_meta = (c >= 2 * QB_A) & (jv >= 2) & (c - 2 * QB_A < N_META)
    msk_scr[...] = jnp.where(in_prev | in_cur | in_meta, 0.0, NEG)
    half = lax.broadcasted_iota(jnp.int32, (QB_A, LANES), 1) < HEAD_DIM
    pairs = A_HEADS // 4

    def logits(g, i, par, r0):
        hd = 2 * (g * pairs + i) + par
        s = s_scr[i * QB_A + r0:i * QB_A + r0 + SUB_A, par * nk:(par + 1) * nk]
        return s + bias_ref[hd, r0:r0 + SUB_A, :] + msk_scr[r0:r0 + SUB_A, :], sink_ref[hd] * LOG2E

    for g in range(2):
        q4 = jnp.concatenate([q[:, (g * pairs + i) * LANES:(g * pairs + i + 1) * LANES]
                              for i in range(pairs)], axis=0)
        s_scr[...] = lax.dot_general(q4, jnp.concatenate(k_half[g], axis=0), _NT,
                                     preferred_element_type=F32)
        for i in range(pairs):
            for par in range(2):
                for r0 in range(0, QB_A, SUB_A):
                    lg, sink = logits(g, i, par, r0)
                    m = jnp.maximum(jnp.max(lg, axis=-1, keepdims=True), sink)
                    m_scr[par, i * QB_A + r0:i * QB_A + r0 + SUB_A] = jnp.broadcast_to(m, (SUB_A, LANES))
        for i in range(pairs):
            for par in range(2):
                for r0 in range(0, QB_A, SUB_A):
                    sl = slice(i * QB_A + r0, i * QB_A + r0 + SUB_A)
                    lg, sink = logits(g, i, par, r0)
                    m = m_scr[par, sl]
                    p = jnp.exp2(lg - jnp.concatenate([m] * 3, axis=1))
                    d_scr[par, sl] = jnp.broadcast_to(
                        jnp.sum(p, axis=-1, keepdims=True), (SUB_A, LANES)) + jnp.exp2(sink - m)
                    p_scr[sl, par * nk:(par + 1) * nk] = p.astype(BF16)
        pv = jnp.dot(p_scr[...], jnp.concatenate(v_half[g], axis=0), preferred_element_type=F32)
        for i in range(pairs):
            sl = slice(i * QB_A, (i + 1) * QB_A)
            col = slice((g * pairs + i) * LANES, (g * pairs + i + 1) * LANES)
            o2 = pv[sl] / jnp.where(half, d_scr[0, sl], d_scr[1, sl])
            o_ref[:, col] = (o2 * _silu(za_ref[:, col])).astype(o_ref.dtype)


def _mixer_a(proj, sinks, bias_a, n_batch, lp):
    nb = lp // QB_A
    rows = n_batch * lp
    ck, cv = C_KA // LANES, C_VA // LANES
    cur = lambda b, j: b * nb + j
    prev = lambda b, j: b * nb + jnp.maximum(j - 1, 0)
    first = lambda b, j: b * nb
    return pl.pallas_call(
        _mixer_a_kernel,
        grid=(n_batch, nb),
        in_specs=[
            pl.BlockSpec(memory_space=pltpu.SMEM),
            pl.BlockSpec((QB_A, A_WIDTH), lambda b, j: (cur(b, j), C_QA // A_WIDTH)),
            pl.BlockSpec((QB_A, LANES), lambda b, j: (cur(b, j), ck)),
            pl.BlockSpec((QB_A, LANES), lambda b, j: (cur(b, j), cv)),
            pl.BlockSpec((QB_A, LANES), lambda b, j: (prev(b, j), ck)),
            pl.BlockSpec((QB_A, LANES), lambda b, j: (prev(b, j), cv)),
            pl.BlockSpec((QB_A, LANES), lambda b, j: (first(b, j), ck)),
            pl.BlockSpec((QB_A, LANES), lambda b, j: (first(b, j), cv)),
            pl.BlockSpec((QB_A, A_WIDTH), lambda b, j: (cur(b, j), C_ZA // A_WIDTH)),
            pl.BlockSpec((A_HEADS, QB_A, 3 * LANES), lambda b, j: (0, 0, 0)),
        ],
        out_specs=pl.BlockSpec((QB_A, A_WIDTH), lambda b, j: (cur(b, j), 0)),
        out_shape=jax.ShapeDtypeStruct((rows, A_WIDTH), BF16),
        scratch_shapes=[
            pltpu.VMEM((A_HEADS // 4 * QB_A, 6 * QB_A), F32),
            pltpu.VMEM((A_HEADS // 4 * QB_A, 6 * QB_A), BF16),
            pltpu.VMEM((2, A_HEADS // 4 * QB_A, LANES), F32),
            pltpu.VMEM((2, A_HEADS // 4 * QB_A, LANES), F32),
            pltpu.VMEM((QB_A, 3 * QB_A), F32),
        ],
        compiler_params=_cparams(2),
        name="mixer_a",
    )(sinks, proj, proj, proj, proj, proj, proj, proj, proj, bias_a)


def _bprep_kernel(cq_ref, ckv_ref, u_ref, qg_ref, kvg_ref, wqb_ref, wiq_ref, wuk_ref, wsm_ref,
                  ql_ref, qi_ref, wi_ref, ckvn_ref, klo_ref, khi_ref, kn2_ref):
    nbt, tm = cq_ref.shape[0], cq_ref.shape[1]
    rows = nbt * tm
    per_batch = lambda x, b: x[b * tm:(b + 1) * tm]
    cqn = _rms(cq_ref[...].reshape(rows, -1), qg_ref[...]).astype(BF16)
    ckvn = _rms(ckv_ref[...].reshape(rows, -1), kvg_ref[...]).astype(ckvn_ref.dtype)
    kn2 = jnp.sum(jnp.square(ckvn.astype(F32)), axis=-1, keepdims=True)
    qb = jnp.dot(cqn, wqb_ref[...], preferred_element_type=F32).astype(BF16)
    qi = jnp.dot(cqn, wiq_ref[...], preferred_element_type=F32)
    small = jnp.dot(u_ref[...].reshape(rows, -1), wsm_ref[...], preferred_element_type=F32)
    for b in range(nbt):
        ckvn_ref[b] = per_batch(ckvn, b)
        kn2_ref[b, 0] = jnp.broadcast_to(jnp.max(per_batch(kn2, b), axis=0, keepdims=True), kn2_ref.shape[2:])
        for pair in range(IDX_HEADS // 2):
            qi_ref[b, pair] = per_batch(qi, b)[:, pair * LANES:(pair + 1) * LANES].astype(qi_ref.dtype)
        klo_ref[b] = per_batch(small, b)[:, :LANES].astype(klo_ref.dtype)
        khi_ref[b] = per_batch(small, b)[:, LANES:2 * LANES].astype(khi_ref.dtype)
        wi_ref[b] = per_batch(small, b)[:, 2 * LANES:] * (IDX_HEADS ** -0.5)
    for pair in range(B_HEADS // 2):
        ql2 = jnp.dot(qb[:, pair * LANES:(pair + 1) * LANES], wuk_ref[pair],
                      preferred_element_type=F32)
        for b in range(nbt):
            ql_ref[b, 2 * pair] = per_batch(ql2, b)[:, :B_KV_RANK].astype(ql_ref.dtype)
            ql_ref[b, 2 * pair + 1] = per_batch(ql2, b)[:, B_KV_RANK:].astype(ql_ref.dtype)


def _bprep(proj, u, qg, kvg, wqb, wiq, wuk_bd, wsmall, layer, n_batch, lp, tm=QB_B):
    nt = lp // tm
    const2 = lambda i: (0, 0)
    lay2 = lambda i: (layer, 0, 0)
    proj3 = proj.reshape(n_batch, lp, proj.shape[-1])
    return pl.pallas_call(
        _bprep_kernel,
        grid=(nt,),
        in_specs=[
            pl.BlockSpec((n_batch, tm, B_Q_RANK), lambda i: (0, i, C_CQ // B_Q_RANK)),
            pl.BlockSpec((n_batch, tm, B_KV_RANK), lambda i: (0, i, C_CKV // B_KV_RANK)),
            pl.BlockSpec((n_batch, tm, D_MODEL), lambda i: (0, i, 0)),
            pl.BlockSpec((1, B_Q_RANK), const2),
            pl.BlockSpec((1, B_KV_RANK), const2),
            pl.BlockSpec((None, B_Q_RANK, B_WIDTH), lay2),
            pl.BlockSpec((None, B_Q_RANK, IDX_HEADS * IDX_DIM), lay2),
            pl.BlockSpec((None, B_HEADS // 2, LANES, 2 * B_KV_RANK), lambda i: (layer, 0, 0, 0)),
            pl.BlockSpec((None, D_MODEL, 3 * LANES), lay2),
        ],
        out_specs=[
            pl.BlockSpec((n_batch, B_HEADS, tm, B_KV_RANK), lambda i: (0, 0, i, 0)),
            pl.BlockSpec((n_batch, IDX_HEADS // 2, tm, LANES), lambda i: (0, 0, i, 0)),
            pl.BlockSpec((n_batch, tm, LANES), lambda i: (0, i, 0)),
            pl.BlockSpec((n_batch, tm, B_KV_RANK), lambda i: (0, i, 0)),
            pl.BlockSpec((n_batch, tm, LANES), lambda i: (0, i, 0)),
            pl.BlockSpec((n_batch, tm, LANES), lambda i: (0, i, 0)),
            pl.BlockSpec((n_batch, 1, SUBLANES, LANES), lambda i: (0, i, 0, 0)),
        ],
        out_shape=[
            jax.ShapeDtypeStruct((n_batch, B_HEADS, lp, B_KV_RANK), BF16),
            jax.ShapeDtypeStruct((n_batch, IDX_HEADS // 2, lp, LANES), BF16),
            jax.ShapeDtypeStruct((n_batch, lp, LANES), F32),
            jax.ShapeDtypeStruct((n_batch, lp, B_KV_RANK), BF16),
            jax.ShapeDtypeStruct((n_batch, lp, LANES), BF16),
            jax.ShapeDtypeStruct((n_batch, lp, LANES), BF16),
            jax.ShapeDtypeStruct((n_batch, nt, SUBLANES, LANES), F32),
        ],
        compiler_params=_cparams(1),
        name="bprep",
    )(proj3, proj3, u.reshape(n_batch, lp, u.shape[-1]), qg, kvg, wqb, wiq, wuk_bd, wsmall)


def _mixer_b_kernel(bmax_ref, ql_ref, qi_ref, wi_ref, ckv_ref, klo_ref, khi_ref, kn2_ref, zb_ref, bias_ref,
                    wuv_ref, o_ref, flag_ref, key_scr, s_scr, acc_scr, m_scr, l_scr, alpha_scr, p_scr,
                    *, bound, tail_q):
    jq = pl.program_id(1)
    nch = jq + 1
    qb, kc = QB_B, QB_B
    rows = B_HEADS * qb
    kf = float(TOPK_MAX)

    qi = qi_ref[0].reshape(IDX_HEADS // 2 * qb, LANES)
    wit = wi_ref[0].T
    krow = lax.broadcasted_iota(jnp.int32, (qb, kc), 0)
    krow_sub = lax.broadcasted_iota(jnp.int32, (SUB_B, kc), 0)
    qpos = jq * qb + lax.broadcasted_iota(jnp.int32, (qb, kc), 1)

    def idx_body(i, carry):
        c0 = jnp.minimum(2 * i, n_chunks_total - 2)
        start = pl.multiple_of(c0 * kc, kc)
        sc = (lax.dot_general(klo_ref[0, pl.ds(start, 2 * kc), :], qi, _NT, preferred_element_type=F32),
              lax.dot_general(khi_ref[0, pl.ds(start, 2 * kc), :], qi, _NT, preferred_element_type=F32))
        for half in range(2):
            rs = slice(half * kc, (half + 1) * kc)
            isc = jnp.zeros((kc, qb), F32)
            for pair in range(IDX_HEADS // 2):
                for par in range(2):
                    hd = 2 * pair + par
                    isc = isc + wit[hd:hd + 1, :] * jnp.maximum(sc[par][rs, pair * qb:(pair + 1) * qb], 0.0)
            key_scr[c0 + half] = jnp.where((c0 + half) * kc + krow <= qpos, isc, -jnp.inf)
        return carry

    n_chunks_total = key_scr.shape[0]
    lax.fori_loop(0, (nch + 1) // 2, idx_body, 0)

    def reduce_chunks(tile_fn, init, combine):
        n_acc = SUB_B // SUBLANES

        def body(c, accs):
            accs = list(accs)
            for r0 in range(0, kc, SUB_B):
                t = tile_fn(c, r0, key_scr[c, r0:r0 + SUB_B, :])
                t = t.reshape(n_acc, SUBLANES, qb)
                for i in range(n_acc):
                    accs[i] = combine(accs[i], t[i])
            return tuple(accs)

        accs = lax.fori_loop(0, nch, body, (init,) * n_acc)
        acc = accs[0]
        for a in accs[1:]:
            acc = combine(acc, a)
        return acc

    def count(pred_fn):
        acc = reduce_chunks(lambda c, r0, k: jnp.where(pred_fn(c, r0, k), 1.0, 0.0),
                            jnp.zeros((SUBLANES, qb), F32), lambda a, b: a + b)
        return jnp.sum(acc, axis=0, keepdims=True)

    def unkey(k):
        return lax.bitcast_convert_type(k ^ ((k >> 31) & INT_MAX), F32)

    def tokey(v):
        b = lax.bitcast_convert_type(v, jnp.int32)
        return b ^ ((b >> 31) & INT_MAX)

    def select_all():
        return jnp.full((1, qb), -jnp.inf, F32), jnp.full((1, qb), -1, jnp.int32)

    def select_topk():
        kmax = reduce_chunks(lambda c, r0, k: k, jnp.full((SUBLANES, qb), -jnp.inf, F32), jnp.maximum)
        kmax = jnp.max(kmax, axis=0, keepdims=True)
        kmin = reduce_chunks(lambda c, r0, k: jnp.where(k == -jnp.inf, jnp.inf, k),
                             jnp.full((SUBLANES, qb), jnp.inf, F32), jnp.minimum)
        kmin = jnp.min(kmin, axis=0, keepdims=True)
        c0 = count(lambda c, r0, k: k >= 0.0)
        c0p = count(lambda c, r0, k: k >= F32_TINY)
        n_valid = (jq * qb + 1 + lax.broadcasted_iota(jnp.int32, (1, qb), 1)).astype(F32)
        pos = c0p >= kf
        zero = (c0 >= kf) & (c0p < kf)
        lo0 = jnp.where(pos, F32_TINY, jnp.where(zero, 0.0, kmin))
        hi0 = jnp.where(pos, unkey(tokey(kmax) + 1), jnp.where(zero, F32_TINY, 0.0))
        c_lo0 = jnp.where(pos, c0p, jnp.where(zero, c0, n_valid))
        c_hi0 = jnp.where(pos, 0.0, jnp.where(zero, c0p, c0))

        def bis_cond(st):
            return (st[0] < 64) & (st[5] > 0)

        def bis_step(it, lo, hi, c_lo, c_hi):
            klo, khi = tokey(lo), tokey(hi)
            mid_i = unkey((klo >> 1) + (khi >> 1) + (klo & khi & 1))
            mid_v = 0.5 * lo + 0.5 * hi
            use_v = (it < 24) & (mid_v > lo) & (mid_v < hi)
            mid = jnp.where(use_v, mid_v, mid_i)
            mid = jnp.where((mid > -F32_TINY) & (mid < 0.0), -F32_TINY, mid)
            cnt = count(lambda c, r0, k: k >= mid)
            ge = cnt >= kf
            return (jnp.where(ge, mid, lo), jnp.where(ge, hi, mid),
                    jnp.where(ge, cnt, c_lo), jnp.where(ge, c_hi, cnt))

        def bis_body(st):
            it, lo, hi, c_lo, c_hi, _ = st
            for _ in range(BIS_UNROLL):
                lo, hi, c_lo, c_hi = bis_step(it, lo, hi, c_lo, c_hi)
                it = it + 1
            return it, lo, hi, c_lo, c_hi, jnp.max(jnp.where(is_active(lo, hi, c_lo, c_hi), 1, 0))

        def is_active(lo, hi, c_lo, c_hi):
            gap = (lo == 0.0) | ((hi == 0.0) & (lo >= -F32_TINY))
            return (tokey(lo) + 1 < tokey(hi)) & (c_lo != kf) & (c_lo - c_hi > 2.0) & jnp.logical_not(gap)

        st0 = (jnp.int32(0), lo0, hi0, c_lo0, c_hi0,
               jnp.max(jnp.where(is_active(lo0, hi0, c_lo0, c_hi0), 1, 0)))
        _, lo, hi, c_lo, c_hi, _ = lax.while_loop(bis_cond, bis_body, st0)
        open_rows = c_lo != kf

        def finish():
            top = reduce_chunks(lambda c, r0, k: jnp.where(k < hi, k, -jnp.inf),
                                jnp.full((SUBLANES, qb), -jnp.inf, F32), jnp.maximum)
            top = jnp.max(top, axis=0, keepdims=True)
            bot = reduce_chunks(lambda c, r0, k: jnp.where(k >= lo, k, jnp.inf),
                                jnp.full((SUBLANES, qb), jnp.inf, F32), jnp.minimum)
            bot = jnp.min(bot, axis=0, keepdims=True)
            thr = jnp.where(open_rows, top, lo)
            tie = open_rows & (top == bot)
            need = kf - c_hi

            def tie_break():
                def tie_body(_, st):
                    jlo, jhi = st
                    mid = (jlo + jhi) >> 1
                    cnt = count(lambda c, r0, k: (k == thr) & (c * kc + r0 + krow_sub <= mid))
                    ge = cnt >= need
                    return jnp.where(ge, jlo, mid), jnp.where(ge, mid, jhi)

                idx_bits = (n_chunks_total * kc - 1).bit_length()
                _, jhi = lax.fori_loop(0, idx_bits, tie_body,
                                       (jnp.full((1, qb), -1, jnp.int32),
                                        jnp.full((1, qb), 2 ** idx_bits - 1, jnp.int32)))
                return jnp.where(tie, jhi, INT_MAX)

            jmax = lax.cond(jnp.max(jnp.where(tie, 1, 0)) > 0, tie_break,
                            lambda: jnp.full((1, qb), INT_MAX, jnp.int32))
            return thr, jmax

        return lax.cond(jnp.max(jnp.where(open_rows, 1, 0)) > 0, finish,
                        lambda: (lo, jnp.full((1, qb), INT_MAX, jnp.int32)))

    thr, jmax = lax.cond(jq >= 1, select_topk, select_all)

    def sel_body(c, carry):
        k = key_scr[c]
        sel = (k > thr) | ((k == thr) & (c * kc + krow <= jmax))
        key_scr[c] = jnp.where(sel, 0.0, NEG).T
        return carry

    lax.fori_loop(0, nch, sel_body, 0)

    def kv_chunk(c):
        return ckv_ref[0, pl.ds(pl.multiple_of(c * kc, kc), kc), :]

    def qk(nq, c, slot):
        s_scr[slot, :B_HEADS * nq] = lax.dot_general(
            ql_ref[0, :, :nq, :].reshape(B_HEADS * nq, B_KV_RANK), kv_chunk(c), _NT,
            preferred_element_type=F32)

    def logits(nq, c, slot, near, hd, r0):
        sub = min(SUB_B, nq)
        sh = s_scr[slot, hd * nq + r0:hd * nq + r0 + sub]
        if near == "cur":
            t0 = bias_ref[hd, r0 % QB_A:r0 % QB_A + sub, LANES:]
            t1 = bias_ref[hd, r0 % QB_A:r0 % QB_A + sub, :LANES]
            sh = sh + jnp.concatenate([t0 if r0 < QB_A else t1, t0], axis=1)
        elif near == "prev" and r0 < QB_A:
            t1 = bias_ref[hd, r0:r0 + sub, :LANES]
            sh = sh + jnp.concatenate([jnp.zeros_like(t1), t1], axis=1)
        return sh + key_scr[c, r0:r0 + sub, :]

    def sub_tiles(nq):
        sub = min(SUB_B, nq)
        for hd in range(B_HEADS):
            for r0 in range(0, nq, sub):
                yield hd, r0, slice(hd * nq + r0, hd * nq + r0 + sub)

    def sm(nq, c, slot, near):
        for hd, r0, sl in sub_tiles(nq):
            m = m_scr[sl]
            p = jnp.exp2(logits(nq, c, slot, near, hd, r0) - jnp.concatenate([m, m], axis=1))
            l_scr[sl] += p[:, :LANES] + p[:, LANES:]
            p_scr[slot, sl] = p.astype(BF16)

    def pv(nq, c, slot):
        o = jnp.dot(p_scr[slot, :B_HEADS * nq], kv_chunk(c), preferred_element_type=F32)
        sub = min(SUB_B, nq)
        for r0 in range(0, B_HEADS * nq, sub):
            acc_scr[r0:r0 + sub] += o[r0:r0 + sub]

    def bound_attention(nq):
        l_scr[...] = jnp.zeros(l_scr.shape, F32)
        acc_scr[...] = jnp.zeros(acc_scr.shape, F32)
        kn2 = lax.fori_loop(0, nch, lambda c, a: jnp.maximum(a, kn2_ref[0, c]),
                            jnp.zeros((SUBLANES, LANES), F32))
        bmax = bmax_ref[0]
        for hd, r0, sl in sub_tiles(nq):
            x = ql_ref[0, hd, r0:r0 + min(SUB_B, nq), :].astype(F32)
            qn2 = jnp.sum(x * x, axis=-1, keepdims=True)
            m_scr[sl] = jnp.sqrt(qn2 * kn2[:1]) * (1.0 + 2.0 ** -10) + bmax

        def stage(t, slot):
            qk(nq, t, slot)
            pv(nq, t - 2, slot)
            sm(nq, t - 1, 1 - slot, None)

        def tail(slot):
            qk(nq, jq, slot)
            pv(nq, jq - 2, slot)
            sm(nq, jq - 1, 1 - slot, "prev")
            pv(nq, jq - 1, 1 - slot)
            sm(nq, jq, slot, "cur")
            pv(nq, jq, slot)

        @pl.when(jq >= 2)
        def _():
            qk(nq, 0, 0)
            qk(nq, 1, 1)
            sm(nq, 0, 0, None)

            def two_stages(i, carry):
                stage(2 + 2 * i, 0)
                stage(3 + 2 * i, 1)
                return carry

            lax.fori_loop(0, (jq - 2) // 2, two_stages, 0)

            @pl.when(jq % 2 == 0)
            def _():
                tail(0)

            @pl.when(jq % 2 == 1)
            def _():
                stage(jq - 1, 0)
                tail(1)

        @pl.when(jq < 2)
        def _():
            @pl.when(jq == 1)
            def _():
                qk(nq, 0, 0)
                sm(nq, 0, 0, "prev")
                pv(nq, 0, 0)

            qk(nq, jq, 1)
            sm(nq, jq, 1, "cur")
            pv(nq, jq, 1)

    def running_max_attention():
        l_scr[...] = jnp.zeros(l_scr.shape, F32)
        acc_scr[...] = jnp.zeros(acc_scr.shape, F32)
        m_scr[...] = jnp.full(m_scr.shape, -jnp.inf, F32)

        def attend(c, near):
            qk(qb, c, 0)
            for hd, r0, sl in sub_tiles(qb):
                m_old = m_scr[sl]
                m_new = jnp.maximum(m_old, jnp.max(logits(qb, c, 0, near, hd, r0), axis=-1, keepdims=True))
                m_scr[sl] = m_new
                alpha_scr[sl] = jnp.exp2(m_old - m_new)
            for hd, r0, sl in sub_tiles(qb):
                m_new = m_scr[sl]
                p = jnp.exp2(logits(qb, c, 0, near, hd, r0) - jnp.concatenate([m_new, m_new], axis=1))
                l_scr[sl] = alpha_scr[sl] * l_scr[sl] + (p[:, :LANES] + p[:, LANES:])
                p_scr[0, sl] = p.astype(BF16)
            o = jnp.dot(p_scr[0], kv_chunk(c), preferred_element_type=F32)
            for r0 in range(0, rows, SUB_B):
                sl = slice(r0, r0 + SUB_B)
                alpha = alpha_scr[sl]
                acc_scr[sl] = acc_scr[sl] * jnp.concatenate([alpha, alpha], axis=1) + o[sl]

        def far_body(c, carry):
            attend(c, None)
            return carry

        lax.fori_loop(0, jq - 1, far_body, 0)

        @pl.when(jq >= 1)
        def _():
            attend(jq - 1, "prev")

        attend(jq, "cur")

    def write_output(nq):
        l_min = jnp.full((nq, 1), jnp.inf, F32)
        for pair in range(B_HEADS // 2):
            col = slice(pair * LANES, (pair + 1) * LANES)
            y2 = jnp.zeros((nq, LANES), F32)
            for par in range(2):
                hd = 2 * pair + par
                sl = slice(hd * nq, (hd + 1) * nq)
                l = jnp.sum(l_scr[sl], axis=-1, keepdims=True)
                l_min = jnp.minimum(l_min, l)
                o_lat = acc_scr[sl] / l
                y2 = y2 + jnp.dot(o_lat.astype(BF16), wuv_ref[pair, par], preferred_element_type=F32)
            o_ref[:nq, col] = (y2 * _silu(zb_ref[:nq, col])).astype(o_ref.dtype)
            if nq < qb:
                o_ref[nq:, col] = jnp.zeros((qb - nq, LANES), o_ref.dtype)
        safe = jnp.min(l_min) >= L_SAFE
        flag_ref[0, 0] = jnp.full(flag_ref.shape[2:], jnp.where(safe, 0.0, 1.0), F32)

    if not bound:
        running_max_attention()
        write_output(qb)
    elif tail_q == qb:
        bound_attention(qb)
        write_output(qb)
    else:
        last = pl.num_programs(1) - 1

        @pl.when(jq < last)
        def _():
            bound_attention(qb)
            write_output(qb)

        @pl.when(jq == last)
        def _():
            bound_attention(tail_q)
            write_output(tail_q)


def _mixer_b(bmax, ql, qi, wi, ckvn, klo, khi, kn2, proj, bias_b, wuv_pairs, layer, n_batch, lp, length,
             bound):
    nb = lp // QB_B
    tail_q = min(QB_B, -(-(length - (nb - 1) * QB_B) // 16) * 16)
    rows = n_batch * lp
    whole = lambda b, j: (b, 0, 0)
    once = pl.Buffered(1)
    return pl.pallas_call(
        functools.partial(_mixer_b_kernel, bound=bound, tail_q=tail_q),
        grid=(n_batch, nb),
        in_specs=[
            pl.BlockSpec(memory_space=pltpu.SMEM),
            pl.BlockSpec((1, B_HEADS, QB_B, B_KV_RANK), lambda b, j: (b, 0, j, 0)),
            pl.BlockSpec((1, IDX_HEADS // 2, QB_B, LANES), lambda b, j: (b, 0, j, 0)),
            pl.BlockSpec((1, QB_B, LANES), lambda b, j: (b, j, 0)),
            pl.BlockSpec((1, lp, B_KV_RANK), whole, pipeline_mode=once),
            pl.BlockSpec((1, lp, LANES), whole, pipeline_mode=once),
            pl.BlockSpec((1, lp, LANES), whole, pipeline_mode=once),
            pl.BlockSpec((1, nb, SUBLANES, LANES), lambda b, j: (b, 0, 0, 0), pipeline_mode=once),
            pl.BlockSpec((QB_B, B_WIDTH), lambda b, j: (b * nb + j, C_ZB // B_WIDTH)),
            pl.BlockSpec((B_HEADS, QB_A, 2 * LANES), lambda b, j: (0, 0, 0), pipeline_mode=once),
            pl.BlockSpec((None, B_HEADS // 2, 2, B_KV_RANK, LANES), lambda b, j: (layer, 0, 0, 0, 0),
                         pipeline_mode=once),
        ],
        out_specs=[pl.BlockSpec((QB_B, B_WIDTH), lambda b, j: (b * nb + j, 0)),
                   pl.BlockSpec((1, 1, SUBLANES, LANES), lambda b, j: (b, j, 0, 0))],
        out_shape=[jax.ShapeDtypeStruct((rows, B_WIDTH), BF16),
                   jax.ShapeDtypeStruct((n_batch, nb, SUBLANES, LANES), F32)],
        scratch_shapes=[
            pltpu.VMEM((nb, QB_B, QB_B), F32),
            pltpu.VMEM((2, B_HEADS * QB_B, QB_B), F32),
            pltpu.VMEM((B_HEADS * QB_B, B_KV_RANK), F32),
            pltpu.VMEM((B_HEADS * QB_B, LANES), F32),
            pltpu.VMEM((B_HEADS * QB_B, LANES), F32),
            pltpu.VMEM((B_HEADS * QB_B, LANES), F32),
            pltpu.VMEM((2, B_HEADS * QB_B, QB_B), BF16),
        ],
        compiler_params=_cparams(2),
        name="mixer_b" if bound else "mixer_b_running_max",
    )(bmax, ql, qi, wi, ckvn, klo, khi, kn2, proj, bias_b, wuv_pairs)


def _merge_kernel(ya_ref, yb_ref, ga_ref, gb_ref, h_ref, wpa_ref, wpb_ref, wo_ref, g_ref, *out_refs,
                  last):
    pa = jnp.dot(ya_ref[...], wpa_ref[...], preferred_element_type=F32)
    pb = jnp.dot(yb_ref[...], wpb_ref[...], preferred_element_type=F32)
    merged = _sigmoid(ga_ref[...]) * pa + _sigmoid(gb_ref[...]) * pb
    h_new = h_ref[...] + jnp.dot(merged.astype(BF16), wo_ref[...], preferred_element_type=F32)
    normed = _rms(h_new, g_ref[...])
    if last:
        out_refs[0][...] = normed
    else:
        out_refs[0][...] = h_new
        out_refs[1][...] = normed.astype(BF16)


def _merge(ya, yb, gates, h, wpa, wpb, wo, g_next, layer, last, tm=256):
    rows, d = h.shape
    const = lambda i: (0, 0)
    lay = lambda i: (layer, 0, 0)
    once = pl.Buffered(1)
    if last:
        out_specs = [pl.BlockSpec((tm, d), lambda i: (i, 0))]
        out_shape = [jax.ShapeDtypeStruct((rows, d), F32)]
    else:
        out_specs = [pl.BlockSpec((tm, d), lambda i: (i, 0)), pl.BlockSpec((tm, d), lambda i: (i, 0))]
        out_shape = [jax.ShapeDtypeStruct((rows, d), F32), jax.ShapeDtypeStruct((rows, d), BF16)]
    return pl.pallas_call(
        functools.partial(_merge_kernel, last=last),
        grid=(rows // tm,),
        in_specs=[
            pl.BlockSpec((tm, A_WIDTH), lambda i: (i, 0)),
            pl.BlockSpec((tm, B_WIDTH), lambda i: (i, 0)),
            pl.BlockSpec((tm, d), lambda i: (i, gates.shape[1] // d - 2)),
            pl.BlockSpec((tm, d), lambda i: (i, gates.shape[1] // d - 1)),
            pl.BlockSpec((tm, d), lambda i: (i, 0)),
            pl.BlockSpec((None, A_WIDTH, d), lay, pipeline_mode=once),
            pl.BlockSpec((None, B_WIDTH, d), lay, pipeline_mode=once),
            pl.BlockSpec((None, d, d), lay, pipeline_mode=once),
            pl.BlockSpec((1, d), const),
        ],
        out_specs=out_specs,
        out_shape=out_shape,
        compiler_params=_cparams(1),
        name="merge_out",
    )(ya, yb, gates, gates, h, wpa, wpb, wo, g_next.reshape(1, d))


def _pack_weights(w_in, w_uk, w_uv, w_qb, w_iq, w_proj_a, w_proj_b, w_out):
    depth = w_in.shape[0]
    sec = lambda a, b: w_in[:, :, a:b]
    kidx, widx = sec(4096, 4160), sec(4160, 4168)
    z64 = jnp.zeros_like(kidx)
    zpad = jnp.zeros(widx.shape[:2] + (LANES - IDX_HEADS,), w_in.dtype)
    w_small = jnp.concatenate([kidx, z64, z64, kidx, widx, zpad], axis=-1).astype(BF16)
    uk = (w_uk * (HEAD_DIM ** -0.5 * LOG2E)).reshape(depth, B_HEADS // 2, 2, HEAD_DIM, B_KV_RANK)
    zuk = jnp.zeros_like(uk[:, :, 0])
    wuk_bd = jnp.concatenate([jnp.concatenate([uk[:, :, 0], zuk], axis=-1),
                              jnp.concatenate([zuk, uk[:, :, 1]], axis=-1)], axis=-2).astype(BF16)
    uv = w_uv.reshape(depth, B_HEADS // 2, 2, B_KV_RANK, HEAD_DIM)
    zuv = jnp.zeros_like(uv[:, :, 0])
    wuv_pairs = jnp.stack([jnp.concatenate([uv[:, :, 0], zuv], axis=-1),
                           jnp.concatenate([zuv, uv[:, :, 1]], axis=-1)], axis=2).astype(BF16)
    w_iq8 = (w_iq * (IDX_DIM ** -0.5)).astype(BF16)
    return (w_small, wuk_bd, wuv_pairs, w_qb.astype(BF16), w_iq8,
            w_proj_a.astype(BF16), w_proj_b.astype(BF16), w_out.astype(BF16))


def kernel(x, meta_tokens, bias_table, norm_g, w_in, q_norm_g, kv_norm_g, w_qb, w_iq, w_uk, w_uv,
           sinks, w_proj_a, w_proj_b, w_out, final_g):
    n_batch, seq, d = x.shape
    depth = w_in.shape[0]
    length = seq + N_META
    lp = -(-length // QB_B) * QB_B
    assert min(TOPK_MAX, seq // 4) == TOPK_MAX and d == D_MODEL
    assert WINDOW == QB_A and lp // QB_B >= 3

    (w_small, wuk_bd, wuv_pairs, wqb, wiq, wpa, wpb, wo) = _pack_weights(
        w_in, w_uk, w_uv, w_qb, w_iq, w_proj_a, w_proj_b, w_out)
    w_t = jnp.swapaxes(w_in, 1, 2)
    bias_a, bias_b = _bias_tiles(bias_table)
    bmax = jnp.max(jnp.abs(bias_b)).reshape(1)

    meta = jnp.broadcast_to(meta_tokens.astype(x.dtype)[None], (n_batch, N_META, d))
    pad = jnp.zeros((n_batch, lp - length, d), x.dtype)
    h = jnp.concatenate([meta, x, pad], axis=1).reshape(n_batch * lp, d)

    u = _rmsnorm(h, norm_g[0])
    out = None
    for l in range(depth):
        proj = _in_proj(u, w_t, l)
        gates = proj
        ya = _mixer_a(proj, sinks[l], bias_a, n_batch, lp)
        ql, qi, wi, ckvn, klo, khi, kn2 = _bprep(
            proj, u, q_norm_g[l].reshape(1, -1), kv_norm_g[l].reshape(1, -1),
            wqb, wiq, wuk_bd, w_small, l, n_batch, lp)
        b_args = (bmax, ql, qi, wi, ckvn, klo, khi, kn2, proj, bias_b, wuv_pairs, l, n_batch, lp, length)
        yb, flags = _mixer_b(*b_args, bound=True)
        yb = lax.cond(jnp.max(flags) > 0.0, lambda: _mixer_b(*b_args, bound=False)[0], lambda: yb)
        last = l == depth - 1
        g_next = final_g if last else norm_g[l + 1]
        res = _merge(ya, yb, gates, h, wpa, wpb, wo, g_next, l, last)
        if last:
            out = res[0]
        else:
            h, u = res
    return out.reshape(n_batch, lp, d)[:, N_META:length]
```
